```python
import math
import jax, jax.numpy as jnp
from jax import lax
import numpy as np

D_MODEL = 1024
BATCH = 8
SEQ = 4096
DEPTH = 1

CTX_LEN = 256
GRID_W = 64
MIX_WIDTH = 2 * D_MODEL
CONV_WIDTH = MIX_WIDTH // 2
SSM_WIDTH = MIX_WIDTH - CONV_WIDTH
CONV_TAPS = 31
CONV_PAD = CONV_TAPS // 2
SSM_GROUP = 16
SSM_GROUPS = SSM_WIDTH // SSM_GROUP
SSM_STATE = 64
DT_MIN = 1e-3
DT_MAX = 1e-1
EPS = 1e-6
IN_COLS = 3 * CONV_WIDTH + 2 * SSM_WIDTH
SPLITS = (CONV_WIDTH, 2 * CONV_WIDTH, 3 * CONV_WIDTH, 3 * CONV_WIDTH + SSM_WIDTH)
U_START = 3 * CONV_WIDTH

kernel_name = 'hybrid_conformer_s5_prefix_dit_layer'


def rmsnorm(x, g):
    xf = x.astype(jnp.float32)
    y = xf * lax.rsqrt(jnp.mean(xf * xf, axis=-1, keepdims=True) + EPS)
    return (y * g.astype(jnp.float32)).astype(x.dtype)


def layernorm(x, g, b):
    xf = x.astype(jnp.float32)
    mu = jnp.mean(xf, axis=-1, keepdims=True)
    var = jnp.mean(jnp.square(xf - mu), axis=-1, keepdims=True)
    y = (xf - mu) * lax.rsqrt(var + EPS)
    return (y * g.astype(jnp.float32) + b.astype(jnp.float32)).astype(x.dtype)


def dwconv_latent(v, w, bias):
    bsz, length, ch = v.shape
    rows = length // GRID_W
    half = ch // 2
    vg = v.reshape(bsz, rows, GRID_W, ch)
    dn = ('NHWC', 'HWIO', 'NHWC')
    w_h = w[:, :half].reshape(1, CONV_TAPS, 1, half)
    w_v = w[:, half:].reshape(CONV_TAPS, 1, 1, ch - half)
    y_h = lax.conv_general_dilated(vg[..., :half], w_h, (1, 1), ((0, 0), (CONV_PAD, CONV_PAD)),
                                   dimension_numbers=dn, feature_group_count=half)
    y_v = lax.conv_general_dilated(vg[..., half:], w_v, (1, 1), ((CONV_PAD, CONV_PAD), (0, 0)),
                                   dimension_numbers=dn, feature_group_count=ch - half)
    return jnp.concatenate([y_h, y_v], axis=-1).reshape(bsz, length, ch) + bias


def dwconv_seq(v, w, bias):
    ch = v.shape[-1]
    y = lax.conv_general_dilated(v, w.reshape(CONV_TAPS, 1, ch), (1,), ((CONV_PAD, CONV_PAD),),
                                 dimension_numbers=('NWC', 'WIO', 'NWC'), feature_group_count=ch)
    return y + bias


def conv_branch(val, glu_gate, silu_gate, w_dw, b_dw, ln_g, ln_b, on_grid):
    v = val * jax.nn.sigmoid(glu_gate)
    v = dwconv_latent(v, w_dw, b_dw) if on_grid else dwconv_seq(v, w_dw, b_dw)
    v = jax.nn.silu(layernorm(v, ln_g, ln_b))
    return v * jax.nn.silu(silu_gate)


def s5_discretise(a_re, a_im, log_dt, b_re, b_im):
    lam = lax.complex(a_re.astype(jnp.float32), a_im.astype(jnp.float32))
    dt = jnp.exp(log_dt.astype(jnp.float32))[:, None]
    lam_bar = jnp.exp(lam * dt)
    b_mat = lax.complex(b_re.astype(jnp.float32), b_im.astype(jnp.float32))
    b_bar = ((lam_bar - 1.0) / lam)[..., None] * b_mat
    return lam_bar, b_bar


def _linear_recurrence(e1, e2):
    a1, b1 = e1
    a2, b2 = e2
    return a1 * a2, a2 * b1 + b2


def s5_direction(u4, lam_bar, b_bar, h0, reverse):
    bu = jnp.einsum('blgh,gph->lbgp', u4.astype(jnp.complex64), b_bar)
    if reverse:
        bu = bu[::-1]
    if h0 is not None:
        bu = bu.at[0].add(lam_bar * h0)
    a = jnp.broadcast_to(lam_bar, (bu.shape[0], 1) + lam_bar.shape)
    _, h = lax.associative_scan(_linear_recurrence, (a, bu), axis=0)
    final = h[-1]
    if reverse:
        h = h[::-1]
    return h, final


def s5_readout(h, c_mat):
    return jnp.real(jnp.einsum('lbgp,ghp->blgh', h, c_mat))


def ssm_branch(u, y_f, y_b, silu_gate, d, glu_w, glu_b):
    bsz, length, _ = u.shape
    y = (y_f + y_b).reshape(bsz, length, SSM_WIDTH) + d.astype(jnp.float32) * u.astype(jnp.float32)
    y = jax.nn.gelu(y).astype(u.dtype)
    y = y * jax.nn.sigmoid(y @ glu_w + glu_b)
    return y * jax.nn.silu(silu_gate)


def setup_inputs(seed: int = 0) -> dict:
    key = jax.random.key(seed)
    ks = jax.random.split(key, 24)
    f32 = jnp.float32
    nrm = lambda k, shape, s: jax.random.normal(k, shape, f32) * s
    n_idx = jnp.arange(SSM_STATE, dtype=f32)
    return {
        'x': nrm(ks[0], (BATCH, SEQ, D_MODEL), 1.0),
        'c': nrm(ks[1], (BATCH, D_MODEL), 1.0),
        'ctx': nrm(ks[2], (BATCH, CTX_LEN, D_MODEL), 1.0),
        'c_ctx': nrm(ks[3], (D_MODEL,), 1.0),
        'norm_g': 1.0 + nrm(ks[4], (DEPTH, D_MODEL), 0.02),
        'w_ada': nrm(ks[5], (DEPTH, D_MODEL, 3 * D_MODEL), D_MODEL ** -0.5),
        'b_ada': nrm(ks[6], (DEPTH, 3 * D_MODEL), 0.02),
        'w_in': nrm(ks[7], (DEPTH, D_MODEL, IN_COLS), D_MODEL ** -0.5),
        'conv_dw': nrm(ks[8], (DEPTH, CONV_TAPS, CONV_WIDTH), CONV_TAPS ** -0.5),
        'conv_db': nrm(ks[9], (DEPTH, CONV_WIDTH), 0.02),
        'conv_ln_g': 1.0 + nrm(ks[10], (DEPTH, CONV_WIDTH), 0.02),
        'conv_ln_b': nrm(ks[11], (DEPTH, CONV_WIDTH), 0.02),
        'ssm_a_re': -0.5 + nrm(ks[12], (DEPTH, 2, SSM_GROUPS, SSM_STATE), 0.01),
        'ssm_a_im': math.pi * n_idx + nrm(ks[13], (DEPTH, 2, SSM_GROUPS, SSM_STATE), 0.01),
        'ssm_log_dt': jax.random.uniform(ks[14], (DEPTH, 2, SSM_GROUPS), f32,
                                         math.log(DT_MIN), math.log(DT_MAX)),
        'ssm_b_re': nrm(ks[15], (DEPTH, 2, SSM_GROUPS, SSM_STATE, SSM_GROUP), (2 * SSM_GROUP) ** -0.5),
        'ssm_b_im': nrm(ks[16], (DEPTH, 2, SSM_GROUPS, SSM_STATE, SSM_GROUP), (2 * SSM_GROUP) ** -0.5),
        'ssm_c_re': nrm(ks[17], (DEPTH, 2, SSM_GROUPS, SSM_GROUP, SSM_STATE), SSM_STATE ** -0.5),
        'ssm_c_im': nrm(ks[18], (DEPTH, 2, SSM_GROUPS, SSM_GROUP, SSM_STATE), SSM_STATE ** -0.5),
        'ssm_d': nrm(ks[19], (DEPTH, SSM_WIDTH), 1.0),
        'ssm_glu_w': nrm(ks[20], (DEPTH, SSM_WIDTH, SSM_WIDTH), SSM_WIDTH ** -0.5),
        'ssm_glu_b': nrm(ks[21], (DEPTH, SSM_WIDTH), 0.02),
        'w_out': nrm(ks[22], (DEPTH, MIX_WIDTH, D_MODEL), MIX_WIDTH ** -0.5),
        'final_g': 1.0 + nrm(ks[23], (D_MODEL,), 0.02),
    }


def reference(x, c, ctx, c_ctx, norm_g, w_ada, b_ada, w_in, conv_dw, conv_db, conv_ln_g, conv_ln_b,
              ssm_a_re, ssm_a_im, ssm_log_dt, ssm_b_re, ssm_b_im, ssm_c_re, ssm_c_im, ssm_d,
              ssm_glu_w, ssm_glu_b, w_out, final_g):
    bsz, length, _ = x.shape
    ctx_len = ctx.shape[1]
    s_c = jax.nn.silu(c)
    s_cc = jax.nn.silu(c_ctx)
    xc = ctx
    for i in range(DEPTH):
        last = i == DEPTH - 1
        shift, scale, gate = jnp.split(s_c @ w_ada[i] + b_ada[i], 3, axis=-1)
        shift_c, scale_c, gate_c = jnp.split(s_cc @ w_ada[i] + b_ada[i], 3, axis=-1)
        h = rmsnorm(x, norm_g[i]) * (1.0 + scale[:, None]) + shift[:, None]
        hc = rmsnorm(xc, norm_g[i]) * (1.0 + scale_c) + shift_c

        lam_f, bbar_f = s5_discretise(ssm_a_re[i, 0], ssm_a_im[i, 0], ssm_log_dt[i, 0], ssm_b_re[i, 0], ssm_b_im[i, 0])
        lam_b, bbar_b = s5_discretise(ssm_a_re[i, 1], ssm_a_im[i, 1], ssm_log_dt[i, 1], ssm_b_re[i, 1], ssm_b_im[i, 1])
        cmat_f = lax.complex(ssm_c_re[i, 0].astype(jnp.float32), ssm_c_im[i, 0].astype(jnp.float32))
        cmat_b = lax.complex(ssm_c_re[i, 1].astype(jnp.float32), ssm_c_im[i, 1].astype(jnp.float32))

        if last:
            u_c = hc @ w_in[i][:, U_START:U_START + SSM_WIDTH]
        else:
            p_c = jnp.split(hc @ w_in[i], SPLITS, axis=-1)
            u_c = p_c[3]
        u_c4 = u_c.astype(jnp.float32).reshape(bsz, ctx_len, SSM_GROUPS, SSM_GROUP)
        hc_f, fin_f = s5_direction(u_c4, lam_f, bbar_f, None, False)
        hc_b, fin_b = s5_direction(u_c4, lam_b, bbar_b, None, True)

        p = jnp.split(h @ w_in[i], SPLITS, axis=-1)
        u4 = p[3].astype(jnp.float32).reshape(bsz, length, SSM_GROUPS, SSM_GROUP)
        hf, _ = s5_direction(u4, lam_f, bbar_f, fin_f, False)
        y_f = s5_readout(hf, cmat_f)
        hb, _ = s5_direction(u4, lam_b, bbar_b, fin_b, True)
        y_b = s5_readout(hb, cmat_b)
        conv_out = conv_branch(p[0], p[1], p[2], conv_dw[i], conv_db[i], conv_ln_g[i], conv_ln_b[i], True)
        ssm_out = ssm_branch(p[3], y_f, y_b, p[4], ssm_d[i], ssm_glu_w[i], ssm_glu_b[i])
        mix = jnp.concatenate([conv_out, ssm_out], axis=-1) @ w_out[i]
        x = x + gate[:, None] * mix

        if not last:
            conv_c = conv_branch(p_c[0], p_c[1], p_c[2], conv_dw[i], conv_db[i], conv_ln_g[i], conv_ln_b[i], False)
            ssm_c = ssm_branch(p_c[3], s5_readout(hc_f, cmat_f), s5_readout(hc_b, cmat_b), p_c[4],
                               ssm_d[i], ssm_glu_w[i], ssm_glu_b[i])
            xc = xc + gate_c * (jnp.concatenate([conv_c, ssm_c], axis=-1) @ w_out[i])
    return rmsnorm(x, final_g)
```

```python
import functools

import jax
import jax.numpy as jnp
from jax import lax
from jax.experimental import pallas as pl
from jax.experimental.pallas import tpu as pltpu

GRID_W = 64
CONV_TAPS = 31
CONV_PAD = CONV_TAPS // 2
SSM_GROUP = 16
SSM_STATE = 64
CHUNK = 16
BATCH = 8
EPS = 1e-6
LANES = 128
VMEM_LIMIT = 56 * 1024 * 1024

F32 = jnp.float32
BF16 = jnp.bfloat16


def _cparams(sem):
    return pltpu.CompilerParams(dimension_semantics=sem, vmem_limit_bytes=VMEM_LIMIT)


def _ada_kernel(c_ref, w_ref, b_ref, o_ref):
    s = jax.nn.silu(c_ref[...])
    o_ref[...] = jnp.dot(s, w_ref[...], preferred_element_type=F32,
                         precision=lax.Precision.HIGHEST) + b_ref[...]


def _ada(cond16, w_ada, b_ada):
    d, n3 = w_ada.shape
    nb = n3 // d
    return pl.pallas_call(
        _ada_kernel,
        grid=(nb,),
        in_specs=[pl.BlockSpec((16, d), lambda j: (0, 0)),
                  pl.BlockSpec((d, d), lambda j: (0, j)),
                  pl.BlockSpec((1, d), lambda j: (0, j))],
        out_specs=pl.BlockSpec((16, d), lambda j: (0, j)),
        out_shape=jax.ShapeDtypeStruct((16, n3), F32),
        compiler_params=_cparams(("arbitrary",)),
        name="ada",
    )(cond16, w_ada, b_ada.reshape(1, n3))


def _modulated_norm(x, a, s):
    cb, _, d = x.shape
    ms = jnp.mean(x * x, axis=-1, keepdims=True)
    xn = (x * lax.rsqrt(ms + EPS)).reshape(cb * 2, BATCH, d)
    return (xn * a[None] + s[None]).reshape(cb, 2 * BATCH, d)


def _inproj_kernel(x_ref, a_ref, s_ref, w012_ref, w34t_ref, v_ref, g1_ref, ut_ref, g2t_ref):
    cb, _, d = x_ref.shape
    cw = v_ref.shape[-1]
    sw = ut_ref.shape[1]
    h = _modulated_norm(x_ref[...], a_ref[...], s_ref[...])
    hb = h.reshape(cb * 2 * BATCH, d).astype(BF16)
    p = jnp.dot(hb, w012_ref[...], preferred_element_type=F32)
    v = p[:, :cw] * jax.nn.sigmoid(p[:, cw:2 * cw])
    v_ref[...] = v.astype(BF16).reshape(cb, 2 * BATCH, cw)
    g1_ref[...] = jax.nn.silu(p[:, 2 * cw:]).astype(BF16).reshape(cb, 2 * BATCH, cw)
    for tl in range(2):
        ht = h[:, tl * BATCH:(tl + 1) * BATCH, :].reshape(cb * BATCH, d).astype(BF16)
        q = lax.dot_general(w34t_ref[...], ht, (((1,), (1,)), ((), ())),
                            preferred_element_type=F32)
        ut_ref[tl] = q[:sw].astype(BF16)
        g2t_ref[tl] = jax.nn.silu(q[sw:]).astype(BF16)


def _inproj(x4, a8, s8, w012, w34t, cb):
    nch, _, _, d = x4.shape
    cw = w012.shape[1] // 3
    sw = w34t.shape[0] // 2
    nl = nch * BATCH
    row_spec = lambda width: pl.BlockSpec((cb, None, 2 * BATCH, width), lambda i, t: (i, t, 0, 0))
    col_spec = pl.BlockSpec((2, sw, cb * BATCH), lambda i, t: (t, 0, i))
    return pl.pallas_call(
        _inproj_kernel,
        grid=(nch // cb, CHUNK // 2),
        in_specs=[row_spec(d),
                  pl.BlockSpec((BATCH, d), lambda i, t: (0, 0)),
                  pl.BlockSpec((BATCH, d), lambda i, t: (0, 0)),
                  pl.BlockSpec(w012.shape, lambda i, t: (0, 0)),
                  pl.BlockSpec(w34t.shape, lambda i, t: (0, 0))],
        out_specs=[row_spec(cw), row_spec(cw), col_spec, col_spec],
        out_shape=[jax.ShapeDtypeStruct((nch, CHUNK // 2, 2 * BATCH, cw), BF16),
                   jax.ShapeDtypeStruct((nch, CHUNK // 2, 2 * BATCH, cw), BF16),
                   jax.ShapeDtypeStruct((CHUNK, sw, nl), BF16),
                   jax.ShapeDtypeStruct((CHUNK, sw, nl), BF16)],
        compiler_params=_cparams(("arbitrary", "arbitrary")),
        name="inproj",
    )(x4, a8, s8, w012, w34t)


def _ctxproj_kernel(x_ref, a_ref, s_ref, wut_ref, ut_ref):
    cb, _, d = x_ref.shape
    h = _modulated_norm(x_ref[...], a_ref[...], s_ref[...])
    for tl in range(2):
        ht = h[:, tl * BATCH:(tl + 1) * BATCH, :].reshape(cb * BATCH, d).astype(BF16)
        q = lax.dot_general(wut_ref[...], ht, (((1,), (1,)), ((), ())),
                            preferred_element_type=F32)
        ut_ref[tl] = q.astype(BF16)


def _ctxproj(x4, a8, s8, wut):
    nch, _, _, d = x4.shape
    sw = wut.shape[0]
    return pl.pallas_call(
        _ctxproj_kernel,
        grid=(CHUNK // 2,),
        in_specs=[pl.BlockSpec((nch, None, 2 * BATCH, d), lambda t: (0, t, 0, 0)),
                  pl.BlockSpec((BATCH, d), lambda t: (0, 0)),
                  pl.BlockSpec((BATCH, d), lambda t: (0, 0)),
                  pl.BlockSpec(wut.shape, lambda t: (0, 0))],
        out_specs=pl.BlockSpec((2, sw, nch * BATCH), lambda t: (t, 0, 0)),
        out_shape=jax.ShapeDtypeStruct((CHUNK, sw, nch * BATCH), BF16),
        compiler_params=_cparams(("arbitrary",)),
        name="ctxproj",
    )(x4, a8, s8, wut)


def _s5_kernel(xt_ref, xct_ref, mt_ref, bst_ref, cot_ref, lam_ref, dcol_ref, y_ref,
               s_sc, h_sc, *, lane_chunk):
    nl = xt_ref.shape[-1]
    nlc = xct_ref.shape[-1]
    n_ch = nl // BATCH
    n_chc = nlc // BATCH
    kdim = CHUNK * SSM_GROUP
    p = SSM_STATE

    def group_x(ref, j):
        return ref[:, j * SSM_GROUP:(j + 1) * SSM_GROUP, :].reshape(kdim, ref.shape[-1])

    for d in range(2):
        for src, off, width in ((xct_ref, 0, nlc), (xt_ref, nlc, nl)):
            st = [jnp.dot(bst_ref[d, j], group_x(src, j), preferred_element_type=F32)
                  for j in range(2)]
            for ri in range(2):
                pair = jnp.concatenate([st[0][ri * p:(ri + 1) * p], st[1][ri * p:(ri + 1) * p]],
                                       axis=0)
                s_sc[d, ri, off:off + width, :] = pair.T

    lam = [[jnp.broadcast_to(lam_ref[d, ri:ri + 1, :], (BATCH, LANES)) for ri in range(2)]
           for d in range(2)]

    def advance(d, state, row):
        hr, hi = state
        ar, ai = lam[d]
        sr = s_sc[d, 0, pl.ds(row, BATCH), :]
        si = s_sc[d, 1, pl.ds(row, BATCH), :]
        return ar * hr - ai * hi + sr, ar * hi + ai * hr + si

    zero = jnp.zeros((BATCH, LANES), F32)

    def ctx_step(i, carry):
        f, b = carry
        f = advance(0, f, pl.multiple_of(i * BATCH, BATCH))
        b = advance(1, b, pl.multiple_of((n_chc - 1 - i) * BATCH, BATCH))
        return f, b

    carry = lax.fori_loop(0, n_chc, ctx_step, ((zero, zero), (zero, zero)))

    def lat_step(i, carry):
        f, b = carry
        rf = pl.multiple_of(i * BATCH, BATCH)
        rb = pl.multiple_of((n_ch - 1 - i) * BATCH, BATCH)
        h_sc[0, 0, pl.ds(rf, BATCH), :] = f[0]
        h_sc[0, 1, pl.ds(rf, BATCH), :] = f[1]
        h_sc[1, 0, pl.ds(rb, BATCH), :] = b[0]
        h_sc[1, 1, pl.ds(rb, BATCH), :] = b[1]
        f = advance(0, f, nlc + rf)
        b = advance(1, b, nlc + rb)
        return f, b

    lax.fori_loop(0, n_ch, lat_step, carry, unroll=8)

    for c0 in range(0, nl, lane_chunk):
        cs = slice(c0, c0 + lane_chunk)
        ht = [[h_sc[d, ri, cs, :].T for ri in range(2)] for d in range(2)]
        for j in range(2):
            x = xt_ref[:, j * SSM_GROUP:(j + 1) * SSM_GROUP, cs].reshape(kdim, lane_chunk)
            acc = dcol_ref[j] * x.astype(F32)
            for d in range(2):
                hj = jnp.concatenate([ht[d][0][j * p:(j + 1) * p], ht[d][1][j * p:(j + 1) * p]],
                                     axis=0).astype(BF16)
                acc = acc + jnp.dot(mt_ref[d, j], x, preferred_element_type=F32)
                acc = acc + jnp.dot(cot_ref[d, j], hj, preferred_element_type=F32)
            y = jax.nn.gelu(acc).astype(BF16)
            y_ref[:, j * SSM_GROUP:(j + 1) * SSM_GROUP, cs] = y.reshape(CHUNK, SSM_GROUP, lane_chunk)


def _s5(ut, uct, mt, bst, cot, lam16, dcol):
    _, sw, nl = ut.shape
    nlc = uct.shape[-1]
    gp = sw // (2 * SSM_GROUP)
    kdim = CHUNK * SSM_GROUP
    lane_chunk = min(nl, 512)
    kern = functools.partial(_s5_kernel, lane_chunk=lane_chunk)
    return pl.pallas_call(
        kern,
        grid=(gp,),
        in_specs=[pl.BlockSpec((CHUNK, 2 * SSM_GROUP, nl), lambda g: (0, g, 0)),
                  pl.BlockSpec((CHUNK, 2 * SSM_GROUP, nlc), lambda g: (0, g, 0)),
                  pl.BlockSpec((2, 2, kdim, kdim), lambda g: (0, g, 0, 0)),
                  pl.BlockSpec((2, 2, 2 * SSM_STATE, kdim), lambda g: (0, g, 0, 0)),
                  pl.BlockSpec((2, 2, kdim, 2 * SSM_STATE), lambda g: (0, g, 0, 0)),
                  pl.BlockSpec((2, None, 2, LANES), lambda g: (0, g, 0, 0)),
                  pl.BlockSpec((2, kdim, 1), lambda g: (g, 0, 0))],
        out_specs=pl.BlockSpec((CHUNK, 2 * SSM_GROUP, nl), lambda g: (0, g, 0)),
        out_shape=jax.ShapeDtypeStruct((CHUNK, sw, nl), BF16),
        scratch_shapes=[pltpu.VMEM((2, 2, nlc + nl, LANES), F32),
                        pltpu.VMEM((2, 2, nl, LANES), F32)],
        compiler_params=_cparams(("arbitrary",)),
        name="s5",
    )(ut, uct, mt, bst, cot, lam16, dcol)


def _ssmout_kernel(y_ref, g2_ref, wt_ref, b_ref, wo_ref, o_ref):
    y = y_ref[...]
    z = jnp.dot(wt_ref[...], y, preferred_element_type=F32) + b_ref[...]
    s = (y.astype(F32) * jax.nn.sigmoid(z) * g2_ref[...].astype(F32)).astype(BF16)
    mix = lax.dot_general(s, wo_ref[...], (((0,), (0,)), ((), ())), preferred_element_type=F32)
    o_ref[...] = mix.reshape(o_ref.shape)


def _ssmout(y1t, g2t, glu_wt, glu_bcol, wo_s, nlb):
    _, sw, nl = y1t.shape
    d = wo_s.shape[1]
    col_spec = pl.BlockSpec((None, sw, nlb), lambda t, i: (t, 0, i))
    return pl.pallas_call(
        _ssmout_kernel,
        grid=(CHUNK, nl // nlb),
        in_specs=[col_spec, col_spec,
                  pl.BlockSpec(glu_wt.shape, lambda t, i: (0, 0)),
                  pl.BlockSpec((sw, 1), lambda t, i: (0, 0)),
                  pl.BlockSpec(wo_s.shape, lambda t, i: (0, 0))],
        out_specs=pl.BlockSpec((nlb // BATCH, None, BATCH, d), lambda t, i: (i, t, 0, 0)),
        out_shape=jax.ShapeDtypeStruct((nl // BATCH, CHUNK, BATCH, d), F32),
        compiler_params=_cparams(("arbitrary", "arbitrary")),
        name="ssmout",
    )(y1t, g2t, glu_wt, glu_bcol, wo_s)


def _conv_h_kernel(v_ref, w_ref, b_ref, o_ref, pad_sc):
    rb, wrows, _ = v_ref.shape
    halo = CONV_PAD * BATCH
    sub = 64
    pad_sc[:, :halo, :] = jnp.zeros((rb, halo, LANES), F32)
    pad_sc[:, halo + wrows:, :] = jnp.zeros((rb, halo, LANES), F32)
    pad_sc[:, halo:halo + wrows, :] = v_ref[...].astype(F32)
    bias = b_ref[...]

    def body(i, _):
        r = i // (wrows // sub)
        j0 = pl.multiple_of((i % (wrows // sub)) * sub, sub)
        acc = jnp.broadcast_to(bias, (sub, LANES))
        for k in range(CONV_TAPS):
            acc = acc + w_ref[k:k + 1, :] * pad_sc[r, pl.ds(j0 + k * BATCH, sub), :]
        o_ref[r, pl.ds(j0, sub), :] = acc
        return 0

    lax.fori_loop(0, rb * (wrows // sub), body, 0)


def _conv_h(v3, w, b, rb):
    r, wrows, _ = v3.shape
    half = w.shape[1]
    return pl.pallas_call(
        _conv_h_kernel,
        grid=(half // LANES, r // rb),
        in_specs=[pl.BlockSpec((rb, wrows, LANES), lambda c, i: (i, 0, c)),
                  pl.BlockSpec((CONV_TAPS, LANES), lambda c, i: (0, c)),
                  pl.BlockSpec((1, LANES), lambda c, i: (0, c))],
        out_specs=pl.BlockSpec((rb, wrows, LANES), lambda c, i: (i, 0, c)),
        out_shape=jax.ShapeDtypeStruct((r, wrows, half), F32),
        scratch_shapes=[pltpu.VMEM((rb, wrows + 2 * CONV_PAD * BATCH, LANES), F32)],
        compiler_params=_cparams(("arbitrary", "arbitrary")),
        name="conv_h",
    )(v3, w, b.reshape(1, half))


def _conv_v_kernel(v_ref, w_ref, b_ref, o_ref, pad_sc):
    r, wb, _ = v_ref.shape
    sub = 64
    pad_sc[:CONV_PAD] = jnp.zeros((CONV_PAD, wb, LANES), F32)
    pad_sc[CONV_PAD + r:] = jnp.zeros((CONV_PAD, wb, LANES), F32)
    pad_sc[CONV_PAD:CONV_PAD + r] = v_ref[...].astype(F32)
    bias = b_ref[...]

    def body(i, _):
        row = i // (wb // sub)
        j0 = pl.multiple_of((i % (wb // sub)) * sub, sub)
        acc = jnp.broadcast_to(bias, (sub, LANES))
        for k in range(CONV_TAPS):
            acc = acc + w_ref[k:k + 1, :] * pad_sc[row + k, pl.ds(j0, sub), :]
        o_ref[row, pl.ds(j0, sub), :] = acc
        return 0

    lax.fori_loop(0, r * (wb // sub), body, 0)


def _conv_v(v3, w, b, wb, chan_off):
    r, wrows, _ = v3.shape
    half = w.shape[1]
    coff = chan_off // LANES
    return pl.pallas_call(
        _conv_v_kernel,
        grid=(half // LANES, wrows // wb),
        in_specs=[pl.BlockSpec((r, wb, LANES), lambda c, i: (0, i, c + coff)),
                  pl.BlockSpec((CONV_TAPS, LANES), lambda c, i: (0, c)),
                  pl.BlockSpec((1, LANES), lambda c, i: (0, c))],
        out_specs=pl.BlockSpec((r, wb, LANES), lambda c, i: (0, i, c)),
        out_shape=jax.ShapeDtypeStruct((r, wrows, half), F32),
        scratch_shapes=[pltpu.VMEM((r + 2 * CONV_PAD, wb, LANES), F32)],
        compiler_params=_cparams(("arbitrary", "arbitrary")),
        name="conv_v",
    )(v3, w, b.reshape(1, half))


def _final_kernel(ch_ref, cv_ref, g1_ref, ms_ref, x_ref, lng_ref, lnb_ref, wo_ref, gate_ref,
                  fg_ref, o_ref):
    tm, d = x_ref.shape
    cv = jnp.concatenate([ch_ref[...], cv_ref[...]], axis=-1)
    mu = jnp.mean(cv, axis=-1, keepdims=True)
    cen = cv - mu
    var = jnp.mean(cen * cen, axis=-1, keepdims=True)
    y = cen * lax.rsqrt(var + EPS) * lng_ref[...] + lnb_ref[...]
    co = (jax.nn.silu(y) * g1_ref[...].astype(F32)).astype(BF16)
    mix = jnp.dot(co, wo_ref[...], preferred_element_type=F32) + ms_ref[...]
    xo = x_ref[...].reshape(tm // BATCH, BATCH, d) + gate_ref[...][None] * mix.reshape(
        tm // BATCH, BATCH, d)
    ms = jnp.mean(xo * xo, axis=-1, keepdims=True)
    o_ref[...] = (xo * lax.rsqrt(ms + EPS) * fg_ref[...][None]).reshape(tm, d)


def _final(conv_h, conv_v, g1, mix_s, x2, ln_g, ln_b, wo_c, gate8, final_g, tm):
    n, d = x2.shape
    half = conv_h.shape[1]
    cw = 2 * half
    row = lambda width: pl.BlockSpec((tm, width), lambda i: (i, 0))
    full = lambda a: pl.BlockSpec(a.shape, lambda i: (0,) * a.ndim)
    ln_g = ln_g.reshape(1, cw)
    ln_b = ln_b.reshape(1, cw)
    final_g = final_g.reshape(1, d)
    return pl.pallas_call(
        _final_kernel,
        grid=(n // tm,),
        in_specs=[row(half), row(half), row(cw), row(d), row(d),
                  full(ln_g), full(ln_b), full(wo_c), full(gate8), full(final_g)],
        out_specs=row(d),
        out_shape=jax.ShapeDtypeStruct((n, d), F32),
        compiler_params=_cparams(("arbitrary",)),
        name="final",
    )(conv_h, conv_v, g1, mix_s, x2, ln_g, ln_b, wo_c, gate8, final_g)


def _s5_matrices(a_re, a_im, log_dt, b_re, b_im, c_re, c_im, reverse):
    g = a_re.shape[0]
    hp = lax.Precision.HIGHEST
    dt = jnp.exp(log_dt)[:, None]
    k = jnp.arange(CHUNK + 1, dtype=F32)[:, None, None]
    mag = jnp.exp(k * (a_re * dt)[None])
    ang = k * (a_im * dt)[None]
    pr, pi = mag * jnp.cos(ang), mag * jnp.sin(ang)
    den = a_re * a_re + a_im * a_im
    nr, ni = pr[1] - 1.0, pi[1]
    qr, qi = (nr * a_re + ni * a_im) / den, (ni * a_re - nr * a_im) / den
    bbr = qr[..., None] * b_re - qi[..., None] * b_im
    bbi = qr[..., None] * b_im + qi[..., None] * b_re
    clr = c_re[None] * pr[:, :, None, :] - c_im[None] * pi[:, :, None, :]
    cli = c_re[None] * pi[:, :, None, :] + c_im[None] * pr[:, :, None, :]
    kern = (jnp.einsum("kgqp,gph->kgqh", clr[:CHUNK], bbr, precision=hp)
            - jnp.einsum("kgqp,gph->kgqh", cli[:CHUNK], bbi, precision=hp))
    t = jnp.arange(CHUNK)
    lag = (t[None, :] - t[:, None]) if reverse else (t[:, None] - t[None, :])
    mask = (lag >= 0).astype(F32)
    toe = kern[jnp.clip(lag, 0, CHUNK - 1)] * mask[:, :, None, None, None]
    mt = toe.transpose(2, 0, 3, 1, 4).reshape(g, CHUNK * SSM_GROUP, CHUNK * SSM_GROUP)
    e_in = t if reverse else (CHUNK - 1 - t)
    bsr = pr[e_in][..., None] * bbr[None] - pi[e_in][..., None] * bbi[None]
    bsi = pr[e_in][..., None] * bbi[None] + pi[e_in][..., None] * bbr[None]
    bst = jnp.concatenate([bsr, bsi], axis=2).transpose(1, 2, 0, 3).reshape(
        g, 2 * SSM_STATE, CHUNK * SSM_GROUP)
    e_out = (CHUNK - t) if reverse else (t + 1)
    cot = jnp.concatenate([clr[e_out], -cli[e_out]], axis=-1).transpose(1, 0, 2, 3).reshape(
        g, CHUNK * SSM_GROUP, 2 * SSM_STATE)
    lam = jnp.stack([pr[CHUNK].reshape(g // 2, LANES), pi[CHUNK].reshape(g // 2, LANES)], axis=1)
    return mt.astype(BF16), bst.astype(BF16), cot.astype(BF16), lam


def kernel(x, c, ctx, c_ctx, norm_g, w_ada, b_ada, w_in, conv_dw, conv_db, conv_ln_g, conv_ln_b,
           ssm_a_re, ssm_a_im, ssm_log_dt, ssm_b_re, ssm_b_im, ssm_c_re, ssm_c_im, ssm_d,
           ssm_glu_w, ssm_glu_b, w_out, final_g):
    bsz, length, d = x.shape
    ctx_len = ctx.shape[1]
    cw = conv_dw.shape[-1]
    sw = ssm_d.shape[-1]
    assert bsz == BATCH and norm_g.shape[0] == 1
    assert length % (GRID_W * CHUNK) == 0 and ctx_len % CHUNK == 0
    n = length * BATCH
    nch = length // CHUNK
    nchc = ctx_len // CHUNK
    rows = length // GRID_W

    cond16 = jnp.zeros((16, d), F32).at[:BATCH].set(c).at[BATCH].set(c_ctx)
    mod = _ada(cond16, w_ada[0], b_ada[0])
    shift, scale, gate = mod[:, :d], mod[:, d:2 * d], mod[:, 2 * d:]
    amp = norm_g[0][None] * (1.0 + scale)
    a8, s8, gate8 = amp[:BATCH], shift[:BATCH], gate[:BATCH]
    a8c = jnp.broadcast_to(amp[BATCH], (BATCH, d))
    s8c = jnp.broadcast_to(shift[BATCH], (BATCH, d))

    w = w_in[0]
    w012 = w[:, :3 * cw].astype(BF16)
    w34t = w[:, 3 * cw:].T.astype(BF16)
    mats = [_s5_matrices(ssm_a_re[0, r], ssm_a_im[0, r], ssm_log_dt[0, r], ssm_b_re[0, r],
                         ssm_b_im[0, r], ssm_c_re[0, r], ssm_c_im[0, r], bool(r)) for r in range(2)]
    mt, bst, cot, lam16 = (jnp.stack([mats[0][i], mats[1][i]]) for i in range(4))
    dcol = jnp.broadcast_to(ssm_d[0].reshape(sw // SSM_GROUP, 1, SSM_GROUP),
                            (sw // SSM_GROUP, CHUNK, SSM_GROUP)).reshape(
                                sw // SSM_GROUP, CHUNK * SSM_GROUP, 1)

    xt = x.transpose(1, 0, 2)
    ctxt = ctx.transpose(1, 0, 2)
    x4 = xt.reshape(nch, CHUNK // 2, 2 * BATCH, d)
    c4 = ctxt.reshape(nchc, CHUNK // 2, 2 * BATCH, d)

    v4, g14, ut, g2t = _inproj(x4, a8, s8, w012, w34t, cb=min(nch, 32))
    uct = _ctxproj(c4, a8c, s8c, w34t[:sw])
    y1t = _s5(ut, uct, mt, bst, cot, lam16, dcol)
    mix_s = _ssmout(y1t, g2t, ssm_glu_w[0].T.astype(BF16), ssm_glu_b[0].reshape(sw, 1),
                    w_out[0, cw:].astype(BF16), nlb=min(nch * BATCH, 512))

    v3 = v4.reshape(rows, GRID_W * BATCH, cw)
    half = cw // 2
    conv_h = _conv_h(v3, conv_dw[0][:, :half], conv_db[0][:half], rb=min(rows, 8))
    conv_v = _conv_v(v3, conv_dw[0][:, half:], conv_db[0][half:], wb=128, chan_off=half)

    out_t = _final(conv_h.reshape(n, half), conv_v.reshape(n, half), g14.reshape(n, cw),
                   mix_s.reshape(n, d), xt.reshape(n, d), conv_ln_g[0], conv_ln_b[0],
                   w_out[0, :cw].astype(BF16), gate8, final_g, tm=512)
    return out_t.reshape(length, BATCH, d).transpose(1, 0, 2)
```

```python
import functools

import jax
import jax.numpy as jnp
from jax import lax
from jax.experimental import pallas as pl
from jax.experimental.pallas import tpu as pltpu

GRID_W = 64
CONV_TAPS = 31
CONV_PAD = CONV_TAPS // 2
SSM_GROUP = 16
SSM_STATE = 64
CHUNK = 16
BATCH = 8
EPS = 1e-6
LANES = 128
VMEM_LIMIT = 56 * 1024 * 1024

F32 = jnp.float32
BF16 = jnp.bfloat16


def _cparams(sem):
    return pltpu.CompilerParams(dimension_semantics=sem, vmem_limit_bytes=VMEM_LIMIT)


def _ada_kernel(c_ref, w_ref, b_ref, o_ref):
    s = jax.nn.silu(c_ref[...])
    o_ref[...] = jnp.dot(s, w_ref[...], preferred_element_type=F32,
                         precision=lax.Precision.HIGHEST) + b_ref[...]


def _ada(cond16, w_ada, b_ada):
    d, n3 = w_ada.shape
    nb = n3 // d
    return pl.pallas_call(
        _ada_kernel,
        grid=(nb,),
        in_specs=[pl.BlockSpec((16, d), lambda j: (0, 0)),
                  pl.BlockSpec((d, d), lambda j: (0, j)),
                  pl.BlockSpec((1, d), lambda j: (0, j))],
        out_specs=pl.BlockSpec((16, d), lambda j: (0, j)),
        out_shape=jax.ShapeDtypeStruct((16, n3), F32),
        compiler_params=_cparams(("arbitrary",)),
        name="ada",
    )(cond16, w_ada, b_ada.reshape(1, n3))


def _modulated_norm(x, a, s):
    cb, _, d = x.shape
    ms = jnp.mean(x * x, axis=-1, keepdims=True)
    xn = (x * lax.rsqrt(ms + EPS)).reshape(cb * 2, BATCH, d)
    return (xn * a[None] + s[None]).reshape(cb, 2 * BATCH, d)


def _inproj_kernel(x_ref, a_ref, s_ref, w012_ref, w34t_ref, v_ref, g1_ref, ut_ref, g2t_ref):
    cb, _, d = x_ref.shape
    cw = v_ref.shape[-1]
    sw = ut_ref.shape[1]
    h = _modulated_norm(x_ref[...], a_ref[...], s_ref[...])
    hb = h.reshape(cb * 2 * BATCH, d).astype(BF16)
    p = jnp.dot(hb, w012_ref[...], preferred_element_type=F32)
    v = p[:, :cw] * jax.nn.sigmoid(p[:, cw:2 * cw])
    v_ref[...] = v.astype(BF16).reshape(cb, 2 * BATCH, cw)
    g1_ref[...] = jax.nn.silu(p[:, 2 * cw:]).astype(BF16).reshape(cb, 2 * BATCH, cw)
    for tl in range(2):
        ht = h[:, tl * BATCH:(tl + 1) * BATCH, :].reshape(cb * BATCH, d).astype(BF16)
        q = lax.dot_general(w34t_ref[...], ht, (((1,), (1,)), ((), ())),
                            preferred_element_type=F32)
        ut_ref[tl] = q[:sw].astype(BF16)
        g2t_ref[tl] = jax.nn.silu(q[sw:]).astype(BF16)


def _inproj(x4, a8, s8, w012, w34t, cb):
    nch, _, _, d = x4.shape
    cw = w012.shape[1] // 3
    sw = w34t.shape[0] // 2
    nl = nch * BATCH
    row_spec = lambda width: pl.BlockSpec((cb, None, 2 * BATCH, width), lambda i, t: (i, t, 0, 0))
    col_spec = pl.BlockSpec((2, sw, cb * BATCH), lambda i, t: (t, 0, i))
    return pl.pallas_call(
        _inproj_kernel,
        grid=(nch // cb, CHUNK // 2),
        in_specs=[row_spec(d),
                  pl.BlockSpec((BATCH, d), lambda i, t: (0, 0)),
                  pl.BlockSpec((BATCH, d), lambda i, t: (0, 0)),
                  pl.BlockSpec(w012.shape, lambda i, t: (0, 0)),
                  pl.BlockSpec(w34t.shape, lambda i, t: (0, 0))],
        out_specs=[row_spec(cw), row_spec(cw), col_spec, col_spec],
        out_shape=[jax.ShapeDtypeStruct((nch, CHUNK // 2, 2 * BATCH, cw), BF16),
                   jax.ShapeDtypeStruct((nch, CHUNK // 2, 2 * BATCH, cw), BF16),
                   jax.ShapeDtypeStruct((CHUNK, sw, nl), BF16),
                   jax.ShapeDtypeStruct((CHUNK, sw, nl), BF16)],
        compiler_params=_cparams(("arbitrary", "arbitrary")),
        name="inproj",
    )(x4, a8, s8, w012, w34t)


def _ctxproj_kernel(x_ref, a_ref, s_ref, wut_ref, ut_ref):
    cb, _, d = x_ref.shape
    h = _modulated_norm(x_ref[...], a_ref[...], s_ref[...])
    for tl in range(2):
        ht = h[:, tl * BATCH:(tl + 1) * BATCH, :].reshape(cb * BATCH, d).astype(BF16)
        q = lax.dot_general(wut_ref[...], ht, (((1,), (1,)), ((), ())),
                            preferred_element_type=F32)
        ut_ref[tl] = q.astype(BF16)


def _ctxproj(x4, a8, s8, wut):
    nch, _, _, d = x4.shape
    sw = wut.shape[0]
    return pl.pallas_call(
        _ctxproj_kernel,
        grid=(CHUNK // 2,),
        in_specs=[pl.BlockSpec((nch, None, 2 * BATCH, d), lambda t: (0, t, 0, 0)),
                  pl.BlockSpec((BATCH, d), lambda t: (0, 0)),
                  pl.BlockSpec((BATCH, d), lambda t: (0, 0)),
                  pl.BlockSpec(wut.shape, lambda t: (0, 0))],
        out_specs=pl.BlockSpec((2, sw, nch * BATCH), lambda t: (t, 0, 0)),
        out_shape=jax.ShapeDtypeStruct((CHUNK, sw, nch * BATCH), BF16),
        compiler_params=_cparams(("arbitrary",)),
        name="ctxproj",
    )(x4, a8, s8, wut)


def _s5_kernel(xt_ref, xct_ref, mt_ref, bst_ref, cot_ref, lam_ref, dcol_ref, y_ref,
               s_sc, h_sc, *, lane_chunk):
    nl = xt_ref.shape[-1]
    nlc = xct_ref.shape[-1]
    n_ch = nl // BATCH
    n_chc = nlc // BATCH
    kdim = CHUNK * SSM_GROUP
    p = SSM_STATE

    def group_x(ref, j):
        return ref[:, j * SSM_GROUP:(j + 1) * SSM_GROUP, :].reshape(kdim, ref.shape[-1])

    for d in range(2):
        for src, off, width in ((xct_ref, 0, nlc), (xt_ref, nlc, nl)):
            st = [jnp.dot(bst_ref[d, j], group_x(src, j), preferred_element_type=F32)
                  for j in range(2)]
            for ri in range(2):
                pair = jnp.concatenate([st[0][ri * p:(ri + 1) * p], st[1][ri * p:(ri + 1) * p]],
                                       axis=0)
                s_sc[d, ri, off:off + width, :] = pair.T

    lam = [[jnp.broadcast_to(lam_ref[d, ri:ri + 1, :], (BATCH, LANES)) for ri in range(2)]
           for d in range(2)]

    def advance(d, state, row):
        hr, hi = state
        ar, ai = lam[d]
        sr = s_sc[d, 0, pl.ds(row, BATCH), :]
        si = s_sc[d, 1, pl.ds(row, BATCH), :]
        return ar * hr - ai * hi + sr, ar * hi + ai * hr + si

    zero = jnp.zeros((BATCH, LANES), F32)

    def ctx_step(i, carry):
        f, b = carry
        f = advance(0, f, pl.multiple_of(i * BATCH, BATCH))
        b = advance(1, b, pl.multiple_of((n_chc - 1 - i) * BATCH, BATCH))
        return f, b

    carry = lax.fori_loop(0, n_chc, ctx_step, ((zero, zero), (zero, zero)))

    def lat_step(i, carry):
        f, b = carry
        rf = pl.multiple_of(i * BATCH, BATCH)
        rb = pl.multiple_of((n_ch - 1 - i) * BATCH, BATCH)
        h_sc[0, 0, pl.ds(rf, BATCH), :] = f[0]
        h_sc[0, 1, pl.ds(rf, BATCH), :] = f[1]
        h_sc[1, 0, pl.ds(rb, BATCH), :] = b[0]
        h_sc[1, 1, pl.ds(rb, BATCH), :] = b[1]
        f = advance(0, f, nlc + rf)
        b = advance(1, b, nlc + rb)
        return f, b

    lax.fori_loop(0, n_ch, lat_step, carry, unroll=8)

    for c0 in range(0, nl, lane_chunk):
        cs = slice(c0, c0 + lane_chunk)
        ht = [[h_sc[d, ri, cs, :].T for ri in range(2)] for d in range(2)]
        for j in range(2):
            x = xt_ref[:, j * SSM_GROUP:(j + 1) * SSM_GROUP, cs].reshape(kdim, lane_chunk)
            acc = dcol_ref[j] * x.astype(F32)
            for d in range(2):
                hj = jnp.concatenate([ht[d][0][j * p:(j + 1) * p], ht[d][1][j * p:(j + 1) * p]],
                                     axis=0).astype(BF16)
                acc = acc + jnp.dot(mt_ref[d, j], x, preferred_element_type=F32)
                acc = acc + jnp.dot(cot_ref[d, j], hj, preferred_element_type=F32)
            y = jax.nn.gelu(acc).astype(BF16)
            y_ref[:, j * SSM_GROUP:(j + 1) * SSM_GROUP, cs] = y.reshape(CHUNK, SSM_GROUP, lane_chunk)


def _s5(ut, uct, mt, bst, cot, lam16, dcol):
    _, sw, nl = ut.shape
    nlc = uct.shape[-1]
    gp = sw // (2 * SSM_GROUP)
    kdim = CHUNK * SSM_GROUP
    lane_chunk = min(nl, 512)
    kern = functools.partial(_s5_kernel, lane_chunk=lane_chunk)
    return pl.pallas_call(
        kern,
        grid=(gp,),
        in_specs=[pl.BlockSpec((CHUNK, 2 * SSM_GROUP, nl), lambda g: (0, g, 0)),
                  pl.BlockSpec((CHUNK, 2 * SSM_GROUP, nlc), lambda g: (0, g, 0)),
                  pl.BlockSpec((2, 2, kdim, kdim), lambda g: (0, g, 0, 0)),
                  pl.BlockSpec((2, 2, 2 * SSM_STATE, kdim), lambda g: (0, g, 0, 0)),
                  pl.BlockSpec((2, 2, kdim, 2 * SSM_STATE), lambda g: (0, g, 0, 0)),
                  pl.BlockSpec((2, None, 2, LANES), lambda g: (0, g, 0, 0)),
                  pl.BlockSpec((2, kdim, 1), lambda g: (g, 0, 0))],
        out_specs=pl.BlockSpec((CHUNK, 2 * SSM_GROUP, nl), lambda g: (0, g, 0)),
        out_shape=jax.ShapeDtypeStruct((CHUNK, sw, nl), BF16),
        scratch_shapes=[pltpu.VMEM((2, 2, nlc + nl, LANES), F32),
                        pltpu.VMEM((2, 2, nl, LANES), F32)],
        compiler_params=_cparams(("arbitrary",)),
        name="s5",
    )(ut, uct, mt, bst, cot, lam16, dcol)


def _ssmout_kernel(y_ref, g2_ref, wt_ref, b_ref, wo_ref, o_ref):
    y = y_ref[...]
    z = jnp.dot(wt_ref[...], y, preferred_element_type=F32) + b_ref[...]
    s = (y.astype(F32) * jax.nn.sigmoid(z) * g2_ref[...].astype(F32)).astype(BF16)
    mix = lax.dot_general(s, wo_ref[...], (((0,), (0,)), ((), ())), preferred_element_type=F32)
    o_ref[...] = mix.reshape(o_ref.shape)


def _ssmout(y1t, g2t, glu_wt, glu_bcol, wo_s, nlb):
    _, sw, nl = y1t.shape
    d = wo_s.shape[1]
    col_spec = pl.BlockSpec((None, sw, nlb), lambda t, i: (t, 0, i))
    return pl.pallas_call(
        _ssmout_kernel,
        grid=(CHUNK, nl // nlb),
        in_specs=[col_spec, col_spec,
                  pl.BlockSpec(glu_wt.shape, lambda t, i: (0, 0)),
                  pl.BlockSpec((sw, 1), lambda t, i: (0, 0)),
                  pl.BlockSpec(wo_s.shape, lambda t, i: (0, 0))],
        out_specs=pl.BlockSpec((nlb // BATCH, None, BATCH, d), lambda t, i: (i, t, 0, 0)),
        out_shape=jax.ShapeDtypeStruct((nl // BATCH, CHUNK, BATCH, d), F32),
        compiler_params=_cparams(("arbitrary", "arbitrary")),
        name="ssmout",
    )(y1t, g2t, glu_wt, glu_bcol, wo_s)


def _conv_h_kernel(v_ref, w_ref, b_ref, o_ref, pad_sc):
    rb, wrows, _ = v_ref.shape
    halo = CONV_PAD * BATCH
    sub = 64
    pad_sc[:, :halo, :] = jnp.zeros((rb, halo, LANES), F32)
    pad_sc[:, halo + wrows:, :] = jnp.zeros((rb, halo, LANES), F32)
    pad_sc[:, halo:halo + wrows, :] = v_ref[...].astype(F32)
    bias = b_ref[...]

    def body(i, _):
        r = i // (wrows // sub)
        j0 = pl.multiple_of((i % (wrows // sub)) * sub, sub)
        acc = jnp.broadcast_to(bias, (sub, LANES))
        for k in range(CONV_TAPS):
            acc = acc + w_ref[k:k + 1, :] * pad_sc[r, pl.ds(j0 + k * BATCH, sub), :]
        o_ref[r, pl.ds(j0, sub), :] = acc
        return 0

    lax.fori_loop(0, rb * (wrows // sub), body, 0)


def _conv_h(v3, w, b, rb):
    r, wrows, _ = v3.shape
    half = w.shape[1]
    return pl.pallas_call(
        _conv_h_kernel,
        grid=(half // LANES, r // rb),
        in_specs=[pl.BlockSpec((rb, wrows, LANES), lambda c, i: (i, 0, c)),
                  pl.BlockSpec((CONV_TAPS, LANES), lambda c, i: (0, c)),
                  pl.BlockSpec((1, LANES), lambda c, i: (0, c))],
        out_specs=pl.BlockSpec((rb, wrows, LANES), lambda c, i: (i, 0, c)),
        out_shape=jax.ShapeDtypeStruct((r, wrows, half), F32),
        scratch_shapes=[pltpu.VMEM((rb, wrows + 2 * CONV_PAD * BATCH, LANES), F32)],
        compiler_params=_cparams(("arbitrary", "arbitrary")),
        name="conv_h",
    )(v3, w, b.reshape(1, half))


def _conv_v_kernel(v_ref, w_ref, b_ref, o_ref, pad_sc):
    r, wb, _ = v_ref.shape
    sub = 64
    pad_sc[:CONV_PAD] = jnp.zeros((CONV_PAD, wb, LANES), F32)
    pad_sc[CONV_PAD + r:] = jnp.zeros((CONV_PAD, wb, LANES), F32)
    pad_sc[CONV_PAD:CONV_PAD + r] = v_ref[...].astype(F32)
    bias = b_ref[...]

    def body(i, _):
        row = i // (wb // sub)
        j0 = pl.multiple_of((i % (wb // sub)) * sub, sub)
        acc = jnp.broadcast_to(bias, (sub, LANES))
        for k in range(CONV_TAPS):
            acc = acc + w_ref[k:k + 1, :] * pad_sc[row + k, pl.ds(j0, sub), :]
        o_ref[row, pl.ds(j0, sub), :] = acc
        return 0

    lax.fori_loop(0, r * (wb // sub), body, 0)


def _conv_v(v3, w, b, wb, chan_off):
    r, wrows, _ = v3.shape
    half = w.shape[1]
    coff = chan_off // LANES
    return pl.pallas_call(
        _conv_v_kernel,
        grid=(half // LANES, wrows // wb),
        in_specs=[pl.BlockSpec((r, wb, LANES), lambda c, i: (0, i, c + coff)),
                  pl.BlockSpec((CONV_TAPS, LANES), lambda c, i: (0, c)),
                  pl.BlockSpec((1, LANES), lambda c, i: (0, c))],
        out_specs=pl.BlockSpec((r, wb, LANES), lambda c, i: (0, i, c)),
        out_shape=jax.ShapeDtypeStruct((r, wrows, half), F32),
        scratch_shapes=[pltpu.VMEM((r + 2 * CONV_PAD, wb, LANES), F32)],
        compiler_params=_cparams(("arbitrary", "arbitrary")),
        name="conv_v",
    )(v3, w, b.reshape(1, half))


def _final_kernel(ch_ref, cv_ref, g1_ref, ms_ref, x_ref, lng_ref, lnb_ref, wo_ref, gate_ref,
                  fg_ref, o_ref, mix_sc):
    tl = x_ref.shape[1]
    cv = jnp.concatenate([ch_ref[...], cv_ref[...]], axis=-1)
    mu = jnp.mean(cv, axis=-1, keepdims=True)
    cen = cv - mu
    var = jnp.mean(cen * cen, axis=-1, keepdims=True)
    y = cen * lax.rsqrt(var + EPS) * lng_ref[...] + lnb_ref[...]
    co = (jax.nn.silu(y) * g1_ref[...].astype(F32)).astype(BF16)
    mix = jnp.dot(co, wo_ref[...], preferred_element_type=F32) + ms_ref[...]
    nlt = mix_sc.shape[0]
    for j in range(nlt):
        mix_sc[j] = mix[:, j * LANES:(j + 1) * LANES]
    for b in range(BATCH):
        mb = jnp.concatenate([mix_sc[j, pl.ds(b, tl, stride=BATCH), :] for j in range(nlt)],
                             axis=-1)
        xo = x_ref[b] + gate_ref[b:b + 1, :] * mb
        ms = jnp.mean(xo * xo, axis=-1, keepdims=True)
        o_ref[b] = xo * lax.rsqrt(ms + EPS) * fg_ref[...]


def _final(conv_h, conv_v, g1, mix_s, x, ln_g, ln_b, wo_c, gate8, final_g, tl):
    _, length, d = x.shape
    half = conv_h.shape[1]
    cw = 2 * half
    tm = tl * BATCH
    row = lambda width: pl.BlockSpec((tm, width), lambda i: (i, 0))
    nat = pl.BlockSpec((BATCH, tl, d), lambda i: (0, i, 0))
    full = lambda a: pl.BlockSpec(a.shape, lambda i: (0,) * a.ndim)
    ln_g = ln_g.reshape(1, cw)
    ln_b = ln_b.reshape(1, cw)
    final_g = final_g.reshape(1, d)
    return pl.pallas_call(
        _final_kernel,
        grid=(length // tl,),
        in_specs=[row(half), row(half), row(cw), row(d), nat,
                  full(ln_g), full(ln_b), full(wo_c), full(gate8), full(final_g)],
        out_specs=nat,
        out_shape=jax.ShapeDtypeStruct(x.shape, F32),
        scratch_shapes=[pltpu.VMEM((d // LANES, tm, LANES), F32)],
        compiler_params=_cparams(("arbitrary",)),
        name="final",
    )(conv_h, conv_v, g1, mix_s, x, ln_g, ln_b, wo_c, gate8, final_g)


def _s5prep_kernel(arow_ref, bt_ref, ct_ref, mt_ref, bst_ref, cot_ref, lam_ref, pw_sc, kk_sc):
    kdim = CHUNK * SSM_GROUP
    hp = lax.Precision.HIGHEST
    lane = lax.broadcasted_iota(jnp.int32, (1, LANES), 1)
    is_re = lane < SSM_STATE
    col_t = lax.broadcasted_iota(jnp.int32, (1, kdim), 1) // SSM_GROUP
    kcol = lax.broadcasted_iota(jnp.int32, (pw_sc.shape[1], 1), 0).astype(F32)

    def tile_rows(a):
        return jnp.concatenate([a] * CHUNK, axis=0)

    def power_rows(exps):
        pr = jnp.concatenate([jnp.broadcast_to(pw_sc[0, e:e + 1, :], (SSM_GROUP, LANES))
                              for e in exps], axis=0)
        pi = jnp.concatenate([jnp.broadcast_to(pw_sc[1, e:e + 1, :], (SSM_GROUP, LANES))
                              for e in exps], axis=0)
        return pr, pi

    def cmul(xr, xi, yr, yi, im_sign):
        return jnp.where(is_re, xr * yr - xi * yi, im_sign * (xr * yi + xi * yr))

    for d in range(2):
        ar, ai = arow_ref[d, 0:1, :], arow_ref[d, 1:2, :]
        dt = jnp.exp(arow_ref[d, 2:3, :])
        mag = jnp.exp(kcol * (ar * dt))
        ang = kcol * (ai * dt)
        pw_sc[0] = mag * jnp.cos(ang)
        pw_sc[1] = mag * jnp.sin(ang)
        lam_ref[d, 0] = pw_sc[0, CHUNK:CHUNK + 8, :]
        lam_ref[d, 1] = pw_sc[1, CHUNK:CHUNK + 8, :]
        nr, ni = pw_sc[0, 1:2, :] - 1.0, pw_sc[1, 1:2, :]
        den = ar * ar + ai * ai
        qr, qi = (nr * ar + ni * ai) / den, (ni * ar - nr * ai) / den
        btr, bti = bt_ref[d, 0], bt_ref[d, 1]
        bbr, bbi = tile_rows(qr * btr - qi * bti), tile_rows(qr * bti + qi * btr)
        cr, ci = tile_rows(ct_ref[d, 0]), tile_rows(ct_ref[d, 1])
        tt = list(range(CHUNK))
        e_lag = [CHUNK - 1 - t for t in tt] if d else tt
        e_in = tt if d else [CHUNK - 1 - t for t in tt]
        e_out = [CHUNK - t for t in tt] if d else [t + 1 for t in tt]
        cot_ref[d] = cmul(cr, ci, *power_rows(e_out), -1.0).astype(BF16)
        bst_ref[d] = cmul(bbr, bbi, *power_rows(e_in), 1.0).T.astype(BF16)
        c_lag = cmul(cr, ci, *power_rows(e_lag), -1.0)
        bb = jnp.where(is_re, bbr, bbi)
        kk = lax.dot_general(c_lag, bb, (((1,), (1,)), ((), ())), precision=hp,
                             preferred_element_type=F32)
        zeros = jnp.zeros((kdim, kdim), F32)
        if d:
            kk_sc[:kdim] = kk
            kk_sc[kdim:] = zeros
            starts = [SSM_GROUP * (CHUNK - 1 - t) for t in tt]
        else:
            kk_sc[:kdim] = zeros
            kk_sc[kdim:] = kk
            starts = [kdim - SSM_GROUP * t for t in tt]
        acc = zeros
        for t in tt:
            acc = jnp.where(col_t == t, kk_sc[starts[t]:starts[t] + kdim, :], acc)
        mt_ref[d] = acc.astype(BF16)


def _s5prep(arow, bt, ct):
    g = arow.shape[0]
    kdim = CHUNK * SSM_GROUP
    return pl.pallas_call(
        _s5prep_kernel,
        grid=(g,),
        in_specs=[pl.BlockSpec((None, 2, 8, LANES), lambda i: (i, 0, 0, 0)),
                  pl.BlockSpec((None, 2, 2, SSM_GROUP, LANES), lambda i: (i, 0, 0, 0, 0)),
                  pl.BlockSpec((None, 2, 2, SSM_GROUP, LANES), lambda i: (i, 0, 0, 0, 0))],
        out_specs=[pl.BlockSpec((2, None, kdim, kdim), lambda i: (0, i, 0, 0)),
                   pl.BlockSpec((2, None, 2 * SSM_STATE, kdim), lambda i: (0, i, 0, 0)),
                   pl.BlockSpec((2, None, kdim, 2 * SSM_STATE), lambda i: (0, i, 0, 0)),
                   pl.BlockSpec((None, 2, 2, 8, LANES), lambda i: (i, 0, 0, 0, 0))],
        out_shape=[jax.ShapeDtypeStruct((2, g, kdim, kdim), BF16),
                   jax.ShapeDtypeStruct((2, g, 2 * SSM_STATE, kdim), BF16),
                   jax.ShapeDtypeStruct((2, g, kdim, 2 * SSM_STATE), BF16),
                   jax.ShapeDtypeStruct((g, 2, 2, 8, LANES), F32)],
        scratch_shapes=[pltpu.VMEM((2, CHUNK + 8, LANES), F32),
                        pltpu.VMEM((2 * kdim, kdim), F32)],
        compiler_params=_cparams(("arbitrary",)),
        name="s5prep",
    )(arow, bt, ct)


def _s5prep_inputs(a_re, a_im, log_dt, b_re, b_im, c_re, c_im):
    g = a_re.shape[1]
    dup = lambda a: jnp.concatenate([a, a], axis=-1)
    rows = jnp.stack([dup(a_re), dup(a_im),
                      jnp.broadcast_to(log_dt[..., None], (2, g, LANES))], axis=2)
    arow = jnp.pad(rows, ((0, 0), (0, 0), (0, 5), (0, 0))).transpose(1, 0, 2, 3)
    bt = jnp.stack([dup(b_re.transpose(0, 1, 3, 2)), dup(b_im.transpose(0, 1, 3, 2))], axis=2)
    ct = jnp.stack([dup(c_re), dup(c_im)], axis=2)
    return arow, bt.transpose(1, 0, 2, 3, 4), ct.transpose(1, 0, 2, 3, 4)


def kernel(x, c, ctx, c_ctx, norm_g, w_ada, b_ada, w_in, conv_dw, conv_db, conv_ln_g, conv_ln_b,
           ssm_a_re, ssm_a_im, ssm_log_dt, ssm_b_re, ssm_b_im, ssm_c_re, ssm_c_im, ssm_d,
           ssm_glu_w, ssm_glu_b, w_out, final_g):
    bsz, length, d = x.shape
    ctx_len = ctx.shape[1]
    cw = conv_dw.shape[-1]
    sw = ssm_d.shape[-1]
    assert bsz == BATCH and norm_g.shape[0] == 1
    assert length % (GRID_W * CHUNK) == 0 and ctx_len % CHUNK == 0
    n = length * BATCH
    nch = length // CHUNK
    nchc = ctx_len // CHUNK
    rows = length // GRID_W

    cond16 = jnp.zeros((16, d), F32).at[:BATCH].set(c).at[BATCH].set(c_ctx)
    mod = _ada(cond16, w_ada[0], b_ada[0])
    shift, scale, gate = mod[:, :d], mod[:, d:2 * d], mod[:, 2 * d:]
    amp = norm_g[0][None] * (1.0 + scale)
    a8, s8, gate8 = amp[:BATCH], shift[:BATCH], gate[:BATCH]
    a8c = jnp.broadcast_to(amp[BATCH], (BATCH, d))
    s8c = jnp.broadcast_to(shift[BATCH], (BATCH, d))

    w = w_in[0]
    w012 = w[:, :3 * cw].astype(BF16)
    w34t = w[:, 3 * cw:].T.astype(BF16)
    mt, bst, cot, lam_raw = _s5prep(*_s5prep_inputs(
        ssm_a_re[0], ssm_a_im[0], ssm_log_dt[0], ssm_b_re[0], ssm_b_im[0], ssm_c_re[0], ssm_c_im[0]))
    ng = sw // SSM_GROUP
    lam16 = lam_raw[:, :, :, 0, :SSM_STATE].reshape(ng // 2, 2, 2, 2, SSM_STATE).transpose(
        2, 0, 3, 1, 4).reshape(2, ng // 2, 2, LANES)
    dcol = jnp.broadcast_to(ssm_d[0].reshape(sw // SSM_GROUP, 1, SSM_GROUP),
                            (sw // SSM_GROUP, CHUNK, SSM_GROUP)).reshape(
                                sw // SSM_GROUP, CHUNK * SSM_GROUP, 1)

    xt = x.transpose(1, 0, 2)
    ctxt = ctx.transpose(1, 0, 2)
    x4 = xt.reshape(nch, CHUNK // 2, 2 * BATCH, d)
    c4 = ctxt.reshape(nchc, CHUNK // 2, 2 * BATCH, d)

    v4, g14, ut, g2t = _inproj(x4, a8, s8, w012, w34t, cb=min(nch, 32))
    uct = _ctxproj(c4, a8c, s8c, w34t[:sw])
    y1t = _s5(ut, uct, mt, bst, cot, lam16, dcol)
    mix_s = _ssmout(y1t, g2t, ssm_glu_w[0].T.astype(BF16), ssm_glu_b[0].reshape(sw, 1),
                    w_out[0, cw:].astype(BF16), nlb=min(nch * BATCH, 512))

    v3 = v4.reshape(rows, GRID_W * BATCH, cw)
    half = cw // 2
    conv_h = _conv_h(v3, conv_dw[0][:, :half], conv_db[0][:half], rb=min(rows, 8))
    conv_v = _conv_v(v3, conv_dw[0][:, half:], conv_db[0][half:], wb=128, chan_off=half)

    return _final(conv_h.reshape(n, half), conv_v.reshape(n, half), g14.reshape(n, cw),
                  mix_s.reshape(n, d), x, conv_ln_g[0], conv_ln_b[0],
                  w_out[0, :cw].astype(BF16), gate8, final_g, tl=64)
```

```python
import functools

import jax
import jax.numpy as jnp
from jax import lax
from jax.experimental import pallas as pl
from jax.experimental.pallas import tpu as pltpu

GRID_W = 64
CONV_TAPS = 31
CONV_PAD = CONV_TAPS // 2
SSM_GROUP = 16
SSM_STATE = 64
CHUNK = 16
BATCH = 8
EPS = 1e-6
LANES = 128
VMEM_LIMIT = 56 * 1024 * 1024

F32 = jnp.float32
BF16 = jnp.bfloat16


def _cparams(sem):
    return pltpu.CompilerParams(dimension_semantics=sem, vmem_limit_bytes=VMEM_LIMIT)


def _ada_kernel(c_ref, w_ref, b_ref, o_ref):
    s = jax.nn.silu(c_ref[...])
    o_ref[...] = jnp.dot(s, w_ref[...], preferred_element_type=F32,
                         precision=lax.Precision.HIGHEST) + b_ref[...]


def _ada(cond16, w_ada, b_ada):
    d, n3 = w_ada.shape
    nb = n3 // d
    return pl.pallas_call(
        _ada_kernel,
        grid=(nb,),
        in_specs=[pl.BlockSpec((16, d), lambda j: (0, 0)),
                  pl.BlockSpec((d, d), lambda j: (0, j)),
                  pl.BlockSpec((1, d), lambda j: (0, j))],
        out_specs=pl.BlockSpec((16, d), lambda j: (0, j)),
        out_shape=jax.ShapeDtypeStruct((16, n3), F32),
        compiler_params=_cparams(("arbitrary",)),
        name="ada",
    )(cond16, w_ada, b_ada.reshape(1, n3))


def _modulated_norm(x, a, s):
    cb, _, d = x.shape
    ms = jnp.mean(x * x, axis=-1, keepdims=True)
    xn = (x * lax.rsqrt(ms + EPS)).reshape(cb * 2, BATCH, d)
    return (xn * a[None] + s[None]).reshape(cb, 2 * BATCH, d)


def _inproj_kernel(x_ref, a_ref, s_ref, w012_ref, w34t_ref, v_ref, g1_ref, ut_ref, g2t_ref):
    cb, _, d = x_ref.shape
    cw = v_ref.shape[-1]
    sw = ut_ref.shape[1]
    h = _modulated_norm(x_ref[...], a_ref[...], s_ref[...])
    hb = h.reshape(cb * 2 * BATCH, d).astype(BF16)
    p = jnp.dot(hb, w012_ref[...], preferred_element_type=F32)
    v = p[:, :cw] * jax.nn.sigmoid(p[:, cw:2 * cw])
    v_ref[...] = v.astype(BF16).reshape(cb, 2 * BATCH, cw)
    g1_ref[...] = jax.nn.silu(p[:, 2 * cw:]).astype(BF16).reshape(cb, 2 * BATCH, cw)
    for tl in range(2):
        ht = h[:, tl * BATCH:(tl + 1) * BATCH, :].reshape(cb * BATCH, d).astype(BF16)
        q = lax.dot_general(w34t_ref[...], ht, (((1,), (1,)), ((), ())),
                            preferred_element_type=F32)
        ut_ref[tl] = q[:sw].astype(BF16)
        g2t_ref[tl] = jax.nn.silu(q[sw:]).astype(BF16)


def _inproj(x4, a8, s8, w012, w34t, cb):
    nch, _, _, d = x4.shape
    cw = w012.shape[1] // 3
    sw = w34t.shape[0] // 2
    nl = nch * BATCH
    row_spec = lambda width: pl.BlockSpec((cb, None, 2 * BATCH, width), lambda i, t: (i, t, 0, 0))
    col_spec = pl.BlockSpec((2, sw, cb * BATCH), lambda i, t: (t, 0, i))
    return pl.pallas_call(
        _inproj_kernel,
        grid=(nch // cb, CHUNK // 2),
        in_specs=[row_spec(d),
                  pl.BlockSpec((BATCH, d), lambda i, t: (0, 0)),
                  pl.BlockSpec((BATCH, d), lambda i, t: (0, 0)),
                  pl.BlockSpec(w012.shape, lambda i, t: (0, 0)),
                  pl.BlockSpec(w34t.shape, lambda i, t: (0, 0))],
        out_specs=[row_spec(cw), row_spec(cw), col_spec, col_spec],
        out_shape=[jax.ShapeDtypeStruct((nch, CHUNK // 2, 2 * BATCH, cw), BF16),
                   jax.ShapeDtypeStruct((nch, CHUNK // 2, 2 * BATCH, cw), BF16),
                   jax.ShapeDtypeStruct((CHUNK, sw, nl), BF16),
                   jax.ShapeDtypeStruct((CHUNK, sw, nl), BF16)],
        compiler_params=_cparams(("arbitrary", "arbitrary")),
        name="inproj",
    )(x4, a8, s8, w012, w34t)


def _ctxproj_kernel(x_ref, a_ref, s_ref, wut_ref, ut_ref):
    cb, _, d = x_ref.shape
    h = _modulated_norm(x_ref[...], a_ref[...], s_ref[...])
    for tl in range(2):
        ht = h[:, tl * BATCH:(tl + 1) * BATCH, :].reshape(cb * BATCH, d).astype(BF16)
        q = lax.dot_general(wut_ref[...], ht, (((1,), (1,)), ((), ())),
                            preferred_element_type=F32)
        ut_ref[tl] = q.astype(BF16)


def _ctxproj(x4, a8, s8, wut):
    nch, _, _, d = x4.shape
    sw = wut.shape[0]
    return pl.pallas_call(
        _ctxproj_kernel,
        grid=(CHUNK // 2,),
        in_specs=[pl.BlockSpec((nch, None, 2 * BATCH, d), lambda t: (0, t, 0, 0)),
                  pl.BlockSpec((BATCH, d), lambda t: (0, 0)),
                  pl.BlockSpec((BATCH, d), lambda t: (0, 0)),
                  pl.BlockSpec(wut.shape, lambda t: (0, 0))],
        out_specs=pl.BlockSpec((2, sw, nch * BATCH), lambda t: (t, 0, 0)),
        out_shape=jax.ShapeDtypeStruct((CHUNK, sw, nch * BATCH), BF16),
        compiler_params=_cparams(("arbitrary",)),
        name="ctxproj",
    )(x4, a8, s8, wut)


def _s5_kernel(xt_ref, xct_ref, mt_ref, bst_ref, cot_ref, lam_ref, dcol_ref, y_ref,
               s_sc, h_sc, *, lane_chunk):
    nl = xt_ref.shape[-1]
    nlc = xct_ref.shape[-1]
    n_ch = nl // BATCH
    n_chc = nlc // BATCH
    kdim = CHUNK * SSM_GROUP
    p = SSM_STATE

    def group_x(ref, j):
        return ref[:, j * SSM_GROUP:(j + 1) * SSM_GROUP, :].reshape(kdim, ref.shape[-1])

    for d in range(2):
        for src, off, width in ((xct_ref, 0, nlc), (xt_ref, nlc, nl)):
            st = [jnp.dot(bst_ref[d, j], group_x(src, j), preferred_element_type=F32)
                  for j in range(2)]
            for ri in range(2):
                pair = jnp.concatenate([st[0][ri * p:(ri + 1) * p], st[1][ri * p:(ri + 1) * p]],
                                       axis=0)
                s_sc[d, ri, off:off + width, :] = pair.T

    lam = [[jnp.broadcast_to(lam_ref[d, ri:ri + 1, :], (BATCH, LANES)) for ri in range(2)]
           for d in range(2)]

    def advance(d, state, row):
        hr, hi = state
        ar, ai = lam[d]
        sr = s_sc[d, 0, pl.ds(row, BATCH), :]
        si = s_sc[d, 1, pl.ds(row, BATCH), :]
        return ar * hr - ai * hi + sr, ar * hi + ai * hr + si

    zero = jnp.zeros((BATCH, LANES), F32)

    def ctx_step(i, carry):
        f, b = carry
        f = advance(0, f, pl.multiple_of(i * BATCH, BATCH))
        b = advance(1, b, pl.multiple_of((n_chc - 1 - i) * BATCH, BATCH))
        return f, b

    carry = lax.fori_loop(0, n_chc, ctx_step, ((zero, zero), (zero, zero)))

    def lat_step(i, carry):
        f, b = carry
        rf = pl.multiple_of(i * BATCH, BATCH)
        rb = pl.multiple_of((n_ch - 1 - i) * BATCH, BATCH)
        h_sc[0, 0, pl.ds(rf, BATCH), :] = f[0]
        h_sc[0, 1, pl.ds(rf, BATCH), :] = f[1]
        h_sc[1, 0, pl.ds(rb, BATCH), :] = b[0]
        h_sc[1, 1, pl.ds(rb, BATCH), :] = b[1]
        f = advance(0, f, nlc + rf)
        b = advance(1, b, nlc + rb)
        return f, b

    lax.fori_loop(0, n_ch, lat_step, carry, unroll=8)

    for c0 in range(0, nl, lane_chunk):
        cs = slice(c0, c0 + lane_chunk)
        ht = [[h_sc[d, ri, cs, :].T for ri in range(2)] for d in range(2)]
        for j in range(2):
            x = xt_ref[:, j * SSM_GROUP:(j + 1) * SSM_GROUP, cs].reshape(kdim, lane_chunk)
            acc = dcol_ref[j] * x.astype(F32)
            for d in range(2):
                hj = jnp.concatenate([ht[d][0][j * p:(j + 1) * p], ht[d][1][j * p:(j + 1) * p]],
                                     axis=0).astype(BF16)
                acc = acc + jnp.dot(mt_ref[d, j], x, preferred_element_type=F32)
                acc = acc + jnp.dot(cot_ref[d, j], hj, preferred_element_type=F32)
            y = jax.nn.gelu(acc).astype(BF16)
            y_ref[:, j * SSM_GROUP:(j + 1) * SSM_GROUP, cs] = y.reshape(CHUNK, SSM_GROUP, lane_chunk)


def _s5(ut, uct, mt, bst, cot, lam16, dcol):
    _, sw, nl = ut.shape
    nlc = uct.shape[-1]
    gp = sw // (2 * SSM_GROUP)
    kdim = CHUNK * SSM_GROUP
    lane_chunk = min(nl, 512)
    kern = functools.partial(_s5_kernel, lane_chunk=lane_chunk)
    return pl.pallas_call(
        kern,
        grid=(gp,),
        in_specs=[pl.BlockSpec((CHUNK, 2 * SSM_GROUP, nl), lambda g: (0, g, 0)),
                  pl.BlockSpec((CHUNK, 2 * SSM_GROUP, nlc), lambda g: (0, g, 0)),
                  pl.BlockSpec((2, 2, kdim, kdim), lambda g: (0, g, 0, 0)),
                  pl.BlockSpec((2, 2, 2 * SSM_STATE, kdim), lambda g: (0, g, 0, 0)),
                  pl.BlockSpec((2, 2, kdim, 2 * SSM_STATE), lambda g: (0, g, 0, 0)),
                  pl.BlockSpec((2, None, 2, LANES), lambda g: (0, g, 0, 0)),
                  pl.BlockSpec((2, kdim, 1), lambda g: (g, 0, 0))],
        out_specs=pl.BlockSpec((CHUNK, 2 * SSM_GROUP, nl), lambda g: (0, g, 0)),
        out_shape=jax.ShapeDtypeStruct((CHUNK, sw, nl), BF16),
        scratch_shapes=[pltpu.VMEM((2, 2, nlc + nl, LANES), F32),
                        pltpu.VMEM((2, 2, nl, LANES), F32)],
        compiler_params=_cparams(("arbitrary",)),
        name="s5",
    )(ut, uct, mt, bst, cot, lam16, dcol)


CONV_SUB = 64


def _ssm_tail(y_ref, g2_ref, wt_ref, b_ref, wo_ref, o_ref):
    y = y_ref[...]
    z = jnp.dot(wt_ref[...], y, preferred_element_type=F32) + b_ref[...]
    s = (y.astype(F32) * jax.nn.sigmoid(z) * g2_ref[...].astype(F32)).astype(BF16)
    mix = lax.dot_general(s, wo_ref[...], (((0,), (0,)), ((), ())), preferred_element_type=F32)
    o_ref[...] = mix.reshape(o_ref.shape)


def _conv_h_tile(v_ref, w_ref, b_ref, o_ref, pad_sc):
    rb, wrows, _ = v_ref.shape
    halo = CONV_PAD * BATCH
    pad_sc[:, :halo, :] = jnp.zeros((rb, halo, LANES), F32)
    pad_sc[:, halo + wrows:, :] = jnp.zeros((rb, halo, LANES), F32)
    pad_sc[:, halo:halo + wrows, :] = v_ref[...].astype(F32)
    bias = jnp.broadcast_to(b_ref[...], (CONV_SUB, LANES))
    for r in range(rb):
        for j0 in range(0, wrows, CONV_SUB):
            acc = bias
            for k in range(CONV_TAPS):
                acc = acc + w_ref[k:k + 1, :] * pad_sc[r, j0 + k * BATCH:j0 + k * BATCH + CONV_SUB, :]
            o_ref[r, j0:j0 + CONV_SUB, :] = acc.astype(o_ref.dtype)


def _conv_v_tile(v_ref, w_ref, b_ref, o_ref, pad_sc):
    r, wb, _ = v_ref.shape
    pad_sc[:CONV_PAD] = jnp.zeros((CONV_PAD, wb, LANES), F32)
    pad_sc[CONV_PAD + r:] = jnp.zeros((CONV_PAD, wb, LANES), F32)
    pad_sc[CONV_PAD:CONV_PAD + r] = v_ref[...].astype(F32)
    bias = jnp.broadcast_to(b_ref[...], (CONV_SUB, LANES))
    for row in range(r):
        for j0 in range(0, wb, CONV_SUB):
            acc = bias
            for k in range(CONV_TAPS):
                acc = acc + w_ref[k:k + 1, :] * pad_sc[row + k, j0:j0 + CONV_SUB, :]
            o_ref[row, j0:j0 + CONV_SUB, :] = acc.astype(o_ref.dtype)


def _ssmconv_kernel(y_ref, g2_ref, wt_ref, b_ref, wo_ref, v_ref, cw_ref, cb_ref, o_ref, co_ref,
                    pad_sc, *, vertical):
    _ssm_tail(y_ref, g2_ref, wt_ref, b_ref, wo_ref, o_ref)
    (_conv_v_tile if vertical else _conv_h_tile)(v_ref, cw_ref, cb_ref, co_ref, pad_sc)


def _ssmconv(y1t, g2t, glu_wt, glu_bcol, wo_s, v3, w, b, t_off, vertical):
    _, sw, nl = y1t.shape
    d = wo_s.shape[1]
    rows, wrows, cw = v3.shape
    half = w.shape[1]
    n_t = CHUNK // 2
    n_c = half // LANES
    nlb = nl // n_c
    col_spec = pl.BlockSpec((None, sw, nlb), lambda t, c: (t + t_off, 0, c))
    if vertical:
        wb = wrows // n_t
        assert wb % CONV_SUB == 0
        coff = (cw - half) // LANES
        v_spec = pl.BlockSpec((rows, wb, LANES), lambda t, c: (0, t, c + coff))
        co_spec = pl.BlockSpec((rows, wb, LANES), lambda t, c: (0, t, c))
        pad_shape = (rows + 2 * CONV_PAD, wb, LANES)
    else:
        rb = rows // n_t
        assert wrows % CONV_SUB == 0
        v_spec = pl.BlockSpec((rb, wrows, LANES), lambda t, c: (t, 0, c))
        co_spec = v_spec
        pad_shape = (rb, wrows + 2 * CONV_PAD * BATCH, LANES)
    return pl.pallas_call(
        functools.partial(_ssmconv_kernel, vertical=vertical),
        grid=(n_t, n_c),
        in_specs=[col_spec, col_spec,
                  pl.BlockSpec(glu_wt.shape, lambda t, c: (0, 0)),
                  pl.BlockSpec((sw, 1), lambda t, c: (0, 0)),
                  pl.BlockSpec(wo_s.shape, lambda t, c: (0, 0)),
                  v_spec,
                  pl.BlockSpec((CONV_TAPS, LANES), lambda t, c: (0, c)),
                  pl.BlockSpec((1, LANES), lambda t, c: (0, c))],
        out_specs=[pl.BlockSpec((nlb // BATCH, None, BATCH, d), lambda t, c: (c, t, 0, 0)),
                   co_spec],
        out_shape=[jax.ShapeDtypeStruct((nl // BATCH, n_t, BATCH, d), F32),
                   jax.ShapeDtypeStruct((rows, wrows, half), BF16)],
        scratch_shapes=[pltpu.VMEM(pad_shape, F32)],
        compiler_params=_cparams(("arbitrary", "arbitrary")),
        name="ssmconv_v" if vertical else "ssmconv_h",
    )(y1t, g2t, glu_wt, glu_bcol, wo_s, v3, w, b.reshape(1, half))


def _final_kernel(ch_ref, cv_ref, g1_ref, msa_ref, msb_ref, x_ref, lng_ref, lnb_ref, wo_ref,
                  gate_ref, fg_ref, o_ref, mix_sc):
    tl, d = x_ref.shape[1:]
    tm = tl * BATCH
    cv = jnp.concatenate([ch_ref[...], cv_ref[...]], axis=-1).astype(F32)
    mu = jnp.mean(cv, axis=-1, keepdims=True)
    cen = cv - mu
    var = jnp.mean(cen * cen, axis=-1, keepdims=True)
    y = cen * lax.rsqrt(var + EPS) * lng_ref[...] + lnb_ref[...]
    co = (jax.nn.silu(y) * g1_ref[...].astype(F32)).astype(BF16)
    mix_s = jnp.concatenate([msa_ref[...], msb_ref[...]], axis=1).reshape(tm, d)
    mix = jnp.dot(co, wo_ref[...], preferred_element_type=F32) + mix_s
    nlt = mix_sc.shape[0]
    for j in range(nlt):
        mix_sc[j] = mix[:, j * LANES:(j + 1) * LANES]
    for b in range(BATCH):
        mb = jnp.concatenate([mix_sc[j, pl.ds(b, tl, stride=BATCH), :] for j in range(nlt)],
                             axis=-1)
        xo = x_ref[b] + gate_ref[b:b + 1, :] * mb
        ms = jnp.mean(xo * xo, axis=-1, keepdims=True)
        o_ref[b] = xo * lax.rsqrt(ms + EPS) * fg_ref[...]


def _final(conv_h, conv_v, g1, mix_sa, mix_sb, x, ln_g, ln_b, wo_c, gate8, final_g, tl):
    _, length, d = x.shape
    half = conv_h.shape[1]
    cw = 2 * half
    tm = tl * BATCH
    row = lambda width: pl.BlockSpec((tm, width), lambda i: (i, 0))
    nat = pl.BlockSpec((BATCH, tl, d), lambda i: (0, i, 0))
    slot = pl.BlockSpec((tl // CHUNK, CHUNK // 2, BATCH, d), lambda i: (i, 0, 0, 0))
    full = lambda a: pl.BlockSpec(a.shape, lambda i: (0,) * a.ndim)
    ln_g = ln_g.reshape(1, cw)
    ln_b = ln_b.reshape(1, cw)
    final_g = final_g.reshape(1, d)
    return pl.pallas_call(
        _final_kernel,
        grid=(length // tl,),
        in_specs=[row(half), row(half), row(cw), slot, slot, nat,
                  full(ln_g), full(ln_b), full(wo_c), full(gate8), full(final_g)],
        out_specs=nat,
        out_shape=jax.ShapeDtypeStruct(x.shape, F32),
        scratch_shapes=[pltpu.VMEM((d // LANES, tm, LANES), F32)],
        compiler_params=_cparams(("arbitrary",)),
        name="final",
    )(conv_h, conv_v, g1, mix_sa, mix_sb, x, ln_g, ln_b, wo_c, gate8, final_g)


def _s5prep_kernel(arow_ref, bt_ref, ct_ref, mt_ref, bst_ref, cot_ref, lam_ref, pw_sc, kk_sc):
    kdim = CHUNK * SSM_GROUP
    hp = lax.Precision.HIGHEST
    lane = lax.broadcasted_iota(jnp.int32, (1, LANES), 1)
    is_re = lane < SSM_STATE
    col_t = lax.broadcasted_iota(jnp.int32, (1, kdim), 1) // SSM_GROUP
    kcol = lax.broadcasted_iota(jnp.int32, (pw_sc.shape[1], 1), 0).astype(F32)

    def tile_rows(a):
        return jnp.concatenate([a] * CHUNK, axis=0)

    def power_rows(exps):
        pr = jnp.concatenate([jnp.broadcast_to(pw_sc[0, e:e + 1, :], (SSM_GROUP, LANES))
                              for e in exps], axis=0)
        pi = jnp.concatenate([jnp.broadcast_to(pw_sc[1, e:e + 1, :], (SSM_GROUP, LANES))
                              for e in exps], axis=0)
        return pr, pi

    def cmul(xr, xi, yr, yi, im_sign):
        return jnp.where(is_re, xr * yr - xi * yi, im_sign * (xr * yi + xi * yr))

    for d in range(2):
        ar, ai = arow_ref[d, 0:1, :], arow_ref[d, 1:2, :]
        dt = jnp.exp(arow_ref[d, 2:3, :])
        mag = jnp.exp(kcol * (ar * dt))
        ang = kcol * (ai * dt)
        pw_sc[0] = mag * jnp.cos(ang)
        pw_sc[1] = mag * jnp.sin(ang)
        lam_ref[d, 0] = pw_sc[0, CHUNK:CHUNK + 8, :]
        lam_ref[d, 1] = pw_sc[1, CHUNK:CHUNK + 8, :]
        nr, ni = pw_sc[0, 1:2, :] - 1.0, pw_sc[1, 1:2, :]
        den = ar * ar + ai * ai
        qr, qi = (nr * ar + ni * ai) / den, (ni * ar - nr * ai) / den
        btr, bti = bt_ref[d, 0], bt_ref[d, 1]
        bbr, bbi = tile_rows(qr * btr - qi * bti), tile_rows(qr * bti + qi * btr)
        cr, ci = tile_rows(ct_ref[d, 0]), tile_rows(ct_ref[d, 1])
        tt = list(range(CHUNK))
        e_lag = [CHUNK - 1 - t for t in tt] if d else tt
        e_in = tt if d else [CHUNK - 1 - t for t in tt]
        e_out = [CHUNK - t for t in tt] if d else [t + 1 for t in tt]
        cot_ref[d] = cmul(cr, ci, *power_rows(e_out), -1.0).astype(BF16)
        bst_ref[d] = cmul(bbr, bbi, *power_rows(e_in), 1.0).T.astype(BF16)
        c_lag = cmul(cr, ci, *power_rows(e_lag), -1.0)
        bb = jnp.where(is_re, bbr, bbi)
        kk = lax.dot_general(c_lag, bb, (((1,), (1,)), ((), ())), precision=hp,
                             preferred_element_type=F32)
        zeros = jnp.zeros((kdim, kdim), F32)
        if d:
            kk_sc[:kdim] = kk
            kk_sc[kdim:] = zeros
            starts = [SSM_GROUP * (CHUNK - 1 - t) for t in tt]
        else:
            kk_sc[:kdim] = zeros
            kk_sc[kdim:] = kk
            starts = [kdim - SSM_GROUP * t for t in tt]
        acc = zeros
        for t in tt:
            acc = jnp.where(col_t == t, kk_sc[starts[t]:starts[t] + kdim, :], acc)
        mt_ref[d] = acc.astype(BF16)


def _s5prep(arow, bt, ct):
    g = arow.shape[0]
    kdim = CHUNK * SSM_GROUP
    return pl.pallas_call(
        _s5prep_kernel,
        grid=(g,),
        in_specs=[pl.BlockSpec((None, 2, 8, LANES), lambda i: (i, 0, 0, 0)),
                  pl.BlockSpec((None, 2, 2, SSM_GROUP, LANES), lambda i: (i, 0, 0, 0, 0)),
                  pl.BlockSpec((None, 2, 2, SSM_GROUP, LANES), lambda i: (i, 0, 0, 0, 0))],
        out_specs=[pl.BlockSpec((2, None, kdim, kdim), lambda i: (0, i, 0, 0)),
                   pl.BlockSpec((2, None, 2 * SSM_STATE, kdim), lambda i: (0, i, 0, 0)),
                   pl.BlockSpec((2, None, kdim, 2 * SSM_STATE), lambda i: (0, i, 0, 0)),
                   pl.BlockSpec((None, 2, 2, 8, LANES), lambda i: (i, 0, 0, 0, 0))],
        out_shape=[jax.ShapeDtypeStruct((2, g, kdim, kdim), BF16),
                   jax.ShapeDtypeStruct((2, g, 2 * SSM_STATE, kdim), BF16),
                   jax.ShapeDtypeStruct((2, g, kdim, 2 * SSM_STATE), BF16),
                   jax.ShapeDtypeStruct((g, 2, 2, 8, LANES), F32)],
        scratch_shapes=[pltpu.VMEM((2, CHUNK + 8, LANES), F32),
                        pltpu.VMEM((2 * kdim, kdim), F32)],
        compiler_params=_cparams(("arbitrary",)),
        name="s5prep",
    )(arow, bt, ct)


def _s5prep_inputs(a_re, a_im, log_dt, b_re, b_im, c_re, c_im):
    g = a_re.shape[1]
    dup = lambda a: jnp.concatenate([a, a], axis=-1)
    rows = jnp.stack([dup(a_re), dup(a_im),
                      jnp.broadcast_to(log_dt[..., None], (2, g, LANES))], axis=2)
    arow = jnp.pad(rows, ((0, 0), (0, 0), (0, 5), (0, 0))).transpose(1, 0, 2, 3)
    bt = jnp.stack([dup(b_re.transpose(0, 1, 3, 2)), dup(b_im.transpose(0, 1, 3, 2))], axis=2)
    ct = jnp.stack([dup(c_re), dup(c_im)], axis=2)
    return arow, bt.transpose(1, 0, 2, 3, 4), ct.transpose(1, 0, 2, 3, 4)


def kernel(x, c, ctx, c_ctx, norm_g, w_ada, b_ada, w_in, conv_dw, conv_db, conv_ln_g, conv_ln_b,
           ssm_a_re, ssm_a_im, ssm_log_dt, ssm_b_re, ssm_b_im, ssm_c_re, ssm_c_im, ssm_d,
           ssm_glu_w, ssm_glu_b, w_out, final_g):
    bsz, length, d = x.shape
    ctx_len = ctx.shape[1]
    cw = conv_dw.shape[-1]
    sw = ssm_d.shape[-1]
    assert bsz == BATCH and norm_g.shape[0] == 1
    assert length % (GRID_W * CHUNK) == 0 and ctx_len % CHUNK == 0
    n = length * BATCH
    nch = length // CHUNK
    nchc = ctx_len // CHUNK
    rows = length // GRID_W

    cond16 = jnp.zeros((16, d), F32).at[:BATCH].set(c).at[BATCH].set(c_ctx)
    mod = _ada(cond16, w_ada[0], b_ada[0])
    shift, scale, gate = mod[:, :d], mod[:, d:2 * d], mod[:, 2 * d:]
    amp = norm_g[0][None] * (1.0 + scale)
    a8, s8, gate8 = amp[:BATCH], shift[:BATCH], gate[:BATCH]
    a8c = jnp.broadcast_to(amp[BATCH], (BATCH, d))
    s8c = jnp.broadcast_to(shift[BATCH], (BATCH, d))

    w = w_in[0]
    w012 = w[:, :3 * cw].astype(BF16)
    w34t = w[:, 3 * cw:].T.astype(BF16)
    mt, bst, cot, lam_raw = _s5prep(*_s5prep_inputs(
        ssm_a_re[0], ssm_a_im[0], ssm_log_dt[0], ssm_b_re[0], ssm_b_im[0], ssm_c_re[0], ssm_c_im[0]))
    ng = sw // SSM_GROUP
    lam16 = lam_raw[:, :, :, 0, :SSM_STATE].reshape(ng // 2, 2, 2, 2, SSM_STATE).transpose(
        2, 0, 3, 1, 4).reshape(2, ng // 2, 2, LANES)
    dcol = jnp.broadcast_to(ssm_d[0].reshape(sw // SSM_GROUP, 1, SSM_GROUP),
                            (sw // SSM_GROUP, CHUNK, SSM_GROUP)).reshape(
                                sw // SSM_GROUP, CHUNK * SSM_GROUP, 1)

    xt = x.transpose(1, 0, 2)
    ctxt = ctx.transpose(1, 0, 2)
    x4 = xt.reshape(nch, CHUNK // 2, 2 * BATCH, d)
    c4 = ctxt.reshape(nchc, CHUNK // 2, 2 * BATCH, d)

    v4, g14, ut, g2t = _inproj(x4, a8, s8, w012, w34t, cb=min(nch, 32))
    uct = _ctxproj(c4, a8c, s8c, w34t[:sw])
    y1t = _s5(ut, uct, mt, bst, cot, lam16, dcol)
    v3 = v4.reshape(rows, GRID_W * BATCH, cw)
    half = cw // 2
    tail = (y1t, g2t, ssm_glu_w[0].T.astype(BF16), ssm_glu_b[0].reshape(sw, 1),
            w_out[0, cw:].astype(BF16), v3)
    mix_sa, conv_h = _ssmconv(*tail, conv_dw[0][:, :half], conv_db[0][:half], 0, False)
    mix_sb, conv_v = _ssmconv(*tail, conv_dw[0][:, half:], conv_db[0][half:], CHUNK // 2, True)

    return _final(conv_h.reshape(n, half), conv_v.reshape(n, half), g14.reshape(n, cw),
                  mix_sa, mix_sb, x, conv_ln_g[0], conv_ln_b[0],
                  w_out[0, :cw].astype(BF16), gate8, final_g, tl=64)
```

```python
import functools

import jax
import jax.numpy as jnp
from jax import lax
from jax.experimental import pallas as pl
from jax.experimental.pallas import tpu as pltpu

GRID_W = 64
CONV_TAPS = 31
CONV_PAD = CONV_TAPS // 2
SSM_GROUP = 16
SSM_STATE = 64
CHUNK = 16
BATCH = 8
EPS = 1e-6
LANES = 128
VMEM_LIMIT = 56 * 1024 * 1024

F32 = jnp.float32
BF16 = jnp.bfloat16


def _cparams(sem):
    return pltpu.CompilerParams(dimension_semantics=sem, vmem_limit_bytes=VMEM_LIMIT)


def _ada_kernel(c_ref, w_ref, b_ref, o_ref):
    s = jax.nn.silu(c_ref[...])
    o_ref[...] = jnp.dot(s, w_ref[...], preferred_element_type=F32,
                         precision=lax.Precision.HIGHEST) + b_ref[...]


def _ada(cond16, w_ada, b_ada):
    d, n3 = w_ada.shape
    nb = n3 // d
    return pl.pallas_call(
        _ada_kernel,
        grid=(nb,),
        in_specs=[pl.BlockSpec((16, d), lambda j: (0, 0)),
                  pl.BlockSpec((d, d), lambda j: (0, j)),
                  pl.BlockSpec((1, d), lambda j: (0, j))],
        out_specs=pl.BlockSpec((16, d), lambda j: (0, j)),
        out_shape=jax.ShapeDtypeStruct((16, n3), F32),
        compiler_params=_cparams(("arbitrary",)),
        name="ada",
    )(cond16, w_ada, b_ada.reshape(1, n3))


def _modulated_norm(x, a, s):
    ms = jnp.mean(x * x, axis=-1, keepdims=True)
    return x * lax.rsqrt(ms + EPS) * a[None] + s[None]


def _conv_gates(q, hc, vt_ref, g1t_ref):
    val = (q[:hc] * jax.nn.sigmoid(q[hc:2 * hc])).astype(BF16)
    for j in range(vt_ref.shape[0]):
        vt_ref[j] = val[:, j * LANES:(j + 1) * LANES]
    g1t_ref[...] = jax.nn.silu(q[2 * hc:]).astype(BF16)


def _inproj_w_kernel(x_ref, a_ref, s_ref, wt_ref, ut_ref, g2t_ref, vt_ref, g1t_ref):
    r, _, d = x_ref.shape
    sw = ut_ref.shape[0]
    hc = g1t_ref.shape[0]
    h = _modulated_norm(x_ref[...], a_ref[...], s_ref[...]).reshape(r * BATCH, d).astype(BF16)
    q = lax.dot_general(wt_ref[...], h, (((1,), (1,)), ((), ())), preferred_element_type=F32)
    ut_ref[...] = q[:sw].astype(BF16)
    g2t_ref[...] = jax.nn.silu(q[sw:2 * sw]).astype(BF16)
    _conv_gates(q[2 * sw:], hc, vt_ref, g1t_ref)


def _inproj_w(x4, a8, s8, wt, sw, hc):
    r, _, _, d = x4.shape
    tl = r * BATCH
    nq = tl // LANES
    cpr = GRID_W // CHUNK
    s5_spec = pl.BlockSpec((None, sw, tl), lambda w: (w % CHUNK, 0, w // CHUNK))
    return pl.pallas_call(
        _inproj_w_kernel,
        grid=(GRID_W,),
        in_specs=[pl.BlockSpec((r, None, BATCH, d), lambda w: (0, w, 0, 0)),
                  pl.BlockSpec((BATCH, d), lambda w: (0, 0)),
                  pl.BlockSpec((BATCH, d), lambda w: (0, 0)),
                  pl.BlockSpec(wt.shape, lambda w: (0, 0))],
        out_specs=[s5_spec, s5_spec,
                   pl.BlockSpec((None, nq, hc, LANES), lambda w: (w, 0, 0, 0)),
                   pl.BlockSpec((hc, tl), lambda w: (0, w))],
        out_shape=[jax.ShapeDtypeStruct((CHUNK, sw, cpr * tl), BF16),
                   jax.ShapeDtypeStruct((CHUNK, sw, cpr * tl), BF16),
                   jax.ShapeDtypeStruct((GRID_W, nq, hc, LANES), BF16),
                   jax.ShapeDtypeStruct((hc, GRID_W * tl), BF16)],
        compiler_params=_cparams(("arbitrary",)),
        name="inproj_w",
    )(x4, a8, s8, wt)


def _inproj_r_kernel(x_ref, a_ref, s_ref, wt_ref, vt_ref, g1t_ref):
    tl, d = x_ref.shape
    hc = g1t_ref.shape[0]
    x = x_ref[...].reshape(tl // BATCH, BATCH, d)
    h = _modulated_norm(x, a_ref[...], s_ref[...]).reshape(tl, d).astype(BF16)
    q = lax.dot_general(wt_ref[...], h, (((1,), (1,)), ((), ())), preferred_element_type=F32)
    _conv_gates(q, hc, vt_ref, g1t_ref)


def _inproj_r(x3, a8, s8, wt, hc):
    r, tl, d = x3.shape
    nq = tl // LANES
    return pl.pallas_call(
        _inproj_r_kernel,
        grid=(r,),
        in_specs=[pl.BlockSpec((None, tl, d), lambda i: (i, 0, 0)),
                  pl.BlockSpec((BATCH, d), lambda i: (0, 0)),
                  pl.BlockSpec((BATCH, d), lambda i: (0, 0)),
                  pl.BlockSpec(wt.shape, lambda i: (0, 0))],
        out_specs=[pl.BlockSpec((None, nq, hc, LANES), lambda i: (i, 0, 0, 0)),
                   pl.BlockSpec((hc, tl), lambda i: (0, i))],
        out_shape=[jax.ShapeDtypeStruct((r, nq, hc, LANES), BF16),
                   jax.ShapeDtypeStruct((hc, r * tl), BF16)],
        compiler_params=_cparams(("arbitrary",)),
        name="inproj_r",
    )(x3, a8, s8, wt)


def _ctxproj_kernel(x_ref, a_ref, s_ref, wut_ref, ut_ref):
    cb, _, d = x_ref.shape
    x = x_ref[...].reshape(cb * 2, BATCH, d)
    h = _modulated_norm(x, a_ref[...], s_ref[...]).reshape(cb, 2 * BATCH, d)
    for tl in range(2):
        ht = h[:, tl * BATCH:(tl + 1) * BATCH, :].reshape(cb * BATCH, d).astype(BF16)
        q = lax.dot_general(wut_ref[...], ht, (((1,), (1,)), ((), ())),
                            preferred_element_type=F32)
        ut_ref[tl] = q.astype(BF16)


def _ctxproj(x4, a8, s8, wut):
    nch, _, _, d = x4.shape
    sw = wut.shape[0]
    return pl.pallas_call(
        _ctxproj_kernel,
        grid=(CHUNK // 2,),
        in_specs=[pl.BlockSpec((nch, None, 2 * BATCH, d), lambda t: (0, t, 0, 0)),
                  pl.BlockSpec((BATCH, d), lambda t: (0, 0)),
                  pl.BlockSpec((BATCH, d), lambda t: (0, 0)),
                  pl.BlockSpec(wut.shape, lambda t: (0, 0))],
        out_specs=pl.BlockSpec((2, sw, nch * BATCH), lambda t: (t, 0, 0)),
        out_shape=jax.ShapeDtypeStruct((CHUNK, sw, nch * BATCH), BF16),
        compiler_params=_cparams(("arbitrary",)),
        name="ctxproj",
    )(x4, a8, s8, wut)


def _s5_kernel(xt_ref, xct_ref, wall_ref, cot_ref, lam_ref, dcol_ref, y_ref,
               s_sc, h_sc, yi_sc, *, lane_chunk):
    nl = xt_ref.shape[-1]
    nlc = xct_ref.shape[-1]
    n_ch = nl // BATCH
    n_chc = nlc // BATCH
    kdim = CHUNK * SSM_GROUP
    p = SSM_STATE

    def group_x(ref, j):
        return ref[:, j * SSM_GROUP:(j + 1) * SSM_GROUP, :].reshape(kdim, ref.shape[-1])

    st_c = [jnp.dot(wall_ref[j, kdim:], group_x(xct_ref, j), preferred_element_type=F32)
            for j in range(2)]
    st = []
    for j in range(2):
        r = jnp.dot(wall_ref[j], group_x(xt_ref, j), preferred_element_type=F32)
        yi_sc[j] = r[:kdim]
        st.append(r[kdim:])
    for d in range(2):
        for src, off, width in ((st_c, 0, nlc), (st, nlc, nl)):
            for ri in range(2):
                lo = (2 * d + ri) * p
                pair = jnp.concatenate([src[0][lo:lo + p], src[1][lo:lo + p]], axis=0)
                s_sc[d, ri, off:off + width, :] = pair.T

    lam = [[jnp.broadcast_to(lam_ref[d, ri:ri + 1, :], (BATCH, LANES)) for ri in range(2)]
           for d in range(2)]

    def advance(d, state, row):
        hr, hi = state
        ar, ai = lam[d]
        sr = s_sc[d, 0, pl.ds(row, BATCH), :]
        si = s_sc[d, 1, pl.ds(row, BATCH), :]
        return ar * hr - ai * hi + sr, ar * hi + ai * hr + si

    zero = jnp.zeros((BATCH, LANES), F32)

    def ctx_step(i, carry):
        f, b = carry
        f = advance(0, f, pl.multiple_of(i * BATCH, BATCH))
        b = advance(1, b, pl.multiple_of((n_chc - 1 - i) * BATCH, BATCH))
        return f, b

    carry = lax.fori_loop(0, n_chc, ctx_step, ((zero, zero), (zero, zero)))

    cpr = GRID_W // CHUNK
    n_rows = n_ch // cpr

    def lane_row(ch):
        return pl.multiple_of(((ch % cpr) * n_rows + ch // cpr) * BATCH, BATCH)

    def lat_step(i, carry):
        f, b = carry
        rf = lane_row(i)
        rb = lane_row(n_ch - 1 - i)
        h_sc[0, 0, pl.ds(rf, BATCH), :] = f[0]
        h_sc[0, 1, pl.ds(rf, BATCH), :] = f[1]
        h_sc[1, 0, pl.ds(rb, BATCH), :] = b[0]
        h_sc[1, 1, pl.ds(rb, BATCH), :] = b[1]
        f = advance(0, f, pl.multiple_of(nlc + rf, BATCH))
        b = advance(1, b, pl.multiple_of(nlc + rb, BATCH))
        return f, b

    lax.fori_loop(0, n_ch, lat_step, carry, unroll=8)

    for c0 in range(0, nl, lane_chunk):
        cs = slice(c0, c0 + lane_chunk)
        ht = [[h_sc[d, ri, cs, :].T for ri in range(2)] for d in range(2)]
        for j in range(2):
            x = xt_ref[:, j * SSM_GROUP:(j + 1) * SSM_GROUP, cs].reshape(kdim, lane_chunk)
            hj = jnp.concatenate([ht[d][ri][j * p:(j + 1) * p] for d in range(2) for ri in range(2)],
                                 axis=0).astype(BF16)
            acc = (yi_sc[j, :, cs] + dcol_ref[j] * x.astype(F32)
                   + jnp.dot(cot_ref[j], hj, preferred_element_type=F32))
            y = jax.nn.gelu(acc).astype(BF16)
            y_ref[:, j * SSM_GROUP:(j + 1) * SSM_GROUP, cs] = y.reshape(CHUNK, SSM_GROUP, lane_chunk)


def _s5(ut, uct, wall, cot, lam16, dcol):
    _, sw, nl = ut.shape
    nlc = uct.shape[-1]
    gp = sw // (2 * SSM_GROUP)
    kdim = CHUNK * SSM_GROUP
    lane_chunk = min(nl, 512)
    kern = functools.partial(_s5_kernel, lane_chunk=lane_chunk)
    return pl.pallas_call(
        kern,
        grid=(gp,),
        in_specs=[pl.BlockSpec((CHUNK, 2 * SSM_GROUP, nl), lambda g: (0, g, 0)),
                  pl.BlockSpec((CHUNK, 2 * SSM_GROUP, nlc), lambda g: (0, g, 0)),
                  pl.BlockSpec((2, 2 * kdim, kdim), lambda g: (g, 0, 0)),
                  pl.BlockSpec((2, kdim, kdim), lambda g: (g, 0, 0)),
                  pl.BlockSpec((2, None, 2, LANES), lambda g: (0, g, 0, 0)),
                  pl.BlockSpec((2, kdim, 1), lambda g: (g, 0, 0))],
        out_specs=pl.BlockSpec((CHUNK, 2 * SSM_GROUP, nl), lambda g: (0, g, 0)),
        out_shape=jax.ShapeDtypeStruct((CHUNK, sw, nl), BF16),
        scratch_shapes=[pltpu.VMEM((2, 2, nlc + nl, LANES), F32),
                        pltpu.VMEM((2, 2, nl, LANES), F32),
                        pltpu.VMEM((2, kdim, nl), F32)],
        compiler_params=_cparams(("arbitrary",)),
        name="s5",
    )(ut, uct, wall, cot, lam16, dcol)


CONV_CB = 64
CONV_GROUP = 4
CONV_SKEW = 8


def _convt_kernel(v_ref, tq_ref, b_ref, o_ref, s1_ref, s2_ref, z_sc):
    a, cb, _ = v_ref.shape
    pitch = cb + CONV_SKEW
    z = v_ref[...].astype(F32)
    for i in range(a):
        z_sc[i * pitch:i * pitch + cb, :] = z[i]
    s1 = jnp.zeros((a, LANES), F32)
    s2 = jnp.zeros((a, LANES), F32)
    for g in range(cb // CONV_GROUP):
        c0 = g * CONV_GROUP
        rhs = jnp.concatenate([z_sc[pl.ds(c0 + j, a, stride=pitch), :] for j in range(CONV_GROUP)],
                              axis=0).astype(BF16)
        out = jnp.dot(tq_ref[g], rhs, preferred_element_type=F32).reshape(CONV_GROUP, a, LANES)
        out = out + b_ref[c0:c0 + CONV_GROUP].reshape(CONV_GROUP, 1, 1)
        o_ref[c0:c0 + CONV_GROUP] = out.astype(o_ref.dtype)
        s1 = s1 + jnp.sum(out, axis=0)
        s2 = s2 + jnp.sum(out * out, axis=0)

    @pl.when(pl.program_id(1) == 0)
    def _():
        s1_ref[...] = jnp.zeros_like(s1_ref)
        s2_ref[...] = jnp.zeros_like(s2_ref)

    s1_ref[...] += s1
    s2_ref[...] += s2


def _convt(vt, tq, bias):
    a, nq, c, _ = vt.shape
    cb = min(c, CONV_CB)
    ka = CONV_GROUP * a
    stat_spec = pl.BlockSpec((a, LANES), lambda q, i: (0, q))
    return pl.pallas_call(
        _convt_kernel,
        grid=(nq, c // cb),
        in_specs=[pl.BlockSpec((a, None, cb, LANES), lambda q, i: (0, q, i, 0)),
                  pl.BlockSpec((cb // CONV_GROUP, ka, ka), lambda q, i: (i, 0, 0)),
                  pl.BlockSpec((cb, 1), lambda q, i: (i, 0))],
        out_specs=[pl.BlockSpec((cb, a, LANES), lambda q, i: (i, 0, q)), stat_spec, stat_spec],
        out_shape=[jax.ShapeDtypeStruct((c, a, nq * LANES), BF16),
                   jax.ShapeDtypeStruct((a, nq * LANES), F32),
                   jax.ShapeDtypeStruct((a, nq * LANES), F32)],
        scratch_shapes=[pltpu.VMEM((a * (cb + CONV_SKEW), LANES), F32)],
        compiler_params=_cparams(("arbitrary", "arbitrary")),
        name="convt",
    )(vt, tq, bias.reshape(c, 1))


def _conv_toeplitz(w, a):
    c = w.shape[1]
    pos = jnp.arange(a)
    k = pos[None, :] - pos[:, None] + CONV_PAD
    t = jnp.where(((k >= 0) & (k < CONV_TAPS))[..., None], w[jnp.clip(k, 0, CONV_TAPS - 1)], 0.0)
    t = t.transpose(2, 0, 1).reshape(c // CONV_GROUP, CONV_GROUP, a, a)
    eye = jnp.eye(CONV_GROUP, dtype=w.dtype)
    blk = t[:, :, :, None, :] * eye[None, :, None, :, None]
    return blk.reshape(c // CONV_GROUP, CONV_GROUP * a, CONV_GROUP * a).astype(BF16)


def _conv_branch_t(cv_ref, g1_ref, st_ref, lng_ref, lnb_ref, n_chan):
    st = st_ref[...]
    mean = (st[0:1] + st[2:3]) * (1.0 / n_chan)
    var = (st[1:2] + st[3:4]) * (1.0 / n_chan) - mean * mean
    y = (cv_ref[...].astype(F32) - mean) * lax.rsqrt(var + EPS) * lng_ref[...] + lnb_ref[...]
    return (jax.nn.silu(y) * g1_ref[...].astype(F32)).astype(BF16)


def _fv_kernel(cv_ref, g1_ref, st_ref, lng_ref, lnb_ref, wo_ref, o_ref, *, n_chan):
    co = _conv_branch_t(cv_ref, g1_ref, st_ref, lng_ref, lnb_ref, n_chan)
    o_ref[...] = lax.dot_general(co, wo_ref[...], (((0,), (0,)), ((), ())),
                                 preferred_element_type=F32)


def _fv(convt, g1t, stats, ln_g, ln_b, wo, n_chan):
    hc, n = convt.shape
    d = wo.shape[1]
    tl = GRID_W * BATCH
    tok = lambda rows: pl.BlockSpec((rows, tl), lambda i: (0, i))
    full = lambda arr: pl.BlockSpec(arr.shape, lambda i: (0,) * arr.ndim)
    return pl.pallas_call(
        functools.partial(_fv_kernel, n_chan=n_chan),
        grid=(n // tl,),
        in_specs=[tok(hc), tok(hc), tok(8), full(ln_g), full(ln_b), full(wo)],
        out_specs=pl.BlockSpec((tl, d), lambda i: (i, 0)),
        out_shape=jax.ShapeDtypeStruct((n, d), F32),
        compiler_params=_cparams(("arbitrary",)),
        name="fv",
    )(convt, g1t, stats, ln_g, ln_b, wo)


def _main_w_kernel(y_ref, g2_ref, cv_ref, g1_ref, st_ref, mv_ref, x_ref, wt_ref, bt_ref, wos_ref,
                   woh_ref, lng_ref, lnb_ref, gate_ref, fg_ref, o_ref, mix_sc, *, n_chan):
    r, d = x_ref.shape[1:]
    tl = r * BATCH
    y = y_ref[...]
    z = jnp.dot(wt_ref[...], y, preferred_element_type=F32) + bt_ref[...]
    s = (y.astype(F32) * jax.nn.sigmoid(z) * g2_ref[...].astype(F32)).astype(BF16)
    co = _conv_branch_t(cv_ref, g1_ref, st_ref, lng_ref, lnb_ref, n_chan)
    tn = (((0,), (0,)), ((), ()))
    mix = (lax.dot_general(s, wos_ref[...], tn, preferred_element_type=F32)
           + lax.dot_general(co, woh_ref[...], tn, preferred_element_type=F32)
           + mv_ref[...].reshape(tl, d))
    nlt = mix_sc.shape[0]
    for j in range(nlt):
        mix_sc[j] = mix[:, j * LANES:(j + 1) * LANES]
    for b in range(BATCH):
        mb = jnp.concatenate([mix_sc[j, pl.ds(b, r, stride=BATCH), :] for j in range(nlt)],
                             axis=-1)
        xo = x_ref[b] + gate_ref[b:b + 1, :] * mb
        ms = jnp.mean(xo * xo, axis=-1, keepdims=True)
        o_ref[b] = xo * lax.rsqrt(ms + EPS) * fg_ref[...]


def _main_w(y1t, g2t, convt_h, g1t_h, stats_h, mix_v4, x3, glu_wt, glu_bcol, wo_s, wo_h, ln_g, ln_b,
            gate8, final_g, n_chan):
    _, sw, _ = y1t.shape
    hc = convt_h.shape[0]
    r = x3.shape[1]
    d = wo_s.shape[1]
    tl = r * BATCH
    s5_spec = pl.BlockSpec((None, sw, tl), lambda w: (w % CHUNK, 0, w // CHUNK))
    tok = lambda rows: pl.BlockSpec((rows, tl), lambda w: (0, w))
    nat = pl.BlockSpec((BATCH, r, d), lambda w: (0, 0, w))
    full = lambda arr: pl.BlockSpec(arr.shape, lambda w: (0,) * arr.ndim)
    return pl.pallas_call(
        functools.partial(_main_w_kernel, n_chan=n_chan),
        grid=(GRID_W,),
        in_specs=[s5_spec, s5_spec, tok(hc), tok(hc), tok(8),
                  pl.BlockSpec((r, None, BATCH, d), lambda w: (0, w, 0, 0)), nat,
                  full(glu_wt), full(glu_bcol), full(wo_s), full(wo_h), full(ln_g), full(ln_b),
                  full(gate8), full(final_g)],
        out_specs=nat,
        out_shape=jax.ShapeDtypeStruct(x3.shape, F32),
        scratch_shapes=[pltpu.VMEM((d // LANES, tl, LANES), F32)],
        compiler_params=_cparams(("arbitrary",)),
        name="main_w",
    )(y1t, g2t, convt_h, g1t_h, stats_h, mix_v4, x3, glu_wt, glu_bcol, wo_s, wo_h, ln_g, ln_b,
      gate8, final_g)


def _s5prep_kernel(arow_ref, bt_ref, ct_ref, wall_ref, cot_ref, lam_ref, pw_sc, kk_sc):
    kdim = CHUNK * SSM_GROUP
    hp = lax.Precision.HIGHEST
    lane = lax.broadcasted_iota(jnp.int32, (1, LANES), 1)
    is_re = lane < SSM_STATE
    col_t = lax.broadcasted_iota(jnp.int32, (1, kdim), 1) // SSM_GROUP
    kcol = lax.broadcasted_iota(jnp.int32, (pw_sc.shape[1], 1), 0).astype(F32)

    def tile_rows(a):
        return jnp.concatenate([a] * CHUNK, axis=0)

    def power_rows(exps):
        pr = jnp.concatenate([jnp.broadcast_to(pw_sc[0, e:e + 1, :], (SSM_GROUP, LANES))
                              for e in exps], axis=0)
        pi = jnp.concatenate([jnp.broadcast_to(pw_sc[1, e:e + 1, :], (SSM_GROUP, LANES))
                              for e in exps], axis=0)
        return pr, pi

    def cmul(xr, xi, yr, yi, im_sign):
        return jnp.where(is_re, xr * yr - xi * yi, im_sign * (xr * yi + xi * yr))

    toeplitz = jnp.zeros((kdim, kdim), F32)
    for d in range(2):
        ar, ai = arow_ref[d, 0:1, :], arow_ref[d, 1:2, :]
        dt = jnp.exp(arow_ref[d, 2:3, :])
        mag = jnp.exp(kcol * (ar * dt))
        ang = kcol * (ai * dt)
        pw_sc[0] = mag * jnp.cos(ang)
        pw_sc[1] = mag * jnp.sin(ang)
        lam_ref[d, 0] = pw_sc[0, CHUNK:CHUNK + 8, :]
        lam_ref[d, 1] = pw_sc[1, CHUNK:CHUNK + 8, :]
        nr, ni = pw_sc[0, 1:2, :] - 1.0, pw_sc[1, 1:2, :]
        den = ar * ar + ai * ai
        qr, qi = (nr * ar + ni * ai) / den, (ni * ar - nr * ai) / den
        btr, bti = bt_ref[d, 0], bt_ref[d, 1]
        bbr, bbi = tile_rows(qr * btr - qi * bti), tile_rows(qr * bti + qi * btr)
        cr, ci = tile_rows(ct_ref[d, 0]), tile_rows(ct_ref[d, 1])
        tt = list(range(CHUNK))
        e_lag = [CHUNK - 1 - t for t in tt] if d else tt
        e_in = tt if d else [CHUNK - 1 - t for t in tt]
        e_out = [CHUNK - t for t in tt] if d else [t + 1 for t in tt]
        cot_ref[:, d * LANES:(d + 1) * LANES] = cmul(cr, ci, *power_rows(e_out), -1.0).astype(BF16)
        wall_ref[kdim + d * LANES:kdim + (d + 1) * LANES, :] = cmul(
            bbr, bbi, *power_rows(e_in), 1.0).T.astype(BF16)
        c_lag = cmul(cr, ci, *power_rows(e_lag), -1.0)
        bb = jnp.where(is_re, bbr, bbi)
        kk = lax.dot_general(c_lag, bb, (((1,), (1,)), ((), ())), precision=hp,
                             preferred_element_type=F32)
        zeros = jnp.zeros((kdim, kdim), F32)
        if d:
            kk_sc[:kdim] = kk
            kk_sc[kdim:] = zeros
            starts = [SSM_GROUP * (CHUNK - 1 - t) for t in tt]
        else:
            kk_sc[:kdim] = zeros
            kk_sc[kdim:] = kk
            starts = [kdim - SSM_GROUP * t for t in tt]
        acc = zeros
        for t in tt:
            acc = jnp.where(col_t == t, kk_sc[starts[t]:starts[t] + kdim, :], acc)
        toeplitz = toeplitz + acc
    wall_ref[:kdim, :] = toeplitz.astype(BF16)


def _s5prep(arow, bt, ct):
    g = arow.shape[0]
    kdim = CHUNK * SSM_GROUP
    return pl.pallas_call(
        _s5prep_kernel,
        grid=(g,),
        in_specs=[pl.BlockSpec((None, 2, 8, LANES), lambda i: (i, 0, 0, 0)),
                  pl.BlockSpec((None, 2, 2, SSM_GROUP, LANES), lambda i: (i, 0, 0, 0, 0)),
                  pl.BlockSpec((None, 2, 2, SSM_GROUP, LANES), lambda i: (i, 0, 0, 0, 0))],
        out_specs=[pl.BlockSpec((None, 2 * kdim, kdim), lambda i: (i, 0, 0)),
                   pl.BlockSpec((None, kdim, kdim), lambda i: (i, 0, 0)),
                   pl.BlockSpec((None, 2, 2, 8, LANES), lambda i: (i, 0, 0, 0, 0))],
        out_shape=[jax.ShapeDtypeStruct((g, 2 * kdim, kdim), BF16),
                   jax.ShapeDtypeStruct((g, kdim, kdim), BF16),
                   jax.ShapeDtypeStruct((g, 2, 2, 8, LANES), F32)],
        scratch_shapes=[pltpu.VMEM((2, CHUNK + 8, LANES), F32),
                        pltpu.VMEM((2 * kdim, kdim), F32)],
        compiler_params=_cparams(("arbitrary",)),
        name="s5prep",
    )(arow, bt, ct)


def _s5prep_inputs(a_re, a_im, log_dt, b_re, b_im, c_re, c_im):
    g = a_re.shape[1]
    dup = lambda a: jnp.concatenate([a, a], axis=-1)
    rows = jnp.stack([dup(a_re), dup(a_im),
                      jnp.broadcast_to(log_dt[..., None], (2, g, LANES))], axis=2)
    arow = jnp.pad(rows, ((0, 0), (0, 0), (0, 5), (0, 0))).transpose(1, 0, 2, 3)
    bt = jnp.stack([dup(b_re.transpose(0, 1, 3, 2)), dup(b_im.transpose(0, 1, 3, 2))], axis=2)
    ct = jnp.stack([dup(c_re), dup(c_im)], axis=2)
    return arow, bt.transpose(1, 0, 2, 3, 4), ct.transpose(1, 0, 2, 3, 4)


def kernel(x, c, ctx, c_ctx, norm_g, w_ada, b_ada, w_in, conv_dw, conv_db, conv_ln_g, conv_ln_b,
           ssm_a_re, ssm_a_im, ssm_log_dt, ssm_b_re, ssm_b_im, ssm_c_re, ssm_c_im, ssm_d,
           ssm_glu_w, ssm_glu_b, w_out, final_g):
    bsz, length, d = x.shape
    ctx_len = ctx.shape[1]
    cw = conv_dw.shape[-1]
    sw = ssm_d.shape[-1]
    assert bsz == BATCH and norm_g.shape[0] == 1
    assert length % (GRID_W * CHUNK) == 0 and ctx_len % CHUNK == 0
    n = length * BATCH
    nch = length // CHUNK
    nchc = ctx_len // CHUNK
    rows = length // GRID_W

    cond16 = jnp.zeros((16, d), F32).at[:BATCH].set(c).at[BATCH].set(c_ctx)
    mod = _ada(cond16, w_ada.reshape(w_ada.shape[1:]), b_ada[0])
    shift, scale, gate = mod[:, :d], mod[:, d:2 * d], mod[:, 2 * d:]
    amp = norm_g[0][None] * (1.0 + scale)
    a8, s8, gate8 = amp[:BATCH], shift[:BATCH], gate[:BATCH]
    a8c = jnp.broadcast_to(amp[BATCH], (BATCH, d))
    s8c = jnp.broadcast_to(shift[BATCH], (BATCH, d))

    w = w_in.reshape(w_in.shape[1:])
    half = cw // 2
    cols = lambda lo, width: w[:, lo:lo + width]
    wt_w = jnp.concatenate([cols(3 * cw, 2 * sw), cols(0, half), cols(cw, half), cols(2 * cw, half)],
                           axis=1).T.astype(BF16)
    wt_r = jnp.concatenate([cols(half, half), cols(cw + half, half), cols(2 * cw + half, half)],
                           axis=1).T.astype(BF16)
    wall, cot, lam_raw = _s5prep(*_s5prep_inputs(
        ssm_a_re[0], ssm_a_im[0], ssm_log_dt[0], ssm_b_re[0], ssm_b_im[0], ssm_c_re[0], ssm_c_im[0]))
    ng = sw // SSM_GROUP
    lam16 = lam_raw[:, :, :, 0, :SSM_STATE].reshape(ng // 2, 2, 2, 2, SSM_STATE).transpose(
        2, 0, 3, 1, 4).reshape(2, ng // 2, 2, LANES)
    dcol = jnp.broadcast_to(ssm_d[0].reshape(ng, 1, SSM_GROUP), (ng, CHUNK, SSM_GROUP)).reshape(
        ng, CHUNK * SSM_GROUP, 1)

    xt = x.transpose(1, 0, 2)
    ctxt = ctx.transpose(1, 0, 2)
    c4 = ctxt.reshape(nchc, CHUNK // 2, 2 * BATCH, d)

    ut, g2t, vt_h, g1t_h = _inproj_w(xt.reshape(rows, GRID_W, BATCH, d), a8, s8, wt_w, sw, half)
    vt_v, g1t_v = _inproj_r(xt.reshape(rows, GRID_W * BATCH, d), a8, s8, wt_r, half)
    uct = _ctxproj(c4, a8c, s8c, wt_w[:sw])
    y1t = _s5(ut, uct, wall, cot, lam16, dcol)

    dw, db = conv_dw[0], conv_db[0]
    ct_h, s1_h, s2_h = _convt(vt_h, _conv_toeplitz(dw[:, :half], GRID_W), db[:half])
    ct_v, s1_v, s2_v = _convt(vt_v, _conv_toeplitz(dw[:, half:], rows), db[half:])
    to_v = lambda a: a.reshape(GRID_W, rows, BATCH).transpose(1, 0, 2).reshape(1, n)
    to_h = lambda a: a.reshape(rows, GRID_W, BATCH).transpose(1, 0, 2).reshape(1, n)
    pad4 = jnp.zeros((4, n), F32)
    stats_v = jnp.concatenate([s1_v.reshape(1, n), s2_v.reshape(1, n), to_v(s1_h), to_v(s2_h), pad4])
    stats_h = jnp.concatenate([s1_h.reshape(1, n), s2_h.reshape(1, n), to_h(s1_v), to_h(s2_v), pad4])

    ln_g, ln_b = conv_ln_g[0].reshape(cw, 1), conv_ln_b[0].reshape(cw, 1)
    wo = w_out.reshape(w_out.shape[1:]).astype(BF16)
    mix_v = _fv(ct_v.reshape(half, n), g1t_v, stats_v, ln_g[half:], ln_b[half:], wo[half:cw], cw)
    out3 = _main_w(y1t, g2t, ct_h.reshape(half, n), g1t_h, stats_h,
                   mix_v.reshape(rows, GRID_W, BATCH, d), x.reshape(BATCH, rows, GRID_W * d),
                   ssm_glu_w[0].T.astype(BF16), ssm_glu_b[0].reshape(sw, 1), wo[cw:], wo[:half],
                   ln_g[:half], ln_b[:half], gate8, final_g.reshape(1, d), cw)
    return out3.reshape(x.shape)
```

```python
import functools

import jax
import jax.numpy as jnp
from jax import lax
from jax.experimental import pallas as pl
from jax.experimental.pallas import tpu as pltpu

GRID_W = 64
CONV_TAPS = 31
CONV_PAD = CONV_TAPS // 2
SSM_GROUP = 16
SSM_STATE = 64
CHUNK = 16
BATCH = 8
EPS = 1e-6
LANES = 128
VMEM_LIMIT = 56 * 1024 * 1024

F32 = jnp.float32
BF16 = jnp.bfloat16


def _cparams(sem):
    return pltpu.CompilerParams(dimension_semantics=sem, vmem_limit_bytes=VMEM_LIMIT)


def _ada_kernel(c_ref, w_ref, b_ref, o_ref):
    s = jax.nn.silu(c_ref[...])
    o_ref[...] = jnp.dot(s, w_ref[...], preferred_element_type=F32,
                         precision=lax.Precision.HIGHEST) + b_ref[...]


def _ada(cond16, w_ada, b_ada):
    d, n3 = w_ada.shape
    nb = n3 // d
    return pl.pallas_call(
        _ada_kernel,
        grid=(nb,),
        in_specs=[pl.BlockSpec((16, d), lambda j: (0, 0)),
                  pl.BlockSpec((d, d), lambda j: (0, j)),
                  pl.BlockSpec((1, d), lambda j: (0, j))],
        out_specs=pl.BlockSpec((16, d), lambda j: (0, j)),
        out_shape=jax.ShapeDtypeStruct((16, n3), F32),
        compiler_params=_cparams(("arbitrary",)),
        name="ada",
    )(cond16, w_ada, b_ada.reshape(1, n3))


def _modulated_norm(x, a, s):
    ms = jnp.mean(x * x, axis=-1, keepdims=True)
    return x * lax.rsqrt(ms + EPS) * a[None] + s[None]


def _conv_gates(q, hc, vt_ref, g1t_ref):
    val = (q[:hc] * jax.nn.sigmoid(q[hc:2 * hc])).astype(BF16)
    for j in range(vt_ref.shape[0]):
        vt_ref[j] = val[:, j * LANES:(j + 1) * LANES]
    g1t_ref[...] = jax.nn.silu(q[2 * hc:]).astype(BF16)


def _inproj_w_kernel(x_ref, a_ref, s_ref, wt_ref, ut_ref, g2t_ref, vt_ref, g1t_ref):
    r, _, d = x_ref.shape
    sw = ut_ref.shape[0]
    hc = g1t_ref.shape[0]
    h = _modulated_norm(x_ref[...], a_ref[...], s_ref[...]).reshape(r * BATCH, d).astype(BF16)
    q = lax.dot_general(wt_ref[...], h, (((1,), (1,)), ((), ())), preferred_element_type=F32)
    ut_ref[...] = q[:sw].astype(BF16)
    g2t_ref[...] = jax.nn.silu(q[sw:2 * sw]).astype(BF16)
    _conv_gates(q[2 * sw:], hc, vt_ref, g1t_ref)


def _inproj_w(x4, a8, s8, wt, sw, hc):
    r, _, _, d = x4.shape
    tl = r * BATCH
    nq = tl // LANES
    cpr = GRID_W // CHUNK
    s5_spec = pl.BlockSpec((None, sw, tl), lambda w: (w % CHUNK, 0, w // CHUNK))
    return pl.pallas_call(
        _inproj_w_kernel,
        grid=(GRID_W,),
        in_specs=[pl.BlockSpec((r, None, BATCH, d), lambda w: (0, w, 0, 0)),
                  pl.BlockSpec((BATCH, d), lambda w: (0, 0)),
                  pl.BlockSpec((BATCH, d), lambda w: (0, 0)),
                  pl.BlockSpec(wt.shape, lambda w: (0, 0))],
        out_specs=[s5_spec, s5_spec,
                   pl.BlockSpec((None, nq, hc, LANES), lambda w: (w, 0, 0, 0)),
                   pl.BlockSpec((hc, tl), lambda w: (0, w))],
        out_shape=[jax.ShapeDtypeStruct((CHUNK, sw, cpr * tl), BF16),
                   jax.ShapeDtypeStruct((CHUNK, sw, cpr * tl), BF16),
                   jax.ShapeDtypeStruct((GRID_W, nq, hc, LANES), BF16),
                   jax.ShapeDtypeStruct((hc, GRID_W * tl), BF16)],
        compiler_params=_cparams(("arbitrary",)),
        name="inproj_w",
    )(x4, a8, s8, wt)


def _inproj_r_kernel(x_ref, a_ref, s_ref, wt_ref, vt_ref, g1t_ref):
    tl, d = x_ref.shape
    hc = g1t_ref.shape[0]
    x = x_ref[...].reshape(tl // BATCH, BATCH, d)
    h = _modulated_norm(x, a_ref[...], s_ref[...]).reshape(tl, d).astype(BF16)
    q = lax.dot_general(wt_ref[...], h, (((1,), (1,)), ((), ())), preferred_element_type=F32)
    _conv_gates(q, hc, vt_ref, g1t_ref)


def _inproj_r(x3, a8, s8, wt, hc):
    r, tl, d = x3.shape
    nq = tl // LANES
    return pl.pallas_call(
        _inproj_r_kernel,
        grid=(r,),
        in_specs=[pl.BlockSpec((None, tl, d), lambda i: (i, 0, 0)),
                  pl.BlockSpec((BATCH, d), lambda i: (0, 0)),
                  pl.BlockSpec((BATCH, d), lambda i: (0, 0)),
                  pl.BlockSpec(wt.shape, lambda i: (0, 0))],
        out_specs=[pl.BlockSpec((None, nq, hc, LANES), lambda i: (i, 0, 0, 0)),
                   pl.BlockSpec((hc, tl), lambda i: (0, i))],
        out_shape=[jax.ShapeDtypeStruct((r, nq, hc, LANES), BF16),
                   jax.ShapeDtypeStruct((hc, r * tl), BF16)],
        compiler_params=_cparams(("arbitrary",)),
        name="inproj_r",
    )(x3, a8, s8, wt)


def _ctxproj_kernel(x_ref, a_ref, s_ref, wut_ref, ut_ref):
    cb, _, d = x_ref.shape
    x = x_ref[...].reshape(cb * 2, BATCH, d)
    h = _modulated_norm(x, a_ref[...], s_ref[...]).reshape(cb, 2 * BATCH, d)
    for tl in range(2):
        ht = h[:, tl * BATCH:(tl + 1) * BATCH, :].reshape(cb * BATCH, d).astype(BF16)
        q = lax.dot_general(wut_ref[...], ht, (((1,), (1,)), ((), ())),
                            preferred_element_type=F32)
        ut_ref[tl] = q.astype(BF16)


def _ctxproj(x4, a8, s8, wut):
    nch, _, _, d = x4.shape
    sw = wut.shape[0]
    return pl.pallas_call(
        _ctxproj_kernel,
        grid=(CHUNK // 2,),
        in_specs=[pl.BlockSpec((nch, None, 2 * BATCH, d), lambda t: (0, t, 0, 0)),
                  pl.BlockSpec((BATCH, d), lambda t: (0, 0)),
                  pl.BlockSpec((BATCH, d), lambda t: (0, 0)),
                  pl.BlockSpec(wut.shape, lambda t: (0, 0))],
        out_specs=pl.BlockSpec((2, sw, nch * BATCH), lambda t: (t, 0, 0)),
        out_shape=jax.ShapeDtypeStruct((CHUNK, sw, nch * BATCH), BF16),
        compiler_params=_cparams(("arbitrary",)),
        name="ctxproj",
    )(x4, a8, s8, wut)


def _s5_kernel(xt_ref, xct_ref, wall_ref, cot_ref, lam_ref, dcol_ref, y_ref,
               s_sc, h_sc, yi_sc, *, lane_chunk):
    nl = xt_ref.shape[-1]
    nlc = xct_ref.shape[-1]
    n_ch = nl // BATCH
    n_chc = nlc // BATCH
    kdim = CHUNK * SSM_GROUP
    p = SSM_STATE

    def group_x(ref, j):
        return ref[:, j * SSM_GROUP:(j + 1) * SSM_GROUP, :].reshape(kdim, ref.shape[-1])

    st_c = [jnp.dot(wall_ref[j, kdim:], group_x(xct_ref, j), preferred_element_type=F32)
            for j in range(2)]
    st = []
    for j in range(2):
        r = jnp.dot(wall_ref[j], group_x(xt_ref, j), preferred_element_type=F32)
        yi_sc[j] = r[:kdim]
        st.append(r[kdim:])
    for d in range(2):
        for src, off, width in ((st_c, 0, nlc), (st, nlc, nl)):
            for ri in range(2):
                lo = (2 * d + ri) * p
                pair = jnp.concatenate([src[0][lo:lo + p], src[1][lo:lo + p]], axis=0)
                s_sc[d, ri, off:off + width, :] = pair.T

    lam = [[jnp.broadcast_to(lam_ref[d, ri:ri + 1, :], (BATCH, LANES)) for ri in range(2)]
           for d in range(2)]

    def advance(d, state, row):
        hr, hi = state
        ar, ai = lam[d]
        sr = s_sc[d, 0, pl.ds(row, BATCH), :]
        si = s_sc[d, 1, pl.ds(row, BATCH), :]
        return ar * hr - ai * hi + sr, ar * hi + ai * hr + si

    zero = jnp.zeros((BATCH, LANES), F32)

    def ctx_step(i, carry):
        f, b = carry
        f = advance(0, f, pl.multiple_of(i * BATCH, BATCH))
        b = advance(1, b, pl.multiple_of((n_chc - 1 - i) * BATCH, BATCH))
        return f, b

    carry = lax.fori_loop(0, n_chc, ctx_step, ((zero, zero), (zero, zero)))

    cpr = GRID_W // CHUNK
    n_rows = n_ch // cpr

    def lane_row(ch):
        within = jnp.bitwise_and(ch, cpr - 1)
        row = lax.shift_right_logical(ch, cpr.bit_length() - 1)
        return pl.multiple_of((within * n_rows + row) * BATCH, BATCH)

    def lat_step(i, carry):
        f, b = carry
        rf = lane_row(i)
        rb = lane_row(n_ch - 1 - i)
        h_sc[0, 0, pl.ds(rf, BATCH), :] = f[0]
        h_sc[0, 1, pl.ds(rf, BATCH), :] = f[1]
        h_sc[1, 0, pl.ds(rb, BATCH), :] = b[0]
        h_sc[1, 1, pl.ds(rb, BATCH), :] = b[1]
        f = advance(0, f, pl.multiple_of(nlc + rf, BATCH))
        b = advance(1, b, pl.multiple_of(nlc + rb, BATCH))
        return f, b

    lax.fori_loop(0, n_ch, lat_step, carry, unroll=8)

    for c0 in range(0, nl, lane_chunk):
        cs = slice(c0, c0 + lane_chunk)
        ht = [[h_sc[d, ri, cs, :].T for ri in range(2)] for d in range(2)]
        for j in range(2):
            x = xt_ref[:, j * SSM_GROUP:(j + 1) * SSM_GROUP, cs].reshape(kdim, lane_chunk)
            hj = jnp.concatenate([ht[d][ri][j * p:(j + 1) * p] for d in range(2) for ri in range(2)],
                                 axis=0).astype(BF16)
            acc = (yi_sc[j, :, cs] + dcol_ref[j] * x.astype(F32)
                   + jnp.dot(cot_ref[j], hj, preferred_element_type=F32))
            y = jax.nn.gelu(acc).astype(BF16)
            y_ref[:, j * SSM_GROUP:(j + 1) * SSM_GROUP, cs] = y.reshape(CHUNK, SSM_GROUP, lane_chunk)


def _s5(ut, uct, wall, cot, lam16, dcol):
    _, sw, nl = ut.shape
    nlc = uct.shape[-1]
    gp = sw // (2 * SSM_GROUP)
    kdim = CHUNK * SSM_GROUP
    lane_chunk = min(nl, 512)
    kern = functools.partial(_s5_kernel, lane_chunk=lane_chunk)
    return pl.pallas_call(
        kern,
        grid=(gp,),
        in_specs=[pl.BlockSpec((CHUNK, 2 * SSM_GROUP, nl), lambda g: (0, g, 0)),
                  pl.BlockSpec((CHUNK, 2 * SSM_GROUP, nlc), lambda g: (0, g, 0)),
                  pl.BlockSpec((2, 2 * kdim, kdim), lambda g: (g, 0, 0)),
                  pl.BlockSpec((2, kdim, kdim), lambda g: (g, 0, 0)),
                  pl.BlockSpec((2, None, 2, LANES), lambda g: (0, g, 0, 0)),
                  pl.BlockSpec((2, kdim, 1), lambda g: (g, 0, 0))],
        out_specs=pl.BlockSpec((CHUNK, 2 * SSM_GROUP, nl), lambda g: (0, g, 0)),
        out_shape=jax.ShapeDtypeStruct((CHUNK, sw, nl), BF16),
        scratch_shapes=[pltpu.VMEM((2, 2, nlc + nl, LANES), F32),
                        pltpu.VMEM((2, 2, nl, LANES), F32),
                        pltpu.VMEM((2, kdim, nl), F32)],
        compiler_params=_cparams(("arbitrary",)),
        name="s5",
    )(ut, uct, wall, cot, lam16, dcol)


CONV_CB = 64
CONV_GROUP = 4
CONV_SKEW = 8


def _convt_kernel(v_ref, tq_ref, b_ref, o_ref, s1_ref, s2_ref, z_sc, o_sc):
    a, cb, _ = v_ref.shape
    pitch = cb + CONV_SKEW
    z = v_ref[...].astype(F32)
    for i in range(a):
        z_sc[i * pitch:i * pitch + cb, :] = z[i]
    s1 = jnp.zeros((a, LANES), F32)
    s2 = jnp.zeros((a, LANES), F32)
    for g in range(cb // CONV_GROUP):
        c0 = g * CONV_GROUP
        rhs = jnp.concatenate([z_sc[pl.ds(c0 + j, a, stride=pitch), :] for j in range(CONV_GROUP)],
                              axis=0).astype(BF16)
        out = jnp.dot(tq_ref[g], rhs, preferred_element_type=F32).reshape(CONV_GROUP, a, LANES)
        out = out + b_ref[c0:c0 + CONV_GROUP].reshape(CONV_GROUP, 1, 1)
        for j in range(CONV_GROUP):
            o_sc[pl.ds(c0 + j, a, stride=pitch), :] = out[j]
        s1 = s1 + jnp.sum(out, axis=0)
        s2 = s2 + jnp.sum(out * out, axis=0)
    for i in range(a):
        o_ref[i] = o_sc[i * pitch:i * pitch + cb, :].astype(o_ref.dtype)

    @pl.when(pl.program_id(1) == 0)
    def _():
        s1_ref[...] = jnp.zeros_like(s1_ref)
        s2_ref[...] = jnp.zeros_like(s2_ref)

    s1_ref[...] += s1
    s2_ref[...] += s2


def _convt(vt, tq, bias):
    a, nq, c, _ = vt.shape
    cb = min(c, CONV_CB)
    ka = CONV_GROUP * a
    stat_spec = pl.BlockSpec((a, LANES), lambda q, i: (0, q))
    return pl.pallas_call(
        _convt_kernel,
        grid=(nq, c // cb),
        in_specs=[pl.BlockSpec((a, None, cb, LANES), lambda q, i: (0, q, i, 0)),
                  pl.BlockSpec((cb // CONV_GROUP, ka, ka), lambda q, i: (i, 0, 0)),
                  pl.BlockSpec((cb, 1), lambda q, i: (i, 0))],
        out_specs=[pl.BlockSpec((a, cb, LANES), lambda q, i: (0, i, q)), stat_spec, stat_spec],
        out_shape=[jax.ShapeDtypeStruct((a, c, nq * LANES), BF16),
                   jax.ShapeDtypeStruct((a, nq * LANES), F32),
                   jax.ShapeDtypeStruct((a, nq * LANES), F32)],
        scratch_shapes=[pltpu.VMEM((a * (cb + CONV_SKEW), LANES), F32),
                        pltpu.VMEM((a * (cb + CONV_SKEW), LANES), F32)],
        compiler_params=_cparams(("arbitrary", "arbitrary")),
        name="convt",
    )(vt, tq, bias.reshape(c, 1))


def _conv_toeplitz(w, a):
    c = w.shape[1]
    g = CONV_GROUP
    pos = jnp.arange(a)
    tap = pos[None, :] - pos[:, None] + CONV_PAD
    hit = tap[None] == jnp.arange(CONV_TAPS)[:, None, None]
    idx = jnp.arange(g)
    diag = (idx[:, None, None] == idx[None, :, None]) & (idx[None, :, None] == idx[None, None, :])
    sel = (hit[:, None, None, :, None, :] & diag[None, :, :, None, :, None]).astype(BF16)
    w4 = w.reshape(CONV_TAPS, c // g, g).transpose(1, 0, 2).reshape(c // g, CONV_TAPS * g)
    tq = jnp.dot(w4.astype(BF16), sel.reshape(CONV_TAPS * g, g * a * g * a),
                 preferred_element_type=F32)
    return tq.reshape(c // g, g * a, g * a).astype(BF16)


def _conv_branch_t(cv_ref, g1_ref, st_ref, lng_ref, lnb_ref, n_chan):
    st = st_ref[...]
    mean = (st[0:1] + st[2:3]) * (1.0 / n_chan)
    var = (st[1:2] + st[3:4]) * (1.0 / n_chan) - mean * mean
    y = (cv_ref[...].astype(F32) - mean) * lax.rsqrt(var + EPS) * lng_ref[...] + lnb_ref[...]
    return (jax.nn.silu(y) * g1_ref[...].astype(F32)).astype(BF16)


def _tail_w_kernel(y_ref, g2_ref, cv_ref, g1_ref, st_ref, wt_ref, bt_ref, wos_ref, woh_ref,
                   lng_ref, lnb_ref, o_ref, *, n_chan):
    y = y_ref[...]
    z = jnp.dot(wt_ref[...], y, preferred_element_type=F32) + bt_ref[...]
    s = (y.astype(F32) * jax.nn.sigmoid(z) * g2_ref[...].astype(F32)).astype(BF16)
    co = _conv_branch_t(cv_ref, g1_ref, st_ref, lng_ref, lnb_ref, n_chan)
    tn = (((0,), (0,)), ((), ()))
    mix = (lax.dot_general(s, wos_ref[...], tn, preferred_element_type=F32)
           + lax.dot_general(co, woh_ref[...], tn, preferred_element_type=F32))
    o_ref[...] = mix.reshape(o_ref.shape)


def _tail_w(y1t, g2t, ct_h, g1t_h, stats_h, glu_wt, glu_bcol, wo_s, wo_h, ln_g, ln_b, n_chan):
    _, sw, _ = y1t.shape
    _, hc, tl = ct_h.shape
    d = wo_s.shape[1]
    r = tl // BATCH
    s5_spec = pl.BlockSpec((None, sw, tl), lambda w: (w % CHUNK, 0, w // CHUNK))
    tok = lambda rows: pl.BlockSpec((rows, tl), lambda w: (0, w))
    full = lambda arr: pl.BlockSpec(arr.shape, lambda w: (0,) * arr.ndim)
    return pl.pallas_call(
        functools.partial(_tail_w_kernel, n_chan=n_chan),
        grid=(GRID_W,),
        in_specs=[s5_spec, s5_spec, pl.BlockSpec((None, hc, tl), lambda w: (w, 0, 0)), tok(hc),
                  tok(8), full(glu_wt), full(glu_bcol), full(wo_s), full(wo_h), full(ln_g),
                  full(ln_b)],
        out_specs=pl.BlockSpec((r, None, BATCH, d), lambda w: (0, w, 0, 0)),
        out_shape=jax.ShapeDtypeStruct((r, GRID_W, BATCH, d), F32),
        compiler_params=_cparams(("arbitrary",)),
        name="tail_w",
    )(y1t, g2t, ct_h, g1t_h, stats_h, glu_wt, glu_bcol, wo_s, wo_h, ln_g, ln_b)


def _final_r_kernel(cv_ref, g1_ref, st_ref, mh_ref, x_ref, wo_ref, lng_ref, lnb_ref, gate_ref,
                    fg_ref, o_ref, mix_sc, *, n_chan):
    tl = x_ref.shape[1]
    co = _conv_branch_t(cv_ref, g1_ref, st_ref, lng_ref, lnb_ref, n_chan)
    mix = lax.dot_general(co, wo_ref[...], (((0,), (0,)), ((), ())),
                          preferred_element_type=F32) + mh_ref[...]
    nlt = mix_sc.shape[0]
    for j in range(nlt):
        mix_sc[j] = mix[:, j * LANES:(j + 1) * LANES]
    for b in range(BATCH):
        mb = jnp.concatenate([mix_sc[j, pl.ds(b, tl, stride=BATCH), :] for j in range(nlt)],
                             axis=-1)
        xo = x_ref[b] + gate_ref[b:b + 1, :] * mb
        ms = jnp.mean(xo * xo, axis=-1, keepdims=True)
        o_ref[b] = xo * lax.rsqrt(ms + EPS) * fg_ref[...]


def _final_r(ct_v, g1t_v, stats_v, mix_h, x, wo_v, ln_g, ln_b, gate8, final_g, n_chan):
    r, hc, tm = ct_v.shape
    _, length, d = x.shape
    tl = tm // BATCH
    tok = lambda rows: pl.BlockSpec((rows, tm), lambda i: (0, i))
    nat = pl.BlockSpec((BATCH, tl, d), lambda i: (0, i, 0))
    full = lambda arr: pl.BlockSpec(arr.shape, lambda i: (0,) * arr.ndim)
    return pl.pallas_call(
        functools.partial(_final_r_kernel, n_chan=n_chan),
        grid=(r,),
        in_specs=[pl.BlockSpec((None, hc, tm), lambda i: (i, 0, 0)), tok(hc), tok(8),
                  pl.BlockSpec((tm, d), lambda i: (i, 0)), nat,
                  full(wo_v), full(ln_g), full(ln_b), full(gate8), full(final_g)],
        out_specs=nat,
        out_shape=jax.ShapeDtypeStruct(x.shape, F32),
        scratch_shapes=[pltpu.VMEM((d // LANES, tm, LANES), F32)],
        compiler_params=_cparams(("arbitrary",)),
        name="final_r",
    )(ct_v, g1t_v, stats_v, mix_h, x, wo_v, ln_g, ln_b, gate8, final_g)


def _s5prep_kernel(arow_ref, bt_ref, ct_ref, wall_ref, cot_ref, lam_ref, pw_sc, kk_sc):
    kdim = CHUNK * SSM_GROUP
    hp = lax.Precision.HIGHEST
    lane = lax.broadcasted_iota(jnp.int32, (1, LANES), 1)
    is_re = lane < SSM_STATE
    col_t = lax.broadcasted_iota(jnp.int32, (1, kdim), 1) // SSM_GROUP
    kcol = lax.broadcasted_iota(jnp.int32, (pw_sc.shape[1], 1), 0).astype(F32)

    def tile_rows(a):
        return jnp.concatenate([a] * CHUNK, axis=0)

    def power_rows(exps):
        pr = jnp.concatenate([jnp.broadcast_to(pw_sc[0, e:e + 1, :], (SSM_GROUP, LANES))
                              for e in exps], axis=0)
        pi = jnp.concatenate([jnp.broadcast_to(pw_sc[1, e:e + 1, :], (SSM_GROUP, LANES))
                              for e in exps], axis=0)
        return pr, pi

    def cmul(xr, xi, yr, yi, im_sign):
        return jnp.where(is_re, xr * yr - xi * yi, im_sign * (xr * yi + xi * yr))

    toeplitz = jnp.zeros((kdim, kdim), F32)
    for d in range(2):
        ar, ai = arow_ref[d, 0:1, :], arow_ref[d, 1:2, :]
        dt = jnp.exp(arow_ref[d, 2:3, :])
        mag = jnp.exp(kcol * (ar * dt))
        ang = kcol * (ai * dt)
        pw_sc[0] = mag * jnp.cos(ang)
        pw_sc[1] = mag * jnp.sin(ang)
        lam_ref[d, 0] = pw_sc[0, CHUNK:CHUNK + 8, :]
        lam_ref[d, 1] = pw_sc[1, CHUNK:CHUNK + 8, :]
        nr, ni = pw_sc[0, 1:2, :] - 1.0, pw_sc[1, 1:2, :]
        den = ar * ar + ai * ai
        qr, qi = (nr * ar + ni * ai) / den, (ni * ar - nr * ai) / den
        btr, bti = bt_ref[d, 0], bt_ref[d, 1]
        bbr, bbi = tile_rows(qr * btr - qi * bti), tile_rows(qr * bti + qi * btr)
        cr, ci = tile_rows(ct_ref[d, 0]), tile_rows(ct_ref[d, 1])
        tt = list(range(CHUNK))
        e_lag = [CHUNK - 1 - t for t in tt] if d else tt
        e_in = tt if d else [CHUNK - 1 - t for t in tt]
        e_out = [CHUNK - t for t in tt] if d else [t + 1 for t in tt]
        cot_ref[:, d * LANES:(d + 1) * LANES] = cmul(cr, ci, *power_rows(e_out), -1.0).astype(BF16)
        wall_ref[kdim + d * LANES:kdim + (d + 1) * LANES, :] = cmul(
            bbr, bbi, *power_rows(e_in), 1.0).T.astype(BF16)
        c_lag = cmul(cr, ci, *power_rows(e_lag), -1.0)
        bb = jnp.where(is_re, bbr, bbi)
        kk = lax.dot_general(c_lag, bb, (((1,), (1,)), ((), ())), precision=hp,
                             preferred_element_type=F32)
        zeros = jnp.zeros((kdim, kdim), F32)
        if d:
            kk_sc[:kdim] = kk
            kk_sc[kdim:] = zeros
            starts = [SSM_GROUP * (CHUNK - 1 - t) for t in tt]
        else:
            kk_sc[:kdim] = zeros
            kk_sc[kdim:] = kk
            starts = [kdim - SSM_GROUP * t for t in tt]
        acc = zeros
        for t in tt:
            acc = jnp.where(col_t == t, kk_sc[starts[t]:starts[t] + kdim, :], acc)
        toeplitz = toeplitz + acc
    wall_ref[:kdim, :] = toeplitz.astype(BF16)


def _s5prep(arow, bt, ct):
    g = arow.shape[0]
    kdim = CHUNK * SSM_GROUP
    return pl.pallas_call(
        _s5prep_kernel,
        grid=(g,),
        in_specs=[pl.BlockSpec((None, 2, 8, LANES), lambda i: (i, 0, 0, 0)),
                  pl.BlockSpec((None, 2, 2, SSM_GROUP, LANES), lambda i: (i, 0, 0, 0, 0)),
                  pl.BlockSpec((None, 2, 2, SSM_GROUP, LANES), lambda i: (i, 0, 0, 0, 0))],
        out_specs=[pl.BlockSpec((None, 2 * kdim, kdim), lambda i: (i, 0, 0)),
                   pl.BlockSpec((None, kdim, kdim), lambda i: (i, 0, 0)),
                   pl.BlockSpec((None, 2, 2, 8, LANES), lambda i: (i, 0, 0, 0, 0))],
        out_shape=[jax.ShapeDtypeStruct((g, 2 * kdim, kdim), BF16),
                   jax.ShapeDtypeStruct((g, kdim, kdim), BF16),
                   jax.ShapeDtypeStruct((g, 2, 2, 8, LANES), F32)],
        scratch_shapes=[pltpu.VMEM((2, CHUNK + 8, LANES), F32),
                        pltpu.VMEM((2 * kdim, kdim), F32)],
        compiler_params=_cparams(("arbitrary",)),
        name="s5prep",
    )(arow, bt, ct)


def _s5prep_inputs(a_re, a_im, log_dt, b_re, b_im, c_re, c_im):
    g = a_re.shape[1]
    dup = lambda a: jnp.concatenate([a, a], axis=-1)
    rows = jnp.stack([dup(a_re), dup(a_im),
                      jnp.broadcast_to(log_dt[..., None], (2, g, LANES))], axis=2)
    arow = jnp.pad(rows, ((0, 0), (0, 0), (0, 5), (0, 0))).transpose(1, 0, 2, 3)
    bt = jnp.stack([dup(b_re.transpose(0, 1, 3, 2)), dup(b_im.transpose(0, 1, 3, 2))], axis=2)
    ct = jnp.stack([dup(c_re), dup(c_im)], axis=2)
    return arow, bt.transpose(1, 0, 2, 3, 4), ct.transpose(1, 0, 2, 3, 4)


def kernel(x, c, ctx, c_ctx, norm_g, w_ada, b_ada, w_in, conv_dw, conv_db, conv_ln_g, conv_ln_b,
           ssm_a_re, ssm_a_im, ssm_log_dt, ssm_b_re, ssm_b_im, ssm_c_re, ssm_c_im, ssm_d,
           ssm_glu_w, ssm_glu_b, w_out, final_g):
    bsz, length, d = x.shape
    ctx_len = ctx.shape[1]
    cw = conv_dw.shape[-1]
    sw = ssm_d.shape[-1]
    assert bsz == BATCH and norm_g.shape[0] == 1
    assert length % (GRID_W * CHUNK) == 0 and ctx_len % CHUNK == 0
    n = length * BATCH
    nch = length // CHUNK
    nchc = ctx_len // CHUNK
    rows = length // GRID_W

    cond16 = jnp.zeros((16, d), F32).at[:BATCH].set(c).at[BATCH].set(c_ctx)
    mod = _ada(cond16, w_ada.reshape(w_ada.shape[1:]), b_ada[0])
    shift, scale, gate = mod[:, :d], mod[:, d:2 * d], mod[:, 2 * d:]
    amp = norm_g[0][None] * (1.0 + scale)
    a8, s8, gate8 = amp[:BATCH], shift[:BATCH], gate[:BATCH]
    a8c = jnp.broadcast_to(amp[BATCH], (BATCH, d))
    s8c = jnp.broadcast_to(shift[BATCH], (BATCH, d))

    w = w_in.reshape(w_in.shape[1:])
    half = cw // 2
    cols = lambda lo, width: w[:, lo:lo + width]
    wt_w = jnp.concatenate([cols(3 * cw, 2 * sw), cols(0, half), cols(cw, half), cols(2 * cw, half)],
                           axis=1).T.astype(BF16)
    wt_r = jnp.concatenate([cols(half, half), cols(cw + half, half), cols(2 * cw + half, half)],
                           axis=1).T.astype(BF16)
    wall, cot, lam_raw = _s5prep(*_s5prep_inputs(
        ssm_a_re[0], ssm_a_im[0], ssm_log_dt[0], ssm_b_re[0], ssm_b_im[0], ssm_c_re[0], ssm_c_im[0]))
    ng = sw // SSM_GROUP
    lam16 = lam_raw[:, :, :, 0, :SSM_STATE].reshape(ng // 2, 2, 2, 2, SSM_STATE).transpose(
        2, 0, 3, 1, 4).reshape(2, ng // 2, 2, LANES)
    dcol = jnp.broadcast_to(ssm_d[0].reshape(ng, 1, SSM_GROUP), (ng, CHUNK, SSM_GROUP)).reshape(
        ng, CHUNK * SSM_GROUP, 1)

    xt = x.transpose(1, 0, 2)
    ctxt = ctx.transpose(1, 0, 2)
    c4 = ctxt.reshape(nchc, CHUNK // 2, 2 * BATCH, d)

    ut, g2t, vt_h, g1t_h = _inproj_w(xt.reshape(rows, GRID_W, BATCH, d), a8, s8, wt_w, sw, half)
    vt_v, g1t_v = _inproj_r(xt.reshape(rows, GRID_W * BATCH, d), a8, s8, wt_r, half)
    uct = _ctxproj(c4, a8c, s8c, wt_w[:sw])
    y1t = _s5(ut, uct, wall, cot, lam16, dcol)

    dw, db = conv_dw[0], conv_db[0]
    ct_h, s1_h, s2_h = _convt(vt_h, _conv_toeplitz(dw[:, :half], GRID_W), db[:half])
    ct_v, s1_v, s2_v = _convt(vt_v, _conv_toeplitz(dw[:, half:], rows), db[half:])
    to_v = lambda a: a.reshape(GRID_W, rows, BATCH).transpose(1, 0, 2).reshape(1, n)
    to_h = lambda a: a.reshape(rows, GRID_W, BATCH).transpose(1, 0, 2).reshape(1, n)
    pad4 = jnp.zeros((4, n), F32)
    stats_v = jnp.concatenate([s1_v.reshape(1, n), s2_v.reshape(1, n), to_v(s1_h), to_v(s2_h), pad4])
    stats_h = jnp.concatenate([s1_h.reshape(1, n), s2_h.reshape(1, n), to_h(s1_v), to_h(s2_v), pad4])

    ln_g, ln_b = conv_ln_g[0].reshape(cw, 1), conv_ln_b[0].reshape(cw, 1)
    wo = w_out.reshape(w_out.shape[1:]).astype(BF16)
    mix_h = _tail_w(y1t, g2t, ct_h, g1t_h, stats_h, ssm_glu_w[0].T.astype(BF16),
                    ssm_glu_b[0].reshape(sw, 1), wo[cw:], wo[:half], ln_g[:half], ln_b[:half], cw)
    return _final_r(ct_v, g1t_v, stats_v, mix_h.reshape(n, d), x, wo[half:cw], ln_g[half:],
                    ln_b[half:], gate8, final_g.reshape(1, d), cw)
```

```python
import functools

import jax
import jax.numpy as jnp
import numpy as np
from jax import lax
from jax.experimental import pallas as pl
from jax.experimental.pallas import tpu as pltpu

GRID_W = 64
CONV_TAPS = 31
CONV_PAD = CONV_TAPS // 2
SSM_GROUP = 16
SSM_STATE = 64
CHUNK = 16
BATCH = 8
EPS = 1e-6
LANES = 128
VMEM_LIMIT = 56 * 1024 * 1024

F32 = jnp.float32
BF16 = jnp.bfloat16


def _cparams(sem):
    return pltpu.CompilerParams(dimension_semantics=sem, vmem_limit_bytes=VMEM_LIMIT)


def _ada_kernel(c_ref, w_ref, b_ref, o_ref):
    s = jax.nn.silu(c_ref[...])
    o_ref[...] = jnp.dot(s, w_ref[...], preferred_element_type=F32,
                         precision=lax.Precision.HIGHEST) + b_ref[...]


def _ada(cond16, w_ada, b_ada):
    d, n3 = w_ada.shape
    nb = n3 // d
    return pl.pallas_call(
        _ada_kernel,
        grid=(nb,),
        in_specs=[pl.BlockSpec((16, d), lambda j: (0, 0)),
                  pl.BlockSpec((d, d), lambda j: (0, j)),
                  pl.BlockSpec((1, d), lambda j: (0, j))],
        out_specs=pl.BlockSpec((16, d), lambda j: (0, j)),
        out_shape=jax.ShapeDtypeStruct((16, n3), F32),
        compiler_params=_cparams(("arbitrary",)),
        name="ada",
    )(cond16, w_ada, b_ada.reshape(1, n3))


def _modulated_norm(x, a, s):
    ms = jnp.mean(x * x, axis=-1, keepdims=True)
    return x * lax.rsqrt(ms + EPS) * a[None] + s[None]


def _conv_gates(q, hc, vt_ref, g1t_ref):
    val = (q[:hc] * jax.nn.sigmoid(q[hc:2 * hc])).astype(BF16)
    for j in range(vt_ref.shape[0]):
        vt_ref[j] = val[:, j * LANES:(j + 1) * LANES]
    g1t_ref[...] = jax.nn.silu(q[2 * hc:]).astype(BF16)


def _inproj_w_kernel(x_ref, a_ref, s_ref, wt_ref, ut_ref, g2t_ref, vt_ref, g1t_ref):
    r, _, d = x_ref.shape
    sw = ut_ref.shape[0]
    hc = g1t_ref.shape[0]
    h = _modulated_norm(x_ref[...], a_ref[...], s_ref[...]).reshape(r * BATCH, d).astype(BF16)
    q = lax.dot_general(wt_ref[...], h, (((1,), (1,)), ((), ())), preferred_element_type=F32)
    ut_ref[...] = q[:sw].astype(BF16)
    g2t_ref[...] = jax.nn.silu(q[sw:2 * sw]).astype(BF16)
    _conv_gates(q[2 * sw:], hc, vt_ref, g1t_ref)


def _inproj_w(x4, a8, s8, wt, sw, hc):
    r, _, _, d = x4.shape
    tl = r * BATCH
    nq = tl // LANES
    cpr = GRID_W // CHUNK
    s5_spec = pl.BlockSpec((None, sw, tl), lambda w: (w % CHUNK, 0, w // CHUNK))
    return pl.pallas_call(
        _inproj_w_kernel,
        grid=(GRID_W,),
        in_specs=[pl.BlockSpec((r, None, BATCH, d), lambda w: (0, w, 0, 0)),
                  pl.BlockSpec((BATCH, d), lambda w: (0, 0)),
                  pl.BlockSpec((BATCH, d), lambda w: (0, 0)),
                  pl.BlockSpec(wt.shape, lambda w: (0, 0))],
        out_specs=[s5_spec, s5_spec,
                   pl.BlockSpec((None, nq, hc, LANES), lambda w: (w, 0, 0, 0)),
                   pl.BlockSpec((hc, tl), lambda w: (0, w))],
        out_shape=[jax.ShapeDtypeStruct((CHUNK, sw, cpr * tl), BF16),
                   jax.ShapeDtypeStruct((CHUNK, sw, cpr * tl), BF16),
                   jax.ShapeDtypeStruct((GRID_W, nq, hc, LANES), BF16),
                   jax.ShapeDtypeStruct((hc, GRID_W * tl), BF16)],
        compiler_params=_cparams(("arbitrary",)),
        name="inproj_w",
    )(x4, a8, s8, wt)


def _inproj_r_kernel(x_ref, a_ref, s_ref, wt_ref, vt_ref, g1t_ref):
    gw, _, d = x_ref.shape
    hc = g1t_ref.shape[0]
    h = _modulated_norm(x_ref[...], a_ref[...], s_ref[...]).reshape(gw * BATCH, d).astype(BF16)
    q = lax.dot_general(wt_ref[...], h, (((1,), (1,)), ((), ())), preferred_element_type=F32)
    _conv_gates(q, hc, vt_ref, g1t_ref)


def _inproj_r(x4, a8, s8, wt, hc):
    r, gw, _, d = x4.shape
    tl = gw * BATCH
    nq = tl // LANES
    return pl.pallas_call(
        _inproj_r_kernel,
        grid=(r,),
        in_specs=[pl.BlockSpec((None, gw, BATCH, d), lambda i: (i, 0, 0, 0)),
                  pl.BlockSpec((BATCH, d), lambda i: (0, 0)),
                  pl.BlockSpec((BATCH, d), lambda i: (0, 0)),
                  pl.BlockSpec(wt.shape, lambda i: (0, 0))],
        out_specs=[pl.BlockSpec((None, nq, hc, LANES), lambda i: (i, 0, 0, 0)),
                   pl.BlockSpec((hc, tl), lambda i: (0, i))],
        out_shape=[jax.ShapeDtypeStruct((r, nq, hc, LANES), BF16),
                   jax.ShapeDtypeStruct((hc, r * tl), BF16)],
        compiler_params=_cparams(("arbitrary",)),
        name="inproj_r",
    )(x4, a8, s8, wt)


def _ctxproj_kernel(x_ref, a_ref, s_ref, wut_ref, ut_ref):
    cb, _, d = x_ref.shape
    x = x_ref[...].reshape(cb * 2, BATCH, d)
    h = _modulated_norm(x, a_ref[...], s_ref[...]).reshape(cb, 2 * BATCH, d)
    for tl in range(2):
        ht = h[:, tl * BATCH:(tl + 1) * BATCH, :].reshape(cb * BATCH, d).astype(BF16)
        q = lax.dot_general(wut_ref[...], ht, (((1,), (1,)), ((), ())),
                            preferred_element_type=F32)
        ut_ref[tl] = q.astype(BF16)


def _ctxproj(x4, a8, s8, wut):
    nch, _, _, d = x4.shape
    sw = wut.shape[0]
    return pl.pallas_call(
        _ctxproj_kernel,
        grid=(CHUNK // 2,),
        in_specs=[pl.BlockSpec((nch, None, 2 * BATCH, d), lambda t: (0, t, 0, 0)),
                  pl.BlockSpec((BATCH, d), lambda t: (0, 0)),
                  pl.BlockSpec((BATCH, d), lambda t: (0, 0)),
                  pl.BlockSpec(wut.shape, lambda t: (0, 0))],
        out_specs=pl.BlockSpec((2, sw, nch * BATCH), lambda t: (t, 0, 0)),
        out_shape=jax.ShapeDtypeStruct((CHUNK, sw, nch * BATCH), BF16),
        compiler_params=_cparams(("arbitrary",)),
        name="ctxproj",
    )(x4, a8, s8, wut)


def _s5_kernel(xt_ref, xct_ref, wall_ref, cot_ref, lam_ref, dcol_ref, y_ref,
               s_sc, h_sc, yi_sc, *, lane_chunk):
    nl = xt_ref.shape[-1]
    nlc = xct_ref.shape[-1]
    n_ch = nl // BATCH
    n_chc = nlc // BATCH
    kdim = CHUNK * SSM_GROUP
    p = SSM_STATE

    def group_x(ref, j):
        return ref[:, j * SSM_GROUP:(j + 1) * SSM_GROUP, :].reshape(kdim, ref.shape[-1])

    st_c = [jnp.dot(wall_ref[j, kdim:], group_x(xct_ref, j), preferred_element_type=F32)
            for j in range(2)]
    st = []
    for j in range(2):
        r = jnp.dot(wall_ref[j], group_x(xt_ref, j), preferred_element_type=F32)
        yi_sc[j] = r[:kdim]
        st.append(r[kdim:])
    for d in range(2):
        for src, off, width in ((st_c, 0, nlc), (st, nlc, nl)):
            for ri in range(2):
                lo = (2 * d + ri) * p
                pair = jnp.concatenate([src[0][lo:lo + p], src[1][lo:lo + p]], axis=0)
                s_sc[d, ri, off:off + width, :] = pair.T

    lam = [[jnp.broadcast_to(lam_ref[d, ri:ri + 1, :], (BATCH, LANES)) for ri in range(2)]
           for d in range(2)]

    def advance(d, state, row):
        hr, hi = state
        ar, ai = lam[d]
        sr = s_sc[d, 0, pl.ds(row, BATCH), :]
        si = s_sc[d, 1, pl.ds(row, BATCH), :]
        return ar * hr - ai * hi + sr, ar * hi + ai * hr + si

    zero = jnp.zeros((BATCH, LANES), F32)

    def ctx_step(i, carry):
        f, b = carry
        f = advance(0, f, pl.multiple_of(i * BATCH, BATCH))
        b = advance(1, b, pl.multiple_of((n_chc - 1 - i) * BATCH, BATCH))
        return f, b

    carry = lax.fori_loop(0, n_chc, ctx_step, ((zero, zero), (zero, zero)))

    cpr = GRID_W // CHUNK
    n_rows = n_ch // cpr

    def lane_row(ch):
        within = jnp.bitwise_and(ch, cpr - 1)
        row = lax.shift_right_logical(ch, cpr.bit_length() - 1)
        return pl.multiple_of((within * n_rows + row) * BATCH, BATCH)

    def lat_step(i, carry):
        f, b = carry
        rf = lane_row(i)
        rb = lane_row(n_ch - 1 - i)
        h_sc[0, 0, pl.ds(rf, BATCH), :] = f[0]
        h_sc[0, 1, pl.ds(rf, BATCH), :] = f[1]
        h_sc[1, 0, pl.ds(rb, BATCH), :] = b[0]
        h_sc[1, 1, pl.ds(rb, BATCH), :] = b[1]
        f = advance(0, f, pl.multiple_of(nlc + rf, BATCH))
        b = advance(1, b, pl.multiple_of(nlc + rb, BATCH))
        return f, b

    lax.fori_loop(0, n_ch, lat_step, carry, unroll=8)

    for c0 in range(0, nl, lane_chunk):
        cs = slice(c0, c0 + lane_chunk)
        ht = [[h_sc[d, ri, cs, :].T for ri in range(2)] for d in range(2)]
        for j in range(2):
            x = xt_ref[:, j * SSM_GROUP:(j + 1) * SSM_GROUP, cs].reshape(kdim, lane_chunk)
            hj = jnp.concatenate([ht[d][ri][j * p:(j + 1) * p] for d in range(2) for ri in range(2)],
                                 axis=0).astype(BF16)
            acc = (yi_sc[j, :, cs] + dcol_ref[j] * x.astype(F32)
                   + jnp.dot(cot_ref[j], hj, preferred_element_type=F32))
            y = jax.nn.gelu(acc).astype(BF16)
            y_ref[:, j * SSM_GROUP:(j + 1) * SSM_GROUP, cs] = y.reshape(CHUNK, SSM_GROUP, lane_chunk)


def _s5(ut, uct, wall, cot, lam16, dcol):
    _, sw, nl = ut.shape
    nlc = uct.shape[-1]
    gp = sw // (2 * SSM_GROUP)
    kdim = CHUNK * SSM_GROUP
    lane_chunk = min(nl, 512)
    kern = functools.partial(_s5_kernel, lane_chunk=lane_chunk)
    return pl.pallas_call(
        kern,
        grid=(gp,),
        in_specs=[pl.BlockSpec((CHUNK, 2 * SSM_GROUP, nl), lambda g: (0, g, 0)),
                  pl.BlockSpec((CHUNK, 2 * SSM_GROUP, nlc), lambda g: (0, g, 0)),
                  pl.BlockSpec((2, 2 * kdim, kdim), lambda g: (g, 0, 0)),
                  pl.BlockSpec((2, kdim, kdim), lambda g: (g, 0, 0)),
                  pl.BlockSpec((2, None, 2, LANES), lambda g: (0, g, 0, 0)),
                  pl.BlockSpec((2, kdim, 1), lambda g: (g, 0, 0))],
        out_specs=pl.BlockSpec((CHUNK, 2 * SSM_GROUP, nl), lambda g: (0, g, 0)),
        out_shape=jax.ShapeDtypeStruct((CHUNK, sw, nl), BF16),
        scratch_shapes=[pltpu.VMEM((2, 2, nlc + nl, LANES), F32),
                        pltpu.VMEM((2, 2, nl, LANES), F32),
                        pltpu.VMEM((2, kdim, nl), F32)],
        compiler_params=_cparams(("arbitrary",)),
        name="s5",
    )(ut, uct, wall, cot, lam16, dcol)


CONV_CB = 64
CONV_GROUP = 4
CONV_SKEW = 8


def _convt_kernel(v_ref, tq_ref, b_ref, o_ref, s1_ref, s2_ref, z_sc, o_sc):
    a, cb, _ = v_ref.shape
    pitch = cb + CONV_SKEW
    z = v_ref[...].astype(F32)
    for i in range(a):
        z_sc[i * pitch:i * pitch + cb, :] = z[i]
    s1 = jnp.zeros((a, LANES), F32)
    s2 = jnp.zeros((a, LANES), F32)
    for g in range(cb // CONV_GROUP):
        c0 = g * CONV_GROUP
        rhs = jnp.concatenate([z_sc[pl.ds(c0 + j, a, stride=pitch), :] for j in range(CONV_GROUP)],
                              axis=0).astype(BF16)
        out = jnp.dot(tq_ref[g], rhs, preferred_element_type=F32).reshape(CONV_GROUP, a, LANES)
        out = out + b_ref[c0:c0 + CONV_GROUP].reshape(CONV_GROUP, 1, 1)
        for j in range(CONV_GROUP):
            o_sc[pl.ds(c0 + j, a, stride=pitch), :] = out[j]
        s1 = s1 + jnp.sum(out, axis=0)
        s2 = s2 + jnp.sum(out * out, axis=0)
    for i in range(a):
        o_ref[i] = o_sc[i * pitch:i * pitch + cb, :].astype(o_ref.dtype)

    @pl.when(pl.program_id(1) == 0)
    def _():
        s1_ref[...] = jnp.zeros_like(s1_ref)
        s2_ref[...] = jnp.zeros_like(s2_ref)

    s1_ref[...] += s1
    s2_ref[...] += s2


def _convt(vt, tq, bias):
    a, nq, c, _ = vt.shape
    cb = min(c, CONV_CB)
    ka = CONV_GROUP * a
    stat_spec = pl.BlockSpec((a, LANES), lambda q, i: (0, q))
    return pl.pallas_call(
        _convt_kernel,
        grid=(nq, c // cb),
        in_specs=[pl.BlockSpec((a, None, cb, LANES), lambda q, i: (0, q, i, 0)),
                  pl.BlockSpec((cb // CONV_GROUP, ka, ka), lambda q, i: (i, 0, 0)),
                  pl.BlockSpec((cb, 1), lambda q, i: (i, 0))],
        out_specs=[pl.BlockSpec((a, cb, LANES), lambda q, i: (0, i, q)), stat_spec, stat_spec],
        out_shape=[jax.ShapeDtypeStruct((a, c, nq * LANES), BF16),
                   jax.ShapeDtypeStruct((a, nq * LANES), F32),
                   jax.ShapeDtypeStruct((a, nq * LANES), F32)],
        scratch_shapes=[pltpu.VMEM((a * (cb + CONV_SKEW), LANES), F32),
                        pltpu.VMEM((a * (cb + CONV_SKEW), LANES), F32)],
        compiler_params=_cparams(("arbitrary", "arbitrary")),
        name="convt",
    )(vt, tq, bias.reshape(c, 1))


@functools.lru_cache(maxsize=None)
def _toeplitz_selector(a):
    g = CONV_GROUP
    pos = np.arange(a)
    tap = pos[None, :] - pos[:, None] + CONV_PAD
    hit = tap[None] == np.arange(CONV_TAPS)[:, None, None]
    idx = np.arange(g)
    diag = (idx[:, None, None] == idx[None, :, None]) & (idx[None, :, None] == idx[None, None, :])
    sel = hit[:, None, None, :, None, :] & diag[None, :, :, None, :, None]
    return sel.reshape(CONV_TAPS * g, g * a, g * a).astype(np.float32)


def _conv_toeplitz(w, a):
    c = w.shape[1]
    g = CONV_GROUP
    w4 = w.reshape(CONV_TAPS, c // g, g).transpose(1, 0, 2).reshape(c // g, CONV_TAPS * g)
    sel = jnp.asarray(_toeplitz_selector(a), dtype=BF16)
    tq = jnp.einsum("qk,kxy->qxy", w4.astype(BF16), sel, preferred_element_type=F32)
    return tq.astype(BF16)


def _conv_branch_t(cv_ref, g1_ref, st_ref, lng_ref, lnb_ref, n_chan):
    st = st_ref[...]
    mean = (st[0:1] + st[2:3]) * (1.0 / n_chan)
    var = (st[1:2] + st[3:4]) * (1.0 / n_chan) - mean * mean
    y = (cv_ref[...].astype(F32) - mean) * lax.rsqrt(var + EPS) * lng_ref[...] + lnb_ref[...]
    return (jax.nn.silu(y) * g1_ref[...].astype(F32)).astype(BF16)


def _tail_w_kernel(y_ref, g2_ref, cv_ref, g1_ref, st_ref, wt_ref, bt_ref, wos_ref, woh_ref,
                   lng_ref, lnb_ref, o_ref, *, n_chan):
    y = y_ref[...]
    z = jnp.dot(wt_ref[...], y, preferred_element_type=F32) + bt_ref[...]
    s = (y.astype(F32) * jax.nn.sigmoid(z) * g2_ref[...].astype(F32)).astype(BF16)
    co = _conv_branch_t(cv_ref, g1_ref, st_ref, lng_ref, lnb_ref, n_chan)
    tn = (((0,), (0,)), ((), ()))
    mix = (lax.dot_general(s, wos_ref[...], tn, preferred_element_type=F32)
           + lax.dot_general(co, woh_ref[...], tn, preferred_element_type=F32))
    o_ref[...] = mix.reshape(o_ref.shape)


def _tail_w(y1t, g2t, ct_h, g1t_h, stats_h, glu_wt, glu_bcol, wo_s, wo_h, ln_g, ln_b, n_chan):
    _, sw, _ = y1t.shape
    _, hc, tl = ct_h.shape
    d = wo_s.shape[1]
    r = tl // BATCH
    s5_spec = pl.BlockSpec((None, sw, tl), lambda w: (w % CHUNK, 0, w // CHUNK))
    tok = lambda rows: pl.BlockSpec((rows, tl), lambda w: (0, w))
    full = lambda arr: pl.BlockSpec(arr.shape, lambda w: (0,) * arr.ndim)
    return pl.pallas_call(
        functools.partial(_tail_w_kernel, n_chan=n_chan),
        grid=(GRID_W,),
        in_specs=[s5_spec, s5_spec, pl.BlockSpec((None, hc, tl), lambda w: (w, 0, 0)), tok(hc),
                  tok(8), full(glu_wt), full(glu_bcol), full(wo_s), full(wo_h), full(ln_g),
                  full(ln_b)],
        out_specs=pl.BlockSpec((r, None, BATCH, d), lambda w: (0, w, 0, 0)),
        out_shape=jax.ShapeDtypeStruct((r, GRID_W, BATCH, d), F32),
        compiler_params=_cparams(("arbitrary",)),
        name="tail_w",
    )(y1t, g2t, ct_h, g1t_h, stats_h, glu_wt, glu_bcol, wo_s, wo_h, ln_g, ln_b)


def _final_r_kernel(cv_ref, g1_ref, st_ref, mh_ref, x_ref, wo_ref, lng_ref, lnb_ref, gate_ref,
                    fg_ref, o_ref, mix_sc, *, n_chan):
    tl, d = x_ref.shape[1:]
    co = _conv_branch_t(cv_ref, g1_ref, st_ref, lng_ref, lnb_ref, n_chan)
    mix = lax.dot_general(co, wo_ref[...], (((0,), (0,)), ((), ())),
                          preferred_element_type=F32) + mh_ref[...].reshape(tl * BATCH, d)
    nlt = mix_sc.shape[0]
    for j in range(nlt):
        mix_sc[j] = mix[:, j * LANES:(j + 1) * LANES]
    for b in range(BATCH):
        mb = jnp.concatenate([mix_sc[j, pl.ds(b, tl, stride=BATCH), :] for j in range(nlt)],
                             axis=-1)
        xo = x_ref[b] + gate_ref[b:b + 1, :] * mb
        ms = jnp.mean(xo * xo, axis=-1, keepdims=True)
        o_ref[b] = xo * lax.rsqrt(ms + EPS) * fg_ref[...]


def _final_r(ct_v, g1t_v, stats_v, mix_h, x, wo_v, ln_g, ln_b, gate8, final_g, n_chan):
    r, hc, tm = ct_v.shape
    _, length, d = x.shape
    tl = tm // BATCH
    tok = lambda rows: pl.BlockSpec((rows, tm), lambda i: (0, i))
    nat = pl.BlockSpec((BATCH, tl, d), lambda i: (0, i, 0))
    full = lambda arr: pl.BlockSpec(arr.shape, lambda i: (0,) * arr.ndim)
    return pl.pallas_call(
        functools.partial(_final_r_kernel, n_chan=n_chan),
        grid=(r,),
        in_specs=[pl.BlockSpec((None, hc, tm), lambda i: (i, 0, 0)), tok(hc), tok(8),
                  pl.BlockSpec((None, tl, BATCH, d), lambda i: (i, 0, 0, 0)), nat,
                  full(wo_v), full(ln_g), full(ln_b), full(gate8), full(final_g)],
        out_specs=nat,
        out_shape=jax.ShapeDtypeStruct(x.shape, F32),
        scratch_shapes=[pltpu.VMEM((d // LANES, tm, LANES), F32)],
        compiler_params=_cparams(("arbitrary",)),
        name="final_r",
    )(ct_v, g1t_v, stats_v, mix_h, x, wo_v, ln_g, ln_b, gate8, final_g)


def _s5prep_kernel(arow_ref, bt_ref, ct_ref, wall_ref, cot_ref, lam_ref, pw_sc, kk_sc):
    kdim = CHUNK * SSM_GROUP
    hp = lax.Precision.HIGHEST
    lane = lax.broadcasted_iota(jnp.int32, (1, LANES), 1)
    is_re = lane < SSM_STATE
    col_t = lax.broadcasted_iota(jnp.int32, (1, kdim), 1) // SSM_GROUP
    kcol = lax.broadcasted_iota(jnp.int32, (pw_sc.shape[1], 1), 0).astype(F32)

    def tile_rows(a):
        return jnp.concatenate([a] * CHUNK, axis=0)

    def power_rows(exps):
        pr = jnp.concatenate([jnp.broadcast_to(pw_sc[0, e:e + 1, :], (SSM_GROUP, LANES))
                              for e in exps], axis=0)
        pi = jnp.concatenate([jnp.broadcast_to(pw_sc[1, e:e + 1, :], (SSM_GROUP, LANES))
                              for e in exps], axis=0)
        return pr, pi

    def cmul(xr, xi, yr, yi, im_sign):
        return jnp.where(is_re, xr * yr - xi * yi, im_sign * (xr * yi + xi * yr))

    toeplitz = jnp.zeros((kdim, kdim), F32)
    for d in range(2):
        ar, ai = arow_ref[d, 0:1, :], arow_ref[d, 1:2, :]
        dt = jnp.exp(arow_ref[d, 2:3, :])
        mag = jnp.exp(kcol * (ar * dt))
        ang = kcol * (ai * dt)
        pw_sc[0] = mag * jnp.cos(ang)
        pw_sc[1] = mag * jnp.sin(ang)
        lam_ref[d, 0] = pw_sc[0, CHUNK:CHUNK + 8, :]
        lam_ref[d, 1] = pw_sc[1, CHUNK:CHUNK + 8, :]
        nr, ni = pw_sc[0, 1:2, :] - 1.0, pw_sc[1, 1:2, :]
        den = ar * ar + ai * ai
        qr, qi = (nr * ar + ni * ai) / den, (ni * ar - nr * ai) / den
        btr, bti = bt_ref[d, 0], bt_ref[d, 1]
        bbr, bbi = tile_rows(qr * btr - qi * bti), tile_rows(qr * bti + qi * btr)
        cr, ci = tile_rows(ct_ref[d, 0]), tile_rows(ct_ref[d, 1])
        tt = list(range(CHUNK))
        e_lag = [CHUNK - 1 - t for t in tt] if d else tt
        e_in = tt if d else [CHUNK - 1 - t for t in tt]
        e_out = [CHUNK - t for t in tt] if d else [t + 1 for t in tt]
        cot_ref[:, d * LANES:(d + 1) * LANES] = cmul(cr, ci, *power_rows(e_out), -1.0).astype(BF16)
        wall_ref[kdim + d * LANES:kdim + (d + 1) * LANES, :] = cmul(
            bbr, bbi, *power_rows(e_in), 1.0).T.astype(BF16)
        c_lag = cmul(cr, ci, *power_rows(e_lag), -1.0)
        bb = jnp.where(is_re, bbr, bbi)
        kk = lax.dot_general(c_lag, bb, (((1,), (1,)), ((), ())), precision=hp,
                             preferred_element_type=F32)
        zeros = jnp.zeros((kdim, kdim), F32)
        if d:
            kk_sc[:kdim] = kk
            kk_sc[kdim:] = zeros
            starts = [SSM_GROUP * (CHUNK - 1 - t) for t in tt]
        else:
            kk_sc[:kdim] = zeros
            kk_sc[kdim:] = kk
            starts = [kdim - SSM_GROUP * t for t in tt]
        acc = zeros
        for t in tt:
            acc = jnp.where(col_t == t, kk_sc[starts[t]:starts[t] + kdim, :], acc)
        toeplitz = toeplitz + acc
    wall_ref[:kdim, :] = toeplitz.astype(BF16)


def _s5prep(arow, bt, ct):
    g = arow.shape[0]
    kdim = CHUNK * SSM_GROUP
    return pl.pallas_call(
        _s5prep_kernel,
        grid=(g,),
        in_specs=[pl.BlockSpec((None, 2, 8, LANES), lambda i: (i, 0, 0, 0)),
                  pl.BlockSpec((None, 2, 2, SSM_GROUP, LANES), lambda i: (i, 0, 0, 0, 0)),
                  pl.BlockSpec((None, 2, 2, SSM_GROUP, LANES), lambda i: (i, 0, 0, 0, 0))],
        out_specs=[pl.BlockSpec((None, 2 * kdim, kdim), lambda i: (i, 0, 0)),
                   pl.BlockSpec((None, kdim, kdim), lambda i: (i, 0, 0)),
                   pl.BlockSpec((None, 2, 2, 8, LANES), lambda i: (i, 0, 0, 0, 0))],
        out_shape=[jax.ShapeDtypeStruct((g, 2 * kdim, kdim), BF16),
                   jax.ShapeDtypeStruct((g, kdim, kdim), BF16),
                   jax.ShapeDtypeStruct((g, 2, 2, 8, LANES), F32)],
        scratch_shapes=[pltpu.VMEM((2, CHUNK + 8, LANES), F32),
                        pltpu.VMEM((2 * kdim, kdim), F32)],
        compiler_params=_cparams(("arbitrary",)),
        name="s5prep",
    )(arow, bt, ct)


def _s5prep_inputs(a_re, a_im, log_dt, b_re, b_im, c_re, c_im):
    g = a_re.shape[1]
    dup = lambda a: jnp.concatenate([a, a], axis=-1)
    rows = jnp.stack([dup(a_re), dup(a_im),
                      jnp.broadcast_to(log_dt[..., None], (2, g, LANES))], axis=2)
    arow = jnp.pad(rows, ((0, 0), (0, 0), (0, 5), (0, 0))).transpose(1, 0, 2, 3)
    bt = jnp.stack([dup(b_re.transpose(0, 1, 3, 2)), dup(b_im.transpose(0, 1, 3, 2))], axis=2)
    ct = jnp.stack([dup(c_re), dup(c_im)], axis=2)
    return arow, bt.transpose(1, 0, 2, 3, 4), ct.transpose(1, 0, 2, 3, 4)


def kernel(x, c, ctx, c_ctx, norm_g, w_ada, b_ada, w_in, conv_dw, conv_db, conv_ln_g, conv_ln_b,
           ssm_a_re, ssm_a_im, ssm_log_dt, ssm_b_re, ssm_b_im, ssm_c_re, ssm_c_im, ssm_d,
           ssm_glu_w, ssm_glu_b, w_out, final_g):
    bsz, length, d = x.shape
    ctx_len = ctx.shape[1]
    cw = conv_dw.shape[-1]
    sw = ssm_d.shape[-1]
    assert bsz == BATCH and norm_g.shape[0] == 1
    assert length % (GRID_W * CHUNK) == 0 and ctx_len % CHUNK == 0
    n = length * BATCH
    nch = length // CHUNK
    nchc = ctx_len // CHUNK
    rows = length // GRID_W

    cond16 = jnp.zeros((16, d), F32).at[:BATCH].set(c).at[BATCH].set(c_ctx)
    mod = _ada(cond16, w_ada.reshape(w_ada.shape[1:]), b_ada[0])
    shift, scale, gate = mod[:, :d], mod[:, d:2 * d], mod[:, 2 * d:]
    amp = norm_g[0][None] * (1.0 + scale)
    a8, s8, gate8 = amp[:BATCH], shift[:BATCH], gate[:BATCH]
    a8c = jnp.broadcast_to(amp[BATCH], (BATCH, d))
    s8c = jnp.broadcast_to(shift[BATCH], (BATCH, d))

    w = w_in.reshape(w_in.shape[1:])
    half = cw // 2
    cols = lambda lo, width: w[:, lo:lo + width]
    wt_w = jnp.concatenate([cols(3 * cw, 2 * sw), cols(0, half), cols(cw, half), cols(2 * cw, half)],
                           axis=1).T.astype(BF16)
    wt_r = jnp.concatenate([cols(half, half), cols(cw + half, half), cols(2 * cw + half, half)],
                           axis=1).T.astype(BF16)
    wall, cot, lam_raw = _s5prep(*_s5prep_inputs(
        ssm_a_re[0], ssm_a_im[0], ssm_log_dt[0], ssm_b_re[0], ssm_b_im[0], ssm_c_re[0], ssm_c_im[0]))
    ng = sw // SSM_GROUP
    lam16 = lam_raw[:, :, :, 0, :SSM_STATE].reshape(ng // 2, 2, 2, 2, SSM_STATE).transpose(
        2, 0, 3, 1, 4).reshape(2, ng // 2, 2, LANES)
    dcol = jnp.broadcast_to(ssm_d[0].reshape(ng, 1, SSM_GROUP), (ng, CHUNK, SSM_GROUP)).reshape(
        ng, CHUNK * SSM_GROUP, 1)

    xt = x.transpose(1, 0, 2)
    ctxt = ctx.transpose(1, 0, 2)
    c4 = ctxt.reshape(nchc, CHUNK // 2, 2 * BATCH, d)

    x4 = xt.reshape(rows, GRID_W, BATCH, d)
    ut, g2t, vt_h, g1t_h = _inproj_w(x4, a8, s8, wt_w, sw, half)
    vt_v, g1t_v = _inproj_r(x4, a8, s8, wt_r, half)
    uct = _ctxproj(c4, a8c, s8c, wt_w[:sw])
    y1t = _s5(ut, uct, wall, cot, lam16, dcol)

    dw, db = conv_dw[0], conv_db[0]
    ct_h, s1_h, s2_h = _convt(vt_h, _conv_toeplitz(dw[:, :half], GRID_W), db[:half])
    ct_v, s1_v, s2_v = _convt(vt_v, _conv_toeplitz(dw[:, half:], rows), db[half:])
    to_v = lambda a: a.reshape(GRID_W, rows, BATCH).transpose(1, 0, 2).reshape(1, n)
    to_h = lambda a: a.reshape(rows, GRID_W, BATCH).transpose(1, 0, 2).reshape(1, n)
    pad4 = jnp.zeros((4, n), F32)
    stats_v = jnp.concatenate([s1_v.reshape(1, n), s2_v.reshape(1, n), to_v(s1_h), to_v(s2_h), pad4])
    stats_h = jnp.concatenate([s1_h.reshape(1, n), s2_h.reshape(1, n), to_h(s1_v), to_h(s2_v), pad4])

    ln_g, ln_b = conv_ln_g[0].reshape(cw, 1), conv_ln_b[0].reshape(cw, 1)
    wo = w_out.reshape(w_out.shape[1:]).astype(BF16)
    mix_h = _tail_w(y1t, g2t, ct_h, g1t_h, stats_h, ssm_glu_w[0].T.astype(BF16),
                    ssm_glu_b[0].reshape(sw, 1), wo[cw:], wo[:half], ln_g[:half], ln_b[:half], cw)
    return _final_r(ct_v, g1t_v, stats_v, mix_h, x, wo[half:cw], ln_g[half:],
                    ln_b[half:], gate8, final_g.reshape(1, d), cw)
```

```python
import functools

import jax
import jax.numpy as jnp
import numpy as np
from jax import lax
from jax.experimental import pallas as pl
from jax.experimental.pallas import tpu as pltpu

GRID_W = 64
CONV_TAPS = 31
CONV_PAD = CONV_TAPS // 2
SSM_GROUP = 16
SSM_STATE = 64
CHUNK = 16
BATCH = 8
EPS = 1e-6
LANES = 128
VMEM_LIMIT = 56 * 1024 * 1024

F32 = jnp.float32
BF16 = jnp.bfloat16


def _cparams(sem):
    return pltpu.CompilerParams(dimension_semantics=sem, vmem_limit_bytes=VMEM_LIMIT)


def _ada_kernel(c_ref, w_ref, b_ref, o_ref):
    s = jax.nn.silu(c_ref[...])
    o_ref[...] = jnp.dot(s, w_ref[...], preferred_element_type=F32,
                         precision=lax.Precision.HIGHEST) + b_ref[...]


def _ada(cond16, w_ada, b_ada):
    d, n3 = w_ada.shape
    nb = n3 // d
    return pl.pallas_call(
        _ada_kernel,
        grid=(nb,),
        in_specs=[pl.BlockSpec((16, d), lambda j: (0, 0)),
                  pl.BlockSpec((d, d), lambda j: (0, j)),
                  pl.BlockSpec((1, d), lambda j: (0, j))],
        out_specs=pl.BlockSpec((16, d), lambda j: (0, j)),
        out_shape=jax.ShapeDtypeStruct((16, n3), F32),
        compiler_params=_cparams(("arbitrary",)),
        name="ada",
    )(cond16, w_ada, b_ada.reshape(1, n3))


def _modulated_norm(x, a, s):
    ms = jnp.mean(x * x, axis=-1, keepdims=True)
    return x * lax.rsqrt(ms + EPS) * a[None] + s[None]


def _conv_gates(q, hc, vt_ref, g1t_ref):
    val = (q[:hc] * jax.nn.sigmoid(q[hc:2 * hc])).astype(BF16)
    for j in range(vt_ref.shape[0]):
        vt_ref[j] = val[:, j * LANES:(j + 1) * LANES]
    g1t_ref[...] = jax.nn.silu(q[2 * hc:]).astype(BF16)


def _project_t(w_refs, h):
    nt = (((1,), (1,)), ((), ()))
    return jnp.concatenate([lax.dot_general(w[...], h, nt, preferred_element_type=F32)
                            for w in w_refs], axis=0)


def _inproj_w_kernel(x_ref, a_ref, s_ref, ws_ref, wv_ref, wg_ref, wa_ref, ut_ref, g2t_ref, vt_ref,
                     g1t_ref):
    r, _, d = x_ref.shape
    sw = ut_ref.shape[0]
    hc = g1t_ref.shape[0]
    h = _modulated_norm(x_ref[...], a_ref[...], s_ref[...]).reshape(r * BATCH, d).astype(BF16)
    q = _project_t([ws_ref], h)
    ut_ref[...] = q[:sw].astype(BF16)
    g2t_ref[...] = jax.nn.silu(q[sw:]).astype(BF16)
    _conv_gates(_project_t([wv_ref, wg_ref, wa_ref], h), hc, vt_ref, g1t_ref)


def _conv_weight_specs(hc, d, part, index_map):
    return [pl.BlockSpec((hc, d), functools.partial(index_map, 2 * k + part)) for k in range(3)]


def _inproj_w(x4, a8, s8, wt_s, wt_c, hc):
    r, _, _, d = x4.shape
    sw = wt_s.shape[0] // 2
    tl = r * BATCH
    nq = tl // LANES
    cpr = GRID_W // CHUNK
    s5_spec = pl.BlockSpec((None, sw, tl), lambda w: (w % CHUNK, 0, w // CHUNK))
    return pl.pallas_call(
        _inproj_w_kernel,
        grid=(GRID_W,),
        in_specs=[pl.BlockSpec((r, None, BATCH, d), lambda w: (0, w, 0, 0)),
                  pl.BlockSpec((BATCH, d), lambda w: (0, 0)),
                  pl.BlockSpec((BATCH, d), lambda w: (0, 0)),
                  pl.BlockSpec(wt_s.shape, lambda w: (0, 0)),
                  *_conv_weight_specs(hc, d, 0, lambda blk, w: (blk, 0))],
        out_specs=[s5_spec, s5_spec,
                   pl.BlockSpec((None, nq, hc, LANES), lambda w: (w, 0, 0, 0)),
                   pl.BlockSpec((hc, tl), lambda w: (0, w))],
        out_shape=[jax.ShapeDtypeStruct((CHUNK, sw, cpr * tl), BF16),
                   jax.ShapeDtypeStruct((CHUNK, sw, cpr * tl), BF16),
                   jax.ShapeDtypeStruct((GRID_W, nq, hc, LANES), BF16),
                   jax.ShapeDtypeStruct((hc, GRID_W * tl), BF16)],
        compiler_params=_cparams(("arbitrary",)),
        name="inproj_w",
    )(x4, a8, s8, wt_s, wt_c, wt_c, wt_c)


def _inproj_r_kernel(x_ref, a_ref, s_ref, wv_ref, wg_ref, wa_ref, vt_ref, g1t_ref):
    gw, _, d = x_ref.shape
    hc = g1t_ref.shape[0]
    h = _modulated_norm(x_ref[...], a_ref[...], s_ref[...]).reshape(gw * BATCH, d).astype(BF16)
    _conv_gates(_project_t([wv_ref, wg_ref, wa_ref], h), hc, vt_ref, g1t_ref)


def _inproj_r(x4, a8, s8, wt_c, hc):
    r, gw, _, d = x4.shape
    tl = gw * BATCH
    nq = tl // LANES
    return pl.pallas_call(
        _inproj_r_kernel,
        grid=(r,),
        in_specs=[pl.BlockSpec((None, gw, BATCH, d), lambda i: (i, 0, 0, 0)),
                  pl.BlockSpec((BATCH, d), lambda i: (0, 0)),
                  pl.BlockSpec((BATCH, d), lambda i: (0, 0)),
                  *_conv_weight_specs(hc, d, 1, lambda blk, i: (blk, 0))],
        out_specs=[pl.BlockSpec((None, nq, hc, LANES), lambda i: (i, 0, 0, 0)),
                   pl.BlockSpec((hc, tl), lambda i: (0, i))],
        out_shape=[jax.ShapeDtypeStruct((r, nq, hc, LANES), BF16),
                   jax.ShapeDtypeStruct((hc, r * tl), BF16)],
        compiler_params=_cparams(("arbitrary",)),
        name="inproj_r",
    )(x4, a8, s8, wt_c, wt_c, wt_c)


def _ctxproj_kernel(x_ref, a_ref, s_ref, wut_ref, ut_ref):
    cb, _, d = x_ref.shape
    x = x_ref[...].reshape(cb * 2, BATCH, d)
    h = _modulated_norm(x, a_ref[...], s_ref[...]).reshape(cb, 2 * BATCH, d)
    for tl in range(2):
        ht = h[:, tl * BATCH:(tl + 1) * BATCH, :].reshape(cb * BATCH, d).astype(BF16)
        q = lax.dot_general(wut_ref[...], ht, (((1,), (1,)), ((), ())),
                            preferred_element_type=F32)
        ut_ref[tl] = q.astype(BF16)


def _ctxproj(x4, a8, s8, wut):
    nch, _, _, d = x4.shape
    sw = wut.shape[0]
    return pl.pallas_call(
        _ctxproj_kernel,
        grid=(CHUNK // 2,),
        in_specs=[pl.BlockSpec((nch, None, 2 * BATCH, d), lambda t: (0, t, 0, 0)),
                  pl.BlockSpec((BATCH, d), lambda t: (0, 0)),
                  pl.BlockSpec((BATCH, d), lambda t: (0, 0)),
                  pl.BlockSpec(wut.shape, lambda t: (0, 0))],
        out_specs=pl.BlockSpec((2, sw, nch * BATCH), lambda t: (t, 0, 0)),
        out_shape=jax.ShapeDtypeStruct((CHUNK, sw, nch * BATCH), BF16),
        compiler_params=_cparams(("arbitrary",)),
        name="ctxproj",
    )(x4, a8, s8, wut)


def _s5_kernel(xt_ref, xct_ref, wall_ref, cot_ref, lam_ref, dcol_ref, y_ref,
               s_sc, h_sc, yi_sc, *, lane_chunk):
    nl = xt_ref.shape[-1]
    nlc = xct_ref.shape[-1]
    n_ch = nl // BATCH
    n_chc = nlc // BATCH
    kdim = CHUNK * SSM_GROUP
    p = SSM_STATE

    def group_x(ref, j):
        return ref[:, j * SSM_GROUP:(j + 1) * SSM_GROUP, :].reshape(kdim, ref.shape[-1])

    st_c = [jnp.dot(wall_ref[j, kdim:], group_x(xct_ref, j), preferred_element_type=F32)
            for j in range(2)]
    st = []
    for j in range(2):
        r = jnp.dot(wall_ref[j], group_x(xt_ref, j), preferred_element_type=F32)
        yi_sc[j] = r[:kdim]
        st.append(r[kdim:])
    for d in range(2):
        for src, off, width in ((st_c, 0, nlc), (st, nlc, nl)):
            for ri in range(2):
                lo = (2 * d + ri) * p
                pair = jnp.concatenate([src[0][lo:lo + p], src[1][lo:lo + p]], axis=0)
                s_sc[d, ri, off:off + width, :] = pair.T

    lam = [[jnp.broadcast_to(lam_ref[d, ri:ri + 1, :], (BATCH, LANES)) for ri in range(2)]
           for d in range(2)]

    def advance(d, state, row):
        hr, hi = state
        ar, ai = lam[d]
        sr = s_sc[d, 0, pl.ds(row, BATCH), :]
        si = s_sc[d, 1, pl.ds(row, BATCH), :]
        return ar * hr - ai * hi + sr, ar * hi + ai * hr + si

    zero = jnp.zeros((BATCH, LANES), F32)

    def ctx_step(i, carry):
        f, b = carry
        f = advance(0, f, pl.multiple_of(i * BATCH, BATCH))
        b = advance(1, b, pl.multiple_of((n_chc - 1 - i) * BATCH, BATCH))
        return f, b

    carry = lax.fori_loop(0, n_chc, ctx_step, ((zero, zero), (zero, zero)))

    cpr = GRID_W // CHUNK
    n_rows = n_ch // cpr

    def lane_row(ch):
        within = jnp.bitwise_and(ch, cpr - 1)
        row = lax.shift_right_logical(ch, cpr.bit_length() - 1)
        return pl.multiple_of((within * n_rows + row) * BATCH, BATCH)

    def lat_step(i, carry):
        f, b = carry
        rf = lane_row(i)
        rb = lane_row(n_ch - 1 - i)
        h_sc[0, 0, pl.ds(rf, BATCH), :] = f[0]
        h_sc[0, 1, pl.ds(rf, BATCH), :] = f[1]
        h_sc[1, 0, pl.ds(rb, BATCH), :] = b[0]
        h_sc[1, 1, pl.ds(rb, BATCH), :] = b[1]
        f = advance(0, f, pl.multiple_of(nlc + rf, BATCH))
        b = advance(1, b, pl.multiple_of(nlc + rb, BATCH))
        return f, b

    lax.fori_loop(0, n_ch, lat_step, carry, unroll=8)

    for c0 in range(0, nl, lane_chunk):
        cs = slice(c0, c0 + lane_chunk)
        ht = [[h_sc[d, ri, cs, :].T for ri in range(2)] for d in range(2)]
        for j in range(2):
            x = xt_ref[:, j * SSM_GROUP:(j + 1) * SSM_GROUP, cs].reshape(kdim, lane_chunk)
            hj = jnp.concatenate([ht[d][ri][j * p:(j + 1) * p] for d in range(2) for ri in range(2)],
                                 axis=0).astype(BF16)
            acc = (yi_sc[j, :, cs] + dcol_ref[j] * x.astype(F32)
                   + jnp.dot(cot_ref[j], hj, preferred_element_type=F32))
            y = jax.nn.gelu(acc).astype(BF16)
            y_ref[:, j * SSM_GROUP:(j + 1) * SSM_GROUP, cs] = y.reshape(CHUNK, SSM_GROUP, lane_chunk)


def _s5(ut, uct, wall, cot, lam16, dcol):
    _, sw, nl = ut.shape
    nlc = uct.shape[-1]
    gp = sw // (2 * SSM_GROUP)
    kdim = CHUNK * SSM_GROUP
    lane_chunk = min(nl, 512)
    kern = functools.partial(_s5_kernel, lane_chunk=lane_chunk)
    return pl.pallas_call(
        kern,
        grid=(gp,),
        in_specs=[pl.BlockSpec((CHUNK, 2 * SSM_GROUP, nl), lambda g: (0, g, 0)),
                  pl.BlockSpec((CHUNK, 2 * SSM_GROUP, nlc), lambda g: (0, g, 0)),
                  pl.BlockSpec((2, 2 * kdim, kdim), lambda g: (g, 0, 0)),
                  pl.BlockSpec((2, kdim, kdim), lambda g: (g, 0, 0)),
                  pl.BlockSpec((2, None, 2, LANES), lambda g: (0, g, 0, 0)),
                  pl.BlockSpec((2, kdim, 1), lambda g: (g, 0, 0))],
        out_specs=pl.BlockSpec((CHUNK, 2 * SSM_GROUP, nl), lambda g: (0, g, 0)),
        out_shape=jax.ShapeDtypeStruct((CHUNK, sw, nl), BF16),
        scratch_shapes=[pltpu.VMEM((2, 2, nlc + nl, LANES), F32),
                        pltpu.VMEM((2, 2, nl, LANES), F32),
                        pltpu.VMEM((2, kdim, nl), F32)],
        compiler_params=_cparams(("arbitrary",)),
        name="s5",
    )(ut, uct, wall, cot, lam16, dcol)


CONV_CB = 64
CONV_GROUP = 4
CONV_SKEW = 8


def _convt_kernel(v_ref, tq_ref, b_ref, o_ref, s1_ref, s2_ref, z_sc, o_sc):
    a, cb, _ = v_ref.shape
    pitch = cb + CONV_SKEW
    z = v_ref[...].astype(F32)
    for i in range(a):
        z_sc[i * pitch:i * pitch + cb, :] = z[i]
    s1 = jnp.zeros((a, LANES), F32)
    s2 = jnp.zeros((a, LANES), F32)
    for g in range(cb // CONV_GROUP):
        c0 = g * CONV_GROUP
        rhs = jnp.concatenate([z_sc[pl.ds(c0 + j, a, stride=pitch), :] for j in range(CONV_GROUP)],
                              axis=0).astype(BF16)
        out = jnp.dot(tq_ref[g], rhs, preferred_element_type=F32).reshape(CONV_GROUP, a, LANES)
        out = out + b_ref[c0:c0 + CONV_GROUP].reshape(CONV_GROUP, 1, 1)
        for j in range(CONV_GROUP):
            o_sc[pl.ds(c0 + j, a, stride=pitch), :] = out[j]
        s1 = s1 + jnp.sum(out, axis=0)
        s2 = s2 + jnp.sum(out * out, axis=0)
    for i in range(a):
        o_ref[i] = o_sc[i * pitch:i * pitch + cb, :].astype(o_ref.dtype)

    @pl.when(pl.program_id(1) == 0)
    def _():
        s1_ref[...] = jnp.zeros_like(s1_ref)
        s2_ref[...] = jnp.zeros_like(s2_ref)

    s1_ref[...] += s1
    s2_ref[...] += s2


def _convt(vt, tq, bias):
    a, nq, c, _ = vt.shape
    cb = min(c, CONV_CB)
    ka = CONV_GROUP * a
    stat_spec = pl.BlockSpec((a, LANES), lambda q, i: (0, q))
    return pl.pallas_call(
        _convt_kernel,
        grid=(nq, c // cb),
        in_specs=[pl.BlockSpec((a, None, cb, LANES), lambda q, i: (0, q, i, 0)),
                  pl.BlockSpec((cb // CONV_GROUP, ka, ka), lambda q, i: (i, 0, 0)),
                  pl.BlockSpec((cb, 1), lambda q, i: (i, 0))],
        out_specs=[pl.BlockSpec((a, cb, LANES), lambda q, i: (0, i, q)), stat_spec, stat_spec],
        out_shape=[jax.ShapeDtypeStruct((a, c, nq * LANES), BF16),
                   jax.ShapeDtypeStruct((a, nq * LANES), F32),
                   jax.ShapeDtypeStruct((a, nq * LANES), F32)],
        scratch_shapes=[pltpu.VMEM((a * (cb + CONV_SKEW), LANES), F32),
                        pltpu.VMEM((a * (cb + CONV_SKEW), LANES), F32)],
        compiler_params=_cparams(("arbitrary", "arbitrary")),
        name="convt",
    )(vt, tq, bias.reshape(c, 1))


@functools.lru_cache(maxsize=None)
def _toeplitz_selector(a):
    g = CONV_GROUP
    pos = np.arange(a)
    tap = pos[None, :] - pos[:, None] + CONV_PAD
    hit = tap[None] == np.arange(CONV_TAPS)[:, None, None]
    idx = np.arange(g)
    diag = (idx[:, None, None] == idx[None, :, None]) & (idx[None, :, None] == idx[None, None, :])
    sel = hit[:, None, None, :, None, :] & diag[None, :, :, None, :, None]
    return sel.reshape(CONV_TAPS * g, g * a, g * a).astype(BF16)


def _conv_toeplitz(w, a):
    c = w.shape[1]
    g = CONV_GROUP
    w4 = w.reshape(CONV_TAPS, c // g, g).transpose(1, 0, 2).reshape(c // g, CONV_TAPS * g)
    return jnp.einsum("qk,kxy->qxy", w4.astype(BF16), jnp.asarray(_toeplitz_selector(a)),
                      preferred_element_type=BF16)


def _conv_branch_t(cv_ref, g1_ref, st_ref, lng_ref, lnb_ref, n_chan):
    st = st_ref[...]
    mean = (st[0:1] + st[2:3]) * (1.0 / n_chan)
    var = (st[1:2] + st[3:4]) * (1.0 / n_chan) - mean * mean
    y = (cv_ref[...].astype(F32) - mean) * lax.rsqrt(var + EPS) * lng_ref[...] + lnb_ref[...]
    return (jax.nn.silu(y) * g1_ref[...].astype(F32)).astype(BF16)


def _tail_w_kernel(y_ref, g2_ref, cv_ref, g1_ref, st_ref, wt_ref, bt_ref, wos_ref, woh_ref,
                   lng_ref, lnb_ref, o_ref, *, n_chan):
    y = y_ref[...]
    z = jnp.dot(wt_ref[...], y, preferred_element_type=F32) + bt_ref[...]
    s = (y.astype(F32) * jax.nn.sigmoid(z) * g2_ref[...].astype(F32)).astype(BF16)
    co = _conv_branch_t(cv_ref, g1_ref, st_ref, lng_ref, lnb_ref, n_chan)
    tn = (((0,), (0,)), ((), ()))
    mix = (lax.dot_general(s, wos_ref[...], tn, preferred_element_type=F32)
           + lax.dot_general(co, woh_ref[...], tn, preferred_element_type=F32))
    o_ref[...] = mix.reshape(o_ref.shape)


def _tail_w(y1t, g2t, ct_h, g1t_h, stats_h, glu_wt, glu_bcol, wo_s, wo_h, ln_g, ln_b, n_chan):
    _, sw, _ = y1t.shape
    _, hc, tl = ct_h.shape
    d = wo_s.shape[1]
    r = tl // BATCH
    s5_spec = pl.BlockSpec((None, sw, tl), lambda w: (w % CHUNK, 0, w // CHUNK))
    tok = lambda rows: pl.BlockSpec((rows, tl), lambda w: (0, w))
    full = lambda arr: pl.BlockSpec(arr.shape, lambda w: (0,) * arr.ndim)
    return pl.pallas_call(
        functools.partial(_tail_w_kernel, n_chan=n_chan),
        grid=(GRID_W,),
        in_specs=[s5_spec, s5_spec, pl.BlockSpec((None, hc, tl), lambda w: (w, 0, 0)), tok(hc),
                  tok(8), full(glu_wt), full(glu_bcol), full(wo_s), full(wo_h), full(ln_g),
                  full(ln_b)],
        out_specs=pl.BlockSpec((r, None, BATCH, d), lambda w: (0, w, 0, 0)),
        out_shape=jax.ShapeDtypeStruct((r, GRID_W, BATCH, d), F32),
        compiler_params=_cparams(("arbitrary",)),
        name="tail_w",
    )(y1t, g2t, ct_h, g1t_h, stats_h, glu_wt, glu_bcol, wo_s, wo_h, ln_g, ln_b)


def _final_r_kernel(cv_ref, g1_ref, st_ref, mh_ref, x_ref, wo_ref, lng_ref, lnb_ref, gate_ref,
                    fg_ref, o_ref, mix_sc, *, n_chan):
    tl, d = x_ref.shape[1:]
    co = _conv_branch_t(cv_ref, g1_ref, st_ref, lng_ref, lnb_ref, n_chan)
    mix = lax.dot_general(co, wo_ref[...], (((0,), (0,)), ((), ())),
                          preferred_element_type=F32) + mh_ref[...].reshape(tl * BATCH, d)
    nlt = mix_sc.shape[0]
    for j in range(nlt):
        mix_sc[j] = mix[:, j * LANES:(j + 1) * LANES]
    for b in range(BATCH):
        mb = jnp.concatenate([mix_sc[j, pl.ds(b, tl, stride=BATCH), :] for j in range(nlt)],
                             axis=-1)
        xo = x_ref[b] + gate_ref[b:b + 1, :] * mb
        ms = jnp.mean(xo * xo, axis=-1, keepdims=True)
        o_ref[b] = xo * lax.rsqrt(ms + EPS) * fg_ref[...]


def _final_r(ct_v, g1t_v, stats_v, mix_h, x, wo_v, ln_g, ln_b, gate8, final_g, n_chan):
    r, hc, tm = ct_v.shape
    _, length, d = x.shape
    tl = tm // BATCH
    tok = lambda rows: pl.BlockSpec((rows, tm), lambda i: (0, i))
    nat = pl.BlockSpec((BATCH, tl, d), lambda i: (0, i, 0))
    full = lambda arr: pl.BlockSpec(arr.shape, lambda i: (0,) * arr.ndim)
    return pl.pallas_call(
        functools.partial(_final_r_kernel, n_chan=n_chan),
        grid=(r,),
        in_specs=[pl.BlockSpec((None, hc, tm), lambda i: (i, 0, 0)), tok(hc), tok(8),
                  pl.BlockSpec((None, tl, BATCH, d), lambda i: (i, 0, 0, 0)), nat,
                  full(wo_v), full(ln_g), full(ln_b), full(gate8), full(final_g)],
        out_specs=nat,
        out_shape=jax.ShapeDtypeStruct(x.shape, F32),
        scratch_shapes=[pltpu.VMEM((d // LANES, tm, LANES), F32)],
        compiler_params=_cparams(("arbitrary",)),
        name="final_r",
    )(ct_v, g1t_v, stats_v, mix_h, x, wo_v, ln_g, ln_b, gate8, final_g)


def _s5prep_kernel(arow_ref, bt_ref, ct_ref, wall_ref, cot_ref, lam_ref, pw_sc, kk_sc):
    kdim = CHUNK * SSM_GROUP
    hp = lax.Precision.HIGHEST
    lane = lax.broadcasted_iota(jnp.int32, (1, LANES), 1)
    is_re = lane < SSM_STATE
    col_t = lax.broadcasted_iota(jnp.int32, (1, kdim), 1) // SSM_GROUP
    kcol = lax.broadcasted_iota(jnp.int32, (pw_sc.shape[1], 1), 0).astype(F32)

    def tile_rows(a):
        return jnp.concatenate([a] * CHUNK, axis=0)

    def power_rows(exps):
        pr = jnp.concatenate([jnp.broadcast_to(pw_sc[0, e:e + 1, :], (SSM_GROUP, LANES))
                              for e in exps], axis=0)
        pi = jnp.concatenate([jnp.broadcast_to(pw_sc[1, e:e + 1, :], (SSM_GROUP, LANES))
                              for e in exps], axis=0)
        return pr, pi

    def cmul(xr, xi, yr, yi, im_sign):
        return jnp.where(is_re, xr * yr - xi * yi, im_sign * (xr * yi + xi * yr))

    toeplitz = jnp.zeros((kdim, kdim), F32)
    for d in range(2):
        ar, ai = arow_ref[d, 0:1, :], arow_ref[d, 1:2, :]
        dt = jnp.exp(arow_ref[d, 2:3, :])
        mag = jnp.exp(kcol * (ar * dt))
        ang = kcol * (ai * dt)
        pw_sc[0] = mag * jnp.cos(ang)
        pw_sc[1] = mag * jnp.sin(ang)
        lam_ref[d, 0] = pw_sc[0, CHUNK:CHUNK + 8, :]
        lam_ref[d, 1] = pw_sc[1, CHUNK:CHUNK + 8, :]
        nr, ni = pw_sc[0, 1:2, :] - 1.0, pw_sc[1, 1:2, :]
        den = ar * ar + ai * ai
        qr, qi = (nr * ar + ni * ai) / den, (ni * ar - nr * ai) / den
        btr, bti = bt_ref[d, 0], bt_ref[d, 1]
        bbr, bbi = tile_rows(qr * btr - qi * bti), tile_rows(qr * bti + qi * btr)
        cr, ci = tile_rows(ct_ref[d, 0]), tile_rows(ct_ref[d, 1])
        tt = list(range(CHUNK))
        e_lag = [CHUNK - 1 - t for t in tt] if d else tt
        e_in = tt if d else [CHUNK - 1 - t for t in tt]
        e_out = [CHUNK - t for t in tt] if d else [t + 1 for t in tt]
        cot_ref[:, d * LANES:(d + 1) * LANES] = cmul(cr, ci, *power_rows(e_out), -1.0).astype(BF16)
        wall_ref[kdim + d * LANES:kdim + (d + 1) * LANES, :] = cmul(
            bbr, bbi, *power_rows(e_in), 1.0).T.astype(BF16)
        c_lag = cmul(cr, ci, *power_rows(e_lag), -1.0)
        bb = jnp.where(is_re, bbr, bbi)
        kk = lax.dot_general(c_lag, bb, (((1,), (1,)), ((), ())), precision=hp,
                             preferred_element_type=F32)
        zeros = jnp.zeros((kdim, kdim), F32)
        if d:
            kk_sc[:kdim] = kk
            kk_sc[kdim:] = zeros
            starts = [SSM_GROUP * (CHUNK - 1 - t) for t in tt]
        else:
            kk_sc[:kdim] = zeros
            kk_sc[kdim:] = kk
            starts = [kdim - SSM_GROUP * t for t in tt]
        tiles = []
        for lt in range(kdim // LANES):
            cols = slice(lt * LANES, (lt + 1) * LANES)
            acc = jnp.zeros((kdim, LANES), F32)
            for t in range(lt * LANES // SSM_GROUP, (lt + 1) * LANES // SSM_GROUP):
                acc = jnp.where(col_t[:, cols] == t, kk_sc[starts[t]:starts[t] + kdim, cols], acc)
            tiles.append(acc)
        toeplitz = toeplitz + jnp.concatenate(tiles, axis=1)
    wall_ref[:kdim, :] = toeplitz.astype(BF16)


def _s5prep(arow, bt, ct):
    g = arow.shape[0]
    kdim = CHUNK * SSM_GROUP
    return pl.pallas_call(
        _s5prep_kernel,
        grid=(g,),
        in_specs=[pl.BlockSpec((None, 2, 8, LANES), lambda i: (i, 0, 0, 0)),
                  pl.BlockSpec((None, 2, 2, SSM_GROUP, LANES), lambda i: (i, 0, 0, 0, 0)),
                  pl.BlockSpec((None, 2, 2, SSM_GROUP, LANES), lambda i: (i, 0, 0, 0, 0))],
        out_specs=[pl.BlockSpec((None, 2 * kdim, kdim), lambda i: (i, 0, 0)),
                   pl.BlockSpec((None, kdim, kdim), lambda i: (i, 0, 0)),
                   pl.BlockSpec((None, 2, 2, 8, LANES), lambda i: (i, 0, 0, 0, 0))],
        out_shape=[jax.ShapeDtypeStruct((g, 2 * kdim, kdim), BF16),
                   jax.ShapeDtypeStruct((g, kdim, kdim), BF16),
                   jax.ShapeDtypeStruct((g, 2, 2, 8, LANES), F32)],
        scratch_shapes=[pltpu.VMEM((2, CHUNK + 8, LANES), F32),
                        pltpu.VMEM((2 * kdim, kdim), F32)],
        compiler_params=_cparams(("arbitrary",)),
        name="s5prep",
    )(arow, bt, ct)


def _s5prep_inputs(a_re, a_im, log_dt, b_re, b_im, c_re, c_im):
    g = a_re.shape[1]
    dup = lambda a: jnp.concatenate([a, a], axis=-1)
    rows = jnp.stack([dup(a_re), dup(a_im),
                      jnp.broadcast_to(log_dt[..., None], (2, g, LANES))], axis=2)
    arow = jnp.pad(rows, ((0, 0), (0, 0), (0, 5), (0, 0))).transpose(1, 0, 2, 3)
    bt = jnp.stack([dup(b_re.transpose(0, 1, 3, 2)), dup(b_im.transpose(0, 1, 3, 2))], axis=2)
    ct = jnp.stack([dup(c_re), dup(c_im)], axis=2)
    return arow, bt.transpose(1, 0, 2, 3, 4), ct.transpose(1, 0, 2, 3, 4)


def kernel(x, c, ctx, c_ctx, norm_g, w_ada, b_ada, w_in, conv_dw, conv_db, conv_ln_g, conv_ln_b,
           ssm_a_re, ssm_a_im, ssm_log_dt, ssm_b_re, ssm_b_im, ssm_c_re, ssm_c_im, ssm_d,
           ssm_glu_w, ssm_glu_b, w_out, final_g):
    bsz, length, d = x.shape
    ctx_len = ctx.shape[1]
    cw = conv_dw.shape[-1]
    sw = ssm_d.shape[-1]
    assert bsz == BATCH and norm_g.shape[0] == 1
    assert length % (GRID_W * CHUNK) == 0 and ctx_len % CHUNK == 0
    n = length * BATCH
    nch = length // CHUNK
    nchc = ctx_len // CHUNK
    rows = length // GRID_W

    cond16 = jnp.zeros((16, d), F32).at[:BATCH].set(c).at[BATCH].set(c_ctx)
    mod = _ada(cond16, w_ada.reshape(w_ada.shape[1:]), b_ada[0])
    shift, scale, gate = mod[:, :d], mod[:, d:2 * d], mod[:, 2 * d:]
    amp = norm_g[0][None] * (1.0 + scale)
    a8, s8, gate8 = amp[:BATCH], shift[:BATCH], gate[:BATCH]
    a8c = jnp.broadcast_to(amp[BATCH], (BATCH, d))
    s8c = jnp.broadcast_to(shift[BATCH], (BATCH, d))

    w = w_in.reshape(w_in.shape[1:])
    half = cw // 2
    wt_c = w[:, :3 * cw].T.astype(BF16)
    wt_s = w[:, 3 * cw:].T.astype(BF16)
    wall, cot, lam_raw = _s5prep(*_s5prep_inputs(
        ssm_a_re[0], ssm_a_im[0], ssm_log_dt[0], ssm_b_re[0], ssm_b_im[0], ssm_c_re[0], ssm_c_im[0]))
    ng = sw // SSM_GROUP
    lam16 = lam_raw[:, :, :, 0, :SSM_STATE].reshape(ng // 2, 2, 2, 2, SSM_STATE).transpose(
        2, 0, 3, 1, 4).reshape(2, ng // 2, 2, LANES)
    dcol = jnp.broadcast_to(ssm_d[0].reshape(ng, 1, SSM_GROUP), (ng, CHUNK, SSM_GROUP)).reshape(
        ng, CHUNK * SSM_GROUP, 1)

    xt = x.transpose(1, 0, 2)
    ctxt = ctx.transpose(1, 0, 2)
    c4 = ctxt.reshape(nchc, CHUNK // 2, 2 * BATCH, d)

    x4 = xt.reshape(rows, GRID_W, BATCH, d)
    ut, g2t, vt_h, g1t_h = _inproj_w(x4, a8, s8, wt_s, wt_c, half)
    vt_v, g1t_v = _inproj_r(x4, a8, s8, wt_c, half)
    uct = _ctxproj(c4, a8c, s8c, wt_s[:sw])
    y1t = _s5(ut, uct, wall, cot, lam16, dcol)

    dw, db = conv_dw[0], conv_db[0]
    ct_h, s1_h, s2_h = _convt(vt_h, _conv_toeplitz(dw[:, :half], GRID_W), db[:half])
    ct_v, s1_v, s2_v = _convt(vt_v, _conv_toeplitz(dw[:, half:], rows), db[half:])
    to_v = lambda a: a.reshape(GRID_W, rows, BATCH).transpose(1, 0, 2).reshape(1, n)
    to_h = lambda a: a.reshape(rows, GRID_W, BATCH).transpose(1, 0, 2).reshape(1, n)
    pad4 = jnp.zeros((4, n), F32)
    stats_v = jnp.concatenate([s1_v.reshape(1, n), s2_v.reshape(1, n), to_v(s1_h), to_v(s2_h), pad4])
    stats_h = jnp.concatenate([s1_h.reshape(1, n), s2_h.reshape(1, n), to_h(s1_v), to_h(s2_v), pad4])

    ln_g, ln_b = conv_ln_g[0].reshape(cw, 1), conv_ln_b[0].reshape(cw, 1)
    wo = w_out.reshape(w_out.shape[1:]).astype(BF16)
    mix_h = _tail_w(y1t, g2t, ct_h, g1t_h, stats_h, ssm_glu_w[0].T.astype(BF16),
                    ssm_glu_b[0].reshape(sw, 1), wo[cw:], wo[:half], ln_g[:half], ln_b[:half], cw)
    return _final_r(ct_v, g1t_v, stats_v, mix_h, x, wo[half:cw], ln_g[half:],
                    ln_b[half:], gate8, final_g.reshape(1, d), cw)
```

```python
import functools

import jax
import jax.numpy as jnp
import numpy as np
from jax import lax
from jax.experimental import pallas as pl
from jax.experimental.pallas import tpu as pltpu

GRID_W = 64
CONV_TAPS = 31
CONV_PAD = CONV_TAPS // 2
SSM_GROUP = 16
SSM_STATE = 64
CHUNK = 16
BATCH = 8
EPS = 1e-6
LANES = 128
VMEM_LIMIT = 56 * 1024 * 1024

F32 = jnp.float32
BF16 = jnp.bfloat16


def _cparams(sem):
    return pltpu.CompilerParams(dimension_semantics=sem, vmem_limit_bytes=VMEM_LIMIT)


def _ada_kernel(c_ref, w_ref, b_ref, o_ref):
    s = jax.nn.silu(c_ref[...])
    o_ref[...] = jnp.dot(s, w_ref[...], preferred_element_type=F32,
                         precision=lax.Precision.HIGHEST) + b_ref[...]


def _ada(cond16, w_ada, b_ada):
    d, n3 = w_ada.shape
    nb = n3 // d
    return pl.pallas_call(
        _ada_kernel,
        grid=(nb,),
        in_specs=[pl.BlockSpec((16, d), lambda j: (0, 0)),
                  pl.BlockSpec((d, d), lambda j: (0, j)),
                  pl.BlockSpec((1, d), lambda j: (0, j))],
        out_specs=pl.BlockSpec((16, d), lambda j: (0, j)),
        out_shape=jax.ShapeDtypeStruct((16, n3), F32),
        compiler_params=_cparams(("arbitrary",)),
        name="ada",
    )(cond16, w_ada, b_ada.reshape(1, n3))


def _modulated_norm(x, a, s):
    ms = jnp.mean(x * x, axis=-1, keepdims=True)
    return x * lax.rsqrt(ms + EPS) * a[None] + s[None]


def _conv_gates(q, hc, vt_ref, g1t_ref):
    val = (q[:hc] * jax.nn.sigmoid(q[hc:2 * hc])).astype(BF16)
    for j in range(vt_ref.shape[0]):
        vt_ref[j] = val[:, j * LANES:(j + 1) * LANES]
    g1t_ref[...] = jax.nn.silu(q[2 * hc:]).astype(BF16)


def _transpose_cast_kernel(w_ref, o_ref):
    o_ref[...] = w_ref[...].T.astype(o_ref.dtype)


def _transpose_cast(w, col0, ncols, tile):
    k = w.shape[0]
    assert col0 % tile == 0 and ncols % tile == 0
    return pl.pallas_call(
        _transpose_cast_kernel,
        grid=(ncols // tile,),
        in_specs=[pl.BlockSpec((k, tile), lambda j: (0, j + col0 // tile))],
        out_specs=pl.BlockSpec((tile, k), lambda j: (j, 0)),
        out_shape=jax.ShapeDtypeStruct((ncols, k), BF16),
        compiler_params=_cparams(("arbitrary",)),
        name="wtrans",
    )(w)


def _project_t(w_refs, h):
    nt = (((1,), (1,)), ((), ()))
    return jnp.concatenate([lax.dot_general(w[...], h, nt, preferred_element_type=F32)
                            for w in w_refs], axis=0)


def _inproj_w_kernel(x_ref, a_ref, s_ref, ws_ref, wv_ref, wg_ref, wa_ref, ut_ref, g2t_ref, vt_ref,
                     g1t_ref):
    r, _, d = x_ref.shape
    sw = ut_ref.shape[0]
    hc = g1t_ref.shape[0]
    h = _modulated_norm(x_ref[...], a_ref[...], s_ref[...]).reshape(r * BATCH, d).astype(BF16)
    q = _project_t([ws_ref], h)
    ut_ref[...] = q[:sw].astype(BF16)
    g2t_ref[...] = jax.nn.silu(q[sw:]).astype(BF16)
    _conv_gates(_project_t([wv_ref, wg_ref, wa_ref], h), hc, vt_ref, g1t_ref)


def _conv_weight_specs(hc, d, part, index_map):
    return [pl.BlockSpec((hc, d), functools.partial(index_map, 2 * k + part)) for k in range(3)]


def _inproj_w(x4, a8, s8, wt_s, wt_c, hc):
    r, _, _, d = x4.shape
    sw = wt_s.shape[0] // 2
    tl = r * BATCH
    nq = tl // LANES
    cpr = GRID_W // CHUNK
    s5_spec = pl.BlockSpec((None, sw, tl), lambda w: (w % CHUNK, 0, w // CHUNK))
    return pl.pallas_call(
        _inproj_w_kernel,
        grid=(GRID_W,),
        in_specs=[pl.BlockSpec((r, None, BATCH, d), lambda w: (0, w, 0, 0)),
                  pl.BlockSpec((BATCH, d), lambda w: (0, 0)),
                  pl.BlockSpec((BATCH, d), lambda w: (0, 0)),
                  pl.BlockSpec(wt_s.shape, lambda w: (0, 0)),
                  *_conv_weight_specs(hc, d, 0, lambda blk, w: (blk, 0))],
        out_specs=[s5_spec, s5_spec,
                   pl.BlockSpec((None, nq, hc, LANES), lambda w: (w, 0, 0, 0)),
                   pl.BlockSpec((hc, tl), lambda w: (0, w))],
        out_shape=[jax.ShapeDtypeStruct((CHUNK, sw, cpr * tl), BF16),
                   jax.ShapeDtypeStruct((CHUNK, sw, cpr * tl), BF16),
                   jax.ShapeDtypeStruct((GRID_W, nq, hc, LANES), BF16),
                   jax.ShapeDtypeStruct((hc, GRID_W * tl), BF16)],
        compiler_params=_cparams(("arbitrary",)),
        name="inproj_w",
    )(x4, a8, s8, wt_s, wt_c, wt_c, wt_c)


def _inproj_r_kernel(x_ref, a_ref, s_ref, wv_ref, wg_ref, wa_ref, vt_ref, g1t_ref):
    gw, _, d = x_ref.shape
    hc = g1t_ref.shape[0]
    h = _modulated_norm(x_ref[...], a_ref[...], s_ref[...]).reshape(gw * BATCH, d).astype(BF16)
    _conv_gates(_project_t([wv_ref, wg_ref, wa_ref], h), hc, vt_ref, g1t_ref)


def _inproj_r(x4, a8, s8, wt_c, hc):
    r, gw, _, d = x4.shape
    tl = gw * BATCH
    nq = tl // LANES
    return pl.pallas_call(
        _inproj_r_kernel,
        grid=(r,),
        in_specs=[pl.BlockSpec((None, gw, BATCH, d), lambda i: (i, 0, 0, 0)),
                  pl.BlockSpec((BATCH, d), lambda i: (0, 0)),
                  pl.BlockSpec((BATCH, d), lambda i: (0, 0)),
                  *_conv_weight_specs(hc, d, 1, lambda blk, i: (blk, 0))],
        out_specs=[pl.BlockSpec((None, nq, hc, LANES), lambda i: (i, 0, 0, 0)),
                   pl.BlockSpec((hc, tl), lambda i: (0, i))],
        out_shape=[jax.ShapeDtypeStruct((r, nq, hc, LANES), BF16),
                   jax.ShapeDtypeStruct((hc, r * tl), BF16)],
        compiler_params=_cparams(("arbitrary",)),
        name="inproj_r",
    )(x4, a8, s8, wt_c, wt_c, wt_c)


def _ctxproj_kernel(x_ref, a_ref, s_ref, wut_ref, ut_ref):
    cb, _, d = x_ref.shape
    x = x_ref[...].reshape(cb * 2, BATCH, d)
    h = _modulated_norm(x, a_ref[...], s_ref[...]).reshape(cb, 2 * BATCH, d)
    for tl in range(2):
        ht = h[:, tl * BATCH:(tl + 1) * BATCH, :].reshape(cb * BATCH, d).astype(BF16)
        q = lax.dot_general(wut_ref[...], ht, (((1,), (1,)), ((), ())),
                            preferred_element_type=F32)
        ut_ref[tl] = q.astype(BF16)


def _ctxproj(x4, a8, s8, wt_s):
    nch, _, _, d = x4.shape
    sw = wt_s.shape[0] // 2
    return pl.pallas_call(
        _ctxproj_kernel,
        grid=(CHUNK // 2,),
        in_specs=[pl.BlockSpec((nch, None, 2 * BATCH, d), lambda t: (0, t, 0, 0)),
                  pl.BlockSpec((BATCH, d), lambda t: (0, 0)),
                  pl.BlockSpec((BATCH, d), lambda t: (0, 0)),
                  pl.BlockSpec((sw, d), lambda t: (0, 0))],
        out_specs=pl.BlockSpec((2, sw, nch * BATCH), lambda t: (t, 0, 0)),
        out_shape=jax.ShapeDtypeStruct((CHUNK, sw, nch * BATCH), BF16),
        compiler_params=_cparams(("arbitrary",)),
        name="ctxproj",
    )(x4, a8, s8, wt_s)


def _s5_kernel(xt_ref, xct_ref, wall_ref, cot_ref, lam_ref, dcol_ref, y_ref,
               s_sc, h_sc, yi_sc, *, lane_chunk):
    nl = xt_ref.shape[-1]
    nlc = xct_ref.shape[-1]
    n_ch = nl // BATCH
    n_chc = nlc // BATCH
    kdim = CHUNK * SSM_GROUP
    p = SSM_STATE

    def group_x(ref, j):
        return ref[:, j * SSM_GROUP:(j + 1) * SSM_GROUP, :].reshape(kdim, ref.shape[-1])

    st_c = [jnp.dot(wall_ref[j, kdim:], group_x(xct_ref, j), preferred_element_type=F32)
            for j in range(2)]
    st = []
    for j in range(2):
        r = jnp.dot(wall_ref[j], group_x(xt_ref, j), preferred_element_type=F32)
        yi_sc[j] = r[:kdim]
        st.append(r[kdim:])
    for d in range(2):
        for src, off, width in ((st_c, 0, nlc), (st, nlc, nl)):
            for ri in range(2):
                lo = (2 * d + ri) * p
                pair = jnp.concatenate([src[0][lo:lo + p], src[1][lo:lo + p]], axis=0)
                s_sc[d, ri, off:off + width, :] = pair.T

    lam = [[jnp.broadcast_to(lam_ref[d, ri:ri + 1, :], (BATCH, LANES)) for ri in range(2)]
           for d in range(2)]

    def advance(d, state, row):
        hr, hi = state
        ar, ai = lam[d]
        sr = s_sc[d, 0, pl.ds(row, BATCH), :]
        si = s_sc[d, 1, pl.ds(row, BATCH), :]
        return ar * hr - ai * hi + sr, ar * hi + ai * hr + si

    zero = jnp.zeros((BATCH, LANES), F32)

    def ctx_step(i, carry):
        f, b = carry
        f = advance(0, f, pl.multiple_of(i * BATCH, BATCH))
        b = advance(1, b, pl.multiple_of((n_chc - 1 - i) * BATCH, BATCH))
        return f, b

    carry = lax.fori_loop(0, n_chc, ctx_step, ((zero, zero), (zero, zero)))

    cpr = GRID_W // CHUNK
    n_rows = n_ch // cpr

    def lane_row(ch):
        within = jnp.bitwise_and(ch, cpr - 1)
        row = lax.shift_right_logical(ch, cpr.bit_length() - 1)
        return pl.multiple_of((within * n_rows + row) * BATCH, BATCH)

    def lat_step(i, carry):
        f, b = carry
        rf = lane_row(i)
        rb = lane_row(n_ch - 1 - i)
        h_sc[0, 0, pl.ds(rf, BATCH), :] = f[0]
        h_sc[0, 1, pl.ds(rf, BATCH), :] = f[1]
        h_sc[1, 0, pl.ds(rb, BATCH), :] = b[0]
        h_sc[1, 1, pl.ds(rb, BATCH), :] = b[1]
        f = advance(0, f, pl.multiple_of(nlc + rf, BATCH))
        b = advance(1, b, pl.multiple_of(nlc + rb, BATCH))
        return f, b

    lax.fori_loop(0, n_ch, lat_step, carry, unroll=8)

    for c0 in range(0, nl, lane_chunk):
        cs = slice(c0, c0 + lane_chunk)
        ht = [[h_sc[d, ri, cs, :].T for ri in range(2)] for d in range(2)]
        for j in range(2):
            x = xt_ref[:, j * SSM_GROUP:(j + 1) * SSM_GROUP, cs].reshape(kdim, lane_chunk)
            hj = jnp.concatenate([ht[d][ri][j * p:(j + 1) * p] for d in range(2) for ri in range(2)],
                                 axis=0).astype(BF16)
            acc = (yi_sc[j, :, cs] + dcol_ref[j] * x.astype(F32)
                   + jnp.dot(cot_ref[j], hj, preferred_element_type=F32))
            y = jax.nn.gelu(acc).astype(BF16)
            y_ref[:, j * SSM_GROUP:(j + 1) * SSM_GROUP, cs] = y.reshape(CHUNK, SSM_GROUP, lane_chunk)


def _s5(ut, uct, wall, cot, lam16, dcol):
    _, sw, nl = ut.shape
    nlc = uct.shape[-1]
    gp = sw // (2 * SSM_GROUP)
    kdim = CHUNK * SSM_GROUP
    lane_chunk = min(nl, 512)
    kern = functools.partial(_s5_kernel, lane_chunk=lane_chunk)
    return pl.pallas_call(
        kern,
        grid=(gp,),
        in_specs=[pl.BlockSpec((CHUNK, 2 * SSM_GROUP, nl), lambda g: (0, g, 0)),
                  pl.BlockSpec((CHUNK, 2 * SSM_GROUP, nlc), lambda g: (0, g, 0)),
                  pl.BlockSpec((2, 2 * kdim, kdim), lambda g: (g, 0, 0)),
                  pl.BlockSpec((2, kdim, kdim), lambda g: (g, 0, 0)),
                  pl.BlockSpec((2, None, 2, LANES), lambda g: (0, g, 0, 0)),
                  pl.BlockSpec((2, kdim, 1), lambda g: (g, 0, 0))],
        out_specs=pl.BlockSpec((CHUNK, 2 * SSM_GROUP, nl), lambda g: (0, g, 0)),
        out_shape=jax.ShapeDtypeStruct((CHUNK, sw, nl), BF16),
        scratch_shapes=[pltpu.VMEM((2, 2, nlc + nl, LANES), F32),
                        pltpu.VMEM((2, 2, nl, LANES), F32),
                        pltpu.VMEM((2, kdim, nl), F32)],
        compiler_params=_cparams(("arbitrary",)),
        name="s5",
    )(ut, uct, wall, cot, lam16, dcol)


CONV_CB = 64
CONV_GROUP = 4
CONV_SKEW = 8


def _convt_kernel(v_ref, tq_ref, b_ref, o_ref, s1_ref, s2_ref, z_sc, o_sc):
    a, cb, _ = v_ref.shape
    pitch = cb + CONV_SKEW
    z = v_ref[...].astype(F32)
    for i in range(a):
        z_sc[i * pitch:i * pitch + cb, :] = z[i]
    s1 = jnp.zeros((a, LANES), F32)
    s2 = jnp.zeros((a, LANES), F32)
    for g in range(cb // CONV_GROUP):
        c0 = g * CONV_GROUP
        rhs = jnp.concatenate([z_sc[pl.ds(c0 + j, a, stride=pitch), :] for j in range(CONV_GROUP)],
                              axis=0).astype(BF16)
        out = jnp.dot(tq_ref[g], rhs, preferred_element_type=F32).reshape(CONV_GROUP, a, LANES)
        out = out + b_ref[c0:c0 + CONV_GROUP].reshape(CONV_GROUP, 1, 1)
        for j in range(CONV_GROUP):
            o_sc[pl.ds(c0 + j, a, stride=pitch), :] = out[j]
        s1 = s1 + jnp.sum(out, axis=0)
        s2 = s2 + jnp.sum(out * out, axis=0)
    for i in range(a):
        o_ref[i] = o_sc[i * pitch:i * pitch + cb, :].astype(o_ref.dtype)

    s1_ref[...] = s1
    s2_ref[...] = s2


def _convt(vt, tq, bias):
    a, nq, c, _ = vt.shape
    cb = min(c, CONV_CB)
    ka = CONV_GROUP * a
    stat_spec = pl.BlockSpec((None, a, LANES), lambda i, q: (i, 0, q))
    stat_shape = jax.ShapeDtypeStruct((c // cb, a, nq * LANES), F32)
    return pl.pallas_call(
        _convt_kernel,
        grid=(c // cb, nq),
        in_specs=[pl.BlockSpec((a, None, cb, LANES), lambda i, q: (0, q, i, 0)),
                  pl.BlockSpec((cb // CONV_GROUP, ka, ka), lambda i, q: (i, 0, 0)),
                  pl.BlockSpec((cb, 1), lambda i, q: (i, 0))],
        out_specs=[pl.BlockSpec((a, cb, LANES), lambda i, q: (0, i, q)), stat_spec, stat_spec],
        out_shape=[jax.ShapeDtypeStruct((a, c, nq * LANES), BF16), stat_shape, stat_shape],
        scratch_shapes=[pltpu.VMEM((a * (cb + CONV_SKEW), LANES), F32),
                        pltpu.VMEM((a * (cb + CONV_SKEW), LANES), F32)],
        compiler_params=_cparams(("arbitrary", "arbitrary")),
        name="convt",
    )(vt, tq, bias.reshape(c, 1))


@functools.lru_cache(maxsize=None)
def _tap_selector(a):
    pos = np.arange(a)
    tap = pos[None, :] - pos[:, None] + CONV_PAD
    return (tap[None] == np.arange(CONV_TAPS)[:, None, None]).astype(BF16)


def _conv_toeplitz(w, a):
    c = w.shape[1]
    g = CONV_GROUP
    t = jnp.einsum("kc,kxy->cxy", w.astype(BF16), jnp.asarray(_tap_selector(a)),
                   preferred_element_type=BF16).reshape(c // g, g, a, a)
    rows = [jnp.pad(t[:, i], ((0, 0), (0, 0), (i * a, (g - 1 - i) * a))) for i in range(g)]
    return jnp.concatenate(rows, axis=1)


def _conv_branch_t(cv_ref, g1_ref, st_ref, lng_ref, lnb_ref, n_chan):
    mean = jnp.sum(st_ref[0], axis=0, keepdims=True) * (1.0 / n_chan)
    var = jnp.sum(st_ref[1], axis=0, keepdims=True) * (1.0 / n_chan) - mean * mean
    y = (cv_ref[...].astype(F32) - mean) * lax.rsqrt(var + EPS) * lng_ref[...] + lnb_ref[...]
    return (jax.nn.silu(y) * g1_ref[...].astype(F32)).astype(BF16)


def _tail_w_kernel(y_ref, g2_ref, cv_ref, g1_ref, st_ref, wt_ref, bt_ref, wos_ref, woh_ref,
                   lng_ref, lnb_ref, o_ref, *, n_chan):
    y = y_ref[...]
    z = jnp.dot(wt_ref[...], y, preferred_element_type=F32) + bt_ref[...]
    s = (y.astype(F32) * jax.nn.sigmoid(z) * g2_ref[...].astype(F32)).astype(BF16)
    co = _conv_branch_t(cv_ref, g1_ref, st_ref, lng_ref, lnb_ref, n_chan)
    tn = (((0,), (0,)), ((), ()))
    mix = (lax.dot_general(s, wos_ref[...], tn, preferred_element_type=F32)
           + lax.dot_general(co, woh_ref[...], tn, preferred_element_type=F32))
    o_ref[...] = mix.reshape(o_ref.shape)


def _tail_w(y1t, g2t, ct_h, g1t_h, stats_h, glu_wt, glu_bcol, wo_s, wo_h, ln_g, ln_b, n_chan):
    _, sw, _ = y1t.shape
    _, hc, tl = ct_h.shape
    d = wo_s.shape[1]
    r = tl // BATCH
    s5_spec = pl.BlockSpec((None, sw, tl), lambda w: (w % CHUNK, 0, w // CHUNK))
    tok = lambda rows: pl.BlockSpec((rows, tl), lambda w: (0, w))
    full = lambda arr: pl.BlockSpec(arr.shape, lambda w: (0,) * arr.ndim)
    return pl.pallas_call(
        functools.partial(_tail_w_kernel, n_chan=n_chan),
        grid=(GRID_W,),
        in_specs=[s5_spec, s5_spec, pl.BlockSpec((None, hc, tl), lambda w: (w, 0, 0)), tok(hc),
                  pl.BlockSpec(stats_h.shape[:2] + (tl,), lambda w: (0, 0, w)),
                  full(glu_wt), full(glu_bcol), full(wo_s), full(wo_h), full(ln_g), full(ln_b)],
        out_specs=pl.BlockSpec((r, None, BATCH, d), lambda w: (0, w, 0, 0)),
        out_shape=jax.ShapeDtypeStruct((r, GRID_W, BATCH, d), F32),
        compiler_params=_cparams(("arbitrary",)),
        name="tail_w",
    )(y1t, g2t, ct_h, g1t_h, stats_h, glu_wt, glu_bcol, wo_s, wo_h, ln_g, ln_b)


def _final_r_kernel(cv_ref, g1_ref, st_ref, mh_ref, x_ref, wo_ref, lng_ref, lnb_ref, gate_ref,
                    fg_ref, o_ref, mix_sc, *, n_chan):
    tl, d = x_ref.shape[1:]
    co = _conv_branch_t(cv_ref, g1_ref, st_ref, lng_ref, lnb_ref, n_chan)
    mix = lax.dot_general(co, wo_ref[...], (((0,), (0,)), ((), ())),
                          preferred_element_type=F32) + mh_ref[...].reshape(tl * BATCH, d)
    nlt = mix_sc.shape[0]
    for j in range(nlt):
        mix_sc[j] = mix[:, j * LANES:(j + 1) * LANES]
    for b in range(BATCH):
        mb = jnp.concatenate([mix_sc[j, pl.ds(b, tl, stride=BATCH), :] for j in range(nlt)],
                             axis=-1)
        xo = x_ref[b] + gate_ref[b:b + 1, :] * mb
        ms = jnp.mean(xo * xo, axis=-1, keepdims=True)
        o_ref[b] = xo * lax.rsqrt(ms + EPS) * fg_ref[...]


def _final_r(ct_v, g1t_v, stats_v, mix_h, x, wo_v, ln_g, ln_b, gate8, final_g, n_chan):
    r, hc, tm = ct_v.shape
    _, length, d = x.shape
    tl = tm // BATCH
    tok = lambda rows: pl.BlockSpec((rows, tm), lambda i: (0, i))
    nat = pl.BlockSpec((BATCH, tl, d), lambda i: (0, i, 0))
    full = lambda arr: pl.BlockSpec(arr.shape, lambda i: (0,) * arr.ndim)
    return pl.pallas_call(
        functools.partial(_final_r_kernel, n_chan=n_chan),
        grid=(r,),
        in_specs=[pl.BlockSpec((None, hc, tm), lambda i: (i, 0, 0)), tok(hc),
                  pl.BlockSpec(stats_v.shape[:2] + (tm,), lambda i: (0, 0, i)),
                  pl.BlockSpec((None, tl, BATCH, d), lambda i: (i, 0, 0, 0)), nat,
                  full(wo_v), full(ln_g), full(ln_b), full(gate8), full(final_g)],
        out_specs=nat,
        out_shape=jax.ShapeDtypeStruct(x.shape, F32),
        scratch_shapes=[pltpu.VMEM((d // LANES, tm, LANES), F32)],
        compiler_params=_cparams(("arbitrary",)),
        name="final_r",
    )(ct_v, g1t_v, stats_v, mix_h, x, wo_v, ln_g, ln_b, gate8, final_g)


def _s5prep_kernel(arow_ref, bt_ref, ct_ref, wall_ref, cot_ref, lam_ref, pw_sc, kk_sc):
    kdim = CHUNK * SSM_GROUP
    hp = lax.Precision.HIGHEST
    lane = lax.broadcasted_iota(jnp.int32, (1, LANES), 1)
    is_re = lane < SSM_STATE
    col_t = lax.broadcasted_iota(jnp.int32, (1, kdim), 1) // SSM_GROUP
    kcol = lax.broadcasted_iota(jnp.int32, (pw_sc.shape[1], 1), 0).astype(F32)

    def tile_rows(a):
        return jnp.concatenate([a] * CHUNK, axis=0)

    def power_rows(exps):
        pr = jnp.concatenate([jnp.broadcast_to(pw_sc[0, e:e + 1, :], (SSM_GROUP, LANES))
                              for e in exps], axis=0)
        pi = jnp.concatenate([jnp.broadcast_to(pw_sc[1, e:e + 1, :], (SSM_GROUP, LANES))
                              for e in exps], axis=0)
        return pr, pi

    def cmul(xr, xi, yr, yi, im_sign):
        return jnp.where(is_re, xr * yr - xi * yi, im_sign * (xr * yi + xi * yr))

    toeplitz = jnp.zeros((kdim, kdim), F32)
    for d in range(2):
        ar, ai = arow_ref[d, 0:1, :], arow_ref[d, 1:2, :]
        dt = jnp.exp(arow_ref[d, 2:3, :])
        mag = jnp.exp(kcol * (ar * dt))
        ang = kcol * (ai * dt)
        pw_sc[0] = mag * jnp.cos(ang)
        pw_sc[1] = mag * jnp.sin(ang)
        lam_ref[d, 0] = pw_sc[0, CHUNK:CHUNK + 8, :]
        lam_ref[d, 1] = pw_sc[1, CHUNK:CHUNK + 8, :]
        nr, ni = pw_sc[0, 1:2, :] - 1.0, pw_sc[1, 1:2, :]
        den = ar * ar + ai * ai
        qr, qi = (nr * ar + ni * ai) / den, (ni * ar - nr * ai) / den
        btr, bti = bt_ref[d, 0], bt_ref[d, 1]
        bbr, bbi = tile_rows(qr * btr - qi * bti), tile_rows(qr * bti + qi * btr)
        cr, ci = tile_rows(ct_ref[d, 0]), tile_rows(ct_ref[d, 1])
        tt = list(range(CHUNK))
        e_lag = [CHUNK - 1 - t for t in tt] if d else tt
        e_in = tt if d else [CHUNK - 1 - t for t in tt]
        e_out = [CHUNK - t for t in tt] if d else [t + 1 for t in tt]
        cot_ref[:, d * LANES:(d + 1) * LANES] = cmul(cr, ci, *power_rows(e_out), -1.0).astype(BF16)
        wall_ref[kdim + d * LANES:kdim + (d + 1) * LANES, :] = cmul(
            bbr, bbi, *power_rows(e_in), 1.0).T.astype(BF16)
        c_lag = cmul(cr, ci, *power_rows(e_lag), -1.0)
        bb = jnp.where(is_re, bbr, bbi)
        kk = lax.dot_general(c_lag, bb, (((1,), (1,)), ((), ())), precision=hp,
                             preferred_element_type=F32)
        zeros = jnp.zeros((kdim, kdim), F32)
        if d:
            kk_sc[:kdim] = kk
            kk_sc[kdim:] = zeros
            starts = [SSM_GROUP * (CHUNK - 1 - t) for t in tt]
        else:
            kk_sc[:kdim] = zeros
            kk_sc[kdim:] = kk
            starts = [kdim - SSM_GROUP * t for t in tt]
        tiles = []
        for lt in range(kdim // LANES):
            cols = slice(lt * LANES, (lt + 1) * LANES)
            acc = jnp.zeros((kdim, LANES), F32)
            for t in range(lt * LANES // SSM_GROUP, (lt + 1) * LANES // SSM_GROUP):
                acc = jnp.where(col_t[:, cols] == t, kk_sc[starts[t]:starts[t] + kdim, cols], acc)
            tiles.append(acc)
        toeplitz = toeplitz + jnp.concatenate(tiles, axis=1)
    wall_ref[:kdim, :] = toeplitz.astype(BF16)


def _s5prep(arow, bt, ct):
    g = arow.shape[0]
    kdim = CHUNK * SSM_GROUP
    return pl.pallas_call(
        _s5prep_kernel,
        grid=(g,),
        in_specs=[pl.BlockSpec((None, 2, 8, LANES), lambda i: (i, 0, 0, 0)),
                  pl.BlockSpec((None, 2, 2, SSM_GROUP, LANES), lambda i: (i, 0, 0, 0, 0)),
                  pl.BlockSpec((None, 2, 2, SSM_GROUP, LANES), lambda i: (i, 0, 0, 0, 0))],
        out_specs=[pl.BlockSpec((None, 2 * kdim, kdim), lambda i: (i, 0, 0)),
                   pl.BlockSpec((None, kdim, kdim), lambda i: (i, 0, 0)),
                   pl.BlockSpec((None, 2, 2, 8, LANES), lambda i: (i, 0, 0, 0, 0))],
        out_shape=[jax.ShapeDtypeStruct((g, 2 * kdim, kdim), BF16),
                   jax.ShapeDtypeStruct((g, kdim, kdim), BF16),
                   jax.ShapeDtypeStruct((g, 2, 2, 8, LANES), F32)],
        scratch_shapes=[pltpu.VMEM((2, CHUNK + 8, LANES), F32),
                        pltpu.VMEM((2 * kdim, kdim), F32)],
        compiler_params=_cparams(("arbitrary",)),
        name="s5prep",
    )(arow, bt, ct)


def _s5prep_inputs(a_re, a_im, log_dt, b_re, b_im, c_re, c_im):
    g = a_re.shape[1]
    dup = lambda a: jnp.concatenate([a, a], axis=-1)
    rows = jnp.stack([dup(a_re), dup(a_im),
                      jnp.broadcast_to(log_dt[..., None], (2, g, LANES))], axis=2)
    arow = jnp.pad(rows, ((0, 0), (0, 0), (0, 5), (0, 0))).transpose(1, 0, 2, 3)
    bt = jnp.stack([dup(b_re.transpose(0, 1, 3, 2)), dup(b_im.transpose(0, 1, 3, 2))], axis=2)
    ct = jnp.stack([dup(c_re), dup(c_im)], axis=2)
    return arow, bt.transpose(1, 0, 2, 3, 4), ct.transpose(1, 0, 2, 3, 4)


def kernel(x, c, ctx, c_ctx, norm_g, w_ada, b_ada, w_in, conv_dw, conv_db, conv_ln_g, conv_ln_b,
           ssm_a_re, ssm_a_im, ssm_log_dt, ssm_b_re, ssm_b_im, ssm_c_re, ssm_c_im, ssm_d,
           ssm_glu_w, ssm_glu_b, w_out, final_g):
    bsz, length, d = x.shape
    ctx_len = ctx.shape[1]
    cw = conv_dw.shape[-1]
    sw = ssm_d.shape[-1]
    assert bsz == BATCH and norm_g.shape[0] == 1
    assert length % (GRID_W * CHUNK) == 0 and ctx_len % CHUNK == 0
    n = length * BATCH
    nch = length // CHUNK
    nchc = ctx_len // CHUNK
    rows = length // GRID_W

    cond16 = jnp.zeros((16, d), F32).at[:BATCH].set(c).at[BATCH].set(c_ctx)
    mod = _ada(cond16, w_ada.reshape(w_ada.shape[1:]), b_ada[0])
    shift, scale, gate = mod[:, :d], mod[:, d:2 * d], mod[:, 2 * d:]
    amp = norm_g[0][None] * (1.0 + scale)
    a8, s8, gate8 = amp[:BATCH], shift[:BATCH], gate[:BATCH]
    a8c = jnp.broadcast_to(amp[BATCH], (BATCH, d))
    s8c = jnp.broadcast_to(shift[BATCH], (BATCH, d))

    w = w_in.reshape(w_in.shape[1:])
    half = cw // 2
    wt_c = _transpose_cast(w, 0, 3 * cw, half)
    wt_s = _transpose_cast(w, 3 * cw, 2 * sw, half)
    wall, cot, lam_raw = _s5prep(*_s5prep_inputs(
        ssm_a_re[0], ssm_a_im[0], ssm_log_dt[0], ssm_b_re[0], ssm_b_im[0], ssm_c_re[0], ssm_c_im[0]))
    ng = sw // SSM_GROUP
    lam16 = lam_raw[:, :, :, 0, :SSM_STATE].reshape(ng // 2, 2, 2, 2, SSM_STATE).transpose(
        2, 0, 3, 1, 4).reshape(2, ng // 2, 2, LANES)
    dcol = jnp.broadcast_to(ssm_d[0].reshape(ng, 1, SSM_GROUP), (ng, CHUNK, SSM_GROUP)).reshape(
        ng, CHUNK * SSM_GROUP, 1)

    xt = x.transpose(1, 0, 2)
    ctxt = ctx.transpose(1, 0, 2)
    c4 = ctxt.reshape(nchc, CHUNK // 2, 2 * BATCH, d)

    x4 = xt.reshape(rows, GRID_W, BATCH, d)
    ut, g2t, vt_h, g1t_h = _inproj_w(x4, a8, s8, wt_s, wt_c, half)
    vt_v, g1t_v = _inproj_r(x4, a8, s8, wt_c, half)
    uct = _ctxproj(c4, a8c, s8c, wt_s)
    y1t = _s5(ut, uct, wall, cot, lam16, dcol)

    dw, db = conv_dw[0], conv_db[0]
    ct_h, s1_h, s2_h = _convt(vt_h, _conv_toeplitz(dw[:, :half], GRID_W), db[:half])
    ct_v, s1_v, s2_v = _convt(vt_v, _conv_toeplitz(dw[:, half:], rows), db[half:])
    npart = s1_h.shape[0]
    st_h = jnp.stack([s1_h, s2_h]).reshape(2, npart, GRID_W, rows, BATCH)
    st_v = jnp.stack([s1_v, s2_v]).reshape(2, npart, rows, GRID_W, BATCH)
    swap = lambda a: a.transpose(0, 1, 3, 2, 4)
    stats_h = jnp.concatenate([st_h, swap(st_v)], axis=1).reshape(2, 2 * npart, n)
    stats_v = jnp.concatenate([st_v, swap(st_h)], axis=1).reshape(2, 2 * npart, n)

    ln_g, ln_b = conv_ln_g[0].reshape(cw, 1), conv_ln_b[0].reshape(cw, 1)
    wo = w_out.reshape(w_out.shape[1:]).astype(BF16)
    mix_h = _tail_w(y1t, g2t, ct_h, g1t_h, stats_h, _transpose_cast(ssm_glu_w[0], 0, sw, half),
                    ssm_glu_b[0].reshape(sw, 1), wo[cw:], wo[:half], ln_g[:half], ln_b[:half], cw)
    return _final_r(ct_v, g1t_v, stats_v, mix_h, x, wo[half:cw], ln_g[half:],
                    ln_b[half:], gate8, final_g.reshape(1, d), cw)
```

```python
import functools

import jax
import jax.numpy as jnp
import numpy as np
from jax import lax
from jax.experimental import pallas as pl
from jax.experimental.pallas import tpu as pltpu

GRID_W = 64
CONV_TAPS = 31
CONV_PAD = CONV_TAPS // 2
SSM_GROUP = 16
SSM_STATE = 64
CHUNK = 16
BATCH = 8
EPS = 1e-6
LANES = 128
VMEM_LIMIT = 56 * 1024 * 1024

F32 = jnp.float32
BF16 = jnp.bfloat16


def _cparams(sem):
    return pltpu.CompilerParams(dimension_semantics=sem, vmem_limit_bytes=VMEM_LIMIT)


def _ada_kernel(c_ref, w_ref, b_ref, o_ref):
    s = jax.nn.silu(c_ref[...])
    o_ref[...] = jnp.dot(s, w_ref[...], preferred_element_type=F32,
                         precision=lax.Precision.HIGHEST) + b_ref[...]


def _ada(cond16, w_ada, b_ada):
    d, n3 = w_ada.shape
    nb = n3 // d
    return pl.pallas_call(
        _ada_kernel,
        grid=(nb,),
        in_specs=[pl.BlockSpec((16, d), lambda j: (0, 0)),
                  pl.BlockSpec((d, d), lambda j: (0, j)),
                  pl.BlockSpec((1, d), lambda j: (0, j))],
        out_specs=pl.BlockSpec((16, d), lambda j: (0, j)),
        out_shape=jax.ShapeDtypeStruct((16, n3), F32),
        compiler_params=_cparams(("arbitrary",)),
        name="ada",
    )(cond16, w_ada, b_ada.reshape(1, n3))


def _modulated_norm(x, a, s):
    ms = jnp.mean(x * x, axis=-1, keepdims=True)
    return x * lax.rsqrt(ms + EPS) * a[None] + s[None]


def _conv_gates(q, hc, vt_ref, g1t_ref):
    val = (q[:hc] * jax.nn.sigmoid(q[hc:2 * hc])).astype(BF16)
    for j in range(vt_ref.shape[0]):
        vt_ref[j] = val[:, j * LANES:(j + 1) * LANES]
    g1t_ref[...] = jax.nn.silu(q[2 * hc:]).astype(BF16)


def _transpose_cast_kernel(w_ref, o_ref):
    o_ref[...] = w_ref[...].T.astype(o_ref.dtype)


def _transpose_cast(w, col0, ncols, tile):
    k = w.shape[0]
    assert col0 % tile == 0 and ncols % tile == 0
    return pl.pallas_call(
        _transpose_cast_kernel,
        grid=(ncols // tile,),
        in_specs=[pl.BlockSpec((k, tile), lambda j: (0, j + col0 // tile))],
        out_specs=pl.BlockSpec((tile, k), lambda j: (j, 0)),
        out_shape=jax.ShapeDtypeStruct((ncols, k), BF16),
        compiler_params=_cparams(("arbitrary",)),
        name="wtrans",
    )(w)


def _project_t(w_refs, h):
    nt = (((1,), (1,)), ((), ()))
    return jnp.concatenate([lax.dot_general(w[...], h, nt, preferred_element_type=F32)
                            for w in w_refs], axis=0)


def _inproj_w_kernel(x_ref, a_ref, s_ref, ws_ref, wv_ref, wg_ref, wa_ref, ut_ref, g2t_ref, vt_ref,
                     g1t_ref):
    r, _, d = x_ref.shape
    sw = ut_ref.shape[0]
    hc = g1t_ref.shape[0]
    h = _modulated_norm(x_ref[...], a_ref[...], s_ref[...]).reshape(r * BATCH, d).astype(BF16)
    q = _project_t([ws_ref], h)
    ut_ref[...] = q[:sw].astype(BF16)
    g2t_ref[...] = jax.nn.silu(q[sw:]).astype(BF16)
    _conv_gates(_project_t([wv_ref, wg_ref, wa_ref], h), hc, vt_ref, g1t_ref)


def _conv_weight_specs(hc, d, part, index_map):
    return [pl.BlockSpec((hc, d), functools.partial(index_map, 2 * k + part)) for k in range(3)]


def _inproj_w(x4, a8, s8, wt_s, wt_c, hc):
    r, _, _, d = x4.shape
    sw = wt_s.shape[0] // 2
    tl = r * BATCH
    nq = tl // LANES
    cpr = GRID_W // CHUNK
    s5_spec = pl.BlockSpec((None, sw, tl), lambda w: (w % CHUNK, 0, w // CHUNK))
    return pl.pallas_call(
        _inproj_w_kernel,
        grid=(GRID_W,),
        in_specs=[pl.BlockSpec((r, None, BATCH, d), lambda w: (0, w, 0, 0)),
                  pl.BlockSpec((BATCH, d), lambda w: (0, 0)),
                  pl.BlockSpec((BATCH, d), lambda w: (0, 0)),
                  pl.BlockSpec(wt_s.shape, lambda w: (0, 0)),
                  *_conv_weight_specs(hc, d, 0, lambda blk, w: (blk, 0))],
        out_specs=[s5_spec, s5_spec,
                   pl.BlockSpec((None, nq, hc, LANES), lambda w: (w, 0, 0, 0)),
                   pl.BlockSpec((hc, tl), lambda w: (0, w))],
        out_shape=[jax.ShapeDtypeStruct((CHUNK, sw, cpr * tl), BF16),
                   jax.ShapeDtypeStruct((CHUNK, sw, cpr * tl), BF16),
                   jax.ShapeDtypeStruct((GRID_W, nq, hc, LANES), BF16),
                   jax.ShapeDtypeStruct((hc, GRID_W * tl), BF16)],
        compiler_params=_cparams(("arbitrary",)),
        name="inproj_w",
    )(x4, a8, s8, wt_s, wt_c, wt_c, wt_c)


def _inproj_r_kernel(x_ref, a_ref, s_ref, wv_ref, wg_ref, wa_ref, vt_ref, g1t_ref):
    gw, _, d = x_ref.shape
    hc = g1t_ref.shape[0]
    h = _modulated_norm(x_ref[...], a_ref[...], s_ref[...]).reshape(gw * BATCH, d).astype(BF16)
    _conv_gates(_project_t([wv_ref, wg_ref, wa_ref], h), hc, vt_ref, g1t_ref)


def _inproj_r(x4, a8, s8, wt_c, hc):
    r, gw, _, d = x4.shape
    tl = gw * BATCH
    nq = tl // LANES
    return pl.pallas_call(
        _inproj_r_kernel,
        grid=(r,),
        in_specs=[pl.BlockSpec((None, gw, BATCH, d), lambda i: (i, 0, 0, 0)),
                  pl.BlockSpec((BATCH, d), lambda i: (0, 0)),
                  pl.BlockSpec((BATCH, d), lambda i: (0, 0)),
                  *_conv_weight_specs(hc, d, 1, lambda blk, i: (blk, 0))],
        out_specs=[pl.BlockSpec((None, nq, hc, LANES), lambda i: (i, 0, 0, 0)),
                   pl.BlockSpec((hc, tl), lambda i: (0, i))],
        out_shape=[jax.ShapeDtypeStruct((r, nq, hc, LANES), BF16),
                   jax.ShapeDtypeStruct((hc, r * tl), BF16)],
        compiler_params=_cparams(("arbitrary",)),
        name="inproj_r",
    )(x4, a8, s8, wt_c, wt_c, wt_c)


def _ctxproj_kernel(x_ref, a_ref, s_ref, wut_ref, ut_ref):
    cb, _, d = x_ref.shape
    x = x_ref[...].reshape(cb * 2, BATCH, d)
    h = _modulated_norm(x, a_ref[...], s_ref[...]).reshape(cb, 2 * BATCH, d)
    for tl in range(2):
        ht = h[:, tl * BATCH:(tl + 1) * BATCH, :].reshape(cb * BATCH, d).astype(BF16)
        q = lax.dot_general(wut_ref[...], ht, (((1,), (1,)), ((), ())),
                            preferred_element_type=F32)
        ut_ref[tl] = q.astype(BF16)


def _ctxproj(x4, a8, s8, wt_s):
    nch, _, _, d = x4.shape
    sw = wt_s.shape[0] // 2
    return pl.pallas_call(
        _ctxproj_kernel,
        grid=(CHUNK // 2,),
        in_specs=[pl.BlockSpec((nch, None, 2 * BATCH, d), lambda t: (0, t, 0, 0)),
                  pl.BlockSpec((BATCH, d), lambda t: (0, 0)),
                  pl.BlockSpec((BATCH, d), lambda t: (0, 0)),
                  pl.BlockSpec((sw, d), lambda t: (0, 0))],
        out_specs=pl.BlockSpec((2, sw, nch * BATCH), lambda t: (t, 0, 0)),
        out_shape=jax.ShapeDtypeStruct((CHUNK, sw, nch * BATCH), BF16),
        compiler_params=_cparams(("arbitrary",)),
        name="ctxproj",
    )(x4, a8, s8, wt_s)


def _s5_kernel(xt_ref, xct_ref, wall_ref, cot_ref, lam_ref, dcol_ref, y_ref,
               s_sc, h_sc, yi_sc, *, lane_chunk):
    nl = xt_ref.shape[-1]
    nlc = xct_ref.shape[-1]
    n_ch = nl // BATCH
    n_chc = nlc // BATCH
    kdim = CHUNK * SSM_GROUP
    p = SSM_STATE

    def group_x(ref, j):
        return ref[:, j * SSM_GROUP:(j + 1) * SSM_GROUP, :].reshape(kdim, ref.shape[-1])

    st_c = [jnp.dot(wall_ref[j, kdim:], group_x(xct_ref, j), preferred_element_type=F32)
            for j in range(2)]
    st = []
    for j in range(2):
        r = jnp.dot(wall_ref[j], group_x(xt_ref, j), preferred_element_type=F32)
        yi_sc[j] = r[:kdim]
        st.append(r[kdim:])
    for d in range(2):
        for src, off, width in ((st_c, 0, nlc), (st, nlc, nl)):
            for ri in range(2):
                lo = (2 * d + ri) * p
                pair = jnp.concatenate([src[0][lo:lo + p], src[1][lo:lo + p]], axis=0)
                s_sc[d, ri, off:off + width, :] = pair.T

    lam = [[jnp.broadcast_to(lam_ref[d, ri:ri + 1, :], (BATCH, LANES)) for ri in range(2)]
           for d in range(2)]

    def advance(d, state, row):
        hr, hi = state
        ar, ai = lam[d]
        sr = s_sc[d, 0, pl.ds(row, BATCH), :]
        si = s_sc[d, 1, pl.ds(row, BATCH), :]
        return ar * hr - ai * hi + sr, ar * hi + ai * hr + si

    zero = jnp.zeros((BATCH, LANES), F32)

    def ctx_step(i, carry):
        f, b = carry
        f = advance(0, f, pl.multiple_of(i * BATCH, BATCH))
        b = advance(1, b, pl.multiple_of((n_chc - 1 - i) * BATCH, BATCH))
        return f, b

    carry = lax.fori_loop(0, n_chc, ctx_step, ((zero, zero), (zero, zero)))

    cpr = GRID_W // CHUNK
    n_rows = n_ch // cpr

    def lane_row(ch):
        within = jnp.bitwise_and(ch, cpr - 1)
        row = lax.shift_right_logical(ch, cpr.bit_length() - 1)
        return pl.multiple_of((within * n_rows + row) * BATCH, BATCH)

    def lat_step(i, carry):
        f, b = carry
        rf = lane_row(i)
        rb = lane_row(n_ch - 1 - i)
        h_sc[0, 0, pl.ds(rf, BATCH), :] = f[0]
        h_sc[0, 1, pl.ds(rf, BATCH), :] = f[1]
        h_sc[1, 0, pl.ds(rb, BATCH), :] = b[0]
        h_sc[1, 1, pl.ds(rb, BATCH), :] = b[1]
        f = advance(0, f, pl.multiple_of(nlc + rf, BATCH))
        b = advance(1, b, pl.multiple_of(nlc + rb, BATCH))
        return f, b

    lax.fori_loop(0, n_ch, lat_step, carry, unroll=8)

    for c0 in range(0, nl, lane_chunk):
        cs = slice(c0, c0 + lane_chunk)
        ht = [[h_sc[d, ri, cs, :].T for ri in range(2)] for d in range(2)]
        for j in range(2):
            x = xt_ref[:, j * SSM_GROUP:(j + 1) * SSM_GROUP, cs].reshape(kdim, lane_chunk)
            hj = jnp.concatenate([ht[d][ri][j * p:(j + 1) * p] for d in range(2) for ri in range(2)],
                                 axis=0).astype(BF16)
            acc = (yi_sc[j, :, cs] + dcol_ref[j] * x.astype(F32)
                   + jnp.dot(cot_ref[j], hj, preferred_element_type=F32))
            y = jax.nn.gelu(acc).astype(BF16)
            y_ref[:, j * SSM_GROUP:(j + 1) * SSM_GROUP, cs] = y.reshape(CHUNK, SSM_GROUP, lane_chunk)


def _s5(ut, uct, wall, cot, lam16, dcol):
    _, sw, nl = ut.shape
    nlc = uct.shape[-1]
    gp = sw // (2 * SSM_GROUP)
    kdim = CHUNK * SSM_GROUP
    lane_chunk = min(nl, 512)
    kern = functools.partial(_s5_kernel, lane_chunk=lane_chunk)
    return pl.pallas_call(
        kern,
        grid=(gp,),
        in_specs=[pl.BlockSpec((CHUNK, 2 * SSM_GROUP, nl), lambda g: (0, g, 0)),
                  pl.BlockSpec((CHUNK, 2 * SSM_GROUP, nlc), lambda g: (0, g, 0)),
                  pl.BlockSpec((2, 2 * kdim, kdim), lambda g: (g, 0, 0)),
                  pl.BlockSpec((2, kdim, kdim), lambda g: (g, 0, 0)),
                  pl.BlockSpec((2, None, 2, LANES), lambda g: (0, g, 0, 0)),
                  pl.BlockSpec((2, kdim, 1), lambda g: (g, 0, 0))],
        out_specs=pl.BlockSpec((CHUNK, 2 * SSM_GROUP, nl), lambda g: (0, g, 0)),
        out_shape=jax.ShapeDtypeStruct((CHUNK, sw, nl), BF16),
        scratch_shapes=[pltpu.VMEM((2, 2, nlc + nl, LANES), F32),
                        pltpu.VMEM((2, 2, nl, LANES), F32),
                        pltpu.VMEM((2, kdim, nl), F32)],
        compiler_params=_cparams(("arbitrary",)),
        name="s5",
    )(ut, uct, wall, cot, lam16, dcol)


CONV_CB = 64
CONV_GROUP = 4
CONV_SKEW = 8


def _convt_kernel(v_ref, tq_ref, b_ref, o_ref, s1_ref, s2_ref, z_sc, o_sc):
    a, cb, _ = v_ref.shape
    pitch = cb + CONV_SKEW
    z = v_ref[...].astype(F32)
    for i in range(a):
        z_sc[i * pitch:i * pitch + cb, :] = z[i]
    s1 = jnp.zeros((a, LANES), F32)
    s2 = jnp.zeros((a, LANES), F32)
    for g in range(cb // CONV_GROUP):
        c0 = g * CONV_GROUP
        rhs = jnp.concatenate([z_sc[pl.ds(c0 + j, a, stride=pitch), :] for j in range(CONV_GROUP)],
                              axis=0).astype(BF16)
        out = jnp.dot(tq_ref[g], rhs, preferred_element_type=F32).reshape(CONV_GROUP, a, LANES)
        out = out + b_ref[c0:c0 + CONV_GROUP].reshape(CONV_GROUP, 1, 1)
        for j in range(CONV_GROUP):
            o_sc[pl.ds(c0 + j, a, stride=pitch), :] = out[j]
        s1 = s1 + jnp.sum(out, axis=0)
        s2 = s2 + jnp.sum(out * out, axis=0)
    for i in range(a):
        o_ref[i] = o_sc[i * pitch:i * pitch + cb, :].astype(o_ref.dtype)

    @pl.when(pl.program_id(1) == 0)
    def _():
        s1_ref[...] = jnp.zeros_like(s1_ref)
        s2_ref[...] = jnp.zeros_like(s2_ref)

    s1_ref[...] += s1
    s2_ref[...] += s2


def _convt(vt, tq, bias):
    a, nq, c, _ = vt.shape
    cb = min(c, CONV_CB)
    ka = CONV_GROUP * a
    stat_spec = pl.BlockSpec((a, LANES), lambda q, i: (0, q))
    return pl.pallas_call(
        _convt_kernel,
        grid=(nq, c // cb),
        in_specs=[pl.BlockSpec((a, None, cb, LANES), lambda q, i: (0, q, i, 0)),
                  pl.BlockSpec((cb // CONV_GROUP, ka, ka), lambda q, i: (i, 0, 0)),
                  pl.BlockSpec((cb, 1), lambda q, i: (i, 0))],
        out_specs=[pl.BlockSpec((a, cb, LANES), lambda q, i: (0, i, q)), stat_spec, stat_spec],
        out_shape=[jax.ShapeDtypeStruct((a, c, nq * LANES), BF16),
                   jax.ShapeDtypeStruct((a, nq * LANES), F32),
                   jax.ShapeDtypeStruct((a, nq * LANES), F32)],
        scratch_shapes=[pltpu.VMEM((a * (cb + CONV_SKEW), LANES), F32),
                        pltpu.VMEM((a * (cb + CONV_SKEW), LANES), F32)],
        compiler_params=_cparams(("arbitrary", "arbitrary")),
        name="convt",
    )(vt, tq, bias.reshape(c, 1))


@functools.lru_cache(maxsize=None)
def _toeplitz_selector(a):
    g = CONV_GROUP
    pos = np.arange(a)
    tap = pos[None, :] - pos[:, None] + CONV_PAD
    hit = tap[None] == np.arange(CONV_TAPS)[:, None, None]
    idx = np.arange(g)
    diag = (idx[:, None, None] == idx[None, :, None]) & (idx[None, :, None] == idx[None, None, :])
    sel = hit[:, None, None, :, None, :] & diag[None, :, :, None, :, None]
    return sel.reshape(CONV_TAPS * g, g * a, g * a).astype(BF16)


def _conv_toeplitz(w, a):
    c = w.shape[1]
    g = CONV_GROUP
    w4 = w.reshape(CONV_TAPS, c // g, g).transpose(1, 0, 2).reshape(c // g, CONV_TAPS * g)
    return jnp.einsum("qk,kxy->qxy", w4.astype(BF16), jnp.asarray(_toeplitz_selector(a)),
                      preferred_element_type=BF16)


def _conv_branch_t(cv, g1, st, lng_ref, lnb_ref, n_chan):
    mean = (st[0:1] + st[2:3]) * (1.0 / n_chan)
    var = (st[1:2] + st[3:4]) * (1.0 / n_chan) - mean * mean
    y = (cv.astype(F32) - mean) * lax.rsqrt(var + EPS) * lng_ref[...] + lnb_ref[...]
    return (jax.nn.silu(y) * g1.astype(F32)).astype(BF16)


def _tail_w_kernel(y_ref, g2_ref, cv_ref, g1_ref, st_ref, wt_ref, bt_ref, wos_ref, woh_ref,
                   lng_ref, lnb_ref, o_ref, *, n_chan):
    tl = y_ref.shape[-1]
    tn = (((0,), (0,)), ((), ()))
    st = st_ref[...]
    mix = []
    for k in range(2):
        lanes = slice(k * tl, (k + 1) * tl)
        y = y_ref[k]
        z = jnp.dot(wt_ref[...], y, preferred_element_type=F32) + bt_ref[...]
        s = (y.astype(F32) * jax.nn.sigmoid(z) * g2_ref[k].astype(F32)).astype(BF16)
        co = _conv_branch_t(cv_ref[k], g1_ref[:, lanes], st[:, lanes], lng_ref, lnb_ref, n_chan)
        m = (lax.dot_general(s, wos_ref[...], tn, preferred_element_type=F32)
             + lax.dot_general(co, woh_ref[...], tn, preferred_element_type=F32))
        mix.append(m.reshape(tl // BATCH, BATCH, m.shape[-1]))
    o_ref[...] = jnp.concatenate(mix, axis=1).astype(o_ref.dtype)


def _tail_w(y1t, g2t, ct_h, g1t_h, stats_h, glu_wt, glu_bcol, wo_s, wo_h, ln_g, ln_b, n_chan):
    _, sw, _ = y1t.shape
    _, hc, tl = ct_h.shape
    d = wo_s.shape[1]
    r = tl // BATCH
    half_chunk = CHUNK // 2
    s5_spec = pl.BlockSpec((2, sw, tl), lambda j: (j % half_chunk, 0, j // half_chunk))
    tok = lambda rows: pl.BlockSpec((rows, 2 * tl), lambda j: (0, j))
    full = lambda arr: pl.BlockSpec(arr.shape, lambda j: (0,) * arr.ndim)
    return pl.pallas_call(
        functools.partial(_tail_w_kernel, n_chan=n_chan),
        grid=(GRID_W // 2,),
        in_specs=[s5_spec, s5_spec, pl.BlockSpec((2, hc, tl), lambda j: (j, 0, 0)), tok(hc),
                  tok(8), full(glu_wt), full(glu_bcol), full(wo_s), full(wo_h), full(ln_g),
                  full(ln_b)],
        out_specs=pl.BlockSpec((r, None, 2 * BATCH, d), lambda j: (0, j, 0, 0)),
        out_shape=jax.ShapeDtypeStruct((r, GRID_W // 2, 2 * BATCH, d), BF16),
        compiler_params=_cparams(("arbitrary",)),
        name="tail_w",
    )(y1t, g2t, ct_h, g1t_h, stats_h, glu_wt, glu_bcol, wo_s, wo_h, ln_g, ln_b)


def _final_r_kernel(cv_ref, g1_ref, st_ref, mh_ref, x_ref, wo_ref, lng_ref, lnb_ref, gate_ref,
                    fg_ref, o_ref, mix_sc, *, n_chan):
    tl, d = x_ref.shape[1:]
    co = _conv_branch_t(cv_ref[...], g1_ref[...], st_ref[...], lng_ref, lnb_ref, n_chan)
    mix = lax.dot_general(co, wo_ref[...], (((0,), (0,)), ((), ())),
                          preferred_element_type=F32) + mh_ref[...].reshape(tl * BATCH, d)
    nlt = mix_sc.shape[0]
    for j in range(nlt):
        mix_sc[j] = mix[:, j * LANES:(j + 1) * LANES]
    for b in range(BATCH):
        mb = jnp.concatenate([mix_sc[j, pl.ds(b, tl, stride=BATCH), :] for j in range(nlt)],
                             axis=-1)
        xo = x_ref[b] + gate_ref[b:b + 1, :] * mb
        ms = jnp.mean(xo * xo, axis=-1, keepdims=True)
        o_ref[b] = xo * lax.rsqrt(ms + EPS) * fg_ref[...]


def _final_r(ct_v, g1t_v, stats_v, mix_h, x, wo_v, ln_g, ln_b, gate8, final_g, n_chan):
    r, hc, tm = ct_v.shape
    _, length, d = x.shape
    tl = tm // BATCH
    tok = lambda rows: pl.BlockSpec((rows, tm), lambda i: (0, i))
    nat = pl.BlockSpec((BATCH, tl, d), lambda i: (0, i, 0))
    full = lambda arr: pl.BlockSpec(arr.shape, lambda i: (0,) * arr.ndim)
    return pl.pallas_call(
        functools.partial(_final_r_kernel, n_chan=n_chan),
        grid=(r,),
        in_specs=[pl.BlockSpec((None, hc, tm), lambda i: (i, 0, 0)), tok(hc), tok(8),
                  pl.BlockSpec((None, tl // 2, 2 * BATCH, d), lambda i: (i, 0, 0, 0)), nat,
                  full(wo_v), full(ln_g), full(ln_b), full(gate8), full(final_g)],
        out_specs=nat,
        out_shape=jax.ShapeDtypeStruct(x.shape, F32),
        scratch_shapes=[pltpu.VMEM((d // LANES, tm, LANES), F32)],
        compiler_params=_cparams(("arbitrary",)),
        name="final_r",
    )(ct_v, g1t_v, stats_v, mix_h, x, wo_v, ln_g, ln_b, gate8, final_g)


def _s5prep_kernel(arow_ref, bt_ref, ct_ref, wall_ref, cot_ref, lam_ref, pw_sc, kk_sc):
    kdim = CHUNK * SSM_GROUP
    hp = lax.Precision.HIGHEST
    lane = lax.broadcasted_iota(jnp.int32, (1, LANES), 1)
    is_re = lane < SSM_STATE
    col_t = lax.broadcasted_iota(jnp.int32, (1, kdim), 1) // SSM_GROUP
    kcol = lax.broadcasted_iota(jnp.int32, (pw_sc.shape[1], 1), 0).astype(F32)

    def tile_rows(a):
        return jnp.concatenate([a] * CHUNK, axis=0)

    def power_rows(exps):
        pr = jnp.concatenate([jnp.broadcast_to(pw_sc[0, e:e + 1, :], (SSM_GROUP, LANES))
                              for e in exps], axis=0)
        pi = jnp.concatenate([jnp.broadcast_to(pw_sc[1, e:e + 1, :], (SSM_GROUP, LANES))
                              for e in exps], axis=0)
        return pr, pi

    def cmul(xr, xi, yr, yi, im_sign):
        return jnp.where(is_re, xr * yr - xi * yi, im_sign * (xr * yi + xi * yr))

    toeplitz = jnp.zeros((kdim, kdim), F32)
    for d in range(2):
        ar, ai = arow_ref[d, 0:1, :], arow_ref[d, 1:2, :]
        dt = jnp.exp(arow_ref[d, 2:3, :])
        mag = jnp.exp(kcol * (ar * dt))
        ang = kcol * (ai * dt)
        pw_sc[0] = mag * jnp.cos(ang)
        pw_sc[1] = mag * jnp.sin(ang)
        lam_ref[d, 0] = pw_sc[0, CHUNK:CHUNK + 8, :]
        lam_ref[d, 1] = pw_sc[1, CHUNK:CHUNK + 8, :]
        nr, ni = pw_sc[0, 1:2, :] - 1.0, pw_sc[1, 1:2, :]
        den = ar * ar + ai * ai
        qr, qi = (nr * ar + ni * ai) / den, (ni * ar - nr * ai) / den
        btr, bti = bt_ref[d, 0], bt_ref[d, 1]
        bbr, bbi = tile_rows(qr * btr - qi * bti), tile_rows(qr * bti + qi * btr)
        cr, ci = tile_rows(ct_ref[d, 0]), tile_rows(ct_ref[d, 1])
        tt = list(range(CHUNK))
        e_lag = [CHUNK - 1 - t for t in tt] if d else tt
        e_in = tt if d else [CHUNK - 1 - t for t in tt]
        e_out = [CHUNK - t for t in tt] if d else [t + 1 for t in tt]
        cot_ref[:, d * LANES:(d + 1) * LANES] = cmul(cr, ci, *power_rows(e_out), -1.0).astype(BF16)
        wall_ref[kdim + d * LANES:kdim + (d + 1) * LANES, :] = cmul(
            bbr, bbi, *power_rows(e_in), 1.0).T.astype(BF16)
        c_lag = cmul(cr, ci, *power_rows(e_lag), -1.0)
        bb = jnp.where(is_re, bbr, bbi)
        kk = lax.dot_general(c_lag, bb, (((1,), (1,)), ((), ())), precision=hp,
                             preferred_element_type=F32)
        zeros = jnp.zeros((kdim, kdim), F32)
        if d:
            kk_sc[:kdim] = kk
            kk_sc[kdim:] = zeros
            starts = [SSM_GROUP * (CHUNK - 1 - t) for t in tt]
        else:
            kk_sc[:kdim] = zeros
            kk_sc[kdim:] = kk
            starts = [kdim - SSM_GROUP * t for t in tt]
        tiles = []
        for lt in range(kdim // LANES):
            cols = slice(lt * LANES, (lt + 1) * LANES)
            acc = jnp.zeros((kdim, LANES), F32)
            for t in range(lt * LANES // SSM_GROUP, (lt + 1) * LANES // SSM_GROUP):
                acc = jnp.where(col_t[:, cols] == t, kk_sc[starts[t]:starts[t] + kdim, cols], acc)
            tiles.append(acc)
        toeplitz = toeplitz + jnp.concatenate(tiles, axis=1)
    wall_ref[:kdim, :] = toeplitz.astype(BF16)


def _s5prep(arow, bt, ct):
    g = arow.shape[0]
    kdim = CHUNK * SSM_GROUP
    return pl.pallas_call(
        _s5prep_kernel,
        grid=(g,),
        in_specs=[pl.BlockSpec((None, 2, 8, LANES), lambda i: (i, 0, 0, 0)),
                  pl.BlockSpec((None, 2, 2, SSM_GROUP, LANES), lambda i: (i, 0, 0, 0, 0)),
                  pl.BlockSpec((None, 2, 2, SSM_GROUP, LANES), lambda i: (i, 0, 0, 0, 0))],
        out_specs=[pl.BlockSpec((None, 2 * kdim, kdim), lambda i: (i, 0, 0)),
                   pl.BlockSpec((None, kdim, kdim), lambda i: (i, 0, 0)),
                   pl.BlockSpec((None, 2, 2, 8, LANES), lambda i: (i, 0, 0, 0, 0))],
        out_shape=[jax.ShapeDtypeStruct((g, 2 * kdim, kdim), BF16),
                   jax.ShapeDtypeStruct((g, kdim, kdim), BF16),
                   jax.ShapeDtypeStruct((g, 2, 2, 8, LANES), F32)],
        scratch_shapes=[pltpu.VMEM((2, CHUNK + 8, LANES), F32),
                        pltpu.VMEM((2 * kdim, kdim), F32)],
        compiler_params=_cparams(("arbitrary",)),
        name="s5prep",
    )(arow, bt, ct)


def _s5prep_inputs(a_re, a_im, log_dt, b_re, b_im, c_re, c_im):
    g = a_re.shape[1]
    dup = lambda a: jnp.concatenate([a, a], axis=-1)
    rows = jnp.stack([dup(a_re), dup(a_im),
                      jnp.broadcast_to(log_dt[..., None], (2, g, LANES))], axis=2)
    arow = jnp.pad(rows, ((0, 0), (0, 0), (0, 5), (0, 0))).transpose(1, 0, 2, 3)
    bt = jnp.stack([dup(b_re.transpose(0, 1, 3, 2)), dup(b_im.transpose(0, 1, 3, 2))], axis=2)
    ct = jnp.stack([dup(c_re), dup(c_im)], axis=2)
    return arow, bt.transpose(1, 0, 2, 3, 4), ct.transpose(1, 0, 2, 3, 4)


def kernel(x, c, ctx, c_ctx, norm_g, w_ada, b_ada, w_in, conv_dw, conv_db, conv_ln_g, conv_ln_b,
           ssm_a_re, ssm_a_im, ssm_log_dt, ssm_b_re, ssm_b_im, ssm_c_re, ssm_c_im, ssm_d,
           ssm_glu_w, ssm_glu_b, w_out, final_g):
    bsz, length, d = x.shape
    ctx_len = ctx.shape[1]
    cw = conv_dw.shape[-1]
    sw = ssm_d.shape[-1]
    assert bsz == BATCH and norm_g.shape[0] == 1
    assert length % (GRID_W * CHUNK) == 0 and ctx_len % CHUNK == 0
    n = length * BATCH
    nch = length // CHUNK
    nchc = ctx_len // CHUNK
    rows = length // GRID_W

    cond16 = jnp.zeros((16, d), F32).at[:BATCH].set(c).at[BATCH].set(c_ctx)
    mod = _ada(cond16, w_ada.reshape(w_ada.shape[1:]), b_ada[0])
    shift, scale, gate = mod[:, :d], mod[:, d:2 * d], mod[:, 2 * d:]
    amp = norm_g[0][None] * (1.0 + scale)
    a8, s8, gate8 = amp[:BATCH], shift[:BATCH], gate[:BATCH]
    a8c = jnp.broadcast_to(amp[BATCH], (BATCH, d))
    s8c = jnp.broadcast_to(shift[BATCH], (BATCH, d))

    w = w_in.reshape(w_in.shape[1:])
    half = cw // 2
    wt_c = _transpose_cast(w, 0, 3 * cw, half)
    wt_s = _transpose_cast(w, 3 * cw, 2 * sw, half)
    wall, cot, lam_raw = _s5prep(*_s5prep_inputs(
        ssm_a_re[0], ssm_a_im[0], ssm_log_dt[0], ssm_b_re[0], ssm_b_im[0], ssm_c_re[0], ssm_c_im[0]))
    ng = sw // SSM_GROUP
    lam16 = lam_raw[:, :, :, 0, :SSM_STATE].reshape(ng // 2, 2, 2, 2, SSM_STATE).transpose(
        2, 0, 3, 1, 4).reshape(2, ng // 2, 2, LANES)
    dcol = jnp.broadcast_to(ssm_d[0].reshape(ng, 1, SSM_GROUP), (ng, CHUNK, SSM_GROUP)).reshape(
        ng, CHUNK * SSM_GROUP, 1)

    xt = x.transpose(1, 0, 2)
    ctxt = ctx.transpose(1, 0, 2)
    c4 = ctxt.reshape(nchc, CHUNK // 2, 2 * BATCH, d)

    x4 = xt.reshape(rows, GRID_W, BATCH, d)
    ut, g2t, vt_h, g1t_h = _inproj_w(x4, a8, s8, wt_s, wt_c, half)
    vt_v, g1t_v = _inproj_r(x4, a8, s8, wt_c, half)
    uct = _ctxproj(c4, a8c, s8c, wt_s)
    y1t = _s5(ut, uct, wall, cot, lam16, dcol)

    dw, db = conv_dw[0], conv_db[0]
    ct_h, s1_h, s2_h = _convt(vt_h, _conv_toeplitz(dw[:, :half], GRID_W), db[:half])
    ct_v, s1_v, s2_v = _convt(vt_v, _conv_toeplitz(dw[:, half:], rows), db[half:])
    to_v = lambda a: a.reshape(GRID_W, rows, BATCH).transpose(1, 0, 2).reshape(1, n)
    to_h = lambda a: a.reshape(rows, GRID_W, BATCH).transpose(1, 0, 2).reshape(1, n)
    pad4 = jnp.zeros((4, n), F32)
    stats_v = jnp.concatenate([s1_v.reshape(1, n), s2_v.reshape(1, n), to_v(s1_h), to_v(s2_h), pad4])
    stats_h = jnp.concatenate([s1_h.reshape(1, n), s2_h.reshape(1, n), to_h(s1_v), to_h(s2_v), pad4])

    ln_g, ln_b = conv_ln_g[0].reshape(cw, 1), conv_ln_b[0].reshape(cw, 1)
    wo = w_out.reshape(w_out.shape[1:]).astype(BF16)
    mix_h = _tail_w(y1t, g2t, ct_h, g1t_h, stats_h, _transpose_cast(ssm_glu_w[0], 0, sw, half),
                    ssm_glu_b[0].reshape(sw, 1), wo[cw:], wo[:half], ln_g[:half], ln_b[:half], cw)
    return _final_r(ct_v, g1t_v, stats_v, mix_h, x, wo[half:cw], ln_g[half:],
                    ln_b[half:], gate8, final_g.reshape(1, d), cw)
```

```python
import functools
import math

import jax
import jax.numpy as jnp
import numpy as np
from jax import lax
from jax.experimental import pallas as pl
from jax.experimental.pallas import tpu as pltpu

GRID_W = 64
CONV_TAPS = 31
CONV_PAD = CONV_TAPS // 2
SSM_GROUP = 16
SSM_STATE = 64
CHUNK = 16
BATCH = 8
EPS = 1e-6
LANES = 128
VMEM_LIMIT = 56 * 1024 * 1024

F32 = jnp.float32
BF16 = jnp.bfloat16


def _cparams(sem):
    return pltpu.CompilerParams(dimension_semantics=sem, vmem_limit_bytes=VMEM_LIMIT)


def _ada_kernel(c_ref, w_ref, b_ref, o_ref):
    s = jax.nn.silu(c_ref[...])
    o_ref[...] = jnp.dot(s, w_ref[...], preferred_element_type=F32,
                         precision=lax.Precision.HIGHEST) + b_ref[...]


def _ada(cond16, w_ada, b_ada):
    d, n3 = w_ada.shape
    nb = n3 // d
    return pl.pallas_call(
        _ada_kernel,
        grid=(nb,),
        in_specs=[pl.BlockSpec((16, d), lambda j: (0, 0)),
                  pl.BlockSpec((d, d), lambda j: (0, j)),
                  pl.BlockSpec((1, d), lambda j: (0, j))],
        out_specs=pl.BlockSpec((16, d), lambda j: (0, j)),
        out_shape=jax.ShapeDtypeStruct((16, n3), F32),
        compiler_params=_cparams(("arbitrary",)),
        name="ada",
    )(cond16, w_ada, b_ada.reshape(1, n3))


def _modulated_norm(x, a, s):
    ms = jnp.mean(x * x, axis=-1, keepdims=True)
    return x * lax.rsqrt(ms + EPS) * a[None] + s[None]


def _conv_gates(q, hc, vt_ref, g1t_ref):
    val = (q[:hc] * jax.nn.sigmoid(q[hc:2 * hc])).astype(BF16)
    for j in range(vt_ref.shape[0]):
        vt_ref[j] = val[:, j * LANES:(j + 1) * LANES]
    g1t_ref[...] = jax.nn.silu(q[2 * hc:]).astype(BF16)


def _transpose_cast_kernel(w_ref, o_ref):
    o_ref[...] = w_ref[...].T.astype(o_ref.dtype)


def _transpose_cast(w, col0, ncols, tile):
    k = w.shape[0]
    assert col0 % tile == 0 and ncols % tile == 0
    return pl.pallas_call(
        _transpose_cast_kernel,
        grid=(ncols // tile,),
        in_specs=[pl.BlockSpec((k, tile), lambda j: (0, j + col0 // tile))],
        out_specs=pl.BlockSpec((tile, k), lambda j: (j, 0)),
        out_shape=jax.ShapeDtypeStruct((ncols, k), BF16),
        compiler_params=_cparams(("arbitrary",)),
        name="wtrans",
    )(w)


def _project_t(w_refs, h):
    nt = (((1,), (1,)), ((), ()))
    return jnp.concatenate([lax.dot_general(w[...], h, nt, preferred_element_type=F32)
                            for w in w_refs], axis=0)


def _inproj_w_kernel(x_ref, a_ref, s_ref, ws_ref, wv_ref, wg_ref, wa_ref, ut_ref, g2t_ref, vt_ref,
                     g1t_ref):
    r, _, d = x_ref.shape
    sw = ut_ref.shape[0]
    hc = g1t_ref.shape[0]
    h = _modulated_norm(x_ref[...], a_ref[...], s_ref[...]).reshape(r * BATCH, d).astype(BF16)
    q = _project_t([ws_ref], h)
    ut_ref[...] = q[:sw].astype(BF16)
    g2t_ref[...] = jax.nn.silu(q[sw:]).astype(BF16)
    _conv_gates(_project_t([wv_ref, wg_ref, wa_ref], h), hc, vt_ref, g1t_ref)


def _conv_weight_specs(hc, d, part, index_map):
    return [pl.BlockSpec((hc, d), functools.partial(index_map, 2 * k + part)) for k in range(3)]


def _inproj_w(x4, a8, s8, wt_s, wt_c, hc):
    r, _, _, d = x4.shape
    sw = wt_s.shape[0] // 2
    tl = r * BATCH
    nq = tl // LANES
    cpr = GRID_W // CHUNK
    s5_spec = pl.BlockSpec((None, sw, tl), lambda w: (w % CHUNK, 0, w // CHUNK))
    return pl.pallas_call(
        _inproj_w_kernel,
        grid=(GRID_W,),
        in_specs=[pl.BlockSpec((r, None, BATCH, d), lambda w: (0, w, 0, 0)),
                  pl.BlockSpec((BATCH, d), lambda w: (0, 0)),
                  pl.BlockSpec((BATCH, d), lambda w: (0, 0)),
                  pl.BlockSpec(wt_s.shape, lambda w: (0, 0)),
                  *_conv_weight_specs(hc, d, 0, lambda blk, w: (blk, 0))],
        out_specs=[s5_spec, s5_spec,
                   pl.BlockSpec((None, nq, hc, LANES), lambda w: (w, 0, 0, 0)),
                   pl.BlockSpec((hc, tl), lambda w: (0, w))],
        out_shape=[jax.ShapeDtypeStruct((CHUNK, sw, cpr * tl), BF16),
                   jax.ShapeDtypeStruct((CHUNK, sw, cpr * tl), BF16),
                   jax.ShapeDtypeStruct((GRID_W, nq, hc, LANES), BF16),
                   jax.ShapeDtypeStruct((hc, GRID_W * tl), BF16)],
        compiler_params=_cparams(("arbitrary",)),
        name="inproj_w",
    )(x4, a8, s8, wt_s, wt_c, wt_c, wt_c)


def _inproj_r_kernel(x_ref, a_ref, s_ref, wv_ref, wg_ref, wa_ref, vt_ref, g1t_ref):
    gw, _, d = x_ref.shape
    hc = g1t_ref.shape[0]
    h = _modulated_norm(x_ref[...], a_ref[...], s_ref[...]).reshape(gw * BATCH, d).astype(BF16)
    _conv_gates(_project_t([wv_ref, wg_ref, wa_ref], h), hc, vt_ref, g1t_ref)


def _inproj_r(x4, a8, s8, wt_c, hc):
    r, gw, _, d = x4.shape
    tl = gw * BATCH
    nq = tl // LANES
    return pl.pallas_call(
        _inproj_r_kernel,
        grid=(r,),
        in_specs=[pl.BlockSpec((None, gw, BATCH, d), lambda i: (i, 0, 0, 0)),
                  pl.BlockSpec((BATCH, d), lambda i: (0, 0)),
                  pl.BlockSpec((BATCH, d), lambda i: (0, 0)),
                  *_conv_weight_specs(hc, d, 1, lambda blk, i: (blk, 0))],
        out_specs=[pl.BlockSpec((None, nq, hc, LANES), lambda i: (i, 0, 0, 0)),
                   pl.BlockSpec((hc, tl), lambda i: (0, i))],
        out_shape=[jax.ShapeDtypeStruct((r, nq, hc, LANES), BF16),
                   jax.ShapeDtypeStruct((hc, r * tl), BF16)],
        compiler_params=_cparams(("arbitrary",)),
        name="inproj_r",
    )(x4, a8, s8, wt_c, wt_c, wt_c)


def _ctxproj_kernel(x_ref, a_ref, s_ref, wut_ref, ut_ref):
    cb, _, d = x_ref.shape
    x = x_ref[...].reshape(cb * 2, BATCH, d)
    h = _modulated_norm(x, a_ref[...], s_ref[...]).reshape(cb, 2 * BATCH, d)
    for tl in range(2):
        ht = h[:, tl * BATCH:(tl + 1) * BATCH, :].reshape(cb * BATCH, d).astype(BF16)
        q = lax.dot_general(wut_ref[...], ht, (((1,), (1,)), ((), ())),
                            preferred_element_type=F32)
        ut_ref[tl] = q.astype(BF16)


def _ctxproj(x4, a8, s8, wt_s):
    nch, _, _, d = x4.shape
    sw = wt_s.shape[0] // 2
    return pl.pallas_call(
        _ctxproj_kernel,
        grid=(CHUNK // 2,),
        in_specs=[pl.BlockSpec((nch, None, 2 * BATCH, d), lambda t: (0, t, 0, 0)),
                  pl.BlockSpec((BATCH, d), lambda t: (0, 0)),
                  pl.BlockSpec((BATCH, d), lambda t: (0, 0)),
                  pl.BlockSpec((sw, d), lambda t: (0, 0))],
        out_specs=pl.BlockSpec((2, sw, nch * BATCH), lambda t: (t, 0, 0)),
        out_shape=jax.ShapeDtypeStruct((CHUNK, sw, nch * BATCH), BF16),
        compiler_params=_cparams(("arbitrary",)),
        name="ctxproj",
    )(x4, a8, s8, wt_s)


GELU_C1 = math.sqrt(2.0 / math.pi)
GELU_C2 = GELU_C1 * 0.044715


def _gelu_tanh(x):
    hx = 0.5 * x
    return hx + hx * jnp.tanh(x * (GELU_C1 + GELU_C2 * (x * x)))


def _s5_kernel(xt_ref, xct_ref, wall_ref, cot_ref, lam_ref, dcol_ref, y_ref,
               s_sc, h_sc, yi_sc, *, lane_chunk):
    nl = xt_ref.shape[-1]
    nlc = xct_ref.shape[-1]
    n_ch = nl // BATCH
    n_chc = nlc // BATCH
    kdim = CHUNK * SSM_GROUP
    p = SSM_STATE

    def group_x(ref, j):
        return ref[:, j * SSM_GROUP:(j + 1) * SSM_GROUP, :].reshape(kdim, ref.shape[-1])

    st_c = [jnp.dot(wall_ref[j, kdim:], group_x(xct_ref, j), preferred_element_type=F32)
            for j in range(2)]
    st = []
    for j in range(2):
        r = jnp.dot(wall_ref[j], group_x(xt_ref, j), preferred_element_type=F32)
        yi_sc[j] = r[:kdim]
        st.append(r[kdim:])
    for d in range(2):
        for src, off, width in ((st_c, 0, nlc), (st, nlc, nl)):
            for ri in range(2):
                lo = (2 * d + ri) * p
                pair = jnp.concatenate([src[0][lo:lo + p], src[1][lo:lo + p]], axis=0)
                s_sc[d, ri, off:off + width, :] = pair.T

    lam = [[jnp.broadcast_to(lam_ref[d, ri:ri + 1, :], (BATCH, LANES)) for ri in range(2)]
           for d in range(2)]

    def advance(d, state, row):
        hr, hi = state
        ar, ai = lam[d]
        sr = s_sc[d, 0, pl.ds(row, BATCH), :]
        si = s_sc[d, 1, pl.ds(row, BATCH), :]
        return ar * hr - ai * hi + sr, ar * hi + ai * hr + si

    zero = jnp.zeros((BATCH, LANES), F32)

    def ctx_step(i, carry):
        f, b = carry
        f = advance(0, f, pl.multiple_of(i * BATCH, BATCH))
        b = advance(1, b, pl.multiple_of((n_chc - 1 - i) * BATCH, BATCH))
        return f, b

    carry = lax.fori_loop(0, n_chc, ctx_step, ((zero, zero), (zero, zero)))

    cpr = GRID_W // CHUNK
    n_rows = n_ch // cpr

    def lane_row(ch):
        within = jnp.bitwise_and(ch, cpr - 1)
        row = lax.shift_right_logical(ch, cpr.bit_length() - 1)
        return pl.multiple_of((within * n_rows + row) * BATCH, BATCH)

    def lat_step(i, carry):
        f, b = carry
        rf = lane_row(i)
        rb = lane_row(n_ch - 1 - i)
        h_sc[0, 0, pl.ds(rf, BATCH), :] = f[0]
        h_sc[0, 1, pl.ds(rf, BATCH), :] = f[1]
        h_sc[1, 0, pl.ds(rb, BATCH), :] = b[0]
        h_sc[1, 1, pl.ds(rb, BATCH), :] = b[1]
        f = advance(0, f, pl.multiple_of(nlc + rf, BATCH))
        b = advance(1, b, pl.multiple_of(nlc + rb, BATCH))
        return f, b

    lax.fori_loop(0, n_ch, lat_step, carry, unroll=8)

    for c0 in range(0, nl, lane_chunk):
        cs = slice(c0, c0 + lane_chunk)
        ht = [[h_sc[d, ri, cs, :].T for ri in range(2)] for d in range(2)]
        for j in range(2):
            x = xt_ref[:, j * SSM_GROUP:(j + 1) * SSM_GROUP, cs].reshape(kdim, lane_chunk)
            hj = jnp.concatenate([ht[d][ri][j * p:(j + 1) * p] for d in range(2) for ri in range(2)],
                                 axis=0).astype(BF16)
            acc = (yi_sc[j, :, cs] + dcol_ref[j] * x.astype(F32)
                   + jnp.dot(cot_ref[j], hj, preferred_element_type=F32))
            y = _gelu_tanh(acc).astype(BF16)
            y_ref[:, j * SSM_GROUP:(j + 1) * SSM_GROUP, cs] = y.reshape(CHUNK, SSM_GROUP, lane_chunk)


def _s5(ut, uct, wall, cot, lam16, dcol):
    _, sw, nl = ut.shape
    nlc = uct.shape[-1]
    gp = sw // (2 * SSM_GROUP)
    kdim = CHUNK * SSM_GROUP
    lane_chunk = min(nl, 512)
    kern = functools.partial(_s5_kernel, lane_chunk=lane_chunk)
    return pl.pallas_call(
        kern,
        grid=(gp,),
        in_specs=[pl.BlockSpec((CHUNK, 2 * SSM_GROUP, nl), lambda g: (0, g, 0)),
                  pl.BlockSpec((CHUNK, 2 * SSM_GROUP, nlc), lambda g: (0, g, 0)),
                  pl.BlockSpec((2, 2 * kdim, kdim), lambda g: (g, 0, 0)),
                  pl.BlockSpec((2, kdim, kdim), lambda g: (g, 0, 0)),
                  pl.BlockSpec((2, None, 2, LANES), lambda g: (0, g, 0, 0)),
                  pl.BlockSpec((2, kdim, 1), lambda g: (g, 0, 0))],
        out_specs=pl.BlockSpec((CHUNK, 2 * SSM_GROUP, nl), lambda g: (0, g, 0)),
        out_shape=jax.ShapeDtypeStruct((CHUNK, sw, nl), BF16),
        scratch_shapes=[pltpu.VMEM((2, 2, nlc + nl, LANES), F32),
                        pltpu.VMEM((2, 2, nl, LANES), F32),
                        pltpu.VMEM((2, kdim, nl), F32)],
        compiler_params=_cparams(("arbitrary",)),
        name="s5",
    )(ut, uct, wall, cot, lam16, dcol)


CONV_CB = 128
CONV_GROUP = 1
CONV_SKEW = 8


def _convt_kernel(v_ref, tq_ref, b_ref, o_ref, s1_ref, s2_ref, z_sc, o_sc):
    a, cb, _ = v_ref.shape
    pitch = cb + CONV_SKEW
    z = v_ref[...].astype(F32)
    for i in range(a):
        z_sc[i * pitch:i * pitch + cb, :] = z[i]
    s1 = jnp.zeros((a, LANES), F32)
    s2 = jnp.zeros((a, LANES), F32)
    for g in range(cb // CONV_GROUP):
        c0 = g * CONV_GROUP
        rhs = jnp.concatenate([z_sc[pl.ds(c0 + j, a, stride=pitch), :] for j in range(CONV_GROUP)],
                              axis=0).astype(BF16)
        out = jnp.dot(tq_ref[g], rhs, preferred_element_type=F32).reshape(CONV_GROUP, a, LANES)
        out = out + b_ref[c0:c0 + CONV_GROUP].reshape(CONV_GROUP, 1, 1)
        for j in range(CONV_GROUP):
            o_sc[pl.ds(c0 + j, a, stride=pitch), :] = out[j]
        s1 = s1 + jnp.sum(out, axis=0)
        s2 = s2 + jnp.sum(out * out, axis=0)
    for i in range(a):
        o_ref[i] = o_sc[i * pitch:i * pitch + cb, :].astype(o_ref.dtype)

    @pl.when(pl.program_id(1) == 0)
    def _():
        s1_ref[...] = jnp.zeros_like(s1_ref)
        s2_ref[...] = jnp.zeros_like(s2_ref)

    s1_ref[...] += s1
    s2_ref[...] += s2


def _convt(vt, tq, bias):
    a, nq, c, _ = vt.shape
    cb = min(c, CONV_CB)
    ka = CONV_GROUP * a
    stat_spec = pl.BlockSpec((a, LANES), lambda q, i: (0, q))
    return pl.pallas_call(
        _convt_kernel,
        grid=(nq, c // cb),
        in_specs=[pl.BlockSpec((a, None, cb, LANES), lambda q, i: (0, q, i, 0)),
                  pl.BlockSpec((cb // CONV_GROUP, ka, ka), lambda q, i: (i, 0, 0)),
                  pl.BlockSpec((cb, 1), lambda q, i: (i, 0))],
        out_specs=[pl.BlockSpec((a, cb, LANES), lambda q, i: (0, i, q)), stat_spec, stat_spec],
        out_shape=[jax.ShapeDtypeStruct((a, c, nq * LANES), BF16),
                   jax.ShapeDtypeStruct((a, nq * LANES), F32),
                   jax.ShapeDtypeStruct((a, nq * LANES), F32)],
        scratch_shapes=[pltpu.VMEM((a * (cb + CONV_SKEW), LANES), F32),
                        pltpu.VMEM((a * (cb + CONV_SKEW), LANES), F32)],
        compiler_params=_cparams(("arbitrary", "arbitrary")),
        name="convt",
    )(vt, tq, bias.reshape(c, 1))


@functools.lru_cache(maxsize=None)
def _toeplitz_selector(a):
    g = CONV_GROUP
    pos = np.arange(a)
    tap = pos[None, :] - pos[:, None] + CONV_PAD
    hit = tap[None] == np.arange(CONV_TAPS)[:, None, None]
    idx = np.arange(g)
    diag = (idx[:, None, None] == idx[None, :, None]) & (idx[None, :, None] == idx[None, None, :])
    sel = hit[:, None, None, :, None, :] & diag[None, :, :, None, :, None]
    return sel.reshape(CONV_TAPS * g, g * a, g * a).astype(BF16)


def _conv_toeplitz(w, a):
    c = w.shape[1]
    g = CONV_GROUP
    w4 = w.reshape(CONV_TAPS, c // g, g).transpose(1, 0, 2).reshape(c // g, CONV_TAPS * g)
    return jnp.einsum("qk,kxy->qxy", w4.astype(BF16), jnp.asarray(_toeplitz_selector(a)),
                      preferred_element_type=BF16)


def _conv_branch_t(cv, g1, st, lng_ref, lnb_ref, n_chan):
    mean = (st[0:1] + st[2:3]) * (1.0 / n_chan)
    var = (st[1:2] + st[3:4]) * (1.0 / n_chan) - mean * mean
    y = (cv.astype(F32) - mean) * lax.rsqrt(var + EPS) * lng_ref[...] + lnb_ref[...]
    return (jax.nn.silu(y) * g1.astype(F32)).astype(BF16)


def _tail_w_kernel(y_ref, g2_ref, cv_ref, g1_ref, st_ref, wt_ref, bt_ref, wos_ref, woh_ref,
                   lng_ref, lnb_ref, o_ref, *, n_chan):
    tl = y_ref.shape[-1]
    tn = (((0,), (0,)), ((), ()))
    st = st_ref[...]
    mix = []
    for k in range(2):
        lanes = slice(k * tl, (k + 1) * tl)
        y = y_ref[k]
        z = jnp.dot(wt_ref[...], y, preferred_element_type=F32) + bt_ref[...]
        s = (y.astype(F32) * jax.nn.sigmoid(z) * g2_ref[k].astype(F32)).astype(BF16)
        co = _conv_branch_t(cv_ref[k], g1_ref[:, lanes], st[:, lanes], lng_ref, lnb_ref, n_chan)
        m = (lax.dot_general(s, wos_ref[...], tn, preferred_element_type=F32)
             + lax.dot_general(co, woh_ref[...], tn, preferred_element_type=F32))
        mix.append(m.reshape(tl // BATCH, BATCH, m.shape[-1]))
    o_ref[...] = jnp.concatenate(mix, axis=1).astype(o_ref.dtype)


def _tail_w(y1t, g2t, ct_h, g1t_h, stats_h, glu_wt, glu_bcol, wo_s, wo_h, ln_g, ln_b, n_chan):
    _, sw, _ = y1t.shape
    _, hc, tl = ct_h.shape
    d = wo_s.shape[1]
    r = tl // BATCH
    half_chunk = CHUNK // 2
    s5_spec = pl.BlockSpec((2, sw, tl), lambda j: (j % half_chunk, 0, j // half_chunk))
    tok = lambda rows: pl.BlockSpec((rows, 2 * tl), lambda j: (0, j))
    full = lambda arr: pl.BlockSpec(arr.shape, lambda j: (0,) * arr.ndim)
    return pl.pallas_call(
        functools.partial(_tail_w_kernel, n_chan=n_chan),
        grid=(GRID_W // 2,),
        in_specs=[s5_spec, s5_spec, pl.BlockSpec((2, hc, tl), lambda j: (j, 0, 0)), tok(hc),
                  tok(8), full(glu_wt), full(glu_bcol), full(wo_s), full(wo_h), full(ln_g),
                  full(ln_b)],
        out_specs=pl.BlockSpec((r, None, 2 * BATCH, d), lambda j: (0, j, 0, 0)),
        out_shape=jax.ShapeDtypeStruct((r, GRID_W // 2, 2 * BATCH, d), BF16),
        compiler_params=_cparams(("arbitrary",)),
        name="tail_w",
    )(y1t, g2t, ct_h, g1t_h, stats_h, glu_wt, glu_bcol, wo_s, wo_h, ln_g, ln_b)


def _final_r_kernel(cv_ref, g1_ref, st_ref, mh_ref, x_ref, wo_ref, lng_ref, lnb_ref, gate_ref,
                    fg_ref, o_ref, mix_sc, *, n_chan):
    tl, d = x_ref.shape[1:]
    co = _conv_branch_t(cv_ref[...], g1_ref[...], st_ref[...], lng_ref, lnb_ref, n_chan)
    mix = lax.dot_general(co, wo_ref[...], (((0,), (0,)), ((), ())),
                          preferred_element_type=F32) + mh_ref[...].reshape(tl * BATCH, d)
    nlt = mix_sc.shape[0]
    for j in range(nlt):
        mix_sc[j] = mix[:, j * LANES:(j + 1) * LANES]
    for b in range(BATCH):
        mb = jnp.concatenate([mix_sc[j, pl.ds(b, tl, stride=BATCH), :] for j in range(nlt)],
                             axis=-1)
        xo = x_ref[b] + gate_ref[b:b + 1, :] * mb
        ms = jnp.mean(xo * xo, axis=-1, keepdims=True)
        o_ref[b] = xo * lax.rsqrt(ms + EPS) * fg_ref[...]


def _final_r(ct_v, g1t_v, stats_v, mix_h, x, wo_v, ln_g, ln_b, gate8, final_g, n_chan):
    r, hc, tm = ct_v.shape
    _, length, d = x.shape
    tl = tm // BATCH
    tok = lambda rows: pl.BlockSpec((rows, tm), lambda i: (0, i))
    nat = pl.BlockSpec((BATCH, tl, d), lambda i: (0, i, 0))
    full = lambda arr: pl.BlockSpec(arr.shape, lambda i: (0,) * arr.ndim)
    return pl.pallas_call(
        functools.partial(_final_r_kernel, n_chan=n_chan),
        grid=(r,),
        in_specs=[pl.BlockSpec((None, hc, tm), lambda i: (i, 0, 0)), tok(hc), tok(8),
                  pl.BlockSpec((None, tl // 2, 2 * BATCH, d), lambda i: (i, 0, 0, 0)), nat,
                  full(wo_v), full(ln_g), full(ln_b), full(gate8), full(final_g)],
        out_specs=nat,
        out_shape=jax.ShapeDtypeStruct(x.shape, F32),
        scratch_shapes=[pltpu.VMEM((d // LANES, tm, LANES), F32)],
        compiler_params=_cparams(("arbitrary",)),
        name="final_r",
    )(ct_v, g1t_v, stats_v, mix_h, x, wo_v, ln_g, ln_b, gate8, final_g)


def _s5prep_kernel(arow_ref, bt_ref, ct_ref, wall_ref, cot_ref, lam_ref, pw_sc, kk_sc):
    kdim = CHUNK * SSM_GROUP
    hp = lax.Precision.HIGHEST
    lane = lax.broadcasted_iota(jnp.int32, (1, LANES), 1)
    is_re = lane < SSM_STATE
    col_t = lax.broadcasted_iota(jnp.int32, (1, kdim), 1) // SSM_GROUP
    kcol = lax.broadcasted_iota(jnp.int32, (pw_sc.shape[1], 1), 0).astype(F32)

    def tile_rows(a):
        return jnp.concatenate([a] * CHUNK, axis=0)

    def power_rows(exps):
        pr = jnp.concatenate([jnp.broadcast_to(pw_sc[0, e:e + 1, :], (SSM_GROUP, LANES))
                              for e in exps], axis=0)
        pi = jnp.concatenate([jnp.broadcast_to(pw_sc[1, e:e + 1, :], (SSM_GROUP, LANES))
                              for e in exps], axis=0)
        return pr, pi

    def cmul(xr, xi, yr, yi, im_sign):
        return jnp.where(is_re, xr * yr - xi * yi, im_sign * (xr * yi + xi * yr))

    toeplitz = jnp.zeros((kdim, kdim), F32)
    for d in range(2):
        ar, ai = arow_ref[d, 0:1, :], arow_ref[d, 1:2, :]
        dt = jnp.exp(arow_ref[d, 2:3, :])
        mag = jnp.exp(kcol * (ar * dt))
        ang = kcol * (ai * dt)
        pw_sc[0] = mag * jnp.cos(ang)
        pw_sc[1] = mag * jnp.sin(ang)
        lam_ref[d, 0] = pw_sc[0, CHUNK:CHUNK + 8, :]
        lam_ref[d, 1] = pw_sc[1, CHUNK:CHUNK + 8, :]
        nr, ni = pw_sc[0, 1:2, :] - 1.0, pw_sc[1, 1:2, :]
        den = ar * ar + ai * ai
        qr, qi = (nr * ar + ni * ai) / den, (ni * ar - nr * ai) / den
        btr, bti = bt_ref[d, 0], bt_ref[d, 1]
        bbr, bbi = tile_rows(qr * btr - qi * bti), tile_rows(qr * bti + qi * btr)
        cr, ci = tile_rows(ct_ref[d, 0]), tile_rows(ct_ref[d, 1])
        tt = list(range(CHUNK))
        e_lag = [CHUNK - 1 - t for t in tt] if d else tt
        e_in = tt if d else [CHUNK - 1 - t for t in tt]
        e_out = [CHUNK - t for t in tt] if d else [t + 1 for t in tt]
        cot_ref[:, d * LANES:(d + 1) * LANES] = cmul(cr, ci, *power_rows(e_out), -1.0).astype(BF16)
        wall_ref[kdim + d * LANES:kdim + (d + 1) * LANES, :] = cmul(
            bbr, bbi, *power_rows(e_in), 1.0).T.astype(BF16)
        c_lag = cmul(cr, ci, *power_rows(e_lag), -1.0)
        bb = jnp.where(is_re, bbr, bbi)
        kk = lax.dot_general(c_lag, bb, (((1,), (1,)), ((), ())), precision=hp,
                             preferred_element_type=F32)
        zeros = jnp.zeros((kdim, kdim), F32)
        if d:
            kk_sc[:kdim] = kk
            kk_sc[kdim:] = zeros
            starts = [SSM_GROUP * (CHUNK - 1 - t) for t in tt]
        else:
            kk_sc[:kdim] = zeros
            kk_sc[kdim:] = kk
            starts = [kdim - SSM_GROUP * t for t in tt]
        tiles = []
        for lt in range(kdim // LANES):
            cols = slice(lt * LANES, (lt + 1) * LANES)
            acc = jnp.zeros((kdim, LANES), F32)
            for t in range(lt * LANES // SSM_GROUP, (lt + 1) * LANES // SSM_GROUP):
                acc = jnp.where(col_t[:, cols] == t, kk_sc[starts[t]:starts[t] + kdim, cols], acc)
            tiles.append(acc)
        toeplitz = toeplitz + jnp.concatenate(tiles, axis=1)
    wall_ref[:kdim, :] = toeplitz.astype(BF16)


def _s5prep(arow, bt, ct):
    g = arow.shape[0]
    kdim = CHUNK * SSM_GROUP
    return pl.pallas_call(
        _s5prep_kernel,
        grid=(g,),
        in_specs=[pl.BlockSpec((None, 2, 8, LANES), lambda i: (i, 0, 0, 0)),
                  pl.BlockSpec((None, 2, 2, SSM_GROUP, LANES), lambda i: (i, 0, 0, 0, 0)),
                  pl.BlockSpec((None, 2, 2, SSM_GROUP, LANES), lambda i: (i, 0, 0, 0, 0))],
        out_specs=[pl.BlockSpec((None, 2 * kdim, kdim), lambda i: (i, 0, 0)),
                   pl.BlockSpec((None, kdim, kdim), lambda i: (i, 0, 0)),
                   pl.BlockSpec((None, 2, 2, 8, LANES), lambda i: (i, 0, 0, 0, 0))],
        out_shape=[jax.ShapeDtypeStruct((g, 2 * kdim, kdim), BF16),
                   jax.ShapeDtypeStruct((g, kdim, kdim), BF16),
                   jax.ShapeDtypeStruct((g, 2, 2, 8, LANES), F32)],
        scratch_shapes=[pltpu.VMEM((2, CHUNK + 8, LANES), F32),
                        pltpu.VMEM((2 * kdim, kdim), F32)],
        compiler_params=_cparams(("arbitrary",)),
        name="s5prep",
    )(arow, bt, ct)


def _s5prep_inputs(a_re, a_im, log_dt, b_re, b_im, c_re, c_im):
    g = a_re.shape[1]
    dup = lambda a: jnp.concatenate([a, a], axis=-1)
    rows = jnp.stack([dup(a_re), dup(a_im),
                      jnp.broadcast_to(log_dt[..., None], (2, g, LANES))], axis=2)
    arow = jnp.pad(rows, ((0, 0), (0, 0), (0, 5), (0, 0))).transpose(1, 0, 2, 3)
    bt = jnp.stack([dup(b_re.transpose(0, 1, 3, 2)), dup(b_im.transpose(0, 1, 3, 2))], axis=2)
    ct = jnp.stack([dup(c_re), dup(c_im)], axis=2)
    return arow, bt.transpose(1, 0, 2, 3, 4), ct.transpose(1, 0, 2, 3, 4)


def kernel(x, c, ctx, c_ctx, norm_g, w_ada, b_ada, w_in, conv_dw, conv_db, conv_ln_g, conv_ln_b,
           ssm_a_re, ssm_a_im, ssm_log_dt, ssm_b_re, ssm_b_im, ssm_c_re, ssm_c_im, ssm_d,
           ssm_glu_w, ssm_glu_b, w_out, final_g):
    bsz, length, d = x.shape
    ctx_len = ctx.shape[1]
    cw = conv_dw.shape[-1]
    sw = ssm_d.shape[-1]
    assert bsz == BATCH and norm_g.shape[0] == 1
    assert length % (GRID_W * CHUNK) == 0 and ctx_len % CHUNK == 0
    n = length * BATCH
    nch = length // CHUNK
    nchc = ctx_len // CHUNK
    rows = length // GRID_W

    cond16 = jnp.zeros((16, d), F32).at[:BATCH].set(c).at[BATCH].set(c_ctx)
    mod = _ada(cond16, w_ada.reshape(w_ada.shape[1:]), b_ada[0])
    shift, scale, gate = mod[:, :d], mod[:, d:2 * d], mod[:, 2 * d:]
    amp = norm_g[0][None] * (1.0 + scale)
    a8, s8, gate8 = amp[:BATCH], shift[:BATCH], gate[:BATCH]
    a8c = jnp.broadcast_to(amp[BATCH], (BATCH, d))
    s8c = jnp.broadcast_to(shift[BATCH], (BATCH, d))

    w = w_in.reshape(w_in.shape[1:])
    half = cw // 2
    wt_c = _transpose_cast(w, 0, 3 * cw, half)
    wt_s = _transpose_cast(w, 3 * cw, 2 * sw, half)
    wall, cot, lam_raw = _s5prep(*_s5prep_inputs(
        ssm_a_re[0], ssm_a_im[0], ssm_log_dt[0], ssm_b_re[0], ssm_b_im[0], ssm_c_re[0], ssm_c_im[0]))
    ng = sw // SSM_GROUP
    lam16 = lam_raw[:, :, :, 0, :SSM_STATE].reshape(ng // 2, 2, 2, 2, SSM_STATE).transpose(
        2, 0, 3, 1, 4).reshape(2, ng // 2, 2, LANES)
    dcol = jnp.broadcast_to(ssm_d[0].reshape(ng, 1, SSM_GROUP), (ng, CHUNK, SSM_GROUP)).reshape(
        ng, CHUNK * SSM_GROUP, 1)

    xt = x.transpose(1, 0, 2)
    ctxt = ctx.transpose(1, 0, 2)
    c4 = ctxt.reshape(nchc, CHUNK // 2, 2 * BATCH, d)

    x4 = xt.reshape(rows, GRID_W, BATCH, d)
    ut, g2t, vt_h, g1t_h = _inproj_w(x4, a8, s8, wt_s, wt_c, half)
    vt_v, g1t_v = _inproj_r(x4, a8, s8, wt_c, half)
    uct = _ctxproj(c4, a8c, s8c, wt_s)
    y1t = _s5(ut, uct, wall, cot, lam16, dcol)

    dw, db = conv_dw[0], conv_db[0]
    ct_h, s1_h, s2_h = _convt(vt_h, _conv_toeplitz(dw[:, :half], GRID_W), db[:half])
    ct_v, s1_v, s2_v = _convt(vt_v, _conv_toeplitz(dw[:, half:], rows), db[half:])
    to_v = lambda a: a.reshape(GRID_W, rows, BATCH).transpose(1, 0, 2).reshape(1, n)
    to_h = lambda a: a.reshape(rows, GRID_W, BATCH).transpose(1, 0, 2).reshape(1, n)
    pad4 = jnp.zeros((4, n), F32)
    stats_v = jnp.concatenate([s1_v.reshape(1, n), s2_v.reshape(1, n), to_v(s1_h), to_v(s2_h), pad4])
    stats_h = jnp.concatenate([s1_h.reshape(1, n), s2_h.reshape(1, n), to_h(s1_v), to_h(s2_v), pad4])

    ln_g, ln_b = conv_ln_g[0].reshape(cw, 1), conv_ln_b[0].reshape(cw, 1)
    wo = w_out.reshape(w_out.shape[1:]).astype(BF16)
    mix_h = _tail_w(y1t, g2t, ct_h, g1t_h, stats_h, _transpose_cast(ssm_glu_w[0], 0, sw, half),
                    ssm_glu_b[0].reshape(sw, 1), wo[cw:], wo[:half], ln_g[:half], ln_b[:half], cw)
    return _final_r(ct_v, g1t_v, stats_v, mix_h, x, wo[half:cw], ln_g[half:],
                    ln_b[half:], gate8, final_g.reshape(1, d), cw)
```

```python
import functools
import math

import jax
import jax.numpy as jnp
import numpy as np
from jax import lax
from jax.experimental import pallas as pl
from jax.experimental.pallas import tpu as pltpu

GRID_W = 64
CONV_TAPS = 31
CONV_PAD = CONV_TAPS // 2
SSM_GROUP = 16
SSM_STATE = 64
CHUNK = 16
BATCH = 8
EPS = 1e-6
LANES = 128
VMEM_LIMIT = 56 * 1024 * 1024

F32 = jnp.float32
BF16 = jnp.bfloat16


def _cparams(sem):
    return pltpu.CompilerParams(dimension_semantics=sem, vmem_limit_bytes=VMEM_LIMIT)


def _ada_kernel(c_ref, w_ref, b_ref, o_ref):
    s = jax.nn.silu(c_ref[...])
    o_ref[...] = jnp.dot(s, w_ref[...], preferred_element_type=F32,
                         precision=lax.Precision.HIGHEST) + b_ref[...]


def _ada(cond16, w_ada, b_ada):
    d, n3 = w_ada.shape
    nb = n3 // d
    return pl.pallas_call(
        _ada_kernel,
        grid=(nb,),
        in_specs=[pl.BlockSpec((16, d), lambda j: (0, 0)),
                  pl.BlockSpec((d, d), lambda j: (0, j)),
                  pl.BlockSpec((1, d), lambda j: (0, j))],
        out_specs=pl.BlockSpec((16, d), lambda j: (0, j)),
        out_shape=jax.ShapeDtypeStruct((16, n3), F32),
        compiler_params=_cparams(("arbitrary",)),
        name="ada",
    )(cond16, w_ada, b_ada.reshape(1, n3))


def _modulated_norm(x, a, s):
    ms = jnp.mean(x * x, axis=-1, keepdims=True)
    return x * lax.rsqrt(ms + EPS) * a[None] + s[None]


MXU_WIDTH = 256


def _lane_pieces(n_lanes):
    width = MXU_WIDTH if n_lanes % MXU_WIDTH == 0 else n_lanes
    return [(lo, width) for lo in range(0, n_lanes, width)]


def _conv_gates(q, hc, vt_ref, g1t_ref, lane0):
    width = q.shape[1]
    val = (q[:hc] * jax.nn.sigmoid(q[hc:2 * hc])).astype(BF16)
    for j in range(width // LANES):
        vt_ref[lane0 // LANES + j] = val[:, j * LANES:(j + 1) * LANES]
    g1t_ref[:, lane0:lane0 + width] = jax.nn.silu(q[2 * hc:]).astype(BF16)


def _transpose_cast_kernel(w_ref, o_ref):
    o_ref[...] = w_ref[...].T.astype(o_ref.dtype)


def _transpose_cast(w, col0, ncols, tile):
    k = w.shape[0]
    assert col0 % tile == 0 and ncols % tile == 0
    return pl.pallas_call(
        _transpose_cast_kernel,
        grid=(ncols // tile,),
        in_specs=[pl.BlockSpec((k, tile), lambda j: (0, j + col0 // tile))],
        out_specs=pl.BlockSpec((tile, k), lambda j: (j, 0)),
        out_shape=jax.ShapeDtypeStruct((ncols, k), BF16),
        compiler_params=_cparams(("arbitrary",)),
        name="wtrans",
    )(w)


def _cast_kernel(w_ref, o_ref):
    o_ref[...] = w_ref[...].astype(o_ref.dtype)


def _cast_bf16(w, tile):
    rows, cols = w.shape
    return pl.pallas_call(
        _cast_kernel,
        grid=(rows // tile,),
        in_specs=[pl.BlockSpec((tile, cols), lambda j: (j, 0))],
        out_specs=pl.BlockSpec((tile, cols), lambda j: (j, 0)),
        out_shape=jax.ShapeDtypeStruct(w.shape, BF16),
        compiler_params=_cparams(("arbitrary",)),
        name="wcast",
    )(w)


def _project_t(w_refs, h):
    nt = (((1,), (1,)), ((), ()))
    return jnp.concatenate([lax.dot_general(w[...], h, nt, preferred_element_type=F32)
                            for w in w_refs], axis=0)


def _inproj_w_kernel(x_ref, a_ref, s_ref, ws_ref, wv_ref, wg_ref, wa_ref, ut_ref, g2t_ref, vt_ref,
                     g1t_ref):
    r, _, d = x_ref.shape
    sw = ut_ref.shape[0]
    hc = g1t_ref.shape[0]
    for lo, width in _lane_pieces(r * BATCH):
        x = x_ref[lo // BATCH:(lo + width) // BATCH]
        h = _modulated_norm(x, a_ref[...], s_ref[...]).reshape(width, d).astype(BF16)
        q = _project_t([ws_ref], h)
        ut_ref[:, lo:lo + width] = q[:sw].astype(BF16)
        g2t_ref[:, lo:lo + width] = jax.nn.silu(q[sw:]).astype(BF16)
        _conv_gates(_project_t([wv_ref, wg_ref, wa_ref], h), hc, vt_ref, g1t_ref, lo)


def _conv_weight_specs(hc, d, part, index_map):
    return [pl.BlockSpec((hc, d), functools.partial(index_map, 2 * k + part)) for k in range(3)]


def _inproj_w(x4, a8, s8, wt_s, wt_c, hc):
    r, _, _, d = x4.shape
    sw = wt_s.shape[0] // 2
    tl = r * BATCH
    nq = tl // LANES
    cpr = GRID_W // CHUNK
    s5_spec = pl.BlockSpec((None, sw, tl), lambda w: (w % CHUNK, 0, w // CHUNK))
    return pl.pallas_call(
        _inproj_w_kernel,
        grid=(GRID_W,),
        in_specs=[pl.BlockSpec((r, None, BATCH, d), lambda w: (0, w, 0, 0)),
                  pl.BlockSpec((BATCH, d), lambda w: (0, 0)),
                  pl.BlockSpec((BATCH, d), lambda w: (0, 0)),
                  pl.BlockSpec(wt_s.shape, lambda w: (0, 0)),
                  *_conv_weight_specs(hc, d, 0, lambda blk, w: (blk, 0))],
        out_specs=[s5_spec, s5_spec,
                   pl.BlockSpec((None, nq, hc, LANES), lambda w: (w, 0, 0, 0)),
                   pl.BlockSpec((hc, tl), lambda w: (0, w))],
        out_shape=[jax.ShapeDtypeStruct((CHUNK, sw, cpr * tl), BF16),
                   jax.ShapeDtypeStruct((CHUNK, sw, cpr * tl), BF16),
                   jax.ShapeDtypeStruct((GRID_W, nq, hc, LANES), BF16),
                   jax.ShapeDtypeStruct((hc, GRID_W * tl), BF16)],
        compiler_params=_cparams(("arbitrary",)),
        name="inproj_w",
    )(x4, a8, s8, wt_s, wt_c, wt_c, wt_c)


def _inproj_r_kernel(x_ref, a_ref, s_ref, wv_ref, wg_ref, wa_ref, vt_ref, g1t_ref):
    gw, _, d = x_ref.shape
    hc = g1t_ref.shape[0]
    for lo, width in _lane_pieces(gw * BATCH):
        x = x_ref[lo // BATCH:(lo + width) // BATCH]
        h = _modulated_norm(x, a_ref[...], s_ref[...]).reshape(width, d).astype(BF16)
        _conv_gates(_project_t([wv_ref, wg_ref, wa_ref], h), hc, vt_ref, g1t_ref, lo)


def _inproj_r(x4, a8, s8, wt_c, hc):
    r, gw, _, d = x4.shape
    tl = gw * BATCH
    nq = tl // LANES
    return pl.pallas_call(
        _inproj_r_kernel,
        grid=(r,),
        in_specs=[pl.BlockSpec((None, gw, BATCH, d), lambda i: (i, 0, 0, 0)),
                  pl.BlockSpec((BATCH, d), lambda i: (0, 0)),
                  pl.BlockSpec((BATCH, d), lambda i: (0, 0)),
                  *_conv_weight_specs(hc, d, 1, lambda blk, i: (blk, 0))],
        out_specs=[pl.BlockSpec((None, nq, hc, LANES), lambda i: (i, 0, 0, 0)),
                   pl.BlockSpec((hc, tl), lambda i: (0, i))],
        out_shape=[jax.ShapeDtypeStruct((r, nq, hc, LANES), BF16),
                   jax.ShapeDtypeStruct((hc, r * tl), BF16)],
        compiler_params=_cparams(("arbitrary",)),
        name="inproj_r",
    )(x4, a8, s8, wt_c, wt_c, wt_c)


def _ctxproj_kernel(x_ref, a_ref, s_ref, wut_ref, ut_ref):
    cb, _, d = x_ref.shape
    x = x_ref[...].reshape(cb * 2, BATCH, d)
    h = _modulated_norm(x, a_ref[...], s_ref[...]).reshape(cb, 2 * BATCH, d)
    for tl in range(2):
        ht = h[:, tl * BATCH:(tl + 1) * BATCH, :].reshape(cb * BATCH, d).astype(BF16)
        q = lax.dot_general(wut_ref[...], ht, (((1,), (1,)), ((), ())),
                            preferred_element_type=F32)
        ut_ref[tl] = q.astype(BF16)


def _ctxproj(x4, a8, s8, wt_s):
    nch, _, _, d = x4.shape
    sw = wt_s.shape[0] // 2
    return pl.pallas_call(
        _ctxproj_kernel,
        grid=(CHUNK // 2,),
        in_specs=[pl.BlockSpec((nch, None, 2 * BATCH, d), lambda t: (0, t, 0, 0)),
                  pl.BlockSpec((BATCH, d), lambda t: (0, 0)),
                  pl.BlockSpec((BATCH, d), lambda t: (0, 0)),
                  pl.BlockSpec((sw, d), lambda t: (0, 0))],
        out_specs=pl.BlockSpec((2, sw, nch * BATCH), lambda t: (t, 0, 0)),
        out_shape=jax.ShapeDtypeStruct((CHUNK, sw, nch * BATCH), BF16),
        compiler_params=_cparams(("arbitrary",)),
        name="ctxproj",
    )(x4, a8, s8, wt_s)


GELU_C1 = math.sqrt(2.0 / math.pi)
GELU_C2 = GELU_C1 * 0.044715


def _gelu_tanh(x):
    hx = 0.5 * x
    return hx + hx * jnp.tanh(x * (GELU_C1 + GELU_C2 * (x * x)))


def _s5_kernel(xt_ref, xct_ref, wall_ref, cot_ref, lam_ref, dcol_ref, y_ref,
               s_sc, h_sc, yi_sc, *, lane_chunk):
    nl = xt_ref.shape[-1]
    nlc = xct_ref.shape[-1]
    n_ch = nl // BATCH
    n_chc = nlc // BATCH
    kdim = CHUNK * SSM_GROUP
    p = SSM_STATE

    def group_x(ref, j):
        return ref[:, j * SSM_GROUP:(j + 1) * SSM_GROUP, :].reshape(kdim, ref.shape[-1])

    st_c = [jnp.dot(wall_ref[j, kdim:], group_x(xct_ref, j), preferred_element_type=F32)
            for j in range(2)]
    st = []
    for j in range(2):
        r = jnp.dot(wall_ref[j], group_x(xt_ref, j), preferred_element_type=F32)
        yi_sc[j] = r[:kdim]
        st.append(r[kdim:])
    for d in range(2):
        for src, off, width in ((st_c, 0, nlc), (st, nlc, nl)):
            for ri in range(2):
                lo = (2 * d + ri) * p
                pair = jnp.concatenate([src[0][lo:lo + p], src[1][lo:lo + p]], axis=0)
                s_sc[d, ri, off:off + width, :] = pair.T

    lam = [[jnp.broadcast_to(lam_ref[d, ri:ri + 1, :], (BATCH, LANES)) for ri in range(2)]
           for d in range(2)]

    def advance(d, state, row):
        hr, hi = state
        ar, ai = lam[d]
        sr = s_sc[d, 0, pl.ds(row, BATCH), :]
        si = s_sc[d, 1, pl.ds(row, BATCH), :]
        return ar * hr - ai * hi + sr, ar * hi + ai * hr + si

    zero = jnp.zeros((BATCH, LANES), F32)

    def ctx_step(i, carry):
        f, b = carry
        f = advance(0, f, pl.multiple_of(i * BATCH, BATCH))
        b = advance(1, b, pl.multiple_of((n_chc - 1 - i) * BATCH, BATCH))
        return f, b

    carry = lax.fori_loop(0, n_chc, ctx_step, ((zero, zero), (zero, zero)))

    cpr = GRID_W // CHUNK
    n_rows = n_ch // cpr

    def lane_row(ch):
        within = jnp.bitwise_and(ch, cpr - 1)
        row = lax.shift_right_logical(ch, cpr.bit_length() - 1)
        return pl.multiple_of((within * n_rows + row) * BATCH, BATCH)

    def lat_step(i, carry):
        f, b = carry
        rf = lane_row(i)
        rb = lane_row(n_ch - 1 - i)
        h_sc[0, 0, pl.ds(rf, BATCH), :] = f[0]
        h_sc[0, 1, pl.ds(rf, BATCH), :] = f[1]
        h_sc[1, 0, pl.ds(rb, BATCH), :] = b[0]
        h_sc[1, 1, pl.ds(rb, BATCH), :] = b[1]
        f = advance(0, f, pl.multiple_of(nlc + rf, BATCH))
        b = advance(1, b, pl.multiple_of(nlc + rb, BATCH))
        return f, b

    lax.fori_loop(0, n_ch, lat_step, carry, unroll=8)

    for c0 in range(0, nl, lane_chunk):
        cs = slice(c0, c0 + lane_chunk)
        ht = [[h_sc[d, ri, cs, :].T for ri in range(2)] for d in range(2)]
        for j in range(2):
            x = xt_ref[:, j * SSM_GROUP:(j + 1) * SSM_GROUP, cs].reshape(kdim, lane_chunk)
            hj = jnp.concatenate([ht[d][ri][j * p:(j + 1) * p] for d in range(2) for ri in range(2)],
                                 axis=0).astype(BF16)
            acc = (yi_sc[j, :, cs] + dcol_ref[j] * x.astype(F32)
                   + jnp.dot(cot_ref[j], hj, preferred_element_type=F32))
            y = _gelu_tanh(acc).astype(BF16)
            y_ref[:, j * SSM_GROUP:(j + 1) * SSM_GROUP, cs] = y.reshape(CHUNK, SSM_GROUP, lane_chunk)


def _s5(ut, uct, wall, cot, lam16, dcol):
    _, sw, nl = ut.shape
    nlc = uct.shape[-1]
    gp = sw // (2 * SSM_GROUP)
    kdim = CHUNK * SSM_GROUP
    lane_chunk = min(nl, 512)
    kern = functools.partial(_s5_kernel, lane_chunk=lane_chunk)
    return pl.pallas_call(
        kern,
        grid=(gp,),
        in_specs=[pl.BlockSpec((CHUNK, 2 * SSM_GROUP, nl), lambda g: (0, g, 0)),
                  pl.BlockSpec((CHUNK, 2 * SSM_GROUP, nlc), lambda g: (0, g, 0)),
                  pl.BlockSpec((2, 2 * kdim, kdim), lambda g: (g, 0, 0)),
                  pl.BlockSpec((2, kdim, kdim), lambda g: (g, 0, 0)),
                  pl.BlockSpec((2, None, 2, LANES), lambda g: (0, g, 0, 0)),
                  pl.BlockSpec((2, kdim, 1), lambda g: (g, 0, 0))],
        out_specs=pl.BlockSpec((CHUNK, 2 * SSM_GROUP, nl), lambda g: (0, g, 0)),
        out_shape=jax.ShapeDtypeStruct((CHUNK, sw, nl), BF16),
        scratch_shapes=[pltpu.VMEM((2, 2, nlc + nl, LANES), F32),
                        pltpu.VMEM((2, 2, nl, LANES), F32),
                        pltpu.VMEM((2, kdim, nl), F32)],
        compiler_params=_cparams(("arbitrary",)),
        name="s5",
    )(ut, uct, wall, cot, lam16, dcol)


CONV_CB = 128
CONV_GROUP = 1
CONV_SKEW = 8


def _convt_kernel(v_ref, tq_ref, b_ref, o_ref, s1_ref, s2_ref, z_sc, o_sc):
    a, cb, _ = v_ref.shape
    pitch = cb + CONV_SKEW
    z = v_ref[...].astype(F32)
    for i in range(a):
        z_sc[i * pitch:i * pitch + cb, :] = z[i]
    s1 = jnp.zeros((a, LANES), F32)
    s2 = jnp.zeros((a, LANES), F32)
    for g in range(cb // CONV_GROUP):
        c0 = g * CONV_GROUP
        rhs = jnp.concatenate([z_sc[pl.ds(c0 + j, a, stride=pitch), :] for j in range(CONV_GROUP)],
                              axis=0).astype(BF16)
        out = jnp.dot(tq_ref[g], rhs, preferred_element_type=F32).reshape(CONV_GROUP, a, LANES)
        out = out + b_ref[c0:c0 + CONV_GROUP].reshape(CONV_GROUP, 1, 1)
        for j in range(CONV_GROUP):
            o_sc[pl.ds(c0 + j, a, stride=pitch), :] = out[j]
        s1 = s1 + jnp.sum(out, axis=0)
        s2 = s2 + jnp.sum(out * out, axis=0)
    for i in range(a):
        o_ref[i] = o_sc[i * pitch:i * pitch + cb, :].astype(o_ref.dtype)

    @pl.when(pl.program_id(1) == 0)
    def _():
        s1_ref[...] = jnp.zeros_like(s1_ref)
        s2_ref[...] = jnp.zeros_like(s2_ref)

    s1_ref[...] += s1
    s2_ref[...] += s2


def _convt(vt, tq, bias):
    a, nq, c, _ = vt.shape
    cb = min(c, CONV_CB)
    ka = CONV_GROUP * a
    stat_spec = pl.BlockSpec((a, LANES), lambda q, i: (0, q))
    return pl.pallas_call(
        _convt_kernel,
        grid=(nq, c // cb),
        in_specs=[pl.BlockSpec((a, None, cb, LANES), lambda q, i: (0, q, i, 0)),
                  pl.BlockSpec((cb // CONV_GROUP, ka, ka), lambda q, i: (i, 0, 0)),
                  pl.BlockSpec((cb, 1), lambda q, i: (i, 0))],
        out_specs=[pl.BlockSpec((a, cb, LANES), lambda q, i: (0, i, q)), stat_spec, stat_spec],
        out_shape=[jax.ShapeDtypeStruct((a, c, nq * LANES), BF16),
                   jax.ShapeDtypeStruct((a, nq * LANES), F32),
                   jax.ShapeDtypeStruct((a, nq * LANES), F32)],
        scratch_shapes=[pltpu.VMEM((a * (cb + CONV_SKEW), LANES), F32),
                        pltpu.VMEM((a * (cb + CONV_SKEW), LANES), F32)],
        compiler_params=_cparams(("arbitrary", "arbitrary")),
        name="convt",
    )(vt, tq, bias.reshape(c, 1))


@functools.lru_cache(maxsize=None)
def _toeplitz_selector(a):
    g = CONV_GROUP
    pos = np.arange(a)
    tap = pos[None, :] - pos[:, None] + CONV_PAD
    hit = tap[None] == np.arange(CONV_TAPS)[:, None, None]
    idx = np.arange(g)
    diag = (idx[:, None, None] == idx[None, :, None]) & (idx[None, :, None] == idx[None, None, :])
    sel = hit[:, None, None, :, None, :] & diag[None, :, :, None, :, None]
    return sel.reshape(CONV_TAPS * g, g * a, g * a).astype(BF16)


def _conv_toeplitz(w, a):
    c = w.shape[1]
    g = CONV_GROUP
    w4 = w.reshape(CONV_TAPS, c // g, g).transpose(1, 0, 2).reshape(c // g, CONV_TAPS * g)
    return jnp.einsum("qk,kxy->qxy", w4.astype(BF16), jnp.asarray(_toeplitz_selector(a)),
                      preferred_element_type=BF16)


def _conv_branch_t(cv, g1, st, lng_ref, lnb_ref, n_chan):
    mean = (st[0:1] + st[2:3]) * (1.0 / n_chan)
    var = (st[1:2] + st[3:4]) * (1.0 / n_chan) - mean * mean
    y = (cv.astype(F32) - mean) * lax.rsqrt(var + EPS) * lng_ref[...] + lnb_ref[...]
    return (jax.nn.silu(y) * g1.astype(F32)).astype(BF16)


def _tail_w_kernel(y_ref, g2_ref, cv_ref, g1_ref, st_ref, wt_ref, bt_ref, wos_ref, woh_ref,
                   lng_ref, lnb_ref, o_ref, *, n_chan):
    tl = y_ref.shape[-1]
    tn = (((0,), (0,)), ((), ()))
    st = st_ref[...]
    mix = []
    for k in range(2):
        lanes = slice(k * tl, (k + 1) * tl)
        y = y_ref[k]
        z = jnp.dot(wt_ref[...], y, preferred_element_type=F32) + bt_ref[...]
        s = (y.astype(F32) * jax.nn.sigmoid(z) * g2_ref[k].astype(F32)).astype(BF16)
        co = _conv_branch_t(cv_ref[k], g1_ref[:, lanes], st[:, lanes], lng_ref, lnb_ref, n_chan)
        m = (lax.dot_general(s, wos_ref[...], tn, preferred_element_type=F32)
             + lax.dot_general(co, woh_ref[...], tn, preferred_element_type=F32))
        mix.append(m.reshape(tl // BATCH, BATCH, m.shape[-1]))
    o_ref[...] = jnp.concatenate(mix, axis=1).astype(o_ref.dtype)


def _tail_w(y1t, g2t, ct_h, g1t_h, stats_h, glu_wt, glu_bcol, wo, ln_g, ln_b, n_chan):
    _, sw, _ = y1t.shape
    _, hc, tl = ct_h.shape
    d = wo.shape[1]
    assert (2 * hc) % sw == 0
    r = tl // BATCH
    half_chunk = CHUNK // 2
    s5_spec = pl.BlockSpec((2, sw, tl), lambda j: (j % half_chunk, 0, j // half_chunk))
    tok = lambda rows: pl.BlockSpec((rows, 2 * tl), lambda j: (0, j))
    full = lambda arr: pl.BlockSpec(arr.shape, lambda j: (0,) * arr.ndim)
    return pl.pallas_call(
        functools.partial(_tail_w_kernel, n_chan=n_chan),
        grid=(GRID_W // 2,),
        in_specs=[s5_spec, s5_spec, pl.BlockSpec((2, hc, tl), lambda j: (j, 0, 0)), tok(hc),
                  tok(8), full(glu_wt), full(glu_bcol),
                  pl.BlockSpec((sw, d), lambda j: (2 * hc // sw, 0)),
                  pl.BlockSpec((hc, d), lambda j: (0, 0)), full(ln_g), full(ln_b)],
        out_specs=pl.BlockSpec((r, None, 2 * BATCH, d), lambda j: (0, j, 0, 0)),
        out_shape=jax.ShapeDtypeStruct((r, GRID_W // 2, 2 * BATCH, d), BF16),
        compiler_params=_cparams(("arbitrary",)),
        name="tail_w",
    )(y1t, g2t, ct_h, g1t_h, stats_h, glu_wt, glu_bcol, wo, wo, ln_g, ln_b)


def _final_r_kernel(cv_ref, g1_ref, st_ref, mh_ref, x_ref, wo_ref, lng_ref, lnb_ref, gate_ref,
                    fg_ref, o_ref, mix_sc, *, n_chan):
    tl, d = x_ref.shape[1:]
    co = _conv_branch_t(cv_ref[...], g1_ref[...], st_ref[...], lng_ref, lnb_ref, n_chan)
    mix = lax.dot_general(co, wo_ref[...], (((0,), (0,)), ((), ())),
                          preferred_element_type=F32) + mh_ref[...].reshape(tl * BATCH, d)
    nlt = mix_sc.shape[0]
    for j in range(nlt):
        mix_sc[j] = mix[:, j * LANES:(j + 1) * LANES]
    for b in range(BATCH):
        mb = jnp.concatenate([mix_sc[j, pl.ds(b, tl, stride=BATCH), :] for j in range(nlt)],
                             axis=-1)
        xo = x_ref[b] + gate_ref[b:b + 1, :] * mb
        ms = jnp.mean(xo * xo, axis=-1, keepdims=True)
        o_ref[b] = xo * lax.rsqrt(ms + EPS) * fg_ref[...]


def _final_r(ct_v, g1t_v, stats_v, mix_h, x, wo, ln_g, ln_b, gate8, final_g, n_chan):
    r, hc, tm = ct_v.shape
    _, length, d = x.shape
    tl = tm // BATCH
    tok = lambda rows: pl.BlockSpec((rows, tm), lambda i: (0, i))
    nat = pl.BlockSpec((BATCH, tl, d), lambda i: (0, i, 0))
    full = lambda arr: pl.BlockSpec(arr.shape, lambda i: (0,) * arr.ndim)
    return pl.pallas_call(
        functools.partial(_final_r_kernel, n_chan=n_chan),
        grid=(r,),
        in_specs=[pl.BlockSpec((None, hc, tm), lambda i: (i, 0, 0)), tok(hc), tok(8),
                  pl.BlockSpec((None, tl // 2, 2 * BATCH, d), lambda i: (i, 0, 0, 0)), nat,
                  pl.BlockSpec((hc, d), lambda i: (1, 0)), full(ln_g), full(ln_b), full(gate8),
                  full(final_g)],
        out_specs=nat,
        out_shape=jax.ShapeDtypeStruct(x.shape, F32),
        scratch_shapes=[pltpu.VMEM((d // LANES, tm, LANES), F32)],
        compiler_params=_cparams(("arbitrary",)),
        name="final_r",
    )(ct_v, g1t_v, stats_v, mix_h, x, wo, ln_g, ln_b, gate8, final_g)


def _s5prep_kernel(arow_ref, bt_ref, ct_ref, wall_ref, cot_ref, lam_ref, pw_sc, kk_sc):
    kdim = CHUNK * SSM_GROUP
    hp = lax.Precision.HIGHEST
    lane = lax.broadcasted_iota(jnp.int32, (1, LANES), 1)
    is_re = lane < SSM_STATE
    col_t = lax.broadcasted_iota(jnp.int32, (1, kdim), 1) // SSM_GROUP
    kcol = lax.broadcasted_iota(jnp.int32, (pw_sc.shape[1], 1), 0).astype(F32)

    def tile_rows(a):
        return jnp.concatenate([a] * CHUNK, axis=0)

    def power_rows(exps):
        pr = jnp.concatenate([jnp.broadcast_to(pw_sc[0, e:e + 1, :], (SSM_GROUP, LANES))
                              for e in exps], axis=0)
        pi = jnp.concatenate([jnp.broadcast_to(pw_sc[1, e:e + 1, :], (SSM_GROUP, LANES))
                              for e in exps], axis=0)
        return pr, pi

    def cmul(xr, xi, yr, yi, im_sign):
        return jnp.where(is_re, xr * yr - xi * yi, im_sign * (xr * yi + xi * yr))

    toeplitz = jnp.zeros((kdim, kdim), F32)
    for d in range(2):
        ar, ai = arow_ref[d, 0:1, :], arow_ref[d, 1:2, :]
        dt = jnp.exp(arow_ref[d, 2:3, :])
        mag = jnp.exp(kcol * (ar * dt))
        ang = kcol * (ai * dt)
        pw_sc[0] = mag * jnp.cos(ang)
        pw_sc[1] = mag * jnp.sin(ang)
        lam_ref[d, 0] = pw_sc[0, CHUNK:CHUNK + 8, :]
        lam_ref[d, 1] = pw_sc[1, CHUNK:CHUNK + 8, :]
        nr, ni = pw_sc[0, 1:2, :] - 1.0, pw_sc[1, 1:2, :]
        den = ar * ar + ai * ai
        qr, qi = (nr * ar + ni * ai) / den, (ni * ar - nr * ai) / den
        btr, bti = bt_ref[d, 0], bt_ref[d, 1]
        bbr, bbi = tile_rows(qr * btr - qi * bti), tile_rows(qr * bti + qi * btr)
        cr, ci = tile_rows(ct_ref[d, 0]), tile_rows(ct_ref[d, 1])
        tt = list(range(CHUNK))
        e_lag = [CHUNK - 1 - t for t in tt] if d else tt
        e_in = tt if d else [CHUNK - 1 - t for t in tt]
        e_out = [CHUNK - t for t in tt] if d else [t + 1 for t in tt]
        cot_ref[:, d * LANES:(d + 1) * LANES] = cmul(cr, ci, *power_rows(e_out), -1.0).astype(BF16)
        wall_ref[kdim + d * LANES:kdim + (d + 1) * LANES, :] = cmul(
            bbr, bbi, *power_rows(e_in), 1.0).T.astype(BF16)
        c_lag = cmul(cr, ci, *power_rows(e_lag), -1.0)
        bb = jnp.where(is_re, bbr, bbi)
        kk = lax.dot_general(c_lag, bb, (((1,), (1,)), ((), ())), precision=hp,
                             preferred_element_type=F32)
        zeros = jnp.zeros((kdim, kdim), F32)
        if d:
            kk_sc[:kdim] = kk
            kk_sc[kdim:] = zeros
            starts = [SSM_GROUP * (CHUNK - 1 - t) for t in tt]
        else:
            kk_sc[:kdim] = zeros
            kk_sc[kdim:] = kk
            starts = [kdim - SSM_GROUP * t for t in tt]
        tiles = []
        for lt in range(kdim // LANES):
            cols = slice(lt * LANES, (lt + 1) * LANES)
            acc = jnp.zeros((kdim, LANES), F32)
            for t in range(lt * LANES // SSM_GROUP, (lt + 1) * LANES // SSM_GROUP):
                acc = jnp.where(col_t[:, cols] == t, kk_sc[starts[t]:starts[t] + kdim, cols], acc)
            tiles.append(acc)
        toeplitz = toeplitz + jnp.concatenate(tiles, axis=1)
    wall_ref[:kdim, :] = toeplitz.astype(BF16)


def _s5prep(arow, bt, ct):
    g = arow.shape[0]
    kdim = CHUNK * SSM_GROUP
    return pl.pallas_call(
        _s5prep_kernel,
        grid=(g,),
        in_specs=[pl.BlockSpec((None, 2, 8, LANES), lambda i: (i, 0, 0, 0)),
                  pl.BlockSpec((None, 2, 2, SSM_GROUP, LANES), lambda i: (i, 0, 0, 0, 0)),
                  pl.BlockSpec((None, 2, 2, SSM_GROUP, LANES), lambda i: (i, 0, 0, 0, 0))],
        out_specs=[pl.BlockSpec((None, 2 * kdim, kdim), lambda i: (i, 0, 0)),
                   pl.BlockSpec((None, kdim, kdim), lambda i: (i, 0, 0)),
                   pl.BlockSpec((None, 2, 2, 8, LANES), lambda i: (i, 0, 0, 0, 0))],
        out_shape=[jax.ShapeDtypeStruct((g, 2 * kdim, kdim), BF16),
                   jax.ShapeDtypeStruct((g, kdim, kdim), BF16),
                   jax.ShapeDtypeStruct((g, 2, 2, 8, LANES), F32)],
        scratch_shapes=[pltpu.VMEM((2, CHUNK + 8, LANES), F32),
                        pltpu.VMEM((2 * kdim, kdim), F32)],
        compiler_params=_cparams(("arbitrary",)),
        name="s5prep",
    )(arow, bt, ct)


def _s5prep_inputs(a_re, a_im, log_dt, b_re, b_im, c_re, c_im):
    g = a_re.shape[1]
    dup = lambda a: jnp.concatenate([a, a], axis=-1)
    rows = jnp.stack([dup(a_re), dup(a_im),
                      jnp.broadcast_to(log_dt[..., None], (2, g, LANES))], axis=2)
    arow = jnp.pad(rows, ((0, 0), (0, 0), (0, 5), (0, 0))).transpose(1, 0, 2, 3)
    bt = jnp.stack([dup(b_re.transpose(0, 1, 3, 2)), dup(b_im.transpose(0, 1, 3, 2))], axis=2)
    ct = jnp.stack([dup(c_re), dup(c_im)], axis=2)
    return arow, bt.transpose(1, 0, 2, 3, 4), ct.transpose(1, 0, 2, 3, 4)


def kernel(x, c, ctx, c_ctx, norm_g, w_ada, b_ada, w_in, conv_dw, conv_db, conv_ln_g, conv_ln_b,
           ssm_a_re, ssm_a_im, ssm_log_dt, ssm_b_re, ssm_b_im, ssm_c_re, ssm_c_im, ssm_d,
           ssm_glu_w, ssm_glu_b, w_out, final_g):
    bsz, length, d = x.shape
    ctx_len = ctx.shape[1]
    cw = conv_dw.shape[-1]
    sw = ssm_d.shape[-1]
    assert bsz == BATCH and norm_g.shape[0] == 1
    assert length % (GRID_W * CHUNK) == 0 and ctx_len % CHUNK == 0
    n = length * BATCH
    nch = length // CHUNK
    nchc = ctx_len // CHUNK
    rows = length // GRID_W

    cond16 = jnp.zeros((16, d), F32).at[:BATCH].set(c).at[BATCH].set(c_ctx)
    mod = _ada(cond16, w_ada.reshape(w_ada.shape[1:]), b_ada[0])
    shift, scale, gate = mod[:, :d], mod[:, d:2 * d], mod[:, 2 * d:]
    amp = norm_g[0][None] * (1.0 + scale)
    a8, s8, gate8 = amp[:BATCH], shift[:BATCH], gate[:BATCH]
    a8c = jnp.broadcast_to(amp[BATCH], (BATCH, d))
    s8c = jnp.broadcast_to(shift[BATCH], (BATCH, d))

    w = w_in.reshape(w_in.shape[1:])
    half = cw // 2
    wt_c = _transpose_cast(w, 0, 3 * cw, half)
    wt_s = _transpose_cast(w, 3 * cw, 2 * sw, half)
    wall, cot, lam_raw = _s5prep(*_s5prep_inputs(
        ssm_a_re[0], ssm_a_im[0], ssm_log_dt[0], ssm_b_re[0], ssm_b_im[0], ssm_c_re[0], ssm_c_im[0]))
    ng = sw // SSM_GROUP
    lam16 = lam_raw[:, :, :, 0, :SSM_STATE].reshape(ng // 2, 2, 2, 2, SSM_STATE).transpose(
        2, 0, 3, 1, 4).reshape(2, ng // 2, 2, LANES)
    dcol = jnp.broadcast_to(ssm_d[0].reshape(ng, 1, SSM_GROUP), (ng, CHUNK, SSM_GROUP)).reshape(
        ng, CHUNK * SSM_GROUP, 1)

    xt = x.transpose(1, 0, 2)
    ctxt = ctx.transpose(1, 0, 2)
    c4 = ctxt.reshape(nchc, CHUNK // 2, 2 * BATCH, d)

    x4 = xt.reshape(rows, GRID_W, BATCH, d)
    ut, g2t, vt_h, g1t_h = _inproj_w(x4, a8, s8, wt_s, wt_c, half)
    vt_v, g1t_v = _inproj_r(x4, a8, s8, wt_c, half)
    uct = _ctxproj(c4, a8c, s8c, wt_s)
    y1t = _s5(ut, uct, wall, cot, lam16, dcol)

    dw, db = conv_dw[0], conv_db[0]
    ct_h, s1_h, s2_h = _convt(vt_h, _conv_toeplitz(dw[:, :half], GRID_W), db[:half])
    ct_v, s1_v, s2_v = _convt(vt_v, _conv_toeplitz(dw[:, half:], rows), db[half:])
    to_v = lambda a: a.reshape(GRID_W, rows, BATCH).transpose(1, 0, 2).reshape(1, n)
    to_h = lambda a: a.reshape(rows, GRID_W, BATCH).transpose(1, 0, 2).reshape(1, n)
    pad4 = jnp.zeros((4, n), F32)
    stats_v = jnp.concatenate([s1_v.reshape(1, n), s2_v.reshape(1, n), to_v(s1_h), to_v(s2_h), pad4])
    stats_h = jnp.concatenate([s1_h.reshape(1, n), s2_h.reshape(1, n), to_h(s1_v), to_h(s2_v), pad4])

    ln_g, ln_b = conv_ln_g[0].reshape(cw, 1), conv_ln_b[0].reshape(cw, 1)
    wo = _cast_bf16(w_out.reshape(w_out.shape[1:]), half)
    mix_h = _tail_w(y1t, g2t, ct_h, g1t_h, stats_h, _transpose_cast(ssm_glu_w[0], 0, sw, half),
                    ssm_glu_b[0].reshape(sw, 1), wo, ln_g[:half], ln_b[:half], cw)
    return _final_r(ct_v, g1t_v, stats_v, mix_h, x, wo, ln_g[half:],
                    ln_b[half:], gate8, final_g.reshape(1, d), cw)
```

```python
import functools
import math

import jax
import jax.numpy as jnp
import numpy as np
from jax import lax
from jax.experimental import pallas as pl
from jax.experimental.pallas import tpu as pltpu

GRID_W = 64
CONV_TAPS = 31
CONV_PAD = CONV_TAPS // 2
SSM_GROUP = 16
SSM_STATE = 64
CHUNK = 16
BATCH = 8
EPS = 1e-6
LANES = 128
VMEM_LIMIT = 56 * 1024 * 1024

F32 = jnp.float32
BF16 = jnp.bfloat16


def _cparams(sem):
    return pltpu.CompilerParams(dimension_semantics=sem, vmem_limit_bytes=VMEM_LIMIT)


def _ada_kernel(c_ref, w_ref, b_ref, o_ref):
    s = jax.nn.silu(c_ref[...])
    o_ref[...] = jnp.dot(s, w_ref[...], preferred_element_type=F32,
                         precision=lax.Precision.HIGHEST) + b_ref[...]


def _ada(cond16, w_ada, b_ada):
    d, n3 = w_ada.shape
    nb = n3 // d
    return pl.pallas_call(
        _ada_kernel,
        grid=(nb,),
        in_specs=[pl.BlockSpec((16, d), lambda j: (0, 0)),
                  pl.BlockSpec((d, d), lambda j: (0, j)),
                  pl.BlockSpec((1, d), lambda j: (0, j))],
        out_specs=pl.BlockSpec((16, d), lambda j: (0, j)),
        out_shape=jax.ShapeDtypeStruct((16, n3), F32),
        compiler_params=_cparams(("arbitrary",)),
        name="ada",
    )(cond16, w_ada, b_ada.reshape(1, n3))


def _modulated_norm(x, a, s):
    ms = jnp.mean(x * x, axis=-1, keepdims=True)
    return x * lax.rsqrt(ms + EPS) * a[None] + s[None]


MXU_WIDTH = 256


def _lane_pieces(n_lanes):
    width = MXU_WIDTH if n_lanes % MXU_WIDTH == 0 else n_lanes
    return [(lo, width) for lo in range(0, n_lanes, width)]


def _conv_gates(q, hc, vt_ref, g1t_ref, lane0):
    width = q.shape[1]
    val = (q[:hc] * jax.nn.sigmoid(q[hc:2 * hc])).astype(BF16)
    for j in range(width // LANES):
        vt_ref[lane0 // LANES + j] = val[:, j * LANES:(j + 1) * LANES]
    g1t_ref[:, lane0:lane0 + width] = jax.nn.silu(q[2 * hc:]).astype(BF16)


def _transpose_cast_kernel(w_ref, o_ref):
    o_ref[...] = w_ref[...].T.astype(o_ref.dtype)


def _transpose_cast(w, col0, ncols, tile):
    k = w.shape[0]
    assert col0 % tile == 0 and ncols % tile == 0
    return pl.pallas_call(
        _transpose_cast_kernel,
        grid=(ncols // tile,),
        in_specs=[pl.BlockSpec((k, tile), lambda j: (0, j + col0 // tile))],
        out_specs=pl.BlockSpec((tile, k), lambda j: (j, 0)),
        out_shape=jax.ShapeDtypeStruct((ncols, k), BF16),
        compiler_params=_cparams(("arbitrary",)),
        name="wtrans",
    )(w)


def _project_t(w_refs, h):
    nt = (((1,), (1,)), ((), ()))
    return jnp.concatenate([lax.dot_general(w[...], h, nt, preferred_element_type=F32)
                            for w in w_refs], axis=0)


def _inproj_w_kernel(x_ref, a_ref, s_ref, ws_ref, wv_ref, wg_ref, wa_ref, ut_ref, g2t_ref, vt_ref,
                     g1t_ref):
    r, _, d = x_ref.shape
    sw = ut_ref.shape[0]
    hc = g1t_ref.shape[0]
    for lo, width in _lane_pieces(r * BATCH):
        x = x_ref[lo // BATCH:(lo + width) // BATCH]
        h = _modulated_norm(x, a_ref[...], s_ref[...]).reshape(width, d).astype(BF16)
        q = _project_t([ws_ref], h)
        ut_ref[:, lo:lo + width] = q[:sw].astype(BF16)
        g2t_ref[:, lo:lo + width] = jax.nn.silu(q[sw:]).astype(BF16)
        _conv_gates(_project_t([wv_ref, wg_ref, wa_ref], h), hc, vt_ref, g1t_ref, lo)


def _conv_weight_specs(hc, d, part, index_map):
    return [pl.BlockSpec((hc, d), functools.partial(index_map, 2 * k + part)) for k in range(3)]


def _inproj_w(x4, a8, s8, wt_s, wt_c, hc):
    r, _, _, d = x4.shape
    sw = wt_s.shape[0] // 2
    tl = r * BATCH
    nq = tl // LANES
    cpr = GRID_W // CHUNK
    s5_spec = pl.BlockSpec((None, sw, tl), lambda w: (w % CHUNK, 0, w // CHUNK))
    return pl.pallas_call(
        _inproj_w_kernel,
        grid=(GRID_W,),
        in_specs=[pl.BlockSpec((r, None, BATCH, d), lambda w: (0, w, 0, 0)),
                  pl.BlockSpec((BATCH, d), lambda w: (0, 0)),
                  pl.BlockSpec((BATCH, d), lambda w: (0, 0)),
                  pl.BlockSpec(wt_s.shape, lambda w: (0, 0)),
                  *_conv_weight_specs(hc, d, 0, lambda blk, w: (blk, 0))],
        out_specs=[s5_spec, s5_spec,
                   pl.BlockSpec((None, nq, hc, LANES), lambda w: (w, 0, 0, 0)),
                   pl.BlockSpec((hc, tl), lambda w: (0, w))],
        out_shape=[jax.ShapeDtypeStruct((CHUNK, sw, cpr * tl), BF16),
                   jax.ShapeDtypeStruct((CHUNK, sw, cpr * tl), BF16),
                   jax.ShapeDtypeStruct((GRID_W, nq, hc, LANES), BF16),
                   jax.ShapeDtypeStruct((hc, GRID_W * tl), BF16)],
        compiler_params=_cparams(("arbitrary",)),
        name="inproj_w",
    )(x4, a8, s8, wt_s, wt_c, wt_c, wt_c)


def _inproj_r_kernel(x_ref, a_ref, s_ref, wv_ref, wg_ref, wa_ref, vt_ref, g1t_ref):
    gw, _, d = x_ref.shape
    hc = g1t_ref.shape[0]
    for lo, width in _lane_pieces(gw * BATCH):
        x = x_ref[lo // BATCH:(lo + width) // BATCH]
        h = _modulated_norm(x, a_ref[...], s_ref[...]).reshape(width, d).astype(BF16)
        _conv_gates(_project_t([wv_ref, wg_ref, wa_ref], h), hc, vt_ref, g1t_ref, lo)


def _inproj_r(x4, a8, s8, wt_c, hc):
    r, gw, _, d = x4.shape
    tl = gw * BATCH
    nq = tl // LANES
    return pl.pallas_call(
        _inproj_r_kernel,
        grid=(r,),
        in_specs=[pl.BlockSpec((None, gw, BATCH, d), lambda i: (i, 0, 0, 0)),
                  pl.BlockSpec((BATCH, d), lambda i: (0, 0)),
                  pl.BlockSpec((BATCH, d), lambda i: (0, 0)),
                  *_conv_weight_specs(hc, d, 1, lambda blk, i: (blk, 0))],
        out_specs=[pl.BlockSpec((None, nq, hc, LANES), lambda i: (i, 0, 0, 0)),
                   pl.BlockSpec((hc, tl), lambda i: (0, i))],
        out_shape=[jax.ShapeDtypeStruct((r, nq, hc, LANES), BF16),
                   jax.ShapeDtypeStruct((hc, r * tl), BF16)],
        compiler_params=_cparams(("arbitrary",)),
        name="inproj_r",
    )(x4, a8, s8, wt_c, wt_c, wt_c)


def _ctxproj_kernel(x_ref, a_ref, s_ref, wut_ref, ut_ref):
    cb, _, d = x_ref.shape
    x = x_ref[...].reshape(cb * 2, BATCH, d)
    h = _modulated_norm(x, a_ref[...], s_ref[...]).reshape(cb, 2 * BATCH, d)
    for tl in range(2):
        ht = h[:, tl * BATCH:(tl + 1) * BATCH, :].reshape(cb * BATCH, d).astype(BF16)
        q = lax.dot_general(wut_ref[...], ht, (((1,), (1,)), ((), ())),
                            preferred_element_type=F32)
        ut_ref[tl] = q.astype(BF16)


def _ctxproj(x4, a8, s8, wt_s):
    nch, _, _, d = x4.shape
    sw = wt_s.shape[0] // 2
    return pl.pallas_call(
        _ctxproj_kernel,
        grid=(CHUNK // 2,),
        in_specs=[pl.BlockSpec((nch, None, 2 * BATCH, d), lambda t: (0, t, 0, 0)),
                  pl.BlockSpec((BATCH, d), lambda t: (0, 0)),
                  pl.BlockSpec((BATCH, d), lambda t: (0, 0)),
                  pl.BlockSpec((sw, d), lambda t: (0, 0))],
        out_specs=pl.BlockSpec((2, sw, nch * BATCH), lambda t: (t, 0, 0)),
        out_shape=jax.ShapeDtypeStruct((CHUNK, sw, nch * BATCH), BF16),
        compiler_params=_cparams(("arbitrary",)),
        name="ctxproj",
    )(x4, a8, s8, wt_s)


GELU_C1 = math.sqrt(2.0 / math.pi)
GELU_C2 = GELU_C1 * 0.044715


def _gelu_tanh(x):
    hx = 0.5 * x
    return hx + hx * jnp.tanh(x * (GELU_C1 + GELU_C2 * (x * x)))


def _s5_kernel(xt_ref, xct_ref, wall_ref, cot_ref, lam_ref, dcol_ref, y_ref,
               s_sc, h_sc, yi_sc, *, lane_chunk):
    nl = xt_ref.shape[-1]
    nlc = xct_ref.shape[-1]
    n_ch = nl // BATCH
    n_chc = nlc // BATCH
    kdim = CHUNK * SSM_GROUP
    p = SSM_STATE

    def group_x(ref, j):
        return ref[:, j * SSM_GROUP:(j + 1) * SSM_GROUP, :].reshape(kdim, ref.shape[-1])

    st_c = [jnp.dot(wall_ref[j, kdim:], group_x(xct_ref, j), preferred_element_type=F32)
            for j in range(2)]
    st = []
    for j in range(2):
        r = jnp.dot(wall_ref[j], group_x(xt_ref, j), preferred_element_type=F32)
        yi_sc[j] = r[:kdim]
        st.append(r[kdim:])
    for d in range(2):
        for src, off, width in ((st_c, 0, nlc), (st, nlc, nl)):
            for ri in range(2):
                lo = (2 * d + ri) * p
                pair = jnp.concatenate([src[0][lo:lo + p], src[1][lo:lo + p]], axis=0)
                s_sc[d, ri, off:off + width, :] = pair.T

    lam = [[jnp.broadcast_to(lam_ref[d, ri:ri + 1, :], (BATCH, LANES)) for ri in range(2)]
           for d in range(2)]

    def advance(d, state, row):
        hr, hi = state
        ar, ai = lam[d]
        sr = s_sc[d, 0, pl.ds(row, BATCH), :]
        si = s_sc[d, 1, pl.ds(row, BATCH), :]
        return ar * hr - ai * hi + sr, ar * hi + ai * hr + si

    zero = jnp.zeros((BATCH, LANES), F32)

    def ctx_step(i, carry):
        f, b = carry
        f = advance(0, f, pl.multiple_of(i * BATCH, BATCH))
        b = advance(1, b, pl.multiple_of((n_chc - 1 - i) * BATCH, BATCH))
        return f, b

    carry = lax.fori_loop(0, n_chc, ctx_step, ((zero, zero), (zero, zero)))

    cpr = GRID_W // CHUNK
    n_rows = n_ch // cpr

    def lane_row(ch):
        within = jnp.bitwise_and(ch, cpr - 1)
        row = lax.shift_right_logical(ch, cpr.bit_length() - 1)
        return pl.multiple_of((within * n_rows + row) * BATCH, BATCH)

    def lat_step(i, carry):
        f, b = carry
        rf = lane_row(i)
        rb = lane_row(n_ch - 1 - i)
        h_sc[0, 0, pl.ds(rf, BATCH), :] = f[0]
        h_sc[0, 1, pl.ds(rf, BATCH), :] = f[1]
        h_sc[1, 0, pl.ds(rb, BATCH), :] = b[0]
        h_sc[1, 1, pl.ds(rb, BATCH), :] = b[1]
        f = advance(0, f, pl.multiple_of(nlc + rf, BATCH))
        b = advance(1, b, pl.multiple_of(nlc + rb, BATCH))
        return f, b

    lax.fori_loop(0, n_ch, lat_step, carry, unroll=8)

    for c0 in range(0, nl, lane_chunk):
        cs = slice(c0, c0 + lane_chunk)
        ht = [[h_sc[d, ri, cs, :].T for ri in range(2)] for d in range(2)]
        for j in range(2):
            x = xt_ref[:, j * SSM_GROUP:(j + 1) * SSM_GROUP, cs].reshape(kdim, lane_chunk)
            hj = jnp.concatenate([ht[d][ri][j * p:(j + 1) * p] for d in range(2) for ri in range(2)],
                                 axis=0).astype(BF16)
            acc = (yi_sc[j, :, cs] + dcol_ref[j] * x.astype(F32)
                   + jnp.dot(cot_ref[j], hj, preferred_element_type=F32))
            y = _gelu_tanh(acc).astype(BF16)
            y_ref[:, j * SSM_GROUP:(j + 1) * SSM_GROUP, cs] = y.reshape(CHUNK, SSM_GROUP, lane_chunk)


def _s5(ut, uct, wall, cot, lam16, dcol):
    _, sw, nl = ut.shape
    nlc = uct.shape[-1]
    gp = sw // (2 * SSM_GROUP)
    kdim = CHUNK * SSM_GROUP
    lane_chunk = min(nl, 512)
    kern = functools.partial(_s5_kernel, lane_chunk=lane_chunk)
    return pl.pallas_call(
        kern,
        grid=(gp,),
        in_specs=[pl.BlockSpec((CHUNK, 2 * SSM_GROUP, nl), lambda g: (0, g, 0)),
                  pl.BlockSpec((CHUNK, 2 * SSM_GROUP, nlc), lambda g: (0, g, 0)),
                  pl.BlockSpec((2, 2 * kdim, kdim), lambda g: (g, 0, 0)),
                  pl.BlockSpec((2, kdim, kdim), lambda g: (g, 0, 0)),
                  pl.BlockSpec((2, None, 2, LANES), lambda g: (0, g, 0, 0)),
                  pl.BlockSpec((2, kdim, 1), lambda g: (g, 0, 0))],
        out_specs=pl.BlockSpec((CHUNK, 2 * SSM_GROUP, nl), lambda g: (0, g, 0)),
        out_shape=jax.ShapeDtypeStruct((CHUNK, sw, nl), BF16),
        scratch_shapes=[pltpu.VMEM((2, 2, nlc + nl, LANES), F32),
                        pltpu.VMEM((2, 2, nl, LANES), F32),
                        pltpu.VMEM((2, kdim, nl), F32)],
        compiler_params=_cparams(("arbitrary",)),
        name="s5",
    )(ut, uct, wall, cot, lam16, dcol)


CONV_CB = 128
CONV_SKEW = 8


def _convt_kernel(v_ref, t_ref, b_ref, o_ref, s1_ref, s2_ref, z_sc, o_sc):
    a, cb, _ = v_ref.shape
    kl = t_ref.shape[-1]
    pitch = cb + CONV_SKEW
    z = v_ref[...].astype(F32)
    for i in range(a):
        z_sc[i * pitch:i * pitch + cb, :] = z[i]
    s1 = jnp.zeros((a, LANES), F32)
    s2 = jnp.zeros((a, LANES), F32)
    k_pad = jnp.zeros((kl - a, LANES), BF16)
    for c in range(cb):
        col = z_sc[pl.ds(c, a, stride=pitch), :].astype(BF16)
        rhs = jnp.concatenate([col, k_pad], axis=0) if kl > a else col
        out = jnp.dot(t_ref[c], rhs, preferred_element_type=F32) + b_ref[c:c + 1]
        o_sc[pl.ds(c, a, stride=pitch), :] = out
        s1 = s1 + out
        s2 = s2 + out * out
    for i in range(a):
        o_ref[i] = o_sc[i * pitch:i * pitch + cb, :].astype(o_ref.dtype)

    @pl.when(pl.program_id(1) == 0)
    def _():
        s1_ref[...] = jnp.zeros_like(s1_ref)
        s2_ref[...] = jnp.zeros_like(s2_ref)

    s1_ref[...] += s1
    s2_ref[...] += s2


def _convt(vt, t, bias):
    a, nq, c, _ = vt.shape
    cb = min(c, CONV_CB)
    kl = t.shape[-1]
    stat_spec = pl.BlockSpec((a, LANES), lambda q, i: (0, q))
    return pl.pallas_call(
        _convt_kernel,
        grid=(nq, c // cb),
        in_specs=[pl.BlockSpec((a, None, cb, LANES), lambda q, i: (0, q, i, 0)),
                  pl.BlockSpec((cb, a, kl), lambda q, i: (i, 0, 0)),
                  pl.BlockSpec((cb, 1), lambda q, i: (i, 0))],
        out_specs=[pl.BlockSpec((a, cb, LANES), lambda q, i: (0, i, q)), stat_spec, stat_spec],
        out_shape=[jax.ShapeDtypeStruct((a, c, nq * LANES), BF16),
                   jax.ShapeDtypeStruct((a, nq * LANES), F32),
                   jax.ShapeDtypeStruct((a, nq * LANES), F32)],
        scratch_shapes=[pltpu.VMEM((a * (cb + CONV_SKEW), LANES), F32),
                        pltpu.VMEM((a * (cb + CONV_SKEW), LANES), F32)],
        compiler_params=_cparams(("arbitrary", "arbitrary")),
        name="convt",
    )(vt, t, bias.reshape(c, 1))


@functools.lru_cache(maxsize=None)
def _tap_selector(a):
    kl = -(-a // LANES) * LANES
    tap = np.arange(kl)[None, :] - np.arange(a)[:, None] + CONV_PAD
    hit = (tap[None] == np.arange(CONV_TAPS)[:, None, None]) & (np.arange(kl) < a)
    return hit.astype(BF16)


def _conv_toeplitz(w, a):
    return jnp.einsum("kc,kxy->cxy", w.astype(BF16), jnp.asarray(_tap_selector(a)),
                      preferred_element_type=BF16)


def _conv_branch_t(cv, g1, st, lng_ref, lnb_ref, n_chan):
    mean = (st[0:1] + st[2:3]) * (1.0 / n_chan)
    var = (st[1:2] + st[3:4]) * (1.0 / n_chan) - mean * mean
    y = (cv.astype(F32) - mean) * lax.rsqrt(var + EPS) * lng_ref[...] + lnb_ref[...]
    return (jax.nn.silu(y) * g1.astype(F32)).astype(BF16)


def _tail_w_kernel(y_ref, g2_ref, cv_ref, g1_ref, st_ref, wt_ref, bt_ref, wos_ref, woh_ref,
                   lng_ref, lnb_ref, o_ref, mt_sc, *, n_chan):
    tl = y_ref.shape[-1]
    st = st_ref[...]
    mix = []
    for k in range(2):
        lanes = slice(k * tl, (k + 1) * tl)
        y = y_ref[k]
        z = jnp.dot(wt_ref[...], y, preferred_element_type=F32) + bt_ref[...]
        s = (y.astype(F32) * jax.nn.sigmoid(z) * g2_ref[k].astype(F32)).astype(BF16)
        co = _conv_branch_t(cv_ref[k], g1_ref[:, lanes], st[:, lanes], lng_ref, lnb_ref, n_chan)
        mt_sc[...] = (jnp.dot(wos_ref[...], s, preferred_element_type=F32)
                      + jnp.dot(woh_ref[...], co, preferred_element_type=F32))
        mix.append(mt_sc[...].T.reshape(tl // BATCH, BATCH, mt_sc.shape[0]))
    o_ref[...] = jnp.concatenate(mix, axis=1).astype(o_ref.dtype)


def _tail_w(y1t, g2t, ct_h, g1t_h, stats_h, glu_wt, glu_bcol, wot, ln_g, ln_b, n_chan):
    _, sw, _ = y1t.shape
    _, hc, tl = ct_h.shape
    d = wot.shape[0]
    assert (2 * hc) % sw == 0
    r = tl // BATCH
    half_chunk = CHUNK // 2
    s5_spec = pl.BlockSpec((2, sw, tl), lambda j: (j % half_chunk, 0, j // half_chunk))
    tok = lambda rows: pl.BlockSpec((rows, 2 * tl), lambda j: (0, j))
    full = lambda arr: pl.BlockSpec(arr.shape, lambda j: (0,) * arr.ndim)
    return pl.pallas_call(
        functools.partial(_tail_w_kernel, n_chan=n_chan),
        grid=(GRID_W // 2,),
        in_specs=[s5_spec, s5_spec, pl.BlockSpec((2, hc, tl), lambda j: (j, 0, 0)), tok(hc),
                  tok(8), full(glu_wt), full(glu_bcol),
                  pl.BlockSpec((d, sw), lambda j: (0, 2 * hc // sw)),
                  pl.BlockSpec((d, hc), lambda j: (0, 0)), full(ln_g), full(ln_b)],
        out_specs=pl.BlockSpec((r, None, 2 * BATCH, d), lambda j: (0, j, 0, 0)),
        out_shape=jax.ShapeDtypeStruct((r, GRID_W // 2, 2 * BATCH, d), BF16),
        scratch_shapes=[pltpu.VMEM((d, tl), F32)],
        compiler_params=_cparams(("arbitrary",)),
        name="tail_w",
    )(y1t, g2t, ct_h, g1t_h, stats_h, glu_wt, glu_bcol, wot, wot, ln_g, ln_b)


def _final_r_kernel(cv_ref, g1_ref, st_ref, mh_ref, x_ref, wo_ref, lng_ref, lnb_ref, gate_ref,
                    fg_ref, o_ref, mix_sc, *, n_chan):
    tl, d = x_ref.shape[1:]
    co = _conv_branch_t(cv_ref[...], g1_ref[...], st_ref[...], lng_ref, lnb_ref, n_chan)
    mix = lax.dot_general(co.T, wo_ref[...], (((1,), (1,)), ((), ())),
                          preferred_element_type=F32) + mh_ref[...].reshape(tl * BATCH, d)
    nlt = mix_sc.shape[0]
    for j in range(nlt):
        mix_sc[j] = mix[:, j * LANES:(j + 1) * LANES]
    for b in range(BATCH):
        mb = jnp.concatenate([mix_sc[j, pl.ds(b, tl, stride=BATCH), :] for j in range(nlt)],
                             axis=-1)
        xo = x_ref[b] + gate_ref[b:b + 1, :] * mb
        ms = jnp.mean(xo * xo, axis=-1, keepdims=True)
        o_ref[b] = xo * lax.rsqrt(ms + EPS) * fg_ref[...]


def _final_r(ct_v, g1t_v, stats_v, mix_h, x, wot, ln_g, ln_b, gate8, final_g, n_chan):
    r, hc, tm = ct_v.shape
    _, length, d = x.shape
    tl = tm // BATCH
    tok = lambda rows: pl.BlockSpec((rows, tm), lambda i: (0, i))
    nat = pl.BlockSpec((BATCH, tl, d), lambda i: (0, i, 0))
    full = lambda arr: pl.BlockSpec(arr.shape, lambda i: (0,) * arr.ndim)
    return pl.pallas_call(
        functools.partial(_final_r_kernel, n_chan=n_chan),
        grid=(r,),
        in_specs=[pl.BlockSpec((None, hc, tm), lambda i: (i, 0, 0)), tok(hc), tok(8),
                  pl.BlockSpec((None, tl // 2, 2 * BATCH, d), lambda i: (i, 0, 0, 0)), nat,
                  pl.BlockSpec((d, hc), lambda i: (0, 1)), full(ln_g), full(ln_b), full(gate8),
                  full(final_g)],
        out_specs=nat,
        out_shape=jax.ShapeDtypeStruct(x.shape, F32),
        scratch_shapes=[pltpu.VMEM((d // LANES, tm, LANES), F32)],
        compiler_params=_cparams(("arbitrary",)),
        name="final_r",
    )(ct_v, g1t_v, stats_v, mix_h, x, wot, ln_g, ln_b, gate8, final_g)


def _s5prep_kernel(arow_ref, bt_ref, ct_ref, wall_ref, cot_ref, lam_ref, pw_sc, kk_sc):
    kdim = CHUNK * SSM_GROUP
    hp = lax.Precision.HIGHEST
    lane = lax.broadcasted_iota(jnp.int32, (1, LANES), 1)
    is_re = lane < SSM_STATE
    col_t = lax.broadcasted_iota(jnp.int32, (1, kdim), 1) // SSM_GROUP
    kcol = lax.broadcasted_iota(jnp.int32, (pw_sc.shape[1], 1), 0).astype(F32)

    def tile_rows(a):
        return jnp.concatenate([a] * CHUNK, axis=0)

    def power_rows(exps):
        pr = jnp.concatenate([jnp.broadcast_to(pw_sc[0, e:e + 1, :], (SSM_GROUP, LANES))
                              for e in exps], axis=0)
        pi = jnp.concatenate([jnp.broadcast_to(pw_sc[1, e:e + 1, :], (SSM_GROUP, LANES))
                              for e in exps], axis=0)
        return pr, pi

    def cmul(xr, xi, yr, yi, im_sign):
        return jnp.where(is_re, xr * yr - xi * yi, im_sign * (xr * yi + xi * yr))

    toeplitz = jnp.zeros((kdim, kdim), F32)
    for d in range(2):
        ar, ai = arow_ref[d, 0:1, :], arow_ref[d, 1:2, :]
        dt = jnp.exp(arow_ref[d, 2:3, :])
        mag = jnp.exp(kcol * (ar * dt))
        ang = kcol * (ai * dt)
        pw_sc[0] = mag * jnp.cos(ang)
        pw_sc[1] = mag * jnp.sin(ang)
        lam_ref[d, 0] = pw_sc[0, CHUNK:CHUNK + 8, :]
        lam_ref[d, 1] = pw_sc[1, CHUNK:CHUNK + 8, :]
        nr, ni = pw_sc[0, 1:2, :] - 1.0, pw_sc[1, 1:2, :]
        den = ar * ar + ai * ai
        qr, qi = (nr * ar + ni * ai) / den, (ni * ar - nr * ai) / den
        btr, bti = bt_ref[d, 0], bt_ref[d, 1]
        bbr, bbi = tile_rows(qr * btr - qi * bti), tile_rows(qr * bti + qi * btr)
        cr, ci = tile_rows(ct_ref[d, 0]), tile_rows(ct_ref[d, 1])
        tt = list(range(CHUNK))
        e_lag = [CHUNK - 1 - t for t in tt] if d else tt
        e_in = tt if d else [CHUNK - 1 - t for t in tt]
        e_out = [CHUNK - t for t in tt] if d else [t + 1 for t in tt]
        cot_ref[:, d * LANES:(d + 1) * LANES] = cmul(cr, ci, *power_rows(e_out), -1.0).astype(BF16)
        wall_ref[kdim + d * LANES:kdim + (d + 1) * LANES, :] = cmul(
            bbr, bbi, *power_rows(e_in), 1.0).T.astype(BF16)
        c_lag = cmul(cr, ci, *power_rows(e_lag), -1.0)
        bb = jnp.where(is_re, bbr, bbi)
        kk = lax.dot_general(c_lag, bb, (((1,), (1,)), ((), ())), precision=hp,
                             preferred_element_type=F32)
        zeros = jnp.zeros((kdim, kdim), F32)
        if d:
            kk_sc[:kdim] = kk
            kk_sc[kdim:] = zeros
            starts = [SSM_GROUP * (CHUNK - 1 - t) for t in tt]
        else:
            kk_sc[:kdim] = zeros
            kk_sc[kdim:] = kk
            starts = [kdim - SSM_GROUP * t for t in tt]
        tiles = []
        for lt in range(kdim // LANES):
            cols = slice(lt * LANES, (lt + 1) * LANES)
            acc = jnp.zeros((kdim, LANES), F32)
            for t in range(lt * LANES // SSM_GROUP, (lt + 1) * LANES // SSM_GROUP):
                acc = jnp.where(col_t[:, cols] == t, kk_sc[starts[t]:starts[t] + kdim, cols], acc)
            tiles.append(acc)
        toeplitz = toeplitz + jnp.concatenate(tiles, axis=1)
    wall_ref[:kdim, :] = toeplitz.astype(BF16)


def _s5prep(arow, bt, ct):
    g = arow.shape[0]
    kdim = CHUNK * SSM_GROUP
    return pl.pallas_call(
        _s5prep_kernel,
        grid=(g,),
        in_specs=[pl.BlockSpec((None, 2, 8, LANES), lambda i: (i, 0, 0, 0)),
                  pl.BlockSpec((None, 2, 2, SSM_GROUP, LANES), lambda i: (i, 0, 0, 0, 0)),
                  pl.BlockSpec((None, 2, 2, SSM_GROUP, LANES), lambda i: (i, 0, 0, 0, 0))],
        out_specs=[pl.BlockSpec((None, 2 * kdim, kdim), lambda i: (i, 0, 0)),
                   pl.BlockSpec((None, kdim, kdim), lambda i: (i, 0, 0)),
                   pl.BlockSpec((None, 2, 2, 8, LANES), lambda i: (i, 0, 0, 0, 0))],
        out_shape=[jax.ShapeDtypeStruct((g, 2 * kdim, kdim), BF16),
                   jax.ShapeDtypeStruct((g, kdim, kdim), BF16),
                   jax.ShapeDtypeStruct((g, 2, 2, 8, LANES), F32)],
        scratch_shapes=[pltpu.VMEM((2, CHUNK + 8, LANES), F32),
                        pltpu.VMEM((2 * kdim, kdim), F32)],
        compiler_params=_cparams(("arbitrary",)),
        name="s5prep",
    )(arow, bt, ct)


def _s5prep_inputs(a_re, a_im, log_dt, b_re, b_im, c_re, c_im):
    g = a_re.shape[1]
    dup = lambda a: jnp.concatenate([a, a], axis=-1)
    rows = jnp.stack([dup(a_re), dup(a_im),
                      jnp.broadcast_to(log_dt[..., None], (2, g, LANES))], axis=2)
    arow = jnp.pad(rows, ((0, 0), (0, 0), (0, 5), (0, 0))).transpose(1, 0, 2, 3)
    bt = jnp.stack([dup(b_re.transpose(0, 1, 3, 2)), dup(b_im.transpose(0, 1, 3, 2))], axis=2)
    ct = jnp.stack([dup(c_re), dup(c_im)], axis=2)
    return arow, bt.transpose(1, 0, 2, 3, 4), ct.transpose(1, 0, 2, 3, 4)


def kernel(x, c, ctx, c_ctx, norm_g, w_ada, b_ada, w_in, conv_dw, conv_db, conv_ln_g, conv_ln_b,
           ssm_a_re, ssm_a_im, ssm_log_dt, ssm_b_re, ssm_b_im, ssm_c_re, ssm_c_im, ssm_d,
           ssm_glu_w, ssm_glu_b, w_out, final_g):
    bsz, length, d = x.shape
    ctx_len = ctx.shape[1]
    cw = conv_dw.shape[-1]
    sw = ssm_d.shape[-1]
    assert bsz == BATCH and norm_g.shape[0] == 1
    assert length % (GRID_W * CHUNK) == 0 and ctx_len % CHUNK == 0
    n = length * BATCH
    nch = length // CHUNK
    nchc = ctx_len // CHUNK
    rows = length // GRID_W

    cond16 = jnp.zeros((16, d), F32).at[:BATCH].set(c).at[BATCH].set(c_ctx)
    mod = _ada(cond16, w_ada.reshape(w_ada.shape[1:]), b_ada[0])
    shift, scale, gate = mod[:, :d], mod[:, d:2 * d], mod[:, 2 * d:]
    amp = norm_g[0][None] * (1.0 + scale)
    a8, s8, gate8 = amp[:BATCH], shift[:BATCH], gate[:BATCH]
    a8c = jnp.broadcast_to(amp[BATCH], (BATCH, d))
    s8c = jnp.broadcast_to(shift[BATCH], (BATCH, d))

    w = w_in.reshape(w_in.shape[1:])
    half = cw // 2
    wt_c = _transpose_cast(w, 0, 3 * cw, half)
    wt_s = _transpose_cast(w, 3 * cw, 2 * sw, half)
    wall, cot, lam_raw = _s5prep(*_s5prep_inputs(
        ssm_a_re[0], ssm_a_im[0], ssm_log_dt[0], ssm_b_re[0], ssm_b_im[0], ssm_c_re[0], ssm_c_im[0]))
    ng = sw // SSM_GROUP
    lam16 = lam_raw[:, :, :, 0, :SSM_STATE].reshape(ng // 2, 2, 2, 2, SSM_STATE).transpose(
        2, 0, 3, 1, 4).reshape(2, ng // 2, 2, LANES)
    dcol = jnp.broadcast_to(ssm_d[0].reshape(ng, 1, SSM_GROUP), (ng, CHUNK, SSM_GROUP)).reshape(
        ng, CHUNK * SSM_GROUP, 1)

    xt = x.transpose(1, 0, 2)
    ctxt = ctx.transpose(1, 0, 2)
    c4 = ctxt.reshape(nchc, CHUNK // 2, 2 * BATCH, d)

    x4 = xt.reshape(rows, GRID_W, BATCH, d)
    ut, g2t, vt_h, g1t_h = _inproj_w(x4, a8, s8, wt_s, wt_c, half)
    vt_v, g1t_v = _inproj_r(x4, a8, s8, wt_c, half)
    uct = _ctxproj(c4, a8c, s8c, wt_s)
    y1t = _s5(ut, uct, wall, cot, lam16, dcol)

    dw, db = conv_dw[0], conv_db[0]
    ct_h, s1_h, s2_h = _convt(vt_h, _conv_toeplitz(dw[:, :half], GRID_W), db[:half])
    ct_v, s1_v, s2_v = _convt(vt_v, _conv_toeplitz(dw[:, half:], rows), db[half:])
    to_v = lambda a: a.reshape(GRID_W, rows, BATCH).transpose(1, 0, 2).reshape(1, n)
    to_h = lambda a: a.reshape(rows, GRID_W, BATCH).transpose(1, 0, 2).reshape(1, n)
    pad4 = jnp.zeros((4, n), F32)
    stats_v = jnp.concatenate([s1_v.reshape(1, n), s2_v.reshape(1, n), to_v(s1_h), to_v(s2_h), pad4])
    stats_h = jnp.concatenate([s1_h.reshape(1, n), s2_h.reshape(1, n), to_h(s1_v), to_h(s2_v), pad4])

    ln_g, ln_b = conv_ln_g[0].reshape(cw, 1), conv_ln_b[0].reshape(cw, 1)
    wot = _transpose_cast(w_out.reshape(w_out.shape[1:]), 0, d, LANES)
    mix_h = _tail_w(y1t, g2t, ct_h, g1t_h, stats_h, _transpose_cast(ssm_glu_w[0], 0, sw, half),
                    ssm_glu_b[0].reshape(sw, 1), wot, ln_g[:half], ln_b[:half], cw)
    return _final_r(ct_v, g1t_v, stats_v, mix_h, x, wot, ln_g[half:],
                    ln_b[half:], gate8, final_g.reshape(1, d), cw)
```

```python
import functools
import math

import jax
import jax.numpy as jnp
import numpy as np
from jax import lax
from jax.experimental import pallas as pl
from jax.experimental.pallas import tpu as pltpu

GRID_W = 64
CONV_TAPS = 31
CONV_PAD = CONV_TAPS // 2
SSM_GROUP = 16
SSM_STATE = 64
CHUNK = 16
BATCH = 8
EPS = 1e-6
LANES = 128
MXU_WIDTH = 256
VMEM_LIMIT = 56 * 1024 * 1024

F32 = jnp.float32
BF16 = jnp.bfloat16


def _cparams(sem):
    return pltpu.CompilerParams(dimension_semantics=sem, vmem_limit_bytes=VMEM_LIMIT)


def _ada_kernel(c_ref, w_ref, b_ref, o_ref):
    s = jax.nn.silu(c_ref[...])
    o_ref[...] = jnp.dot(s, w_ref[...], preferred_element_type=F32,
                         precision=lax.Precision.HIGHEST) + b_ref[...]


def _ada(cond16, w_ada, b_ada):
    d, n3 = w_ada.shape
    cols = MXU_WIDTH
    return pl.pallas_call(
        _ada_kernel,
        grid=(n3 // cols,),
        in_specs=[pl.BlockSpec((16, d), lambda j: (0, 0)),
                  pl.BlockSpec((d, cols), lambda j: (0, j)),
                  pl.BlockSpec((1, cols), lambda j: (0, j))],
        out_specs=pl.BlockSpec((16, cols), lambda j: (0, j)),
        out_shape=jax.ShapeDtypeStruct((16, n3), F32),
        compiler_params=_cparams(("arbitrary",)),
        name="ada",
    )(cond16, w_ada, b_ada.reshape(1, n3))


def _modulated_norm(x, a, s):
    ms = jnp.mean(x * x, axis=-1, keepdims=True)
    return x * lax.rsqrt(ms + EPS) * a[None] + s[None]


def _lane_pieces(n_lanes):
    width = MXU_WIDTH if n_lanes % MXU_WIDTH == 0 else n_lanes
    return [(lo, width) for lo in range(0, n_lanes, width)]


def _conv_gates(q, hc, vt_ref, g1t_ref, lane0):
    width = q.shape[1]
    val = (q[:hc] * jax.nn.sigmoid(q[hc:2 * hc])).astype(BF16)
    for j in range(width // LANES):
        vt_ref[lane0 // LANES + j] = val[:, j * LANES:(j + 1) * LANES]
    g1t_ref[:, lane0:lane0 + width] = jax.nn.silu(q[2 * hc:]).astype(BF16)


def _transpose_cast_kernel(w_ref, o_ref):
    o_ref[...] = w_ref[...].T.astype(o_ref.dtype)


def _transpose_cast(w, col0, ncols, tile):
    k = w.shape[0]
    assert col0 % tile == 0 and ncols % tile == 0
    return pl.pallas_call(
        _transpose_cast_kernel,
        grid=(ncols // tile,),
        in_specs=[pl.BlockSpec((k, tile), lambda j: (0, j + col0 // tile))],
        out_specs=pl.BlockSpec((tile, k), lambda j: (j, 0)),
        out_shape=jax.ShapeDtypeStruct((ncols, k), BF16),
        compiler_params=_cparams(("arbitrary",)),
        name="wtrans",
    )(w)


def _project_t(w_refs, h):
    nt = (((1,), (1,)), ((), ()))
    return jnp.concatenate([lax.dot_general(w[...], h, nt, preferred_element_type=F32)
                            for w in w_refs], axis=0)


def _inproj_w_kernel(x_ref, a_ref, s_ref, ws_ref, wv_ref, wg_ref, wa_ref, ut_ref, g2t_ref, vt_ref,
                     g1t_ref):
    r, _, d = x_ref.shape
    sw = ut_ref.shape[0]
    hc = g1t_ref.shape[0]
    for lo, width in _lane_pieces(r * BATCH):
        x = x_ref[lo // BATCH:(lo + width) // BATCH]
        h = _modulated_norm(x, a_ref[...], s_ref[...]).reshape(width, d).astype(BF16)
        q = _project_t([ws_ref], h)
        ut_ref[:, lo:lo + width] = q[:sw].astype(BF16)
        g2t_ref[:, lo:lo + width] = jax.nn.silu(q[sw:]).astype(BF16)
        _conv_gates(_project_t([wv_ref, wg_ref, wa_ref], h), hc, vt_ref, g1t_ref, lo)


def _conv_weight_specs(hc, d, part, index_map):
    return [pl.BlockSpec((hc, d), functools.partial(index_map, 2 * k + part)) for k in range(3)]


def _inproj_w(x4, a8, s8, wt_s, wt_c, hc):
    r, _, _, d = x4.shape
    sw = wt_s.shape[0] // 2
    tl = r * BATCH
    nq = tl // LANES
    cpr = GRID_W // CHUNK
    s5_spec = pl.BlockSpec((None, sw, tl), lambda w: (w % CHUNK, 0, w // CHUNK))
    return pl.pallas_call(
        _inproj_w_kernel,
        grid=(GRID_W,),
        in_specs=[pl.BlockSpec((r, None, BATCH, d), lambda w: (0, w, 0, 0)),
                  pl.BlockSpec((BATCH, d), lambda w: (0, 0)),
                  pl.BlockSpec((BATCH, d), lambda w: (0, 0)),
                  pl.BlockSpec(wt_s.shape, lambda w: (0, 0)),
                  *_conv_weight_specs(hc, d, 0, lambda blk, w: (blk, 0))],
        out_specs=[s5_spec, s5_spec,
                   pl.BlockSpec((None, nq, hc, LANES), lambda w: (w, 0, 0, 0)),
                   pl.BlockSpec((hc, tl), lambda w: (0, w))],
        out_shape=[jax.ShapeDtypeStruct((CHUNK, sw, cpr * tl), BF16),
                   jax.ShapeDtypeStruct((CHUNK, sw, cpr * tl), BF16),
                   jax.ShapeDtypeStruct((GRID_W, nq, hc, LANES), BF16),
                   jax.ShapeDtypeStruct((hc, GRID_W * tl), BF16)],
        compiler_params=_cparams(("arbitrary",)),
        name="inproj_w",
    )(x4, a8, s8, wt_s, wt_c, wt_c, wt_c)


def _inproj_r_kernel(x_ref, a_ref, s_ref, wv_ref, wg_ref, wa_ref, vt_ref, g1t_ref):
    gw, _, d = x_ref.shape
    hc = g1t_ref.shape[0]
    for lo, width in _lane_pieces(gw * BATCH):
        x = x_ref[lo // BATCH:(lo + width) // BATCH]
        h = _modulated_norm(x, a_ref[...], s_ref[...]).reshape(width, d).astype(BF16)
        _conv_gates(_project_t([wv_ref, wg_ref, wa_ref], h), hc, vt_ref, g1t_ref, lo)


def _inproj_r(x4, a8, s8, wt_c, hc):
    r, gw, _, d = x4.shape
    tl = gw * BATCH
    nq = tl // LANES
    return pl.pallas_call(
        _inproj_r_kernel,
        grid=(r,),
        in_specs=[pl.BlockSpec((None, gw, BATCH, d), lambda i: (i, 0, 0, 0)),
                  pl.BlockSpec((BATCH, d), lambda i: (0, 0)),
                  pl.BlockSpec((BATCH, d), lambda i: (0, 0)),
                  *_conv_weight_specs(hc, d, 1, lambda blk, i: (blk, 0))],
        out_specs=[pl.BlockSpec((None, nq, hc, LANES), lambda i: (i, 0, 0, 0)),
                   pl.BlockSpec((hc, tl), lambda i: (0, i))],
        out_shape=[jax.ShapeDtypeStruct((r, nq, hc, LANES), BF16),
                   jax.ShapeDtypeStruct((hc, r * tl), BF16)],
        compiler_params=_cparams(("arbitrary",)),
        name="inproj_r",
    )(x4, a8, s8, wt_c, wt_c, wt_c)


def _ctxproj_kernel(x_ref, a_ref, s_ref, wut_ref, ut_ref):
    cb, _, d = x_ref.shape
    x = x_ref[...].reshape(cb * 2, BATCH, d)
    h = _modulated_norm(x, a_ref[...], s_ref[...]).reshape(cb, 2 * BATCH, d)
    for tl in range(2):
        ht = h[:, tl * BATCH:(tl + 1) * BATCH, :].reshape(cb * BATCH, d).astype(BF16)
        q = lax.dot_general(wut_ref[...], ht, (((1,), (1,)), ((), ())),
                            preferred_element_type=F32)
        ut_ref[tl] = q.astype(BF16)


def _ctxproj(x4, a8, s8, wt_s):
    nch, _, _, d = x4.shape
    sw = wt_s.shape[0] // 2
    return pl.pallas_call(
        _ctxproj_kernel,
        grid=(CHUNK // 2,),
        in_specs=[pl.BlockSpec((nch, None, 2 * BATCH, d), lambda t: (0, t, 0, 0)),
                  pl.BlockSpec((BATCH, d), lambda t: (0, 0)),
                  pl.BlockSpec((BATCH, d), lambda t: (0, 0)),
                  pl.BlockSpec((sw, d), lambda t: (0, 0))],
        out_specs=pl.BlockSpec((2, sw, nch * BATCH), lambda t: (t, 0, 0)),
        out_shape=jax.ShapeDtypeStruct((CHUNK, sw, nch * BATCH), BF16),
        compiler_params=_cparams(("arbitrary",)),
        name="ctxproj",
    )(x4, a8, s8, wt_s)


GELU_C1 = math.sqrt(2.0 / math.pi)
GELU_C2 = GELU_C1 * 0.044715


def _gelu_tanh(x):
    hx = 0.5 * x
    return hx + hx * jnp.tanh(x * (GELU_C1 + GELU_C2 * (x * x)))


def _s5_kernel(xt_ref, xct_ref, wall_ref, cot_ref, lam_ref, dcol_ref, y_ref,
               s_sc, h_sc, *, lane_chunk):
    nl = xt_ref.shape[-1]
    nlc = xct_ref.shape[-1]
    n_ch = nl // BATCH
    n_chc = nlc // BATCH
    kdim = CHUNK * SSM_GROUP
    p = SSM_STATE

    def group_x(ref, j):
        return ref[:, j * SSM_GROUP:(j + 1) * SSM_GROUP, :].reshape(kdim, ref.shape[-1])

    st_c = [jnp.dot(wall_ref[j], group_x(xct_ref, j), preferred_element_type=F32)
            for j in range(2)]
    st = [jnp.dot(wall_ref[j], group_x(xt_ref, j), preferred_element_type=F32) for j in range(2)]
    for d in range(2):
        for src, off, width in ((st_c, 0, nlc), (st, nlc, nl)):
            for ri in range(2):
                lo = (2 * d + ri) * p
                pair = jnp.concatenate([src[0][lo:lo + p], src[1][lo:lo + p]], axis=0)
                s_sc[d, ri, off:off + width, :] = pair.T

    lam = [[jnp.broadcast_to(lam_ref[d, ri:ri + 1, :], (BATCH, LANES)) for ri in range(2)]
           for d in range(2)]

    def advance(d, state, row):
        hr, hi = state
        ar, ai = lam[d]
        sr = s_sc[d, 0, pl.ds(row, BATCH), :]
        si = s_sc[d, 1, pl.ds(row, BATCH), :]
        return ar * hr - ai * hi + sr, ar * hi + ai * hr + si

    zero = jnp.zeros((BATCH, LANES), F32)

    def ctx_step(i, carry):
        f, b = carry
        f = advance(0, f, pl.multiple_of(i * BATCH, BATCH))
        b = advance(1, b, pl.multiple_of((n_chc - 1 - i) * BATCH, BATCH))
        return f, b

    carry = lax.fori_loop(0, n_chc, ctx_step, ((zero, zero), (zero, zero)))

    cpr = GRID_W // CHUNK
    n_rows = n_ch // cpr

    def lane_row(ch):
        within = jnp.bitwise_and(ch, cpr - 1)
        row = lax.shift_right_logical(ch, cpr.bit_length() - 1)
        return pl.multiple_of((within * n_rows + row) * BATCH, BATCH)

    def lat_step(i, carry):
        f, b = carry
        rf = lane_row(i)
        rb = lane_row(n_ch - 1 - i)
        h_sc[0, 0, pl.ds(rf, BATCH), :] = f[0]
        h_sc[0, 1, pl.ds(rf, BATCH), :] = f[1]
        h_sc[1, 0, pl.ds(rb, BATCH), :] = b[0]
        h_sc[1, 1, pl.ds(rb, BATCH), :] = b[1]
        f = advance(0, f, pl.multiple_of(nlc + rf, BATCH))
        b = advance(1, b, pl.multiple_of(nlc + rb, BATCH))
        return f, b

    lax.fori_loop(0, n_ch, lat_step, carry, unroll=8)

    for c0 in range(0, nl, lane_chunk):
        cs = slice(c0, c0 + lane_chunk)
        ht = [[h_sc[d, ri, cs, :].T for ri in range(2)] for d in range(2)]
        for j in range(2):
            x = xt_ref[:, j * SSM_GROUP:(j + 1) * SSM_GROUP, cs].reshape(kdim, lane_chunk)
            hj = jnp.concatenate([ht[d][ri][j * p:(j + 1) * p] for d in range(2) for ri in range(2)],
                                 axis=0).astype(BF16)
            acc = (dcol_ref[j] * x.astype(F32)
                   + jnp.dot(cot_ref[j], jnp.concatenate([x, hj], axis=0),
                             preferred_element_type=F32))
            y = _gelu_tanh(acc).astype(BF16)
            y_ref[:, j * SSM_GROUP:(j + 1) * SSM_GROUP, cs] = y.reshape(CHUNK, SSM_GROUP, lane_chunk)


def _s5(ut, uct, wall, cot, lam16, dcol):
    _, sw, nl = ut.shape
    nlc = uct.shape[-1]
    gp = sw // (2 * SSM_GROUP)
    kdim = CHUNK * SSM_GROUP
    lane_chunk = min(nl, 512)
    kern = functools.partial(_s5_kernel, lane_chunk=lane_chunk)
    return pl.pallas_call(
        kern,
        grid=(gp,),
        in_specs=[pl.BlockSpec((CHUNK, 2 * SSM_GROUP, nl), lambda g: (0, g, 0)),
                  pl.BlockSpec((CHUNK, 2 * SSM_GROUP, nlc), lambda g: (0, g, 0)),
                  pl.BlockSpec((2, kdim, kdim), lambda g: (g, 0, 0)),
                  pl.BlockSpec((2, kdim, 2 * kdim), lambda g: (g, 0, 0)),
                  pl.BlockSpec((2, None, 2, LANES), lambda g: (0, g, 0, 0)),
                  pl.BlockSpec((2, kdim, 1), lambda g: (g, 0, 0))],
        out_specs=pl.BlockSpec((CHUNK, 2 * SSM_GROUP, nl), lambda g: (0, g, 0)),
        out_shape=jax.ShapeDtypeStruct((CHUNK, sw, nl), BF16),
        scratch_shapes=[pltpu.VMEM((2, 2, nlc + nl, LANES), F32),
                        pltpu.VMEM((2, 2, nl, LANES), F32)],
        compiler_params=_cparams(("arbitrary",)),
        name="s5",
    )(ut, uct, wall, cot, lam16, dcol)


CONV_CB = 128
CONV_SKEW = 8


def _convt_kernel(v_ref, t_ref, b_ref, o_ref, s1_ref, s2_ref, z_sc, o_sc):
    a, cb, _ = v_ref.shape
    kl = t_ref.shape[-1]
    pitch = cb + CONV_SKEW
    z = v_ref[...].astype(F32)
    for i in range(a):
        z_sc[i * pitch:i * pitch + cb, :] = z[i]
    s1 = jnp.zeros((a, LANES), F32)
    s2 = jnp.zeros((a, LANES), F32)
    k_pad = jnp.zeros((kl - a, LANES), BF16)
    for c in range(cb):
        col = z_sc[pl.ds(c, a, stride=pitch), :].astype(BF16)
        rhs = jnp.concatenate([col, k_pad], axis=0) if kl > a else col
        out = jnp.dot(t_ref[c], rhs, preferred_element_type=F32) + b_ref[c:c + 1]
        o_sc[pl.ds(c, a, stride=pitch), :] = out
        s1 = s1 + out
        s2 = s2 + out * out
    for i in range(a):
        o_ref[i] = o_sc[i * pitch:i * pitch + cb, :].astype(o_ref.dtype)

    @pl.when(pl.program_id(1) == 0)
    def _():
        s1_ref[...] = jnp.zeros_like(s1_ref)
        s2_ref[...] = jnp.zeros_like(s2_ref)

    s1_ref[...] += s1
    s2_ref[...] += s2


def _convt(vt, t, bias):
    a, nq, c, _ = vt.shape
    cb = min(c, CONV_CB)
    kl = t.shape[-1]
    stat_spec = pl.BlockSpec((a, LANES), lambda q, i: (0, q))
    return pl.pallas_call(
        _convt_kernel,
        grid=(nq, c // cb),
        in_specs=[pl.BlockSpec((a, None, cb, LANES), lambda q, i: (0, q, i, 0)),
                  pl.BlockSpec((cb, a, kl), lambda q, i: (i, 0, 0)),
                  pl.BlockSpec((cb, 1), lambda q, i: (i, 0))],
        out_specs=[pl.BlockSpec((a, cb, LANES), lambda q, i: (0, i, q)), stat_spec, stat_spec],
        out_shape=[jax.ShapeDtypeStruct((a, c, nq * LANES), BF16),
                   jax.ShapeDtypeStruct((a, nq * LANES), F32),
                   jax.ShapeDtypeStruct((a, nq * LANES), F32)],
        scratch_shapes=[pltpu.VMEM((a * (cb + CONV_SKEW), LANES), F32),
                        pltpu.VMEM((a * (cb + CONV_SKEW), LANES), F32)],
        compiler_params=_cparams(("arbitrary", "arbitrary")),
        name="convt",
    )(vt, t, bias.reshape(c, 1))


@functools.lru_cache(maxsize=None)
def _tap_selector(a):
    kl = -(-a // LANES) * LANES
    tap = np.arange(kl)[None, :] - np.arange(a)[:, None] + CONV_PAD
    hit = (tap[None] == np.arange(CONV_TAPS)[:, None, None]) & (np.arange(kl) < a)
    return hit.astype(BF16)


def _conv_toeplitz(w, a):
    return jnp.einsum("kc,kxy->cxy", w.astype(BF16), jnp.asarray(_tap_selector(a)),
                      preferred_element_type=BF16)


def _conv_branch_t(cv, g1, st, lng_ref, lnb_ref, n_chan):
    mean = (st[0:1] + st[2:3]) * (1.0 / n_chan)
    var = (st[1:2] + st[3:4]) * (1.0 / n_chan) - mean * mean
    y = (cv.astype(F32) - mean) * lax.rsqrt(var + EPS) * lng_ref[...] + lnb_ref[...]
    return (jax.nn.silu(y) * g1.astype(F32)).astype(BF16)


def _tail_w_kernel(y_ref, g2_ref, cv_ref, g1_ref, st_ref, wt_ref, bt_ref, wos_ref, woh_ref,
                   lng_ref, lnb_ref, o_ref, mt_sc, *, n_chan):
    tl = y_ref.shape[-1]
    st = st_ref[...]
    mix = []
    for k in range(2):
        lanes = slice(k * tl, (k + 1) * tl)
        y = y_ref[k]
        z = jnp.dot(wt_ref[...], y, preferred_element_type=F32) + bt_ref[...]
        s = (y.astype(F32) * jax.nn.sigmoid(z) * g2_ref[k].astype(F32)).astype(BF16)
        co = _conv_branch_t(cv_ref[k], g1_ref[:, lanes], st[:, lanes], lng_ref, lnb_ref, n_chan)
        mt_sc[...] = (jnp.dot(wos_ref[...], s, preferred_element_type=F32)
                      + jnp.dot(woh_ref[...], co, preferred_element_type=F32))
        mix.append(mt_sc[...].T.reshape(tl // BATCH, BATCH, mt_sc.shape[0]))
    o_ref[...] = jnp.concatenate(mix, axis=1).astype(o_ref.dtype)


def _tail_w(y1t, g2t, ct_h, g1t_h, stats_h, glu_wt, glu_bcol, wot, ln_g, ln_b, n_chan):
    _, sw, _ = y1t.shape
    _, hc, tl = ct_h.shape
    d = wot.shape[0]
    assert (2 * hc) % sw == 0
    r = tl // BATCH
    half_chunk = CHUNK // 2
    s5_spec = pl.BlockSpec((2, sw, tl), lambda j: (j % half_chunk, 0, j // half_chunk))
    tok = lambda rows: pl.BlockSpec((rows, 2 * tl), lambda j: (0, j))
    full = lambda arr: pl.BlockSpec(arr.shape, lambda j: (0,) * arr.ndim)
    return pl.pallas_call(
        functools.partial(_tail_w_kernel, n_chan=n_chan),
        grid=(GRID_W // 2,),
        in_specs=[s5_spec, s5_spec, pl.BlockSpec((2, hc, tl), lambda j: (j, 0, 0)), tok(hc),
                  tok(8), full(glu_wt), full(glu_bcol),
                  pl.BlockSpec((d, sw), lambda j: (0, 2 * hc // sw)),
                  pl.BlockSpec((d, hc), lambda j: (0, 0)), full(ln_g), full(ln_b)],
        out_specs=pl.BlockSpec((r, None, 2 * BATCH, d), lambda j: (0, j, 0, 0)),
        out_shape=jax.ShapeDtypeStruct((r, GRID_W // 2, 2 * BATCH, d), BF16),
        scratch_shapes=[pltpu.VMEM((d, tl), F32)],
        compiler_params=_cparams(("arbitrary",)),
        name="tail_w",
    )(y1t, g2t, ct_h, g1t_h, stats_h, glu_wt, glu_bcol, wot, wot, ln_g, ln_b)


def _final_r_kernel(cv_ref, g1_ref, st_ref, mh_ref, x_ref, wo_ref, lng_ref, lnb_ref, gate_ref,
                    fg_ref, o_ref, mix_sc, *, n_chan):
    tl, d = x_ref.shape[1:]
    co = _conv_branch_t(cv_ref[...], g1_ref[...], st_ref[...], lng_ref, lnb_ref, n_chan)
    mix = lax.dot_general(co.T, wo_ref[...], (((1,), (1,)), ((), ())),
                          preferred_element_type=F32) + mh_ref[...].reshape(tl * BATCH, d)
    nlt = mix_sc.shape[0]
    for j in range(nlt):
        mix_sc[j] = mix[:, j * LANES:(j + 1) * LANES]
    for b in range(BATCH):
        mb = jnp.concatenate([mix_sc[j, pl.ds(b, tl, stride=BATCH), :] for j in range(nlt)],
                             axis=-1)
        xo = x_ref[b] + gate_ref[b:b + 1, :] * mb
        ms = jnp.mean(xo * xo, axis=-1, keepdims=True)
        o_ref[b] = xo * lax.rsqrt(ms + EPS) * fg_ref[...]


def _final_r(ct_v, g1t_v, stats_v, mix_h, x, wot, ln_g, ln_b, gate8, final_g, n_chan):
    r, hc, tm = ct_v.shape
    _, length, d = x.shape
    tl = tm // BATCH
    tok = lambda rows: pl.BlockSpec((rows, tm), lambda i: (0, i))
    nat = pl.BlockSpec((BATCH, tl, d), lambda i: (0, i, 0))
    full = lambda arr: pl.BlockSpec(arr.shape, lambda i: (0,) * arr.ndim)
    return pl.pallas_call(
        functools.partial(_final_r_kernel, n_chan=n_chan),
        grid=(r,),
        in_specs=[pl.BlockSpec((None, hc, tm), lambda i: (i, 0, 0)), tok(hc), tok(8),
                  pl.BlockSpec((None, tl // 2, 2 * BATCH, d), lambda i: (i, 0, 0, 0)), nat,
                  pl.BlockSpec((d, hc), lambda i: (0, 1)), full(ln_g), full(ln_b), full(gate8),
                  full(final_g)],
        out_specs=nat,
        out_shape=jax.ShapeDtypeStruct(x.shape, F32),
        scratch_shapes=[pltpu.VMEM((d // LANES, tm, LANES), F32)],
        compiler_params=_cparams(("arbitrary",)),
        name="final_r",
    )(ct_v, g1t_v, stats_v, mix_h, x, wot, ln_g, ln_b, gate8, final_g)


S5PREP_GROUPS = 4


def _s5prep_kernel(arow_ref, bt_ref, ct_ref, wall_ref, cot_ref, lam_ref, pw_sc, kk_sc):
    for gi in range(arow_ref.shape[0]):
        _s5prep_group(arow_ref.at[gi], bt_ref.at[gi], ct_ref.at[gi], wall_ref.at[gi], cot_ref.at[gi],
                      lam_ref.at[gi], pw_sc, kk_sc)


def _s5prep_group(arow_ref, bt_ref, ct_ref, wall_ref, cot_ref, lam_ref, pw_sc, kk_sc):
    kdim = CHUNK * SSM_GROUP
    hp = lax.Precision.HIGHEST
    lane = lax.broadcasted_iota(jnp.int32, (1, LANES), 1)
    is_re = lane < SSM_STATE
    col_t = lax.broadcasted_iota(jnp.int32, (1, kdim), 1) // SSM_GROUP
    kcol = lax.broadcasted_iota(jnp.int32, (pw_sc.shape[1], 1), 0).astype(F32)

    def tile_rows(a):
        return jnp.concatenate([a] * CHUNK, axis=0)

    def power_rows(exps):
        pr = jnp.concatenate([jnp.broadcast_to(pw_sc[0, e:e + 1, :], (SSM_GROUP, LANES))
                              for e in exps], axis=0)
        pi = jnp.concatenate([jnp.broadcast_to(pw_sc[1, e:e + 1, :], (SSM_GROUP, LANES))
                              for e in exps], axis=0)
        return pr, pi

    def cmul(xr, xi, yr, yi, im_sign):
        return jnp.where(is_re, xr * yr - xi * yi, im_sign * (xr * yi + xi * yr))

    toeplitz = jnp.zeros((kdim, kdim), F32)
    for d in range(2):
        ar, ai = arow_ref[d, 0:1, :], arow_ref[d, 1:2, :]
        dt = jnp.exp(arow_ref[d, 2:3, :])
        mag = jnp.exp(kcol * (ar * dt))
        ang = kcol * (ai * dt)
        pw_sc[0] = mag * jnp.cos(ang)
        pw_sc[1] = mag * jnp.sin(ang)
        lam_ref[d, 0] = pw_sc[0, CHUNK:CHUNK + 8, :]
        lam_ref[d, 1] = pw_sc[1, CHUNK:CHUNK + 8, :]
        nr, ni = pw_sc[0, 1:2, :] - 1.0, pw_sc[1, 1:2, :]
        den = ar * ar + ai * ai
        qr, qi = (nr * ar + ni * ai) / den, (ni * ar - nr * ai) / den
        btr, bti = bt_ref[d, 0], bt_ref[d, 1]
        bbr, bbi = tile_rows(qr * btr - qi * bti), tile_rows(qr * bti + qi * btr)
        cr, ci = tile_rows(ct_ref[d, 0]), tile_rows(ct_ref[d, 1])
        tt = list(range(CHUNK))
        e_lag = [CHUNK - 1 - t for t in tt] if d else tt
        e_in = tt if d else [CHUNK - 1 - t for t in tt]
        e_out = [CHUNK - t for t in tt] if d else [t + 1 for t in tt]
        cot_ref[:, kdim + d * LANES:kdim + (d + 1) * LANES] = cmul(
            cr, ci, *power_rows(e_out), -1.0).astype(BF16)
        wall_ref[d * LANES:(d + 1) * LANES, :] = cmul(bbr, bbi, *power_rows(e_in), 1.0).T.astype(BF16)
        c_lag = cmul(cr, ci, *power_rows(e_lag), -1.0)
        bb = jnp.where(is_re, bbr, bbi)
        kk = lax.dot_general(c_lag, bb, (((1,), (1,)), ((), ())), precision=hp,
                             preferred_element_type=F32)
        zeros = jnp.zeros((kdim, kdim), F32)
        if d:
            kk_sc[:kdim] = kk
            kk_sc[kdim:] = zeros
            starts = [SSM_GROUP * (CHUNK - 1 - t) for t in tt]
        else:
            kk_sc[:kdim] = zeros
            kk_sc[kdim:] = kk
            starts = [kdim - SSM_GROUP * t for t in tt]
        tiles = []
        for lt in range(kdim // LANES):
            cols = slice(lt * LANES, (lt + 1) * LANES)
            acc = jnp.zeros((kdim, LANES), F32)
            for t in range(lt * LANES // SSM_GROUP, (lt + 1) * LANES // SSM_GROUP):
                acc = jnp.where(col_t[:, cols] == t, kk_sc[starts[t]:starts[t] + kdim, cols], acc)
            tiles.append(acc)
        toeplitz = toeplitz + jnp.concatenate(tiles, axis=1)
    cot_ref[:, :kdim] = toeplitz.astype(BF16)


def _s5prep(arow, bt, ct):
    g = arow.shape[0]
    gs = math.gcd(g, S5PREP_GROUPS)
    kdim = CHUNK * SSM_GROUP
    return pl.pallas_call(
        _s5prep_kernel,
        grid=(g // gs,),
        in_specs=[pl.BlockSpec((gs, 2, 8, LANES), lambda i: (i, 0, 0, 0)),
                  pl.BlockSpec((gs, 2, 2, SSM_GROUP, LANES), lambda i: (i, 0, 0, 0, 0)),
                  pl.BlockSpec((gs, 2, 2, SSM_GROUP, LANES), lambda i: (i, 0, 0, 0, 0))],
        out_specs=[pl.BlockSpec((gs, kdim, kdim), lambda i: (i, 0, 0)),
                   pl.BlockSpec((gs, kdim, 2 * kdim), lambda i: (i, 0, 0)),
                   pl.BlockSpec((gs, 2, 2, 8, LANES), lambda i: (i, 0, 0, 0, 0))],
        out_shape=[jax.ShapeDtypeStruct((g, kdim, kdim), BF16),
                   jax.ShapeDtypeStruct((g, kdim, 2 * kdim), BF16),
                   jax.ShapeDtypeStruct((g, 2, 2, 8, LANES), F32)],
        scratch_shapes=[pltpu.VMEM((2, CHUNK + 8, LANES), F32),
                        pltpu.VMEM((2 * kdim, kdim), F32)],
        compiler_params=_cparams(("arbitrary",)),
        name="s5prep",
    )(arow, bt, ct)


def _s5prep_inputs(a_re, a_im, log_dt, b_re, b_im, c_re, c_im):
    g = a_re.shape[1]
    dup = lambda a: jnp.concatenate([a, a], axis=-1)
    rows = jnp.stack([dup(a_re), dup(a_im),
                      jnp.broadcast_to(log_dt[..., None], (2, g, LANES))], axis=2)
    arow = jnp.pad(rows, ((0, 0), (0, 0), (0, 5), (0, 0))).transpose(1, 0, 2, 3)
    bt = jnp.stack([dup(b_re.transpose(0, 1, 3, 2)), dup(b_im.transpose(0, 1, 3, 2))], axis=2)
    ct = jnp.stack([dup(c_re), dup(c_im)], axis=2)
    return arow, bt.transpose(1, 0, 2, 3, 4), ct.transpose(1, 0, 2, 3, 4)


def kernel(x, c, ctx, c_ctx, norm_g, w_ada, b_ada, w_in, conv_dw, conv_db, conv_ln_g, conv_ln_b,
           ssm_a_re, ssm_a_im, ssm_log_dt, ssm_b_re, ssm_b_im, ssm_c_re, ssm_c_im, ssm_d,
           ssm_glu_w, ssm_glu_b, w_out, final_g):
    bsz, length, d = x.shape
    ctx_len = ctx.shape[1]
    cw = conv_dw.shape[-1]
    sw = ssm_d.shape[-1]
    assert bsz == BATCH and norm_g.shape[0] == 1
    assert length % (GRID_W * CHUNK) == 0 and ctx_len % CHUNK == 0
    n = length * BATCH
    nch = length // CHUNK
    nchc = ctx_len // CHUNK
    rows = length // GRID_W

    cond16 = jnp.zeros((16, d), F32).at[:BATCH].set(c).at[BATCH].set(c_ctx)
    mod = _ada(cond16, w_ada.reshape(w_ada.shape[1:]), b_ada[0])
    shift, scale, gate = mod[:, :d], mod[:, d:2 * d], mod[:, 2 * d:]
    amp = norm_g[0][None] * (1.0 + scale)
    a8, s8, gate8 = amp[:BATCH], shift[:BATCH], gate[:BATCH]
    a8c = jnp.broadcast_to(amp[BATCH], (BATCH, d))
    s8c = jnp.broadcast_to(shift[BATCH], (BATCH, d))

    w = w_in.reshape(w_in.shape[1:])
    half = cw // 2
    wt_c = _transpose_cast(w, 0, 3 * cw, half)
    wt_s = _transpose_cast(w, 3 * cw, 2 * sw, half)
    wall, cot, lam_raw = _s5prep(*_s5prep_inputs(
        ssm_a_re[0], ssm_a_im[0], ssm_log_dt[0], ssm_b_re[0], ssm_b_im[0], ssm_c_re[0], ssm_c_im[0]))
    ng = sw // SSM_GROUP
    lam16 = lam_raw[:, :, :, 0, :SSM_STATE].reshape(ng // 2, 2, 2, 2, SSM_STATE).transpose(
        2, 0, 3, 1, 4).reshape(2, ng // 2, 2, LANES)
    dcol = jnp.broadcast_to(ssm_d[0].reshape(ng, 1, SSM_GROUP), (ng, CHUNK, SSM_GROUP)).reshape(
        ng, CHUNK * SSM_GROUP, 1)

    xt = x.transpose(1, 0, 2)
    ctxt = ctx.transpose(1, 0, 2)
    c4 = ctxt.reshape(nchc, CHUNK // 2, 2 * BATCH, d)

    x4 = xt.reshape(rows, GRID_W, BATCH, d)
    ut, g2t, vt_h, g1t_h = _inproj_w(x4, a8, s8, wt_s, wt_c, half)
    vt_v, g1t_v = _inproj_r(x4, a8, s8, wt_c, half)
    uct = _ctxproj(c4, a8c, s8c, wt_s)
    y1t = _s5(ut, uct, wall, cot, lam16, dcol)

    dw, db = conv_dw[0], conv_db[0]
    ct_h, s1_h, s2_h = _convt(vt_h, _conv_toeplitz(dw[:, :half], GRID_W), db[:half])
    ct_v, s1_v, s2_v = _convt(vt_v, _conv_toeplitz(dw[:, half:], rows), db[half:])
    to_v = lambda a: a.reshape(GRID_W, rows, BATCH).transpose(1, 0, 2).reshape(1, n)
    to_h = lambda a: a.reshape(rows, GRID_W, BATCH).transpose(1, 0, 2).reshape(1, n)
    pad4 = jnp.zeros((4, n), F32)
    stats_v = jnp.concatenate([s1_v.reshape(1, n), s2_v.reshape(1, n), to_v(s1_h), to_v(s2_h), pad4])
    stats_h = jnp.concatenate([s1_h.reshape(1, n), s2_h.reshape(1, n), to_h(s1_v), to_h(s2_v), pad4])

    ln_g, ln_b = conv_ln_g[0].reshape(cw, 1), conv_ln_b[0].reshape(cw, 1)
    wot = _transpose_cast(w_out.reshape(w_out.shape[1:]), 0, d, LANES)
    mix_h = _tail_w(y1t, g2t, ct_h, g1t_h, stats_h, _transpose_cast(ssm_glu_w[0], 0, sw, half),
                    ssm_glu_b[0].reshape(sw, 1), wot, ln_g[:half], ln_b[:half], cw)
    return _final_r(ct_v, g1t_v, stats_v, mix_h, x, wot, ln_g[half:],
                    ln_b[half:], gate8, final_g.reshape(1, d), cw)
```

```python
import functools
import math

import jax
import jax.numpy as jnp
import numpy as np
from jax import lax
from jax.experimental import pallas as pl
from jax.experimental.pallas import tpu as pltpu

GRID_W = 64
CONV_TAPS = 31
CONV_PAD = CONV_TAPS // 2
SSM_GROUP = 16
SSM_STATE = 64
CHUNK = 16
BATCH = 8
EPS = 1e-6
LANES = 128
MXU_WIDTH = 256
VMEM_LIMIT = 56 * 1024 * 1024

F32 = jnp.float32
BF16 = jnp.bfloat16


def _cparams(sem):
    return pltpu.CompilerParams(dimension_semantics=sem, vmem_limit_bytes=VMEM_LIMIT)


def _ada_kernel(c_ref, w_ref, b_ref, o_ref):
    s = jax.nn.silu(c_ref[...])
    o_ref[...] = jnp.dot(s, w_ref[...], preferred_element_type=F32,
                         precision=lax.Precision.HIGHEST) + b_ref[...]


def _ada(cond16, w_ada, b_ada):
    d, n3 = w_ada.shape
    return pl.pallas_call(
        _ada_kernel,
        grid=(n3 // d,),
        in_specs=[pl.BlockSpec((16, d), lambda j: (0, 0)),
                  pl.BlockSpec((d, d), lambda j: (0, j)),
                  pl.BlockSpec((1, d), lambda j: (0, j))],
        out_specs=pl.BlockSpec((16, d), lambda j: (0, j)),
        out_shape=jax.ShapeDtypeStruct((16, n3), F32),
        compiler_params=_cparams(("arbitrary",)),
        name="ada",
    )(cond16, w_ada, b_ada.reshape(1, n3))


def _modulated_norm(x, a, s):
    ms = jnp.mean(x * x, axis=-1, keepdims=True)
    return x * lax.rsqrt(ms + EPS) * a[None] + s[None]


def _lane_pieces(n_lanes):
    width = MXU_WIDTH if n_lanes % MXU_WIDTH == 0 else n_lanes
    return [(lo, width) for lo in range(0, n_lanes, width)]


def _conv_gates(q, hc, vt_ref, g1t_ref, lane0, gate_lane0=0):
    width = q.shape[1]
    val = (q[:hc] * jax.nn.sigmoid(q[hc:2 * hc])).astype(BF16)
    for j in range(width // LANES):
        vt_ref[lane0 // LANES + j] = val[:, j * LANES:(j + 1) * LANES]
    lo = gate_lane0 + lane0
    g1t_ref[:, lo:lo + width] = jax.nn.silu(q[2 * hc:]).astype(BF16)


def _transpose_cast_kernel(w_ref, o_ref):
    o_ref[...] = w_ref[...].T.astype(o_ref.dtype)


def _transpose_cast(w, col0, ncols, tile):
    k = w.shape[0]
    assert col0 % tile == 0 and ncols % tile == 0
    return pl.pallas_call(
        _transpose_cast_kernel,
        grid=(ncols // tile,),
        in_specs=[pl.BlockSpec((k, tile), lambda j: (0, j + col0 // tile))],
        out_specs=pl.BlockSpec((tile, k), lambda j: (j, 0)),
        out_shape=jax.ShapeDtypeStruct((ncols, k), BF16),
        compiler_params=_cparams(("arbitrary",)),
        name="wtrans",
    )(w)


def _project_t(w_refs, h):
    nt = (((1,), (1,)), ((), ()))
    return jnp.concatenate([lax.dot_general(w[...], h, nt, preferred_element_type=F32)
                            for w in w_refs], axis=0)


def _inproj_w_kernel(x_ref, a_ref, s_ref, ws_ref, wv_ref, wg_ref, wa_ref, ut_ref, g2t_ref, vt_ref,
                     g1t_ref):
    r, ncol, _, d = x_ref.shape
    sw = ut_ref.shape[1]
    hc = g1t_ref.shape[0]
    tl = r * BATCH
    for k in range(ncol):
        for lo, width in _lane_pieces(tl):
            x = x_ref[lo // BATCH:(lo + width) // BATCH, k]
            h = _modulated_norm(x, a_ref[...], s_ref[...]).reshape(width, d).astype(BF16)
            q = _project_t([ws_ref], h)
            ut_ref[k, :, lo:lo + width] = q[:sw].astype(BF16)
            g2t_ref[k, :, lo:lo + width] = jax.nn.silu(q[sw:]).astype(BF16)
            _conv_gates(_project_t([wv_ref, wg_ref, wa_ref], h), hc, vt_ref.at[k], g1t_ref,
                        lo, k * tl)


def _conv_weight_specs(hc, d, part, index_map):
    return [pl.BlockSpec((hc, d), functools.partial(index_map, 2 * k + part)) for k in range(3)]


INPROJ_COLS = 2


def _inproj_w(x4, a8, s8, wt_s, wt_c, hc):
    r, _, _, d = x4.shape
    sw = wt_s.shape[0] // 2
    tl = r * BATCH
    nq = tl // LANES
    cpr = GRID_W // CHUNK
    nc = INPROJ_COLS
    per_chunk = CHUNK // nc
    s5_spec = pl.BlockSpec((nc, sw, tl), lambda j: (j % per_chunk, 0, j // per_chunk))
    return pl.pallas_call(
        _inproj_w_kernel,
        grid=(GRID_W // nc,),
        in_specs=[pl.BlockSpec((r, nc, BATCH, d), lambda j: (0, j, 0, 0)),
                  pl.BlockSpec((BATCH, d), lambda j: (0, 0)),
                  pl.BlockSpec((BATCH, d), lambda j: (0, 0)),
                  pl.BlockSpec(wt_s.shape, lambda j: (0, 0)),
                  *_conv_weight_specs(hc, d, 0, lambda blk, j: (blk, 0))],
        out_specs=[s5_spec, s5_spec,
                   pl.BlockSpec((nc, nq, hc, LANES), lambda j: (j, 0, 0, 0)),
                   pl.BlockSpec((hc, nc * tl), lambda j: (0, j))],
        out_shape=[jax.ShapeDtypeStruct((CHUNK, sw, cpr * tl), BF16),
                   jax.ShapeDtypeStruct((CHUNK, sw, cpr * tl), BF16),
                   jax.ShapeDtypeStruct((GRID_W, nq, hc, LANES), BF16),
                   jax.ShapeDtypeStruct((hc, GRID_W * tl), BF16)],
        compiler_params=_cparams(("arbitrary",)),
        name="inproj_w",
    )(x4, a8, s8, wt_s, wt_c, wt_c, wt_c)


def _inproj_r_kernel(x_ref, a_ref, s_ref, wv_ref, wg_ref, wa_ref, vt_ref, g1t_ref):
    nrow, gw, _, d = x_ref.shape
    hc = g1t_ref.shape[0]
    tl = gw * BATCH
    for k in range(nrow):
        for lo, width in _lane_pieces(tl):
            x = x_ref[k, lo // BATCH:(lo + width) // BATCH]
            h = _modulated_norm(x, a_ref[...], s_ref[...]).reshape(width, d).astype(BF16)
            _conv_gates(_project_t([wv_ref, wg_ref, wa_ref], h), hc, vt_ref.at[k], g1t_ref,
                        lo, k * tl)


def _inproj_r(x4, a8, s8, wt_c, hc):
    r, gw, _, d = x4.shape
    tl = gw * BATCH
    nq = tl // LANES
    nr = math.gcd(r, INPROJ_COLS)
    return pl.pallas_call(
        _inproj_r_kernel,
        grid=(r // nr,),
        in_specs=[pl.BlockSpec((nr, gw, BATCH, d), lambda i: (i, 0, 0, 0)),
                  pl.BlockSpec((BATCH, d), lambda i: (0, 0)),
                  pl.BlockSpec((BATCH, d), lambda i: (0, 0)),
                  *_conv_weight_specs(hc, d, 1, lambda blk, i: (blk, 0))],
        out_specs=[pl.BlockSpec((nr, nq, hc, LANES), lambda i: (i, 0, 0, 0)),
                   pl.BlockSpec((hc, nr * tl), lambda i: (0, i))],
        out_shape=[jax.ShapeDtypeStruct((r, nq, hc, LANES), BF16),
                   jax.ShapeDtypeStruct((hc, r * tl), BF16)],
        compiler_params=_cparams(("arbitrary",)),
        name="inproj_r",
    )(x4, a8, s8, wt_c, wt_c, wt_c)


def _ctxproj_kernel(x_ref, a_ref, s_ref, wut_ref, ut_ref):
    cb, _, d = x_ref.shape
    x = x_ref[...].reshape(cb * 2, BATCH, d)
    h = _modulated_norm(x, a_ref[...], s_ref[...]).reshape(cb, 2 * BATCH, d)
    for tl in range(2):
        ht = h[:, tl * BATCH:(tl + 1) * BATCH, :].reshape(cb * BATCH, d).astype(BF16)
        q = lax.dot_general(wut_ref[...], ht, (((1,), (1,)), ((), ())),
                            preferred_element_type=F32)
        ut_ref[tl] = q.astype(BF16)


def _ctxproj(x4, a8, s8, wt_s):
    nch, _, _, d = x4.shape
    sw = wt_s.shape[0] // 2
    return pl.pallas_call(
        _ctxproj_kernel,
        grid=(CHUNK // 2,),
        in_specs=[pl.BlockSpec((nch, None, 2 * BATCH, d), lambda t: (0, t, 0, 0)),
                  pl.BlockSpec((BATCH, d), lambda t: (0, 0)),
                  pl.BlockSpec((BATCH, d), lambda t: (0, 0)),
                  pl.BlockSpec((sw, d), lambda t: (0, 0))],
        out_specs=pl.BlockSpec((2, sw, nch * BATCH), lambda t: (t, 0, 0)),
        out_shape=jax.ShapeDtypeStruct((CHUNK, sw, nch * BATCH), BF16),
        compiler_params=_cparams(("arbitrary",)),
        name="ctxproj",
    )(x4, a8, s8, wt_s)


GELU_C1 = math.sqrt(2.0 / math.pi)
GELU_C2 = GELU_C1 * 0.044715


def _gelu_tanh(x):
    hx = 0.5 * x
    return hx + hx * jnp.tanh(x * (GELU_C1 + GELU_C2 * (x * x)))


def _s5_kernel(xt_ref, xct_ref, wall_ref, cot_ref, lam_ref, dcol_ref, y_ref,
               s_sc, h_sc, *, lane_chunk):
    nl = xt_ref.shape[-1]
    nlc = xct_ref.shape[-1]
    n_ch = nl // BATCH
    n_chc = nlc // BATCH
    kdim = CHUNK * SSM_GROUP
    p = SSM_STATE

    def group_x(ref, j):
        return ref[:, j * SSM_GROUP:(j + 1) * SSM_GROUP, :].reshape(kdim, ref.shape[-1])

    st_c = [jnp.dot(wall_ref[j], group_x(xct_ref, j), preferred_element_type=F32)
            for j in range(2)]
    st = [jnp.dot(wall_ref[j], group_x(xt_ref, j), preferred_element_type=F32) for j in range(2)]
    for d in range(2):
        for src, off, width in ((st_c, 0, nlc), (st, nlc, nl)):
            for ri in range(2):
                lo = (2 * d + ri) * p
                pair = jnp.concatenate([src[0][lo:lo + p], src[1][lo:lo + p]], axis=0)
                s_sc[d, ri, off:off + width, :] = pair.T

    lam = [[jnp.broadcast_to(lam_ref[d, ri:ri + 1, :], (BATCH, LANES)) for ri in range(2)]
           for d in range(2)]

    def advance(d, state, row):
        hr, hi = state
        ar, ai = lam[d]
        sr = s_sc[d, 0, pl.ds(row, BATCH), :]
        si = s_sc[d, 1, pl.ds(row, BATCH), :]
        return ar * hr - ai * hi + sr, ar * hi + ai * hr + si

    zero = jnp.zeros((BATCH, LANES), F32)

    def ctx_step(i, carry):
        f, b = carry
        f = advance(0, f, pl.multiple_of(i * BATCH, BATCH))
        b = advance(1, b, pl.multiple_of((n_chc - 1 - i) * BATCH, BATCH))
        return f, b

    carry = lax.fori_loop(0, n_chc, ctx_step, ((zero, zero), (zero, zero)))

    cpr = GRID_W // CHUNK
    n_rows = n_ch // cpr

    def lane_row(ch):
        within = jnp.bitwise_and(ch, cpr - 1)
        row = lax.shift_right_logical(ch, cpr.bit_length() - 1)
        return pl.multiple_of((within * n_rows + row) * BATCH, BATCH)

    def lat_step(i, carry):
        f, b = carry
        rf = lane_row(i)
        rb = lane_row(n_ch - 1 - i)
        h_sc[0, 0, pl.ds(rf, BATCH), :] = f[0]
        h_sc[0, 1, pl.ds(rf, BATCH), :] = f[1]
        h_sc[1, 0, pl.ds(rb, BATCH), :] = b[0]
        h_sc[1, 1, pl.ds(rb, BATCH), :] = b[1]
        f = advance(0, f, pl.multiple_of(nlc + rf, BATCH))
        b = advance(1, b, pl.multiple_of(nlc + rb, BATCH))
        return f, b

    lax.fori_loop(0, n_ch, lat_step, carry, unroll=8)

    for c0 in range(0, nl, lane_chunk):
        cs = slice(c0, c0 + lane_chunk)
        ht = [[h_sc[d, ri, cs, :].T for ri in range(2)] for d in range(2)]
        for j in range(2):
            x = xt_ref[:, j * SSM_GROUP:(j + 1) * SSM_GROUP, cs].reshape(kdim, lane_chunk)
            hj = jnp.concatenate([ht[d][ri][j * p:(j + 1) * p] for d in range(2) for ri in range(2)],
                                 axis=0).astype(BF16)
            acc = (dcol_ref[j] * x.astype(F32)
                   + jnp.dot(cot_ref[j], jnp.concatenate([x, hj], axis=0),
                             preferred_element_type=F32))
            y = _gelu_tanh(acc).astype(BF16)
            y_ref[:, j * SSM_GROUP:(j + 1) * SSM_GROUP, cs] = y.reshape(CHUNK, SSM_GROUP, lane_chunk)


def _s5(ut, uct, wall, cot, lam16, dcol):
    _, sw, nl = ut.shape
    nlc = uct.shape[-1]
    gp = sw // (2 * SSM_GROUP)
    kdim = CHUNK * SSM_GROUP
    lane_chunk = min(nl, 512)
    kern = functools.partial(_s5_kernel, lane_chunk=lane_chunk)
    return pl.pallas_call(
        kern,
        grid=(gp,),
        in_specs=[pl.BlockSpec((CHUNK, 2 * SSM_GROUP, nl), lambda g: (0, g, 0)),
                  pl.BlockSpec((CHUNK, 2 * SSM_GROUP, nlc), lambda g: (0, g, 0)),
                  pl.BlockSpec((2, kdim, kdim), lambda g: (g, 0, 0)),
                  pl.BlockSpec((2, kdim, 2 * kdim), lambda g: (g, 0, 0)),
                  pl.BlockSpec((2, None, 2, LANES), lambda g: (0, g, 0, 0)),
                  pl.BlockSpec((2, kdim, 1), lambda g: (g, 0, 0))],
        out_specs=pl.BlockSpec((CHUNK, 2 * SSM_GROUP, nl), lambda g: (0, g, 0)),
        out_shape=jax.ShapeDtypeStruct((CHUNK, sw, nl), BF16),
        scratch_shapes=[pltpu.VMEM((2, 2, nlc + nl, LANES), F32),
                        pltpu.VMEM((2, 2, nl, LANES), F32)],
        compiler_params=_cparams(("arbitrary",)),
        name="s5",
    )(ut, uct, wall, cot, lam16, dcol)


CONV_CB = 128
CONV_SKEW = 8


def _convt_kernel(v_ref, t_ref, b_ref, o_ref, s1_ref, s2_ref, z_sc, o_sc):
    a, cb, _ = v_ref.shape
    kl = t_ref.shape[-1]
    pitch = cb + CONV_SKEW
    z = v_ref[...].astype(F32)
    for i in range(a):
        z_sc[i * pitch:i * pitch + cb, :] = z[i]
    s1 = jnp.zeros((a, LANES), F32)
    s2 = jnp.zeros((a, LANES), F32)
    k_pad = jnp.zeros((kl - a, LANES), BF16)
    for c in range(cb):
        col = z_sc[pl.ds(c, a, stride=pitch), :].astype(BF16)
        rhs = jnp.concatenate([col, k_pad], axis=0) if kl > a else col
        out = jnp.dot(t_ref[c], rhs, preferred_element_type=F32) + b_ref[c:c + 1]
        o_sc[pl.ds(c, a, stride=pitch), :] = out
        s1 = s1 + out
        s2 = s2 + out * out
    for i in range(a):
        o_ref[i] = o_sc[i * pitch:i * pitch + cb, :].astype(o_ref.dtype)

    @pl.when(pl.program_id(1) == 0)
    def _():
        s1_ref[...] = jnp.zeros_like(s1_ref)
        s2_ref[...] = jnp.zeros_like(s2_ref)

    s1_ref[...] += s1
    s2_ref[...] += s2


def _convt(vt, t, bias):
    a, nq, c, _ = vt.shape
    cb = min(c, CONV_CB)
    kl = t.shape[-1]
    stat_spec = pl.BlockSpec((a, LANES), lambda q, i: (0, q))
    return pl.pallas_call(
        _convt_kernel,
        grid=(nq, c // cb),
        in_specs=[pl.BlockSpec((a, None, cb, LANES), lambda q, i: (0, q, i, 0)),
                  pl.BlockSpec((cb, a, kl), lambda q, i: (i, 0, 0)),
                  pl.BlockSpec((cb, 1), lambda q, i: (i, 0))],
        out_specs=[pl.BlockSpec((a, cb, LANES), lambda q, i: (0, i, q)), stat_spec, stat_spec],
        out_shape=[jax.ShapeDtypeStruct((a, c, nq * LANES), BF16),
                   jax.ShapeDtypeStruct((a, nq * LANES), F32),
                   jax.ShapeDtypeStruct((a, nq * LANES), F32)],
        scratch_shapes=[pltpu.VMEM((a * (cb + CONV_SKEW), LANES), F32),
                        pltpu.VMEM((a * (cb + CONV_SKEW), LANES), F32)],
        compiler_params=_cparams(("arbitrary", "arbitrary")),
        name="convt",
    )(vt, t, bias.reshape(c, 1))


@functools.lru_cache(maxsize=None)
def _tap_selector(a):
    kl = -(-a // LANES) * LANES
    tap = np.arange(kl)[None, :] - np.arange(a)[:, None] + CONV_PAD
    hit = (tap[None] == np.arange(CONV_TAPS)[:, None, None]) & (np.arange(kl) < a)
    return hit.astype(BF16)


def _conv_toeplitz(w, a):
    return jnp.einsum("kc,kxy->cxy", w.astype(BF16), jnp.asarray(_tap_selector(a)),
                      preferred_element_type=BF16)


def _conv_branch_t(cv, g1, st, lng_ref, lnb_ref, n_chan):
    mean = (st[0:1] + st[2:3]) * (1.0 / n_chan)
    var = (st[1:2] + st[3:4]) * (1.0 / n_chan) - mean * mean
    y = (cv.astype(F32) - mean) * lax.rsqrt(var + EPS) * lng_ref[...] + lnb_ref[...]
    return (jax.nn.silu(y) * g1.astype(F32)).astype(BF16)


def _tail_w_kernel(y_ref, g2_ref, cv_ref, g1_ref, st_ref, wt_ref, bt_ref, wos_ref, woh_ref,
                   lng_ref, lnb_ref, o_ref, mt_sc, *, n_chan):
    tl = y_ref.shape[-1]
    st = st_ref[...]
    mix = []
    for k in range(2):
        lanes = slice(k * tl, (k + 1) * tl)
        y = y_ref[k]
        z = jnp.dot(wt_ref[...], y, preferred_element_type=F32) + bt_ref[...]
        s = (y.astype(F32) * jax.nn.sigmoid(z) * g2_ref[k].astype(F32)).astype(BF16)
        co = _conv_branch_t(cv_ref[k], g1_ref[:, lanes], st[:, lanes], lng_ref, lnb_ref, n_chan)
        mt_sc[...] = (jnp.dot(wos_ref[...], s, preferred_element_type=F32)
                      + jnp.dot(woh_ref[...], co, preferred_element_type=F32))
        mix.append(mt_sc[...].T.reshape(tl // BATCH, BATCH, mt_sc.shape[0]))
    o_ref[...] = jnp.concatenate(mix, axis=1).astype(o_ref.dtype)


def _tail_w(y1t, g2t, ct_h, g1t_h, stats_h, glu_wt, glu_bcol, wot, ln_g, ln_b, n_chan):
    _, sw, _ = y1t.shape
    _, hc, tl = ct_h.shape
    d = wot.shape[0]
    assert (2 * hc) % sw == 0
    r = tl // BATCH
    half_chunk = CHUNK // 2
    s5_spec = pl.BlockSpec((2, sw, tl), lambda j: (j % half_chunk, 0, j // half_chunk))
    tok = lambda rows: pl.BlockSpec((rows, 2 * tl), lambda j: (0, j))
    full = lambda arr: pl.BlockSpec(arr.shape, lambda j: (0,) * arr.ndim)
    return pl.pallas_call(
        functools.partial(_tail_w_kernel, n_chan=n_chan),
        grid=(GRID_W // 2,),
        in_specs=[s5_spec, s5_spec, pl.BlockSpec((2, hc, tl), lambda j: (j, 0, 0)), tok(hc),
                  tok(8), full(glu_wt), full(glu_bcol),
                  pl.BlockSpec((d, sw), lambda j: (0, 2 * hc // sw)),
                  pl.BlockSpec((d, hc), lambda j: (0, 0)), full(ln_g), full(ln_b)],
        out_specs=pl.BlockSpec((r, None, 2 * BATCH, d), lambda j: (0, j, 0, 0)),
        out_shape=jax.ShapeDtypeStruct((r, GRID_W // 2, 2 * BATCH, d), BF16),
        scratch_shapes=[pltpu.VMEM((d, tl), F32)],
        compiler_params=_cparams(("arbitrary",)),
        name="tail_w",
    )(y1t, g2t, ct_h, g1t_h, stats_h, glu_wt, glu_bcol, wot, wot, ln_g, ln_b)


FINAL_ROWS = 2


def _final_r_kernel(cv_ref, g1_ref, st_ref, mh_ref, x_ref, wo_ref, lng_ref, lnb_ref, gate_ref,
                    fg_ref, o_ref, mix_sc, *, n_chan):
    tl, d = x_ref.shape[1:]
    cv = jnp.concatenate([cv_ref[i] for i in range(cv_ref.shape[0])], axis=-1)
    co = _conv_branch_t(cv, g1_ref[...], st_ref[...], lng_ref, lnb_ref, n_chan)
    mix = lax.dot_general(co.T, wo_ref[...], (((1,), (1,)), ((), ())),
                          preferred_element_type=F32) + mh_ref[...].reshape(tl * BATCH, d)
    nlt = mix_sc.shape[0]
    for j in range(nlt):
        mix_sc[j] = mix[:, j * LANES:(j + 1) * LANES]
    for b in range(BATCH):
        mb = jnp.concatenate([mix_sc[j, pl.ds(b, tl, stride=BATCH), :] for j in range(nlt)],
                             axis=-1)
        xo = x_ref[b] + gate_ref[b:b + 1, :] * mb
        ms = jnp.mean(xo * xo, axis=-1, keepdims=True)
        o_ref[b] = xo * lax.rsqrt(ms + EPS) * fg_ref[...]


def _final_r(ct_v, g1t_v, stats_v, mix_h, x, wot, ln_g, ln_b, gate8, final_g, n_chan):
    r, hc, row_tokens = ct_v.shape
    _, length, d = x.shape
    rs = math.gcd(r, FINAL_ROWS)
    tm = rs * row_tokens
    tl = tm // BATCH
    tok = lambda rows: pl.BlockSpec((rows, tm), lambda i: (0, i))
    nat = pl.BlockSpec((BATCH, tl, d), lambda i: (0, i, 0))
    full = lambda arr: pl.BlockSpec(arr.shape, lambda i: (0,) * arr.ndim)
    return pl.pallas_call(
        functools.partial(_final_r_kernel, n_chan=n_chan),
        grid=(r // rs,),
        in_specs=[pl.BlockSpec((rs, hc, row_tokens), lambda i: (i, 0, 0)), tok(hc), tok(8),
                  pl.BlockSpec((rs, GRID_W // 2, 2 * BATCH, d), lambda i: (i, 0, 0, 0)), nat,
                  pl.BlockSpec((d, hc), lambda i: (0, 1)), full(ln_g), full(ln_b), full(gate8),
                  full(final_g)],
        out_specs=nat,
        out_shape=jax.ShapeDtypeStruct(x.shape, F32),
        scratch_shapes=[pltpu.VMEM((d // LANES, tm, LANES), F32)],
        compiler_params=_cparams(("arbitrary",)),
        name="final_r",
    )(ct_v, g1t_v, stats_v, mix_h, x, wot, ln_g, ln_b, gate8, final_g)


S5PREP_GROUPS = 4


def _s5prep_kernel(arow_ref, bt_ref, ct_ref, wall_ref, cot_ref, lam_ref, pw_sc, kk_sc):
    for gi in range(arow_ref.shape[0]):
        _s5prep_group(arow_ref.at[gi], bt_ref.at[gi], ct_ref.at[gi], wall_ref.at[gi], cot_ref.at[gi],
                      lam_ref.at[gi], pw_sc, kk_sc)


def _s5prep_group(arow_ref, bt_ref, ct_ref, wall_ref, cot_ref, lam_ref, pw_sc, kk_sc):
    kdim = CHUNK * SSM_GROUP
    hp = lax.Precision.HIGHEST
    lane = lax.broadcasted_iota(jnp.int32, (1, LANES), 1)
    is_re = lane < SSM_STATE
    col_t = lax.broadcasted_iota(jnp.int32, (1, kdim), 1) // SSM_GROUP
    kcol = lax.broadcasted_iota(jnp.int32, (pw_sc.shape[1], 1), 0).astype(F32)

    def tile_rows(a):
        return jnp.concatenate([a] * CHUNK, axis=0)

    def power_rows(exps):
        pr = jnp.concatenate([jnp.broadcast_to(pw_sc[0, e:e + 1, :], (SSM_GROUP, LANES))
                              for e in exps], axis=0)
        pi = jnp.concatenate([jnp.broadcast_to(pw_sc[1, e:e + 1, :], (SSM_GROUP, LANES))
                              for e in exps], axis=0)
        return pr, pi

    def cmul(xr, xi, yr, yi, im_sign):
        return jnp.where(is_re, xr * yr - xi * yi, im_sign * (xr * yi + xi * yr))

    toeplitz = jnp.zeros((kdim, kdim), F32)
    for d in range(2):
        ar, ai = arow_ref[d, 0:1, :], arow_ref[d, 1:2, :]
        dt = jnp.exp(arow_ref[d, 2:3, :])
        mag = jnp.exp(kcol * (ar * dt))
        ang = kcol * (ai * dt)
        pw_sc[0] = mag * jnp.cos(ang)
        pw_sc[1] = mag * jnp.sin(ang)
        lam_ref[d, 0] = pw_sc[0, CHUNK:CHUNK + 8, :]
        lam_ref[d, 1] = pw_sc[1, CHUNK:CHUNK + 8, :]
        nr, ni = pw_sc[0, 1:2, :] - 1.0, pw_sc[1, 1:2, :]
        den = ar * ar + ai * ai
        qr, qi = (nr * ar + ni * ai) / den, (ni * ar - nr * ai) / den
        btr, bti = bt_ref[d, 0], bt_ref[d, 1]
        bbr, bbi = tile_rows(qr * btr - qi * bti), tile_rows(qr * bti + qi * btr)
        cr, ci = tile_rows(ct_ref[d, 0]), tile_rows(ct_ref[d, 1])
        tt = list(range(CHUNK))
        e_lag = [CHUNK - 1 - t for t in tt] if d else tt
        e_in = tt if d else [CHUNK - 1 - t for t in tt]
        e_out = [CHUNK - t for t in tt] if d else [t + 1 for t in tt]
        cot_ref[:, kdim + d * LANES:kdim + (d + 1) * LANES] = cmul(
            cr, ci, *power_rows(e_out), -1.0).astype(BF16)
        wall_ref[d * LANES:(d + 1) * LANES, :] = cmul(bbr, bbi, *power_rows(e_in), 1.0).T.astype(BF16)
        c_lag = cmul(cr, ci, *power_rows(e_lag), -1.0)
        bb = jnp.where(is_re, bbr, bbi)
        kk = lax.dot_general(c_lag, bb, (((1,), (1,)), ((), ())), precision=hp,
                             preferred_element_type=F32)
        zeros = jnp.zeros((kdim, kdim), F32)
        if d:
            kk_sc[:kdim] = kk
            kk_sc[kdim:] = zeros
            starts = [SSM_GROUP * (CHUNK - 1 - t) for t in tt]
        else:
            kk_sc[:kdim] = zeros
            kk_sc[kdim:] = kk
            starts = [kdim - SSM_GROUP * t for t in tt]
        tiles = []
        for lt in range(kdim // LANES):
            cols = slice(lt * LANES, (lt + 1) * LANES)
            acc = jnp.zeros((kdim, LANES), F32)
            for t in range(lt * LANES // SSM_GROUP, (lt + 1) * LANES // SSM_GROUP):
                acc = jnp.where(col_t[:, cols] == t, kk_sc[starts[t]:starts[t] + kdim, cols], acc)
            tiles.append(acc)
        toeplitz = toeplitz + jnp.concatenate(tiles, axis=1)
    cot_ref[:, :kdim] = toeplitz.astype(BF16)


def _s5prep(arow, bt, ct):
    g = arow.shape[0]
    gs = math.gcd(g, S5PREP_GROUPS)
    kdim = CHUNK * SSM_GROUP
    return pl.pallas_call(
        _s5prep_kernel,
        grid=(g // gs,),
        in_specs=[pl.BlockSpec((gs, 2, 8, LANES), lambda i: (i, 0, 0, 0)),
                  pl.BlockSpec((gs, 2, 2, SSM_GROUP, LANES), lambda i: (i, 0, 0, 0, 0)),
                  pl.BlockSpec((gs, 2, 2, SSM_GROUP, LANES), lambda i: (i, 0, 0, 0, 0))],
        out_specs=[pl.BlockSpec((gs, kdim, kdim), lambda i: (i, 0, 0)),
                   pl.BlockSpec((gs, kdim, 2 * kdim), lambda i: (i, 0, 0)),
                   pl.BlockSpec((gs, 2, 2, 8, LANES), lambda i: (i, 0, 0, 0, 0))],
        out_shape=[jax.ShapeDtypeStruct((g, kdim, kdim), BF16),
                   jax.ShapeDtypeStruct((g, kdim, 2 * kdim), BF16),
                   jax.ShapeDtypeStruct((g, 2, 2, 8, LANES), F32)],
        scratch_shapes=[pltpu.VMEM((2, CHUNK + 8, LANES), F32),
                        pltpu.VMEM((2 * kdim, kdim), F32)],
        compiler_params=_cparams(("arbitrary",)),
        name="s5prep",
    )(arow, bt, ct)


def _s5prep_inputs(a_re, a_im, log_dt, b_re, b_im, c_re, c_im):
    g = a_re.shape[1]
    dup = lambda a: jnp.concatenate([a, a], axis=-1)
    rows = jnp.stack([dup(a_re), dup(a_im),
                      jnp.broadcast_to(log_dt[..., None], (2, g, LANES))], axis=2)
    arow = jnp.pad(rows, ((0, 0), (0, 0), (0, 5), (0, 0))).transpose(1, 0, 2, 3)
    bt = jnp.stack([dup(b_re.transpose(0, 1, 3, 2)), dup(b_im.transpose(0, 1, 3, 2))], axis=2)
    ct = jnp.stack([dup(c_re), dup(c_im)], axis=2)
    return arow, bt.transpose(1, 0, 2, 3, 4), ct.transpose(1, 0, 2, 3, 4)


def kernel(x, c, ctx, c_ctx, norm_g, w_ada, b_ada, w_in, conv_dw, conv_db, conv_ln_g, conv_ln_b,
           ssm_a_re, ssm_a_im, ssm_log_dt, ssm_b_re, ssm_b_im, ssm_c_re, ssm_c_im, ssm_d,
           ssm_glu_w, ssm_glu_b, w_out, final_g):
    bsz, length, d = x.shape
    ctx_len = ctx.shape[1]
    cw = conv_dw.shape[-1]
    sw = ssm_d.shape[-1]
    assert bsz == BATCH and norm_g.shape[0] == 1
    assert length % (GRID_W * CHUNK) == 0 and ctx_len % CHUNK == 0
    n = length * BATCH
    nch = length // CHUNK
    nchc = ctx_len // CHUNK
    rows = length // GRID_W

    cond16 = jnp.zeros((16, d), F32).at[:BATCH].set(c).at[BATCH].set(c_ctx)
    mod = _ada(cond16, w_ada.reshape(w_ada.shape[1:]), b_ada[0])
    shift, scale, gate = mod[:, :d], mod[:, d:2 * d], mod[:, 2 * d:]
    amp = norm_g[0][None] * (1.0 + scale)
    a8, s8, gate8 = amp[:BATCH], shift[:BATCH], gate[:BATCH]
    a8c = jnp.broadcast_to(amp[BATCH], (BATCH, d))
    s8c = jnp.broadcast_to(shift[BATCH], (BATCH, d))

    w = w_in.reshape(w_in.shape[1:])
    half = cw // 2
    wt_c = _transpose_cast(w, 0, 3 * cw, half)
    wt_s = _transpose_cast(w, 3 * cw, 2 * sw, half)
    wall, cot, lam_raw = _s5prep(*_s5prep_inputs(
        ssm_a_re[0], ssm_a_im[0], ssm_log_dt[0], ssm_b_re[0], ssm_b_im[0], ssm_c_re[0], ssm_c_im[0]))
    ng = sw // SSM_GROUP
    lam16 = lam_raw[:, :, :, 0, :SSM_STATE].reshape(ng // 2, 2, 2, 2, SSM_STATE).transpose(
        2, 0, 3, 1, 4).reshape(2, ng // 2, 2, LANES)
    dcol = jnp.broadcast_to(ssm_d[0].reshape(ng, 1, SSM_GROUP), (ng, CHUNK, SSM_GROUP)).reshape(
        ng, CHUNK * SSM_GROUP, 1)

    xt = x.transpose(1, 0, 2)
    ctxt = ctx.transpose(1, 0, 2)
    c4 = ctxt.reshape(nchc, CHUNK // 2, 2 * BATCH, d)

    x4 = xt.reshape(rows, GRID_W, BATCH, d)
    ut, g2t, vt_h, g1t_h = _inproj_w(x4, a8, s8, wt_s, wt_c, half)
    vt_v, g1t_v = _inproj_r(x4, a8, s8, wt_c, half)
    uct = _ctxproj(c4, a8c, s8c, wt_s)
    y1t = _s5(ut, uct, wall, cot, lam16, dcol)

    dw, db = conv_dw[0], conv_db[0]
    ct_h, s1_h, s2_h = _convt(vt_h, _conv_toeplitz(dw[:, :half], GRID_W), db[:half])
    ct_v, s1_v, s2_v = _convt(vt_v, _conv_toeplitz(dw[:, half:], rows), db[half:])
    to_v = lambda a: a.reshape(GRID_W, rows, BATCH).transpose(1, 0, 2).reshape(1, n)
    to_h = lambda a: a.reshape(rows, GRID_W, BATCH).transpose(1, 0, 2).reshape(1, n)
    pad4 = jnp.zeros((4, n), F32)
    stats_v = jnp.concatenate([s1_v.reshape(1, n), s2_v.reshape(1, n), to_v(s1_h), to_v(s2_h), pad4])
    stats_h = jnp.concatenate([s1_h.reshape(1, n), s2_h.reshape(1, n), to_h(s1_v), to_h(s2_v), pad4])

    ln_g, ln_b = conv_ln_g[0].reshape(cw, 1), conv_ln_b[0].reshape(cw, 1)
    wot = _transpose_cast(w_out.reshape(w_out.shape[1:]), 0, d, LANES)
    mix_h = _tail_w(y1t, g2t, ct_h, g1t_h, stats_h, _transpose_cast(ssm_glu_w[0], 0, sw, half),
                    ssm_glu_b[0].reshape(sw, 1), wot, ln_g[:half], ln_b[:half], cw)
    return _final_r(ct_v, g1t_v, stats_v, mix_h, x, wot, ln_g[half:],
                    ln_b[half:], gate8, final_g.reshape(1, d), cw)
```

```python
import functools
import math

import jax
import jax.numpy as jnp
import numpy as np
from jax import lax
from jax.experimental import pallas as pl
from jax.experimental.pallas import tpu as pltpu

GRID_W = 64
CONV_TAPS = 31
CONV_PAD = CONV_TAPS // 2
SSM_GROUP = 16
SSM_STATE = 64
CHUNK = 16
BATCH = 8
EPS = 1e-6
LANES = 128
MXU_WIDTH = 256
VMEM_LIMIT = 56 * 1024 * 1024

F32 = jnp.float32
BF16 = jnp.bfloat16


def _cparams(sem):
    return pltpu.CompilerParams(dimension_semantics=sem, vmem_limit_bytes=VMEM_LIMIT)


def _ada_kernel(c_ref, w_ref, b_ref, o_ref):
    s = jax.nn.silu(c_ref[...])
    o_ref[...] = jnp.dot(s, w_ref[...], preferred_element_type=F32,
                         precision=lax.Precision.HIGHEST) + b_ref[...]


def _ada(cond16, w_ada, b_ada):
    d, n3 = w_ada.shape
    return pl.pallas_call(
        _ada_kernel,
        grid=(n3 // d,),
        in_specs=[pl.BlockSpec((16, d), lambda j: (0, 0)),
                  pl.BlockSpec((d, d), lambda j: (0, j)),
                  pl.BlockSpec((1, d), lambda j: (0, j))],
        out_specs=pl.BlockSpec((16, d), lambda j: (0, j)),
        out_shape=jax.ShapeDtypeStruct((16, n3), F32),
        compiler_params=_cparams(("arbitrary",)),
        name="ada",
    )(cond16, w_ada, b_ada.reshape(1, n3))


def _modulated_norm(x, a, s):
    ms = jnp.mean(x * x, axis=-1, keepdims=True)
    return x * lax.rsqrt(ms + EPS) * a[None] + s[None]


def _lane_pieces(n_lanes):
    width = MXU_WIDTH if n_lanes % MXU_WIDTH == 0 else n_lanes
    return [(lo, width) for lo in range(0, n_lanes, width)]


def _conv_gates(q, hc, vt_ref, g1t_ref, lane0, gate_lane0=0):
    width = q.shape[1]
    val = (q[:hc] * jax.nn.sigmoid(q[hc:2 * hc])).astype(BF16)
    for j in range(width // LANES):
        vt_ref[lane0 // LANES + j] = val[:, j * LANES:(j + 1) * LANES]
    lo = gate_lane0 + lane0
    g1t_ref[:, lo:lo + width] = jax.nn.silu(q[2 * hc:]).astype(BF16)


def _transpose_cast_kernel(w_ref, o_ref):
    o_ref[...] = w_ref[...].T.astype(o_ref.dtype)


def _transpose_cast(w, col0, ncols, tile):
    k = w.shape[0]
    assert col0 % tile == 0 and ncols % tile == 0
    return pl.pallas_call(
        _transpose_cast_kernel,
        grid=(ncols // tile,),
        in_specs=[pl.BlockSpec((k, tile), lambda j: (0, j + col0 // tile))],
        out_specs=pl.BlockSpec((tile, k), lambda j: (j, 0)),
        out_shape=jax.ShapeDtypeStruct((ncols, k), BF16),
        compiler_params=_cparams(("arbitrary",)),
        name="wtrans",
    )(w)


def _project_t(w_refs, h):
    nt = (((1,), (1,)), ((), ()))
    return jnp.concatenate([lax.dot_general(w[...], h, nt, preferred_element_type=F32)
                            for w in w_refs], axis=0)


def _inproj_w_kernel(x_ref, a_ref, s_ref, ws_ref, wv_ref, wg_ref, wa_ref, ut_ref, g2t_ref, vt_ref,
                     g1t_ref):
    r, ncol, _, d = x_ref.shape
    sw = ut_ref.shape[1]
    hc = g1t_ref.shape[0]
    tl = r * BATCH
    for k in range(ncol):
        for lo, width in _lane_pieces(tl):
            x = x_ref[lo // BATCH:(lo + width) // BATCH, k]
            h = _modulated_norm(x, a_ref[...], s_ref[...]).reshape(width, d).astype(BF16)
            q = _project_t([ws_ref], h)
            ut_ref[k, :, lo:lo + width] = q[:sw].astype(BF16)
            g2t_ref[k, :, lo:lo + width] = jax.nn.silu(q[sw:]).astype(BF16)
            _conv_gates(_project_t([wv_ref, wg_ref, wa_ref], h), hc, vt_ref.at[k], g1t_ref,
                        lo, k * tl)


def _conv_weight_specs(hc, d, part, index_map):
    return [pl.BlockSpec((hc, d), functools.partial(index_map, 2 * k + part)) for k in range(3)]


INPROJ_COLS = 2


def _inproj_w(x4, a8, s8, wt_s, wt_c, hc):
    r, _, _, d = x4.shape
    sw = wt_s.shape[0] // 2
    tl = r * BATCH
    nq = tl // LANES
    cpr = GRID_W // CHUNK
    nc = INPROJ_COLS
    per_chunk = CHUNK // nc
    s5_spec = pl.BlockSpec((nc, sw, tl), lambda j: (j % per_chunk, 0, j // per_chunk))
    return pl.pallas_call(
        _inproj_w_kernel,
        grid=(GRID_W // nc,),
        in_specs=[pl.BlockSpec((r, nc, BATCH, d), lambda j: (0, j, 0, 0)),
                  pl.BlockSpec((BATCH, d), lambda j: (0, 0)),
                  pl.BlockSpec((BATCH, d), lambda j: (0, 0)),
                  pl.BlockSpec(wt_s.shape, lambda j: (0, 0)),
                  *_conv_weight_specs(hc, d, 0, lambda blk, j: (blk, 0))],
        out_specs=[s5_spec, s5_spec,
                   pl.BlockSpec((nc, nq, hc, LANES), lambda j: (j, 0, 0, 0)),
                   pl.BlockSpec((hc, nc * tl), lambda j: (0, j))],
        out_shape=[jax.ShapeDtypeStruct((CHUNK, sw, cpr * tl), BF16),
                   jax.ShapeDtypeStruct((CHUNK, sw, cpr * tl), BF16),
                   jax.ShapeDtypeStruct((GRID_W, nq, hc, LANES), BF16),
                   jax.ShapeDtypeStruct((hc, GRID_W * tl), BF16)],
        compiler_params=_cparams(("arbitrary",)),
        name="inproj_w",
    )(x4, a8, s8, wt_s, wt_c, wt_c, wt_c)


def _inproj_r_kernel(x_ref, a_ref, s_ref, wv_ref, wg_ref, wa_ref, vt_ref, g1t_ref):
    nrow, gw, _, d = x_ref.shape
    hc = g1t_ref.shape[0]
    tl = gw * BATCH
    for k in range(nrow):
        for lo, width in _lane_pieces(tl):
            x = x_ref[k, lo // BATCH:(lo + width) // BATCH]
            h = _modulated_norm(x, a_ref[...], s_ref[...]).reshape(width, d).astype(BF16)
            _conv_gates(_project_t([wv_ref, wg_ref, wa_ref], h), hc, vt_ref.at[k], g1t_ref,
                        lo, k * tl)


def _inproj_r(x4, a8, s8, wt_c, hc):
    r, gw, _, d = x4.shape
    tl = gw * BATCH
    nq = tl // LANES
    nr = math.gcd(r, INPROJ_COLS)
    return pl.pallas_call(
        _inproj_r_kernel,
        grid=(r // nr,),
        in_specs=[pl.BlockSpec((nr, gw, BATCH, d), lambda i: (i, 0, 0, 0)),
                  pl.BlockSpec((BATCH, d), lambda i: (0, 0)),
                  pl.BlockSpec((BATCH, d), lambda i: (0, 0)),
                  *_conv_weight_specs(hc, d, 1, lambda blk, i: (blk, 0))],
        out_specs=[pl.BlockSpec((nr, nq, hc, LANES), lambda i: (i, 0, 0, 0)),
                   pl.BlockSpec((hc, nr * tl), lambda i: (0, i))],
        out_shape=[jax.ShapeDtypeStruct((r, nq, hc, LANES), BF16),
                   jax.ShapeDtypeStruct((hc, r * tl), BF16)],
        compiler_params=_cparams(("arbitrary",)),
        name="inproj_r",
    )(x4, a8, s8, wt_c, wt_c, wt_c)


def _ctxproj_kernel(x_ref, a_ref, s_ref, wut_ref, ut_ref):
    cb, _, d = x_ref.shape
    x = x_ref[...].reshape(cb * 2, BATCH, d)
    h = _modulated_norm(x, a_ref[...], s_ref[...]).reshape(cb, 2 * BATCH, d)
    for tl in range(2):
        ht = h[:, tl * BATCH:(tl + 1) * BATCH, :].reshape(cb * BATCH, d).astype(BF16)
        q = lax.dot_general(wut_ref[...], ht, (((1,), (1,)), ((), ())),
                            preferred_element_type=F32)
        ut_ref[tl] = q.astype(BF16)


def _ctxproj(x4, a8, s8, wt_s):
    nch, _, _, d = x4.shape
    sw = wt_s.shape[0] // 2
    return pl.pallas_call(
        _ctxproj_kernel,
        grid=(CHUNK // 2,),
        in_specs=[pl.BlockSpec((nch, None, 2 * BATCH, d), lambda t: (0, t, 0, 0)),
                  pl.BlockSpec((BATCH, d), lambda t: (0, 0)),
                  pl.BlockSpec((BATCH, d), lambda t: (0, 0)),
                  pl.BlockSpec((sw, d), lambda t: (0, 0))],
        out_specs=pl.BlockSpec((2, sw, nch * BATCH), lambda t: (t, 0, 0)),
        out_shape=jax.ShapeDtypeStruct((CHUNK, sw, nch * BATCH), BF16),
        compiler_params=_cparams(("arbitrary",)),
        name="ctxproj",
    )(x4, a8, s8, wt_s)


GELU_C1 = math.sqrt(2.0 / math.pi)
GELU_C2 = GELU_C1 * 0.044715


def _gelu_tanh(x):
    hx = 0.5 * x
    return hx + hx * jnp.tanh(x * (GELU_C1 + GELU_C2 * (x * x)))


def _s5_kernel(xt_ref, xct_ref, wall_ref, cot_ref, lam_ref, dcol_ref, y_ref,
               s_sc, h_sc, *, lane_chunk):
    nl = xt_ref.shape[-1]
    nlc = xct_ref.shape[-1]
    n_ch = nl // BATCH
    n_chc = nlc // BATCH
    kdim = CHUNK * SSM_GROUP
    p = SSM_STATE

    def group_x(ref, j):
        return ref[:, j * SSM_GROUP:(j + 1) * SSM_GROUP, :].reshape(kdim, ref.shape[-1])

    st_c = [jnp.dot(wall_ref[j], group_x(xct_ref, j), preferred_element_type=F32)
            for j in range(2)]
    st = [jnp.dot(wall_ref[j], group_x(xt_ref, j), preferred_element_type=F32) for j in range(2)]
    for d in range(2):
        for src, off, width in ((st_c, 0, nlc), (st, nlc, nl)):
            for ri in range(2):
                lo = (2 * d + ri) * p
                pair = jnp.concatenate([src[0][lo:lo + p], src[1][lo:lo + p]], axis=0)
                s_sc[d, ri, off:off + width, :] = pair.T

    lam = [[jnp.broadcast_to(lam_ref[d, ri:ri + 1, :], (BATCH, LANES)) for ri in range(2)]
           for d in range(2)]

    def advance(d, state, row):
        hr, hi = state
        ar, ai = lam[d]
        sr = s_sc[d, 0, pl.ds(row, BATCH), :]
        si = s_sc[d, 1, pl.ds(row, BATCH), :]
        return ar * hr - ai * hi + sr, ar * hi + ai * hr + si

    zero = jnp.zeros((BATCH, LANES), F32)

    f = b = (zero, zero)
    for i in range(n_chc):
        f = advance(0, f, i * BATCH)
        b = advance(1, b, (n_chc - 1 - i) * BATCH)

    cpr = GRID_W // CHUNK
    n_rows = n_ch // cpr

    def lane_row(ch):
        return ((ch % cpr) * n_rows + ch // cpr) * BATCH

    for i in range(n_ch):
        rf = lane_row(i)
        rb = lane_row(n_ch - 1 - i)
        h_sc[0, 0, rf:rf + BATCH, :] = f[0]
        h_sc[0, 1, rf:rf + BATCH, :] = f[1]
        h_sc[1, 0, rb:rb + BATCH, :] = b[0]
        h_sc[1, 1, rb:rb + BATCH, :] = b[1]
        f = advance(0, f, nlc + rf)
        b = advance(1, b, nlc + rb)

    for c0 in range(0, nl, lane_chunk):
        cs = slice(c0, c0 + lane_chunk)
        ht = [[h_sc[d, ri, cs, :].T for ri in range(2)] for d in range(2)]
        for j in range(2):
            x = xt_ref[:, j * SSM_GROUP:(j + 1) * SSM_GROUP, cs].reshape(kdim, lane_chunk)
            hj = jnp.concatenate([ht[d][ri][j * p:(j + 1) * p] for d in range(2) for ri in range(2)],
                                 axis=0).astype(BF16)
            acc = (dcol_ref[j] * x.astype(F32)
                   + jnp.dot(cot_ref[j], jnp.concatenate([x, hj], axis=0),
                             preferred_element_type=F32))
            y = _gelu_tanh(acc).astype(BF16)
            y_ref[:, j * SSM_GROUP:(j + 1) * SSM_GROUP, cs] = y.reshape(CHUNK, SSM_GROUP, lane_chunk)


def _s5(ut, uct, wall, cot, lam16, dcol):
    _, sw, nl = ut.shape
    nlc = uct.shape[-1]
    gp = sw // (2 * SSM_GROUP)
    kdim = CHUNK * SSM_GROUP
    lane_chunk = min(nl, 512)
    kern = functools.partial(_s5_kernel, lane_chunk=lane_chunk)
    return pl.pallas_call(
        kern,
        grid=(gp,),
        in_specs=[pl.BlockSpec((CHUNK, 2 * SSM_GROUP, nl), lambda g: (0, g, 0)),
                  pl.BlockSpec((CHUNK, 2 * SSM_GROUP, nlc), lambda g: (0, g, 0)),
                  pl.BlockSpec((2, kdim, kdim), lambda g: (g, 0, 0)),
                  pl.BlockSpec((2, kdim, 2 * kdim), lambda g: (g, 0, 0)),
                  pl.BlockSpec((2, None, 2, LANES), lambda g: (0, g, 0, 0)),
                  pl.BlockSpec((2, kdim, 1), lambda g: (g, 0, 0))],
        out_specs=pl.BlockSpec((CHUNK, 2 * SSM_GROUP, nl), lambda g: (0, g, 0)),
        out_shape=jax.ShapeDtypeStruct((CHUNK, sw, nl), BF16),
        scratch_shapes=[pltpu.VMEM((2, 2, nlc + nl, LANES), F32),
                        pltpu.VMEM((2, 2, nl, LANES), F32)],
        compiler_params=_cparams(("arbitrary",)),
        name="s5",
    )(ut, uct, wall, cot, lam16, dcol)


CONV_CB = 128
CONV_SKEW = 8


def _convt_kernel(v_ref, t_ref, b_ref, o_ref, s1_ref, s2_ref, z_sc, o_sc):
    a, cb, _ = v_ref.shape
    kl = t_ref.shape[-1]
    pitch = cb + CONV_SKEW
    z = v_ref[...].astype(F32)
    for i in range(a):
        z_sc[i * pitch:i * pitch + cb, :] = z[i]
    s1 = jnp.zeros((a, LANES), F32)
    s2 = jnp.zeros((a, LANES), F32)
    k_pad = jnp.zeros((kl - a, LANES), BF16)
    for c in range(cb):
        col = z_sc[pl.ds(c, a, stride=pitch), :].astype(BF16)
        rhs = jnp.concatenate([col, k_pad], axis=0) if kl > a else col
        out = jnp.dot(t_ref[c], rhs, preferred_element_type=F32) + b_ref[c:c + 1]
        o_sc[pl.ds(c, a, stride=pitch), :] = out
        s1 = s1 + out
        s2 = s2 + out * out
    for i in range(a):
        o_ref[i] = o_sc[i * pitch:i * pitch + cb, :].astype(o_ref.dtype)

    @pl.when(pl.program_id(1) == 0)
    def _():
        s1_ref[...] = jnp.zeros_like(s1_ref)
        s2_ref[...] = jnp.zeros_like(s2_ref)

    s1_ref[...] += s1
    s2_ref[...] += s2


def _convt(vt, t, bias):
    a, nq, c, _ = vt.shape
    cb = min(c, CONV_CB)
    kl = t.shape[-1]
    stat_spec = pl.BlockSpec((a, LANES), lambda q, i: (0, q))
    return pl.pallas_call(
        _convt_kernel,
        grid=(nq, c // cb),
        in_specs=[pl.BlockSpec((a, None, cb, LANES), lambda q, i: (0, q, i, 0)),
                  pl.BlockSpec((cb, a, kl), lambda q, i: (i, 0, 0)),
                  pl.BlockSpec((cb, 1), lambda q, i: (i, 0))],
        out_specs=[pl.BlockSpec((a, cb, LANES), lambda q, i: (0, i, q)), stat_spec, stat_spec],
        out_shape=[jax.ShapeDtypeStruct((a, c, nq * LANES), BF16),
                   jax.ShapeDtypeStruct((a, nq * LANES), F32),
                   jax.ShapeDtypeStruct((a, nq * LANES), F32)],
        scratch_shapes=[pltpu.VMEM((a * (cb + CONV_SKEW), LANES), F32),
                        pltpu.VMEM((a * (cb + CONV_SKEW), LANES), F32)],
        compiler_params=_cparams(("arbitrary", "arbitrary")),
        name="convt",
    )(vt, t, bias.reshape(c, 1))


@functools.lru_cache(maxsize=None)
def _tap_selector(a):
    kl = -(-a // LANES) * LANES
    tap = np.arange(kl)[None, :] - np.arange(a)[:, None] + CONV_PAD
    hit = (tap[None] == np.arange(CONV_TAPS)[:, None, None]) & (np.arange(kl) < a)
    return hit.astype(BF16)


def _conv_toeplitz(w, a):
    return jnp.einsum("kc,kxy->cxy", w.astype(BF16), jnp.asarray(_tap_selector(a)),
                      preferred_element_type=BF16)


def _conv_branch_t(cv, g1, st, lng_ref, lnb_ref, n_chan):
    mean = (st[0:1] + st[2:3]) * (1.0 / n_chan)
    var = (st[1:2] + st[3:4]) * (1.0 / n_chan) - mean * mean
    y = (cv.astype(F32) - mean) * lax.rsqrt(var + EPS) * lng_ref[...] + lnb_ref[...]
    return (jax.nn.silu(y) * g1.astype(F32)).astype(BF16)


def _tail_w_kernel(y_ref, g2_ref, cv_ref, g1_ref, st_ref, wt_ref, bt_ref, wos_ref, woh_ref,
                   lng_ref, lnb_ref, o_ref, mt_sc, *, n_chan):
    tl = y_ref.shape[-1]
    st = st_ref[...]
    mix = []
    for k in range(2):
        lanes = slice(k * tl, (k + 1) * tl)
        y = y_ref[k]
        z = jnp.dot(wt_ref[...], y, preferred_element_type=F32) + bt_ref[...]
        s = (y.astype(F32) * jax.nn.sigmoid(z) * g2_ref[k].astype(F32)).astype(BF16)
        co = _conv_branch_t(cv_ref[k], g1_ref[:, lanes], st[:, lanes], lng_ref, lnb_ref, n_chan)
        mt_sc[...] = (jnp.dot(wos_ref[...], s, preferred_element_type=F32)
                      + jnp.dot(woh_ref[...], co, preferred_element_type=F32))
        mix.append(mt_sc[...].T.reshape(tl // BATCH, BATCH, mt_sc.shape[0]))
    o_ref[...] = jnp.concatenate(mix, axis=1).astype(o_ref.dtype)


def _tail_w(y1t, g2t, ct_h, g1t_h, stats_h, glu_wt, glu_bcol, wot, ln_g, ln_b, n_chan):
    _, sw, _ = y1t.shape
    _, hc, tl = ct_h.shape
    d = wot.shape[0]
    assert (2 * hc) % sw == 0
    r = tl // BATCH
    half_chunk = CHUNK // 2
    s5_spec = pl.BlockSpec((2, sw, tl), lambda j: (j % half_chunk, 0, j // half_chunk))
    tok = lambda rows: pl.BlockSpec((rows, 2 * tl), lambda j: (0, j))
    full = lambda arr: pl.BlockSpec(arr.shape, lambda j: (0,) * arr.ndim)
    return pl.pallas_call(
        functools.partial(_tail_w_kernel, n_chan=n_chan),
        grid=(GRID_W // 2,),
        in_specs=[s5_spec, s5_spec, pl.BlockSpec((2, hc, tl), lambda j: (j, 0, 0)), tok(hc),
                  tok(8), full(glu_wt), full(glu_bcol),
                  pl.BlockSpec((d, sw), lambda j: (0, 2 * hc // sw)),
                  pl.BlockSpec((d, hc), lambda j: (0, 0)), full(ln_g), full(ln_b)],
        out_specs=pl.BlockSpec((r, None, 2 * BATCH, d), lambda j: (0, j, 0, 0)),
        out_shape=jax.ShapeDtypeStruct((r, GRID_W // 2, 2 * BATCH, d), BF16),
        scratch_shapes=[pltpu.VMEM((d, tl), F32)],
        compiler_params=_cparams(("arbitrary",)),
        name="tail_w",
    )(y1t, g2t, ct_h, g1t_h, stats_h, glu_wt, glu_bcol, wot, wot, ln_g, ln_b)


FINAL_ROWS = 2


def _final_r_kernel(cv_ref, g1_ref, st_ref, mh_ref, x_ref, wo_ref, lng_ref, lnb_ref, gate_ref,
                    fg_ref, o_ref, mix_sc, *, n_chan):
    tl, d = x_ref.shape[1:]
    cv = jnp.concatenate([cv_ref[i] for i in range(cv_ref.shape[0])], axis=-1)
    co = _conv_branch_t(cv, g1_ref[...], st_ref[...], lng_ref, lnb_ref, n_chan)
    mix = lax.dot_general(co.T, wo_ref[...], (((1,), (1,)), ((), ())),
                          preferred_element_type=F32) + mh_ref[...].reshape(tl * BATCH, d)
    nlt = mix_sc.shape[0]
    for j in range(nlt):
        mix_sc[j] = mix[:, j * LANES:(j + 1) * LANES]
    for b in range(BATCH):
        mb = jnp.concatenate([mix_sc[j, pl.ds(b, tl, stride=BATCH), :] for j in range(nlt)],
                             axis=-1)
        xo = x_ref[b] + gate_ref[b:b + 1, :] * mb
        ms = jnp.mean(xo * xo, axis=-1, keepdims=True)
        o_ref[b] = xo * lax.rsqrt(ms + EPS) * fg_ref[...]


def _final_r(ct_v, g1t_v, stats_v, mix_h, x, wot, ln_g, ln_b, gate8, final_g, n_chan):
    r, hc, row_tokens = ct_v.shape
    _, length, d = x.shape
    rs = math.gcd(r, FINAL_ROWS)
    tm = rs * row_tokens
    tl = tm // BATCH
    tok = lambda rows: pl.BlockSpec((rows, tm), lambda i: (0, i))
    nat = pl.BlockSpec((BATCH, tl, d), lambda i: (0, i, 0))
    full = lambda arr: pl.BlockSpec(arr.shape, lambda i: (0,) * arr.ndim)
    return pl.pallas_call(
        functools.partial(_final_r_kernel, n_chan=n_chan),
        grid=(r // rs,),
        in_specs=[pl.BlockSpec((rs, hc, row_tokens), lambda i: (i, 0, 0)), tok(hc), tok(8),
                  pl.BlockSpec((rs, GRID_W // 2, 2 * BATCH, d), lambda i: (i, 0, 0, 0)), nat,
                  pl.BlockSpec((d, hc), lambda i: (0, 1)), full(ln_g), full(ln_b), full(gate8),
                  full(final_g)],
        out_specs=nat,
        out_shape=jax.ShapeDtypeStruct(x.shape, F32),
        scratch_shapes=[pltpu.VMEM((d // LANES, tm, LANES), F32)],
        compiler_params=_cparams(("arbitrary",)),
        name="final_r",
    )(ct_v, g1t_v, stats_v, mix_h, x, wot, ln_g, ln_b, gate8, final_g)


S5PREP_GROUPS = 4


def _s5prep_kernel(arow_ref, bt_ref, ct_ref, wall_ref, cot_ref, lam_ref, pw_sc, kk_sc):
    for gi in range(arow_ref.shape[0]):
        _s5prep_group(arow_ref.at[gi], bt_ref.at[gi], ct_ref.at[gi], wall_ref.at[gi], cot_ref.at[gi],
                      lam_ref.at[gi], pw_sc, kk_sc)


def _s5prep_group(arow_ref, bt_ref, ct_ref, wall_ref, cot_ref, lam_ref, pw_sc, kk_sc):
    kdim = CHUNK * SSM_GROUP
    hp = lax.Precision.HIGHEST
    lane = lax.broadcasted_iota(jnp.int32, (1, LANES), 1)
    is_re = lane < SSM_STATE
    col_t = lax.broadcasted_iota(jnp.int32, (1, kdim), 1) // SSM_GROUP
    kcol = lax.broadcasted_iota(jnp.int32, (pw_sc.shape[1], 1), 0).astype(F32)

    def tile_rows(a):
        return jnp.concatenate([a] * CHUNK, axis=0)

    def power_rows(exps):
        pr = jnp.concatenate([jnp.broadcast_to(pw_sc[0, e:e + 1, :], (SSM_GROUP, LANES))
                              for e in exps], axis=0)
        pi = jnp.concatenate([jnp.broadcast_to(pw_sc[1, e:e + 1, :], (SSM_GROUP, LANES))
                              for e in exps], axis=0)
        return pr, pi

    def cmul(xr, xi, yr, yi, im_sign):
        return jnp.where(is_re, xr * yr - xi * yi, im_sign * (xr * yi + xi * yr))

    toeplitz = jnp.zeros((kdim, kdim), F32)
    for d in range(2):
        ar, ai = arow_ref[d, 0:1, :], arow_ref[d, 1:2, :]
        dt = jnp.exp(arow_ref[d, 2:3, :])
        mag = jnp.exp(kcol * (ar * dt))
        ang = kcol * (ai * dt)
        pw_sc[0] = mag * jnp.cos(ang)
        pw_sc[1] = mag * jnp.sin(ang)
        lam_ref[d, 0] = pw_sc[0, CHUNK:CHUNK + 8, :]
        lam_ref[d, 1] = pw_sc[1, CHUNK:CHUNK + 8, :]
        nr, ni = pw_sc[0, 1:2, :] - 1.0, pw_sc[1, 1:2, :]
        den = ar * ar + ai * ai
        qr, qi = (nr * ar + ni * ai) / den, (ni * ar - nr * ai) / den
        btr, bti = bt_ref[d, 0], bt_ref[d, 1]
        bbr, bbi = tile_rows(qr * btr - qi * bti), tile_rows(qr * bti + qi * btr)
        cr, ci = tile_rows(ct_ref[d, 0]), tile_rows(ct_ref[d, 1])
        tt = list(range(CHUNK))
        e_lag = [CHUNK - 1 - t for t in tt] if d else tt
        e_in = tt if d else [CHUNK - 1 - t for t in tt]
        e_out = [CHUNK - t for t in tt] if d else [t + 1 for t in tt]
        cot_ref[:, kdim + d * LANES:kdim + (d + 1) * LANES] = cmul(
            cr, ci, *power_rows(e_out), -1.0).astype(BF16)
        wall_ref[d * LANES:(d + 1) * LANES, :] = cmul(bbr, bbi, *power_rows(e_in), 1.0).T.astype(BF16)
        c_lag = cmul(cr, ci, *power_rows(e_lag), -1.0)
        bb = jnp.where(is_re, bbr, bbi)
        kk = lax.dot_general(c_lag, bb, (((1,), (1,)), ((), ())), precision=hp,
                             preferred_element_type=F32)
        zeros = jnp.zeros((kdim, kdim), F32)
        if d:
            kk_sc[:kdim] = kk
            kk_sc[kdim:] = zeros
            starts = [SSM_GROUP * (CHUNK - 1 - t) for t in tt]
        else:
            kk_sc[:kdim] = zeros
            kk_sc[kdim:] = kk
            starts = [kdim - SSM_GROUP * t for t in tt]
        tiles = []
        for lt in range(kdim // LANES):
            cols = slice(lt * LANES, (lt + 1) * LANES)
            acc = jnp.zeros((kdim, LANES), F32)
            for t in range(lt * LANES // SSM_GROUP, (lt + 1) * LANES // SSM_GROUP):
                acc = jnp.where(col_t[:, cols] == t, kk_sc[starts[t]:starts[t] + kdim, cols], acc)
            tiles.append(acc)
        toeplitz = toeplitz + jnp.concatenate(tiles, axis=1)
    cot_ref[:, :kdim] = toeplitz.astype(BF16)


def _s5prep(arow, bt, ct):
    g = arow.shape[0]
    gs = math.gcd(g, S5PREP_GROUPS)
    kdim = CHUNK * SSM_GROUP
    return pl.pallas_call(
        _s5prep_kernel,
        grid=(g // gs,),
        in_specs=[pl.BlockSpec((gs, 2, 8, LANES), lambda i: (i, 0, 0, 0)),
                  pl.BlockSpec((gs, 2, 2, SSM_GROUP, LANES), lambda i: (i, 0, 0, 0, 0)),
                  pl.BlockSpec((gs, 2, 2, SSM_GROUP, LANES), lambda i: (i, 0, 0, 0, 0))],
        out_specs=[pl.BlockSpec((gs, kdim, kdim), lambda i: (i, 0, 0)),
                   pl.BlockSpec((gs, kdim, 2 * kdim), lambda i: (i, 0, 0)),
                   pl.BlockSpec((gs, 2, 2, 8, LANES), lambda i: (i, 0, 0, 0, 0))],
        out_shape=[jax.ShapeDtypeStruct((g, kdim, kdim), BF16),
                   jax.ShapeDtypeStruct((g, kdim, 2 * kdim), BF16),
                   jax.ShapeDtypeStruct((g, 2, 2, 8, LANES), F32)],
        scratch_shapes=[pltpu.VMEM((2, CHUNK + 8, LANES), F32),
                        pltpu.VMEM((2 * kdim, kdim), F32)],
        compiler_params=_cparams(("arbitrary",)),
        name="s5prep",
    )(arow, bt, ct)


def _s5prep_inputs(a_re, a_im, log_dt, b_re, b_im, c_re, c_im):
    g = a_re.shape[1]
    dup = lambda a: jnp.concatenate([a, a], axis=-1)
    rows = jnp.stack([dup(a_re), dup(a_im),
                      jnp.broadcast_to(log_dt[..., None], (2, g, LANES))], axis=2)
    arow = jnp.pad(rows, ((0, 0), (0, 0), (0, 5), (0, 0))).transpose(1, 0, 2, 3)
    bt = jnp.stack([dup(b_re.transpose(0, 1, 3, 2)), dup(b_im.transpose(0, 1, 3, 2))], axis=2)
    ct = jnp.stack([dup(c_re), dup(c_im)], axis=2)
    return arow, bt.transpose(1, 0, 2, 3, 4), ct.transpose(1, 0, 2, 3, 4)


def kernel(x, c, ctx, c_ctx, norm_g, w_ada, b_ada, w_in, conv_dw, conv_db, conv_ln_g, conv_ln_b,
           ssm_a_re, ssm_a_im, ssm_log_dt, ssm_b_re, ssm_b_im, ssm_c_re, ssm_c_im, ssm_d,
           ssm_glu_w, ssm_glu_b, w_out, final_g):
    bsz, length, d = x.shape
    ctx_len = ctx.shape[1]
    cw = conv_dw.shape[-1]
    sw = ssm_d.shape[-1]
    assert bsz == BATCH and norm_g.shape[0] == 1
    assert length % (GRID_W * CHUNK) == 0 and ctx_len % CHUNK == 0
    n = length * BATCH
    nch = length // CHUNK
    nchc = ctx_len // CHUNK
    rows = length // GRID_W

    cond16 = jnp.zeros((16, d), F32).at[:BATCH].set(c).at[BATCH].set(c_ctx)
    mod = _ada(cond16, w_ada.reshape(w_ada.shape[1:]), b_ada[0])
    shift, scale, gate = mod[:, :d], mod[:, d:2 * d], mod[:, 2 * d:]
    amp = norm_g[0][None] * (1.0 + scale)
    a8, s8, gate8 = amp[:BATCH], shift[:BATCH], gate[:BATCH]
    a8c = jnp.broadcast_to(amp[BATCH], (BATCH, d))
    s8c = jnp.broadcast_to(shift[BATCH], (BATCH, d))

    w = w_in.reshape(w_in.shape[1:])
    half = cw // 2
    wt_c = _transpose_cast(w, 0, 3 * cw, half)
    wt_s = _transpose_cast(w, 3 * cw, 2 * sw, half)
    wall, cot, lam_raw = _s5prep(*_s5prep_inputs(
        ssm_a_re[0], ssm_a_im[0], ssm_log_dt[0], ssm_b_re[0], ssm_b_im[0], ssm_c_re[0], ssm_c_im[0]))
    ng = sw // SSM_GROUP
    lam16 = lam_raw[:, :, :, 0, :SSM_STATE].reshape(ng // 2, 2, 2, 2, SSM_STATE).transpose(
        2, 0, 3, 1, 4).reshape(2, ng // 2, 2, LANES)
    dcol = jnp.broadcast_to(ssm_d[0].reshape(ng, 1, SSM_GROUP), (ng, CHUNK, SSM_GROUP)).reshape(
        ng, CHUNK * SSM_GROUP, 1)

    xt = x.transpose(1, 0, 2)
    ctxt = ctx.transpose(1, 0, 2)
    c4 = ctxt.reshape(nchc, CHUNK // 2, 2 * BATCH, d)

    x4 = xt.reshape(rows, GRID_W, BATCH, d)
    ut, g2t, vt_h, g1t_h = _inproj_w(x4, a8, s8, wt_s, wt_c, half)
    vt_v, g1t_v = _inproj_r(x4, a8, s8, wt_c, half)
    uct = _ctxproj(c4, a8c, s8c, wt_s)
    y1t = _s5(ut, uct, wall, cot, lam16, dcol)

    dw, db = conv_dw[0], conv_db[0]
    ct_h, s1_h, s2_h = _convt(vt_h, _conv_toeplitz(dw[:, :half], GRID_W), db[:half])
    ct_v, s1_v, s2_v = _convt(vt_v, _conv_toeplitz(dw[:, half:], rows), db[half:])
    to_v = lambda a: a.reshape(GRID_W, rows, BATCH).transpose(1, 0, 2).reshape(1, n)
    to_h = lambda a: a.reshape(rows, GRID_W, BATCH).transpose(1, 0, 2).reshape(1, n)
    pad4 = jnp.zeros((4, n), F32)
    stats_v = jnp.concatenate([s1_v.reshape(1, n), s2_v.reshape(1, n), to_v(s1_h), to_v(s2_h), pad4])
    stats_h = jnp.concatenate([s1_h.reshape(1, n), s2_h.reshape(1, n), to_h(s1_v), to_h(s2_v), pad4])

    ln_g, ln_b = conv_ln_g[0].reshape(cw, 1), conv_ln_b[0].reshape(cw, 1)
    wot = _transpose_cast(w_out.reshape(w_out.shape[1:]), 0, d, LANES)
    mix_h = _tail_w(y1t, g2t, ct_h, g1t_h, stats_h, _transpose_cast(ssm_glu_w[0], 0, sw, half),
                    ssm_glu_b[0].reshape(sw, 1), wot, ln_g[:half], ln_b[:half], cw)
    return _final_r(ct_v, g1t_v, stats_v, mix_h, x, wot, ln_g[half:],
                    ln_b[half:], gate8, final_g.reshape(1, d), cw)
```

```python
import functools
import math

import jax
import jax.numpy as jnp
import numpy as np
from jax import lax
from jax.experimental import pallas as pl
from jax.experimental.pallas import tpu as pltpu

GRID_W = 64
CONV_TAPS = 31
CONV_PAD = CONV_TAPS // 2
SSM_GROUP = 16
SSM_STATE = 64
CHUNK = 16
BATCH = 8
EPS = 1e-6
LANES = 128
MXU_WIDTH = 256
VMEM_LIMIT = 56 * 1024 * 1024

F32 = jnp.float32
BF16 = jnp.bfloat16


def _cparams(sem):
    return pltpu.CompilerParams(dimension_semantics=sem, vmem_limit_bytes=VMEM_LIMIT)


def _ada_kernel(c_ref, w_ref, b_ref, o_ref):
    s = jax.nn.silu(c_ref[...])
    o_ref[...] = jnp.dot(s, w_ref[...], preferred_element_type=F32,
                         precision=lax.Precision.HIGHEST) + b_ref[...]


def _ada(cond16, w_ada, b_ada):
    d, n3 = w_ada.shape
    return pl.pallas_call(
        _ada_kernel,
        grid=(n3 // d,),
        in_specs=[pl.BlockSpec((16, d), lambda j: (0, 0)),
                  pl.BlockSpec((d, d), lambda j: (0, j)),
                  pl.BlockSpec((1, d), lambda j: (0, j))],
        out_specs=pl.BlockSpec((16, d), lambda j: (0, j)),
        out_shape=jax.ShapeDtypeStruct((16, n3), F32),
        compiler_params=_cparams(("arbitrary",)),
        name="ada",
    )(cond16, w_ada, b_ada.reshape(1, n3))


def _modulated_norm(x, a, s):
    ms = jnp.mean(x * x, axis=-1, keepdims=True)
    return x * lax.rsqrt(ms + EPS) * a[None] + s[None]


def _lane_pieces(n_lanes):
    width = MXU_WIDTH if n_lanes % MXU_WIDTH == 0 else n_lanes
    return [(lo, width) for lo in range(0, n_lanes, width)]


def _conv_gates(q, hc, vt_ref, g1t_ref, lane0, gate_lane0=0):
    width = q.shape[1]
    val = (q[:hc] * jax.nn.sigmoid(q[hc:2 * hc])).astype(BF16)
    for j in range(width // LANES):
        vt_ref[lane0 // LANES + j] = val[:, j * LANES:(j + 1) * LANES]
    lo = gate_lane0 + lane0
    g1t_ref[:, lo:lo + width] = jax.nn.silu(q[2 * hc:]).astype(BF16)


def _transpose_cast_kernel(w_ref, o_ref):
    o_ref[...] = w_ref[...].T.astype(o_ref.dtype)


def _transpose_cast(w, col0, ncols, tile):
    k = w.shape[0]
    assert col0 % tile == 0 and ncols % tile == 0
    return pl.pallas_call(
        _transpose_cast_kernel,
        grid=(ncols // tile,),
        in_specs=[pl.BlockSpec((k, tile), lambda j: (0, j + col0 // tile))],
        out_specs=pl.BlockSpec((tile, k), lambda j: (j, 0)),
        out_shape=jax.ShapeDtypeStruct((ncols, k), BF16),
        compiler_params=_cparams(("arbitrary",)),
        name="wtrans",
    )(w)


def _project_t(w_refs, h):
    nt = (((1,), (1,)), ((), ()))
    return jnp.concatenate([lax.dot_general(w[...], h, nt, preferred_element_type=F32)
                            for w in w_refs], axis=0)


def _inproj_w_kernel(x_ref, a_ref, s_ref, ws_ref, wv_ref, wg_ref, wa_ref, ut_ref, g2t_ref, vt_ref,
                     g1t_ref):
    r, ncol, _, d = x_ref.shape
    sw = ut_ref.shape[1]
    hc = g1t_ref.shape[0]
    tl = r * BATCH
    for k in range(ncol):
        for lo, width in _lane_pieces(tl):
            x = x_ref[lo // BATCH:(lo + width) // BATCH, k]
            h = _modulated_norm(x, a_ref[...], s_ref[...]).reshape(width, d).astype(BF16)
            q = _project_t([ws_ref], h)
            ut_ref[k, :, lo:lo + width] = q[:sw].astype(BF16)
            g2t_ref[k, :, lo:lo + width] = jax.nn.silu(q[sw:]).astype(BF16)
            _conv_gates(_project_t([wv_ref, wg_ref, wa_ref], h), hc, vt_ref.at[k], g1t_ref,
                        lo, k * tl)


def _conv_weight_specs(hc, d, part, index_map):
    return [pl.BlockSpec((hc, d), functools.partial(index_map, 2 * k + part)) for k in range(3)]


INPROJ_COLS = 2


def _inproj_w(x4, a8, s8, wt_s, wt_c, hc):
    r, _, _, d = x4.shape
    sw = wt_s.shape[0] // 2
    tl = r * BATCH
    nq = tl // LANES
    cpr = GRID_W // CHUNK
    nc = INPROJ_COLS
    per_chunk = CHUNK // nc
    s5_spec = pl.BlockSpec((nc, sw, tl), lambda j: (j % per_chunk, 0, j // per_chunk))
    return pl.pallas_call(
        _inproj_w_kernel,
        grid=(GRID_W // nc,),
        in_specs=[pl.BlockSpec((r, nc, BATCH, d), lambda j: (0, j, 0, 0)),
                  pl.BlockSpec((BATCH, d), lambda j: (0, 0)),
                  pl.BlockSpec((BATCH, d), lambda j: (0, 0)),
                  pl.BlockSpec(wt_s.shape, lambda j: (0, 0)),
                  *_conv_weight_specs(hc, d, 0, lambda blk, j: (blk, 0))],
        out_specs=[s5_spec, s5_spec,
                   pl.BlockSpec((nc, nq, hc, LANES), lambda j: (j, 0, 0, 0)),
                   pl.BlockSpec((hc, nc * tl), lambda j: (0, j))],
        out_shape=[jax.ShapeDtypeStruct((CHUNK, sw, cpr * tl), BF16),
                   jax.ShapeDtypeStruct((CHUNK, sw, cpr * tl), BF16),
                   jax.ShapeDtypeStruct((GRID_W, nq, hc, LANES), BF16),
                   jax.ShapeDtypeStruct((hc, GRID_W * tl), BF16)],
        compiler_params=_cparams(("arbitrary",)),
        name="inproj_w",
    )(x4, a8, s8, wt_s, wt_c, wt_c, wt_c)


def _inproj_r_kernel(x_ref, a_ref, s_ref, wv_ref, wg_ref, wa_ref, vt_ref, g1t_ref):
    nrow, gw, _, d = x_ref.shape
    hc = g1t_ref.shape[0]
    tl = gw * BATCH
    for k in range(nrow):
        for lo, width in _lane_pieces(tl):
            x = x_ref[k, lo // BATCH:(lo + width) // BATCH]
            h = _modulated_norm(x, a_ref[...], s_ref[...]).reshape(width, d).astype(BF16)
            _conv_gates(_project_t([wv_ref, wg_ref, wa_ref], h), hc, vt_ref.at[k], g1t_ref,
                        lo, k * tl)


def _inproj_r(x4, a8, s8, wt_c, hc):
    r, gw, _, d = x4.shape
    tl = gw * BATCH
    nq = tl // LANES
    nr = math.gcd(r, INPROJ_COLS)
    return pl.pallas_call(
        _inproj_r_kernel,
        grid=(r // nr,),
        in_specs=[pl.BlockSpec((nr, gw, BATCH, d), lambda i: (i, 0, 0, 0)),
                  pl.BlockSpec((BATCH, d), lambda i: (0, 0)),
                  pl.BlockSpec((BATCH, d), lambda i: (0, 0)),
                  *_conv_weight_specs(hc, d, 1, lambda blk, i: (blk, 0))],
        out_specs=[pl.BlockSpec((nr, nq, hc, LANES), lambda i: (i, 0, 0, 0)),
                   pl.BlockSpec((hc, nr * tl), lambda i: (0, i))],
        out_shape=[jax.ShapeDtypeStruct((r, nq, hc, LANES), BF16),
                   jax.ShapeDtypeStruct((hc, r * tl), BF16)],
        compiler_params=_cparams(("arbitrary",)),
        name="inproj_r",
    )(x4, a8, s8, wt_c, wt_c, wt_c)


def _ctxproj_kernel(x_ref, a_ref, s_ref, wut_ref, ut_ref):
    cb, _, d = x_ref.shape
    x = x_ref[...].reshape(cb * 2, BATCH, d)
    h = _modulated_norm(x, a_ref[...], s_ref[...]).reshape(cb, 2 * BATCH, d)
    for tl in range(2):
        ht = h[:, tl * BATCH:(tl + 1) * BATCH, :].reshape(cb * BATCH, d).astype(BF16)
        q = lax.dot_general(wut_ref[...], ht, (((1,), (1,)), ((), ())),
                            preferred_element_type=F32)
        ut_ref[tl] = q.astype(BF16)


def _ctxproj(x4, a8, s8, wt_s):
    nch, _, _, d = x4.shape
    sw = wt_s.shape[0] // 2
    return pl.pallas_call(
        _ctxproj_kernel,
        grid=(CHUNK // 2,),
        in_specs=[pl.BlockSpec((nch, None, 2 * BATCH, d), lambda t: (0, t, 0, 0)),
                  pl.BlockSpec((BATCH, d), lambda t: (0, 0)),
                  pl.BlockSpec((BATCH, d), lambda t: (0, 0)),
                  pl.BlockSpec((sw, d), lambda t: (0, 0))],
        out_specs=pl.BlockSpec((2, sw, nch * BATCH), lambda t: (t, 0, 0)),
        out_shape=jax.ShapeDtypeStruct((CHUNK, sw, nch * BATCH), BF16),
        compiler_params=_cparams(("arbitrary",)),
        name="ctxproj",
    )(x4, a8, s8, wt_s)


GELU_C1 = math.sqrt(2.0 / math.pi)
GELU_C2 = GELU_C1 * 0.044715


def _gelu_tanh(x):
    hx = 0.5 * x
    return hx + hx * jnp.tanh(x * (GELU_C1 + GELU_C2 * (x * x)))


def _s5_kernel(xt_ref, xct_ref, wall_ref, cot_ref, lam_ref, dcol_ref, y_ref,
               s_sc, h_sc, *, lane_chunk):
    nl = xt_ref.shape[-1]
    nlc = xct_ref.shape[-1]
    n_ch = nl // BATCH
    n_chc = nlc // BATCH
    kdim = CHUNK * SSM_GROUP
    p = SSM_STATE

    def group_x(ref, j):
        return ref[:, j * SSM_GROUP:(j + 1) * SSM_GROUP, :].reshape(kdim, ref.shape[-1])

    st_c = [jnp.dot(wall_ref[j], group_x(xct_ref, j), preferred_element_type=F32)
            for j in range(2)]
    st = [jnp.dot(wall_ref[j], group_x(xt_ref, j), preferred_element_type=F32) for j in range(2)]
    for d in range(2):
        for src, off, width in ((st_c, 0, nlc), (st, nlc, nl)):
            for ri in range(2):
                lo = (2 * d + ri) * p
                pair = jnp.concatenate([src[0][lo:lo + p], src[1][lo:lo + p]], axis=0)
                s_sc[d, ri, off:off + width, :] = pair.T

    lam = [[jnp.broadcast_to(lam_ref[d, ri:ri + 1, :], (BATCH, LANES)) for ri in range(2)]
           for d in range(2)]

    def advance(d, state, row):
        hr, hi = state
        ar, ai = lam[d]
        sr = s_sc[d, 0, pl.ds(row, BATCH), :]
        si = s_sc[d, 1, pl.ds(row, BATCH), :]
        return ar * hr - ai * hi + sr, ar * hi + ai * hr + si

    zero = jnp.zeros((BATCH, LANES), F32)

    f = b = (zero, zero)
    for i in range(n_chc):
        f = advance(0, f, i * BATCH)
        b = advance(1, b, (n_chc - 1 - i) * BATCH)

    cpr = GRID_W // CHUNK
    n_rows = n_ch // cpr

    def lane_row(ch):
        return ((ch % cpr) * n_rows + ch // cpr) * BATCH

    for i in range(n_ch):
        rf = lane_row(i)
        rb = lane_row(n_ch - 1 - i)
        h_sc[0, 0, rf:rf + BATCH, :] = f[0]
        h_sc[0, 1, rf:rf + BATCH, :] = f[1]
        h_sc[1, 0, rb:rb + BATCH, :] = b[0]
        h_sc[1, 1, rb:rb + BATCH, :] = b[1]
        f = advance(0, f, nlc + rf)
        b = advance(1, b, nlc + rb)

    for c0 in range(0, nl, lane_chunk):
        cs = slice(c0, c0 + lane_chunk)
        ht = [[h_sc[d, ri, cs, :].T for ri in range(2)] for d in range(2)]
        for j in range(2):
            x = xt_ref[:, j * SSM_GROUP:(j + 1) * SSM_GROUP, cs].reshape(kdim, lane_chunk)
            hj = jnp.concatenate([ht[d][ri][j * p:(j + 1) * p] for d in range(2) for ri in range(2)],
                                 axis=0).astype(BF16)
            acc = (dcol_ref[j] * x.astype(F32)
                   + jnp.dot(cot_ref[j], jnp.concatenate([x, hj], axis=0),
                             preferred_element_type=F32))
            y = _gelu_tanh(acc).astype(BF16)
            y_ref[:, j * SSM_GROUP:(j + 1) * SSM_GROUP, cs] = y.reshape(CHUNK, SSM_GROUP, lane_chunk)


def _s5(ut, uct, wall, cot, lam16, dcol):
    _, sw, nl = ut.shape
    nlc = uct.shape[-1]
    gp = sw // (2 * SSM_GROUP)
    kdim = CHUNK * SSM_GROUP
    lane_chunk = min(nl, 512)
    kern = functools.partial(_s5_kernel, lane_chunk=lane_chunk)
    return pl.pallas_call(
        kern,
        grid=(gp,),
        in_specs=[pl.BlockSpec((CHUNK, 2 * SSM_GROUP, nl), lambda g: (0, g, 0)),
                  pl.BlockSpec((CHUNK, 2 * SSM_GROUP, nlc), lambda g: (0, g, 0)),
                  pl.BlockSpec((2, kdim, kdim), lambda g: (g, 0, 0)),
                  pl.BlockSpec((2, kdim, 2 * kdim), lambda g: (g, 0, 0)),
                  pl.BlockSpec((2, None, 2, LANES), lambda g: (0, g, 0, 0)),
                  pl.BlockSpec((2, kdim, 1), lambda g: (g, 0, 0))],
        out_specs=pl.BlockSpec((CHUNK, 2 * SSM_GROUP, nl), lambda g: (0, g, 0)),
        out_shape=jax.ShapeDtypeStruct((CHUNK, sw, nl), BF16),
        scratch_shapes=[pltpu.VMEM((2, 2, nlc + nl, LANES), F32),
                        pltpu.VMEM((2, 2, nl, LANES), F32)],
        compiler_params=_cparams(("arbitrary",)),
        name="s5",
    )(ut, uct, wall, cot, lam16, dcol)


CONV_CB = 32
CONV_SKEW = 8


def _convt_kernel(v_ref, t_ref, b_ref, o_ref, s1_ref, s2_ref, z_sc, o_sc):
    a, nq, cb, _ = v_ref.shape
    kl = t_ref.shape[-1]
    pitch = cb + CONV_SKEW
    k_pad = jnp.zeros((kl - a, LANES), BF16)
    for q in range(nq):
        lanes = slice(q * LANES, (q + 1) * LANES)
        z = v_ref[:, q].astype(F32)
        for i in range(a):
            z_sc[i * pitch:i * pitch + cb, :] = z[i]
        s1 = jnp.zeros((a, LANES), F32)
        s2 = jnp.zeros((a, LANES), F32)
        for c in range(cb):
            col = z_sc[pl.ds(c, a, stride=pitch), :].astype(BF16)
            rhs = jnp.concatenate([col, k_pad], axis=0) if kl > a else col
            out = jnp.dot(t_ref[c], rhs, preferred_element_type=F32) + b_ref[c:c + 1]
            o_sc[pl.ds(c, a, stride=pitch), :] = out
            s1 = s1 + out
            s2 = s2 + out * out
        for i in range(a):
            o_ref[i, :, lanes] = o_sc[i * pitch:i * pitch + cb, :].astype(o_ref.dtype)

        @pl.when(pl.program_id(0) == 0)
        def _():
            s1_ref[:, lanes] = jnp.zeros((a, LANES), F32)
            s2_ref[:, lanes] = jnp.zeros((a, LANES), F32)

        s1_ref[:, lanes] += s1
        s2_ref[:, lanes] += s2


def _convt(vt, t, bias):
    a, nq, c, _ = vt.shape
    cb = min(c, CONV_CB)
    kl = t.shape[-1]
    stat_spec = pl.BlockSpec((a, nq * LANES), lambda i: (0, 0))
    return pl.pallas_call(
        _convt_kernel,
        grid=(c // cb,),
        in_specs=[pl.BlockSpec((a, nq, cb, LANES), lambda i: (0, 0, i, 0)),
                  pl.BlockSpec((cb, a, kl), lambda i: (i, 0, 0)),
                  pl.BlockSpec((cb, 1), lambda i: (i, 0))],
        out_specs=[pl.BlockSpec((a, cb, nq * LANES), lambda i: (0, i, 0)), stat_spec, stat_spec],
        out_shape=[jax.ShapeDtypeStruct((a, c, nq * LANES), BF16),
                   jax.ShapeDtypeStruct((a, nq * LANES), F32),
                   jax.ShapeDtypeStruct((a, nq * LANES), F32)],
        scratch_shapes=[pltpu.VMEM((a * (cb + CONV_SKEW), LANES), F32),
                        pltpu.VMEM((a * (cb + CONV_SKEW), LANES), F32)],
        compiler_params=_cparams(("arbitrary",)),
        name="convt",
    )(vt, t, bias.reshape(c, 1))


@functools.lru_cache(maxsize=None)
def _tap_selector(a):
    kl = -(-a // LANES) * LANES
    tap = np.arange(kl)[None, :] - np.arange(a)[:, None] + CONV_PAD
    hit = (tap[None] == np.arange(CONV_TAPS)[:, None, None]) & (np.arange(kl) < a)
    return hit.astype(BF16)


def _conv_toeplitz(w, a):
    return jnp.einsum("kc,kxy->cxy", w.astype(BF16), jnp.asarray(_tap_selector(a)),
                      preferred_element_type=BF16)


def _conv_branch_t(cv, g1, st, lng_ref, lnb_ref, n_chan):
    mean = (st[0:1] + st[2:3]) * (1.0 / n_chan)
    var = (st[1:2] + st[3:4]) * (1.0 / n_chan) - mean * mean
    y = (cv.astype(F32) - mean) * lax.rsqrt(var + EPS) * lng_ref[...] + lnb_ref[...]
    return (jax.nn.silu(y) * g1.astype(F32)).astype(BF16)


def _tail_w_kernel(y_ref, g2_ref, cv_ref, g1_ref, st_ref, wt_ref, bt_ref, wos_ref, woh_ref,
                   lng_ref, lnb_ref, o_ref, mt_sc, *, n_chan):
    tl = y_ref.shape[-1]
    st = st_ref[...]
    mix = []
    for k in range(2):
        lanes = slice(k * tl, (k + 1) * tl)
        y = y_ref[k]
        z = jnp.dot(wt_ref[...], y, preferred_element_type=F32) + bt_ref[...]
        s = (y.astype(F32) * jax.nn.sigmoid(z) * g2_ref[k].astype(F32)).astype(BF16)
        co = _conv_branch_t(cv_ref[k], g1_ref[:, lanes], st[:, lanes], lng_ref, lnb_ref, n_chan)
        mt_sc[...] = (jnp.dot(wos_ref[...], s, preferred_element_type=F32)
                      + jnp.dot(woh_ref[...], co, preferred_element_type=F32))
        mix.append(mt_sc[...].T.reshape(tl // BATCH, BATCH, mt_sc.shape[0]))
    o_ref[...] = jnp.concatenate(mix, axis=1).astype(o_ref.dtype)


def _tail_w(y1t, g2t, ct_h, g1t_h, stats_h, glu_wt, glu_bcol, wot, ln_g, ln_b, n_chan):
    _, sw, _ = y1t.shape
    _, hc, tl = ct_h.shape
    d = wot.shape[0]
    assert (2 * hc) % sw == 0
    r = tl // BATCH
    half_chunk = CHUNK // 2
    s5_spec = pl.BlockSpec((2, sw, tl), lambda j: (j % half_chunk, 0, j // half_chunk))
    tok = lambda rows: pl.BlockSpec((rows, 2 * tl), lambda j: (0, j))
    full = lambda arr: pl.BlockSpec(arr.shape, lambda j: (0,) * arr.ndim)
    return pl.pallas_call(
        functools.partial(_tail_w_kernel, n_chan=n_chan),
        grid=(GRID_W // 2,),
        in_specs=[s5_spec, s5_spec, pl.BlockSpec((2, hc, tl), lambda j: (j, 0, 0)), tok(hc),
                  tok(8), full(glu_wt), full(glu_bcol),
                  pl.BlockSpec((d, sw), lambda j: (0, 2 * hc // sw)),
                  pl.BlockSpec((d, hc), lambda j: (0, 0)), full(ln_g), full(ln_b)],
        out_specs=pl.BlockSpec((r, None, 2 * BATCH, d), lambda j: (0, j, 0, 0)),
        out_shape=jax.ShapeDtypeStruct((r, GRID_W // 2, 2 * BATCH, d), BF16),
        scratch_shapes=[pltpu.VMEM((d, tl), F32)],
        compiler_params=_cparams(("arbitrary",)),
        name="tail_w",
    )(y1t, g2t, ct_h, g1t_h, stats_h, glu_wt, glu_bcol, wot, wot, ln_g, ln_b)


FINAL_ROWS = 2


def _final_r_kernel(cv_ref, g1_ref, st_ref, mh_ref, x_ref, wo_ref, lng_ref, lnb_ref, gate_ref,
                    fg_ref, o_ref, mix_sc, *, n_chan):
    tl, d = x_ref.shape[1:]
    cv = jnp.concatenate([cv_ref[i] for i in range(cv_ref.shape[0])], axis=-1)
    co = _conv_branch_t(cv, g1_ref[...], st_ref[...], lng_ref, lnb_ref, n_chan)
    mix = lax.dot_general(co.T, wo_ref[...], (((1,), (1,)), ((), ())),
                          preferred_element_type=F32) + mh_ref[...].reshape(tl * BATCH, d)
    nlt = mix_sc.shape[0]
    for j in range(nlt):
        mix_sc[j] = mix[:, j * LANES:(j + 1) * LANES]
    for b in range(BATCH):
        mb = jnp.concatenate([mix_sc[j, pl.ds(b, tl, stride=BATCH), :] for j in range(nlt)],
                             axis=-1)
        xo = x_ref[b] + gate_ref[b:b + 1, :] * mb
        ms = jnp.mean(xo * xo, axis=-1, keepdims=True)
        o_ref[b] = xo * lax.rsqrt(ms + EPS) * fg_ref[...]


def _final_r(ct_v, g1t_v, stats_v, mix_h, x, wot, ln_g, ln_b, gate8, final_g, n_chan):
    r, hc, row_tokens = ct_v.shape
    _, length, d = x.shape
    rs = math.gcd(r, FINAL_ROWS)
    tm = rs * row_tokens
    tl = tm // BATCH
    tok = lambda rows: pl.BlockSpec((rows, tm), lambda i: (0, i))
    nat = pl.BlockSpec((BATCH, tl, d), lambda i: (0, i, 0))
    full = lambda arr: pl.BlockSpec(arr.shape, lambda i: (0,) * arr.ndim)
    return pl.pallas_call(
        functools.partial(_final_r_kernel, n_chan=n_chan),
        grid=(r // rs,),
        in_specs=[pl.BlockSpec((rs, hc, row_tokens), lambda i: (i, 0, 0)), tok(hc), tok(8),
                  pl.BlockSpec((rs, GRID_W // 2, 2 * BATCH, d), lambda i: (i, 0, 0, 0)), nat,
                  pl.BlockSpec((d, hc), lambda i: (0, 1)), full(ln_g), full(ln_b), full(gate8),
                  full(final_g)],
        out_specs=nat,
        out_shape=jax.ShapeDtypeStruct(x.shape, F32),
        scratch_shapes=[pltpu.VMEM((d // LANES, tm, LANES), F32)],
        compiler_params=_cparams(("arbitrary",)),
        name="final_r",
    )(ct_v, g1t_v, stats_v, mix_h, x, wot, ln_g, ln_b, gate8, final_g)


S5PREP_GROUPS = 4


def _s5prep_kernel(arow_ref, bt_ref, ct_ref, wall_ref, cot_ref, lam_ref, pw_sc, kk_sc):
    for gi in range(arow_ref.shape[0]):
        _s5prep_group(arow_ref.at[gi], bt_ref.at[gi], ct_ref.at[gi], wall_ref.at[gi], cot_ref.at[gi],
                      lam_ref.at[gi], pw_sc, kk_sc)


def _s5prep_group(arow_ref, bt_ref, ct_ref, wall_ref, cot_ref, lam_ref, pw_sc, kk_sc):
    kdim = CHUNK * SSM_GROUP
    hp = lax.Precision.HIGHEST
    lane = lax.broadcasted_iota(jnp.int32, (1, LANES), 1)
    is_re = lane < SSM_STATE
    col_t = lax.broadcasted_iota(jnp.int32, (1, kdim), 1) // SSM_GROUP
    kcol = lax.broadcasted_iota(jnp.int32, (pw_sc.shape[1], 1), 0).astype(F32)

    def tile_rows(a):
        return jnp.concatenate([a] * CHUNK, axis=0)

    def power_rows(exps):
        pr = jnp.concatenate([jnp.broadcast_to(pw_sc[0, e:e + 1, :], (SSM_GROUP, LANES))
                              for e in exps], axis=0)
        pi = jnp.concatenate([jnp.broadcast_to(pw_sc[1, e:e + 1, :], (SSM_GROUP, LANES))
                              for e in exps], axis=0)
        return pr, pi

    def cmul(xr, xi, yr, yi, im_sign):
        return jnp.where(is_re, xr * yr - xi * yi, im_sign * (xr * yi + xi * yr))

    toeplitz = jnp.zeros((kdim, kdim), F32)
    for d in range(2):
        ar, ai = arow_ref[d, 0:1, :], arow_ref[d, 1:2, :]
        dt = jnp.exp(arow_ref[d, 2:3, :])
        mag = jnp.exp(kcol * (ar * dt))
        ang = kcol * (ai * dt)
        pw_sc[0] = mag * jnp.cos(ang)
        pw_sc[1] = mag * jnp.sin(ang)
        lam_ref[d, 0] = pw_sc[0, CHUNK:CHUNK + 8, :]
        lam_ref[d, 1] = pw_sc[1, CHUNK:CHUNK + 8, :]
        nr, ni = pw_sc[0, 1:2, :] - 1.0, pw_sc[1, 1:2, :]
        den = ar * ar + ai * ai
        qr, qi = (nr * ar + ni * ai) / den, (ni * ar - nr * ai) / den
        btr, bti = bt_ref[d, 0], bt_ref[d, 1]
        bbr, bbi = tile_rows(qr * btr - qi * bti), tile_rows(qr * bti + qi * btr)
        cr, ci = tile_rows(ct_ref[d, 0]), tile_rows(ct_ref[d, 1])
        tt = list(range(CHUNK))
        e_lag = [CHUNK - 1 - t for t in tt] if d else tt
        e_in = tt if d else [CHUNK - 1 - t for t in tt]
        e_out = [CHUNK - t for t in tt] if d else [t + 1 for t in tt]
        cot_ref[:, kdim + d * LANES:kdim + (d + 1) * LANES] = cmul(
            cr, ci, *power_rows(e_out), -1.0).astype(BF16)
        wall_ref[d * LANES:(d + 1) * LANES, :] = cmul(bbr, bbi, *power_rows(e_in), 1.0).T.astype(BF16)
        c_lag = cmul(cr, ci, *power_rows(e_lag), -1.0)
        bb = jnp.where(is_re, bbr, bbi)
        kk = lax.dot_general(c_lag, bb, (((1,), (1,)), ((), ())), precision=hp,
                             preferred_element_type=F32)
        zeros = jnp.zeros((kdim, kdim), F32)
        if d:
            kk_sc[:kdim] = kk
            kk_sc[kdim:] = zeros
            starts = [SSM_GROUP * (CHUNK - 1 - t) for t in tt]
        else:
            kk_sc[:kdim] = zeros
            kk_sc[kdim:] = kk
            starts = [kdim - SSM_GROUP * t for t in tt]
        tiles = []
        for lt in range(kdim // LANES):
            cols = slice(lt * LANES, (lt + 1) * LANES)
            acc = jnp.zeros((kdim, LANES), F32)
            for t in range(lt * LANES // SSM_GROUP, (lt + 1) * LANES // SSM_GROUP):
                acc = jnp.where(col_t[:, cols] == t, kk_sc[starts[t]:starts[t] + kdim, cols], acc)
            tiles.append(acc)
        toeplitz = toeplitz + jnp.concatenate(tiles, axis=1)
    cot_ref[:, :kdim] = toeplitz.astype(BF16)


def _s5prep(arow, bt, ct):
    g = arow.shape[0]
    gs = math.gcd(g, S5PREP_GROUPS)
    kdim = CHUNK * SSM_GROUP
    return pl.pallas_call(
        _s5prep_kernel,
        grid=(g // gs,),
        in_specs=[pl.BlockSpec((gs, 2, 8, LANES), lambda i: (i, 0, 0, 0)),
                  pl.BlockSpec((gs, 2, 2, SSM_GROUP, LANES), lambda i: (i, 0, 0, 0, 0)),
                  pl.BlockSpec((gs, 2, 2, SSM_GROUP, LANES), lambda i: (i, 0, 0, 0, 0))],
        out_specs=[pl.BlockSpec((gs, kdim, kdim), lambda i: (i, 0, 0)),
                   pl.BlockSpec((gs, kdim, 2 * kdim), lambda i: (i, 0, 0)),
                   pl.BlockSpec((gs, 2, 2, 8, LANES), lambda i: (i, 0, 0, 0, 0))],
        out_shape=[jax.ShapeDtypeStruct((g, kdim, kdim), BF16),
                   jax.ShapeDtypeStruct((g, kdim, 2 * kdim), BF16),
                   jax.ShapeDtypeStruct((g, 2, 2, 8, LANES), F32)],
        scratch_shapes=[pltpu.VMEM((2, CHUNK + 8, LANES), F32),
                        pltpu.VMEM((2 * kdim, kdim), F32)],
        compiler_params=_cparams(("arbitrary",)),
        name="s5prep",
    )(arow, bt, ct)


def _s5prep_inputs(a_re, a_im, log_dt, b_re, b_im, c_re, c_im):
    g = a_re.shape[1]
    dup = lambda a: jnp.concatenate([a, a], axis=-1)
    rows = jnp.stack([dup(a_re), dup(a_im),
                      jnp.broadcast_to(log_dt[..., None], (2, g, LANES))], axis=2)
    arow = jnp.pad(rows, ((0, 0), (0, 0), (0, 5), (0, 0))).transpose(1, 0, 2, 3)
    bt = jnp.stack([dup(b_re.transpose(0, 1, 3, 2)), dup(b_im.transpose(0, 1, 3, 2))], axis=2)
    ct = jnp.stack([dup(c_re), dup(c_im)], axis=2)
    return arow, bt.transpose(1, 0, 2, 3, 4), ct.transpose(1, 0, 2, 3, 4)


def kernel(x, c, ctx, c_ctx, norm_g, w_ada, b_ada, w_in, conv_dw, conv_db, conv_ln_g, conv_ln_b,
           ssm_a_re, ssm_a_im, ssm_log_dt, ssm_b_re, ssm_b_im, ssm_c_re, ssm_c_im, ssm_d,
           ssm_glu_w, ssm_glu_b, w_out, final_g):
    bsz, length, d = x.shape
    ctx_len = ctx.shape[1]
    cw = conv_dw.shape[-1]
    sw = ssm_d.shape[-1]
    assert bsz == BATCH and norm_g.shape[0] == 1
    assert length % (GRID_W * CHUNK) == 0 and ctx_len % CHUNK == 0
    n = length * BATCH
    nch = length // CHUNK
    nchc = ctx_len // CHUNK
    rows = length // GRID_W

    cond16 = jnp.zeros((16, d), F32).at[:BATCH].set(c).at[BATCH].set(c_ctx)
    mod = _ada(cond16, w_ada.reshape(w_ada.shape[1:]), b_ada[0])
    shift, scale, gate = mod[:, :d], mod[:, d:2 * d], mod[:, 2 * d:]
    amp = norm_g[0][None] * (1.0 + scale)
    a8, s8, gate8 = amp[:BATCH], shift[:BATCH], gate[:BATCH]
    a8c = jnp.broadcast_to(amp[BATCH], (BATCH, d))
    s8c = jnp.broadcast_to(shift[BATCH], (BATCH, d))

    w = w_in.reshape(w_in.shape[1:])
    half = cw // 2
    wt_c = _transpose_cast(w, 0, 3 * cw, half)
    wt_s = _transpose_cast(w, 3 * cw, 2 * sw, half)
    wall, cot, lam_raw = _s5prep(*_s5prep_inputs(
        ssm_a_re[0], ssm_a_im[0], ssm_log_dt[0], ssm_b_re[0], ssm_b_im[0], ssm_c_re[0], ssm_c_im[0]))
    ng = sw // SSM_GROUP
    lam16 = lam_raw[:, :, :, 0, :SSM_STATE].reshape(ng // 2, 2, 2, 2, SSM_STATE).transpose(
        2, 0, 3, 1, 4).reshape(2, ng // 2, 2, LANES)
    dcol = jnp.broadcast_to(ssm_d[0].reshape(ng, 1, SSM_GROUP), (ng, CHUNK, SSM_GROUP)).reshape(
        ng, CHUNK * SSM_GROUP, 1)

    xt = x.transpose(1, 0, 2)
    ctxt = ctx.transpose(1, 0, 2)
    c4 = ctxt.reshape(nchc, CHUNK // 2, 2 * BATCH, d)

    x4 = xt.reshape(rows, GRID_W, BATCH, d)
    ut, g2t, vt_h, g1t_h = _inproj_w(x4, a8, s8, wt_s, wt_c, half)
    vt_v, g1t_v = _inproj_r(x4, a8, s8, wt_c, half)
    uct = _ctxproj(c4, a8c, s8c, wt_s)
    y1t = _s5(ut, uct, wall, cot, lam16, dcol)

    dw, db = conv_dw[0], conv_db[0]
    ct_h, s1_h, s2_h = _convt(vt_h, _conv_toeplitz(dw[:, :half], GRID_W), db[:half])
    ct_v, s1_v, s2_v = _convt(vt_v, _conv_toeplitz(dw[:, half:], rows), db[half:])
    to_v = lambda a: a.reshape(GRID_W, rows, BATCH).transpose(1, 0, 2).reshape(1, n)
    to_h = lambda a: a.reshape(rows, GRID_W, BATCH).transpose(1, 0, 2).reshape(1, n)
    pad4 = jnp.zeros((4, n), F32)
    stats_v = jnp.concatenate([s1_v.reshape(1, n), s2_v.reshape(1, n), to_v(s1_h), to_v(s2_h), pad4])
    stats_h = jnp.concatenate([s1_h.reshape(1, n), s2_h.reshape(1, n), to_h(s1_v), to_h(s2_v), pad4])

    ln_g, ln_b = conv_ln_g[0].reshape(cw, 1), conv_ln_b[0].reshape(cw, 1)
    wot = _transpose_cast(w_out.reshape(w_out.shape[1:]), 0, d, LANES)
    mix_h = _tail_w(y1t, g2t, ct_h, g1t_h, stats_h, _transpose_cast(ssm_glu_w[0], 0, sw, half),
                    ssm_glu_b[0].reshape(sw, 1), wot, ln_g[:half], ln_b[:half], cw)
    return _final_r(ct_v, g1t_v, stats_v, mix_h, x, wot, ln_g[half:],
                    ln_b[half:], gate8, final_g.reshape(1, d), cw)
```

```python
import functools
import math

import jax
import jax.numpy as jnp
import numpy as np
from jax import lax
from jax.experimental import pallas as pl
from jax.experimental.pallas import tpu as pltpu

GRID_W = 64
CONV_TAPS = 31
CONV_PAD = CONV_TAPS // 2
SSM_GROUP = 16
SSM_STATE = 64
CHUNK = 16
BATCH = 8
EPS = 1e-6
LANES = 128
MXU_WIDTH = 256
VMEM_LIMIT = 56 * 1024 * 1024

F32 = jnp.float32
BF16 = jnp.bfloat16


def _cparams(sem):
    return pltpu.CompilerParams(dimension_semantics=sem, vmem_limit_bytes=VMEM_LIMIT)


def _ada_kernel(c_ref, w_ref, b_ref, o_ref):
    s = jax.nn.silu(c_ref[...])
    o_ref[...] = jnp.dot(s, w_ref[...], preferred_element_type=F32,
                         precision=lax.Precision.HIGHEST) + b_ref[...]


def _ada(cond16, w_ada, b_ada):
    d, n3 = w_ada.shape
    return pl.pallas_call(
        _ada_kernel,
        grid=(n3 // d,),
        in_specs=[pl.BlockSpec((16, d), lambda j: (0, 0)),
                  pl.BlockSpec((d, d), lambda j: (0, j)),
                  pl.BlockSpec((1, d), lambda j: (0, j))],
        out_specs=pl.BlockSpec((16, d), lambda j: (0, j)),
        out_shape=jax.ShapeDtypeStruct((16, n3), F32),
        compiler_params=_cparams(("arbitrary",)),
        name="ada",
    )(cond16, w_ada, b_ada.reshape(1, n3))


def _modulated_norm(x, a, s):
    ms = jnp.mean(x * x, axis=-1, keepdims=True)
    return x * lax.rsqrt(ms + EPS) * a[None] + s[None]


def _lane_pieces(n_lanes):
    width = MXU_WIDTH if n_lanes % MXU_WIDTH == 0 else n_lanes
    return [(lo, width) for lo in range(0, n_lanes, width)]


def _conv_gates(q, hc, vt_ref, g1t_ref, lane0, gate_lane0=0):
    width = q.shape[1]
    val = (q[:hc] * jax.nn.sigmoid(q[hc:2 * hc])).astype(BF16)
    for j in range(width // LANES):
        vt_ref[lane0 // LANES + j] = val[:, j * LANES:(j + 1) * LANES]
    lo = gate_lane0 + lane0
    g1t_ref[:, lo:lo + width] = jax.nn.silu(q[2 * hc:]).astype(BF16)


def _transpose_cast_kernel(w_ref, o_ref):
    o_ref[...] = w_ref[...].T.astype(o_ref.dtype)


def _transpose_cast(w, col0, ncols, tile):
    k = w.shape[0]
    assert col0 % tile == 0 and ncols % tile == 0
    return pl.pallas_call(
        _transpose_cast_kernel,
        grid=(ncols // tile,),
        in_specs=[pl.BlockSpec((k, tile), lambda j: (0, j + col0 // tile))],
        out_specs=pl.BlockSpec((tile, k), lambda j: (j, 0)),
        out_shape=jax.ShapeDtypeStruct((ncols, k), BF16),
        compiler_params=_cparams(("arbitrary",)),
        name="wtrans",
    )(w)


def _project_t(w_refs, h):
    nt = (((1,), (1,)), ((), ()))
    return jnp.concatenate([lax.dot_general(w[...], h, nt, preferred_element_type=F32)
                            for w in w_refs], axis=0)


def _inproj_w_kernel(x_ref, a_ref, s_ref, ws_ref, wv_ref, wg_ref, wa_ref, ut_ref, g2t_ref, vt_ref,
                     g1t_ref):
    r, ncol, _, d = x_ref.shape
    sw = ut_ref.shape[1]
    hc = g1t_ref.shape[0]
    tl = r * BATCH
    for k in range(ncol):
        for lo, width in _lane_pieces(tl):
            x = x_ref[lo // BATCH:(lo + width) // BATCH, k]
            h = _modulated_norm(x, a_ref[...], s_ref[...]).reshape(width, d).astype(BF16)
            q = _project_t([ws_ref], h)
            ut_ref[k, :, lo:lo + width] = q[:sw].astype(BF16)
            g2t_ref[k, :, lo:lo + width] = jax.nn.silu(q[sw:]).astype(BF16)
            _conv_gates(_project_t([wv_ref, wg_ref, wa_ref], h), hc, vt_ref.at[k], g1t_ref,
                        lo, k * tl)


def _conv_weight_specs(hc, d, part, index_map):
    return [pl.BlockSpec((hc, d), functools.partial(index_map, 2 * k + part)) for k in range(3)]


INPROJ_COLS = 2


def _inproj_w(x4, a8, s8, wt_s, wt_c, hc):
    r, _, _, d = x4.shape
    sw = wt_s.shape[0] // 2
    tl = r * BATCH
    nq = tl // LANES
    cpr = GRID_W // CHUNK
    nc = INPROJ_COLS
    per_chunk = CHUNK // nc
    s5_spec = pl.BlockSpec((nc, sw, tl), lambda j: (j % per_chunk, 0, j // per_chunk))
    return pl.pallas_call(
        _inproj_w_kernel,
        grid=(GRID_W // nc,),
        in_specs=[pl.BlockSpec((r, nc, BATCH, d), lambda j: (0, j, 0, 0)),
                  pl.BlockSpec((BATCH, d), lambda j: (0, 0)),
                  pl.BlockSpec((BATCH, d), lambda j: (0, 0)),
                  pl.BlockSpec(wt_s.shape, lambda j: (0, 0)),
                  *_conv_weight_specs(hc, d, 0, lambda blk, j: (blk, 0))],
        out_specs=[s5_spec, s5_spec,
                   pl.BlockSpec((nc, nq, hc, LANES), lambda j: (j, 0, 0, 0)),
                   pl.BlockSpec((hc, nc * tl), lambda j: (0, j))],
        out_shape=[jax.ShapeDtypeStruct((CHUNK, sw, cpr * tl), BF16),
                   jax.ShapeDtypeStruct((CHUNK, sw, cpr * tl), BF16),
                   jax.ShapeDtypeStruct((GRID_W, nq, hc, LANES), BF16),
                   jax.ShapeDtypeStruct((hc, GRID_W * tl), BF16)],
        compiler_params=_cparams(("arbitrary",)),
        name="inproj_w",
    )(x4, a8, s8, wt_s, wt_c, wt_c, wt_c)


def _inproj_r_kernel(x_ref, a_ref, s_ref, wv_ref, wg_ref, wa_ref, vt_ref, g1t_ref):
    nrow, gw, _, d = x_ref.shape
    hc = g1t_ref.shape[0]
    tl = gw * BATCH
    for k in range(nrow):
        for lo, width in _lane_pieces(tl):
            x = x_ref[k, lo // BATCH:(lo + width) // BATCH]
            h = _modulated_norm(x, a_ref[...], s_ref[...]).reshape(width, d).astype(BF16)
            _conv_gates(_project_t([wv_ref, wg_ref, wa_ref], h), hc, vt_ref.at[k], g1t_ref,
                        lo, k * tl)


def _inproj_r(x4, a8, s8, wt_c, hc):
    r, gw, _, d = x4.shape
    tl = gw * BATCH
    nq = tl // LANES
    nr = math.gcd(r, INPROJ_COLS)
    return pl.pallas_call(
        _inproj_r_kernel,
        grid=(r // nr,),
        in_specs=[pl.BlockSpec((nr, gw, BATCH, d), lambda i: (i, 0, 0, 0)),
                  pl.BlockSpec((BATCH, d), lambda i: (0, 0)),
                  pl.BlockSpec((BATCH, d), lambda i: (0, 0)),
                  *_conv_weight_specs(hc, d, 1, lambda blk, i: (blk, 0))],
        out_specs=[pl.BlockSpec((nr, nq, hc, LANES), lambda i: (i, 0, 0, 0)),
                   pl.BlockSpec((hc, nr * tl), lambda i: (0, i))],
        out_shape=[jax.ShapeDtypeStruct((r, nq, hc, LANES), BF16),
                   jax.ShapeDtypeStruct((hc, r * tl), BF16)],
        compiler_params=_cparams(("arbitrary",)),
        name="inproj_r",
    )(x4, a8, s8, wt_c, wt_c, wt_c)


def _ctxproj_kernel(x_ref, a_ref, s_ref, wut_ref, ut_ref):
    cb, _, d = x_ref.shape
    x = x_ref[...].reshape(cb * 2, BATCH, d)
    h = _modulated_norm(x, a_ref[...], s_ref[...]).reshape(cb, 2 * BATCH, d)
    for tl in range(2):
        ht = h[:, tl * BATCH:(tl + 1) * BATCH, :].reshape(cb * BATCH, d).astype(BF16)
        q = lax.dot_general(wut_ref[...], ht, (((1,), (1,)), ((), ())),
                            preferred_element_type=F32)
        ut_ref[tl] = q.astype(BF16)


def _ctxproj(x4, a8, s8, wt_s):
    nch, _, _, d = x4.shape
    sw = wt_s.shape[0] // 2
    return pl.pallas_call(
        _ctxproj_kernel,
        grid=(CHUNK // 2,),
        in_specs=[pl.BlockSpec((nch, None, 2 * BATCH, d), lambda t: (0, t, 0, 0)),
                  pl.BlockSpec((BATCH, d), lambda t: (0, 0)),
                  pl.BlockSpec((BATCH, d), lambda t: (0, 0)),
                  pl.BlockSpec((sw, d), lambda t: (0, 0))],
        out_specs=pl.BlockSpec((2, sw, nch * BATCH), lambda t: (t, 0, 0)),
        out_shape=jax.ShapeDtypeStruct((CHUNK, sw, nch * BATCH), BF16),
        compiler_params=_cparams(("arbitrary",)),
        name="ctxproj",
    )(x4, a8, s8, wt_s)


GELU_C1 = math.sqrt(2.0 / math.pi)
GELU_C2 = GELU_C1 * 0.044715


def _gelu_tanh(x):
    hx = 0.5 * x
    return hx + hx * jnp.tanh(x * (GELU_C1 + GELU_C2 * (x * x)))


def _s5_kernel(xt_ref, xct_ref, wall_ref, cot_ref, lam_ref, dcol_ref, y_ref,
               s_sc, h_sc, *, lane_chunk):
    nl = xt_ref.shape[-1]
    nlc = xct_ref.shape[-1]
    n_ch = nl // BATCH
    n_chc = nlc // BATCH
    kdim = CHUNK * SSM_GROUP
    p = SSM_STATE

    def group_x(ref, j):
        return ref[:, j * SSM_GROUP:(j + 1) * SSM_GROUP, :].reshape(kdim, ref.shape[-1])

    st_c = [jnp.dot(wall_ref[j], group_x(xct_ref, j), preferred_element_type=F32)
            for j in range(2)]
    st = [jnp.dot(wall_ref[j], group_x(xt_ref, j), preferred_element_type=F32) for j in range(2)]
    for d in range(2):
        for src, off, width in ((st_c, 0, nlc), (st, nlc, nl)):
            for ri in range(2):
                lo = (2 * d + ri) * p
                pair = jnp.concatenate([src[0][lo:lo + p], src[1][lo:lo + p]], axis=0)
                s_sc[d, ri, off:off + width, :] = pair.T

    lam = [[jnp.broadcast_to(lam_ref[d, ri:ri + 1, :], (BATCH, LANES)) for ri in range(2)]
           for d in range(2)]

    def advance(d, state, row):
        hr, hi = state
        ar, ai = lam[d]
        sr = s_sc[d, 0, pl.ds(row, BATCH), :]
        si = s_sc[d, 1, pl.ds(row, BATCH), :]
        return ar * hr - ai * hi + sr, ar * hi + ai * hr + si

    zero = jnp.zeros((BATCH, LANES), F32)

    f = b = (zero, zero)
    for i in range(n_chc):
        f = advance(0, f, i * BATCH)
        b = advance(1, b, (n_chc - 1 - i) * BATCH)

    cpr = GRID_W // CHUNK
    n_rows = n_ch // cpr

    def lane_row(ch):
        return ((ch % cpr) * n_rows + ch // cpr) * BATCH

    for i in range(n_ch):
        rf = lane_row(i)
        rb = lane_row(n_ch - 1 - i)
        h_sc[0, 0, rf:rf + BATCH, :] = f[0]
        h_sc[0, 1, rf:rf + BATCH, :] = f[1]
        h_sc[1, 0, rb:rb + BATCH, :] = b[0]
        h_sc[1, 1, rb:rb + BATCH, :] = b[1]
        f = advance(0, f, nlc + rf)
        b = advance(1, b, nlc + rb)

    for c0 in range(0, nl, lane_chunk):
        cs = slice(c0, c0 + lane_chunk)
        ht = [[h_sc[d, ri, cs, :].T for ri in range(2)] for d in range(2)]
        for j in range(2):
            x = xt_ref[:, j * SSM_GROUP:(j + 1) * SSM_GROUP, cs].reshape(kdim, lane_chunk)
            hj = jnp.concatenate([ht[d][ri][j * p:(j + 1) * p] for d in range(2) for ri in range(2)],
                                 axis=0).astype(BF16)
            acc = (dcol_ref[j] * x.astype(F32)
                   + jnp.dot(cot_ref[j], jnp.concatenate([x, hj], axis=0),
                             preferred_element_type=F32))
            y = _gelu_tanh(acc).astype(BF16)
            y_ref[:, j * SSM_GROUP:(j + 1) * SSM_GROUP, cs] = y.reshape(CHUNK, SSM_GROUP, lane_chunk)


def _s5(ut, uct, wall, cot, lam16, dcol):
    _, sw, nl = ut.shape
    nlc = uct.shape[-1]
    gp = sw // (2 * SSM_GROUP)
    kdim = CHUNK * SSM_GROUP
    lane_chunk = min(nl, 512)
    kern = functools.partial(_s5_kernel, lane_chunk=lane_chunk)
    return pl.pallas_call(
        kern,
        grid=(gp,),
        in_specs=[pl.BlockSpec((CHUNK, 2 * SSM_GROUP, nl), lambda g: (0, g, 0)),
                  pl.BlockSpec((CHUNK, 2 * SSM_GROUP, nlc), lambda g: (0, g, 0)),
                  pl.BlockSpec((2, kdim, kdim), lambda g: (g, 0, 0)),
                  pl.BlockSpec((2, kdim, 2 * kdim), lambda g: (g, 0, 0)),
                  pl.BlockSpec((2, None, 2, LANES), lambda g: (0, g, 0, 0)),
                  pl.BlockSpec((2, kdim, 1), lambda g: (g, 0, 0))],
        out_specs=pl.BlockSpec((CHUNK, 2 * SSM_GROUP, nl), lambda g: (0, g, 0)),
        out_shape=jax.ShapeDtypeStruct((CHUNK, sw, nl), BF16),
        scratch_shapes=[pltpu.VMEM((2, 2, nlc + nl, LANES), F32),
                        pltpu.VMEM((2, 2, nl, LANES), F32)],
        compiler_params=_cparams(("arbitrary",)),
        name="s5",
    )(ut, uct, wall, cot, lam16, dcol)


CONV_CB = 32
CONV_SKEW = 8


def _convt_kernel(v_ref, t_ref, b_ref, o_ref, s1_ref, s2_ref, z_sc, o_sc):
    a, nq, cb, _ = v_ref.shape
    kl = t_ref.shape[-1]
    pitch = cb + CONV_SKEW
    k_pad = jnp.zeros((kl - a, LANES), BF16)
    for q in range(nq):
        lanes = slice(q * LANES, (q + 1) * LANES)
        z = v_ref[:, q].astype(F32)
        for i in range(a):
            z_sc[i * pitch:i * pitch + cb, :] = z[i]
        s1 = jnp.zeros((a, LANES), F32)
        s2 = jnp.zeros((a, LANES), F32)
        for c in range(cb):
            col = z_sc[pl.ds(c, a, stride=pitch), :].astype(BF16)
            rhs = jnp.concatenate([col, k_pad], axis=0) if kl > a else col
            out = jnp.dot(t_ref[c], rhs, preferred_element_type=F32) + b_ref[c:c + 1]
            o_sc[pl.ds(c, a, stride=pitch), :] = out
            s1 = s1 + out
            s2 = s2 + out * out
        for i in range(a):
            o_ref[i, :, lanes] = o_sc[i * pitch:i * pitch + cb, :].astype(o_ref.dtype)

        @pl.when(pl.program_id(0) == 0)
        def _():
            s1_ref[:, lanes] = jnp.zeros((a, LANES), F32)
            s2_ref[:, lanes] = jnp.zeros((a, LANES), F32)

        s1_ref[:, lanes] += s1
        s2_ref[:, lanes] += s2


def _convt(vt, t, bias):
    a, nq, c, _ = vt.shape
    cb = min(c, CONV_CB)
    kl = t.shape[-1]
    stat_spec = pl.BlockSpec((a, nq * LANES), lambda i: (0, 0))
    return pl.pallas_call(
        _convt_kernel,
        grid=(c // cb,),
        in_specs=[pl.BlockSpec((a, nq, cb, LANES), lambda i: (0, 0, i, 0)),
                  pl.BlockSpec((cb, a, kl), lambda i: (i, 0, 0)),
                  pl.BlockSpec((cb, 1), lambda i: (i, 0))],
        out_specs=[pl.BlockSpec((a, cb, nq * LANES), lambda i: (0, i, 0)), stat_spec, stat_spec],
        out_shape=[jax.ShapeDtypeStruct((a, c, nq * LANES), BF16),
                   jax.ShapeDtypeStruct((a, nq * LANES), F32),
                   jax.ShapeDtypeStruct((a, nq * LANES), F32)],
        scratch_shapes=[pltpu.VMEM((a * (cb + CONV_SKEW), LANES), F32),
                        pltpu.VMEM((a * (cb + CONV_SKEW), LANES), F32)],
        compiler_params=_cparams(("arbitrary",)),
        name="convt",
    )(vt, t, bias.reshape(c, 1))


@functools.lru_cache(maxsize=None)
def _tap_selector(a):
    kl = -(-a // LANES) * LANES
    tap = np.arange(kl)[None, :] - np.arange(a)[:, None] + CONV_PAD
    hit = (tap[None] == np.arange(CONV_TAPS)[:, None, None]) & (np.arange(kl) < a)
    return hit.astype(BF16)


def _conv_toeplitz(w, a):
    return jnp.einsum("kc,kxy->cxy", w.astype(BF16), jnp.asarray(_tap_selector(a)),
                      preferred_element_type=BF16)


def _conv_branch_t(cv, g1, st, lng_ref, lnb_ref, n_chan):
    mean = (st[0:1] + st[2:3]) * (1.0 / n_chan)
    var = (st[1:2] + st[3:4]) * (1.0 / n_chan) - mean * mean
    y = (cv.astype(F32) - mean) * lax.rsqrt(var + EPS) * lng_ref[...] + lnb_ref[...]
    return (jax.nn.silu(y) * g1.astype(F32)).astype(BF16)


def _tail_w_kernel(y_ref, g2_ref, cv_ref, g1_ref, st_ref, wt_ref, bt_ref, wos_ref, woh_ref,
                   lng_ref, lnb_ref, o_ref, mt_sc, *, n_chan):
    tl = y_ref.shape[-1]
    st = st_ref[...]
    mix = []
    for k in range(2):
        lanes = slice(k * tl, (k + 1) * tl)
        y = y_ref[k]
        z = jnp.dot(wt_ref[...], y, preferred_element_type=F32) + bt_ref[...]
        s = (y.astype(F32) * jax.nn.sigmoid(z) * g2_ref[k].astype(F32)).astype(BF16)
        co = _conv_branch_t(cv_ref[k], g1_ref[:, lanes], st[:, lanes], lng_ref, lnb_ref, n_chan)
        mt_sc[...] = (jnp.dot(wos_ref[...], s, preferred_element_type=F32)
                      + jnp.dot(woh_ref[...], co, preferred_element_type=F32))
        mix.append(mt_sc[...].T.reshape(tl // BATCH, BATCH, mt_sc.shape[0]))
    o_ref[...] = jnp.concatenate(mix, axis=1).astype(o_ref.dtype)


def _tail_w(y1t, g2t, ct_h, g1t_h, stats_h, glu_wt, glu_bcol, wot, ln_g, ln_b, n_chan):
    _, sw, _ = y1t.shape
    _, hc, tl = ct_h.shape
    d = wot.shape[0]
    assert (2 * hc) % sw == 0
    r = tl // BATCH
    half_chunk = CHUNK // 2
    s5_spec = pl.BlockSpec((2, sw, tl), lambda j: (j % half_chunk, 0, j // half_chunk))
    tok = lambda rows: pl.BlockSpec((rows, 2 * tl), lambda j: (0, j))
    full = lambda arr: pl.BlockSpec(arr.shape, lambda j: (0,) * arr.ndim)
    return pl.pallas_call(
        functools.partial(_tail_w_kernel, n_chan=n_chan),
        grid=(GRID_W // 2,),
        in_specs=[s5_spec, s5_spec, pl.BlockSpec((2, hc, tl), lambda j: (j, 0, 0)), tok(hc),
                  tok(8), full(glu_wt), full(glu_bcol),
                  pl.BlockSpec((d, sw), lambda j: (0, 2 * hc // sw)),
                  pl.BlockSpec((d, hc), lambda j: (0, 0)), full(ln_g), full(ln_b)],
        out_specs=pl.BlockSpec((r, None, 2 * BATCH, d), lambda j: (0, j, 0, 0)),
        out_shape=jax.ShapeDtypeStruct((r, GRID_W // 2, 2 * BATCH, d), BF16),
        scratch_shapes=[pltpu.VMEM((d, tl), F32)],
        compiler_params=_cparams(("arbitrary",)),
        name="tail_w",
    )(y1t, g2t, ct_h, g1t_h, stats_h, glu_wt, glu_bcol, wot, wot, ln_g, ln_b)


FINAL_ROWS = 2


def _final_r_kernel(cv_ref, g1_ref, st_ref, mh_ref, x_ref, wo_ref, lng_ref, lnb_ref, gate_ref,
                    fg_ref, o_ref, mix_sc, *, n_chan):
    tl, d = x_ref.shape[1:]
    cv = jnp.concatenate([cv_ref[i] for i in range(cv_ref.shape[0])], axis=-1)
    co = _conv_branch_t(cv, g1_ref[...], st_ref[...], lng_ref, lnb_ref, n_chan)
    mix = lax.dot_general(co.T, wo_ref[...], (((1,), (1,)), ((), ())),
                          preferred_element_type=F32) + mh_ref[...].reshape(tl * BATCH, d)
    nlt = mix_sc.shape[0]
    for j in range(nlt):
        mix_sc[j] = mix[:, j * LANES:(j + 1) * LANES]
    for b in range(BATCH):
        mb = jnp.concatenate([mix_sc[j, pl.ds(b, tl, stride=BATCH), :] for j in range(nlt)],
                             axis=-1)
        xo = x_ref[b] + gate_ref[b:b + 1, :] * mb
        ms = jnp.mean(xo * xo, axis=-1, keepdims=True)
        o_ref[b] = xo * lax.rsqrt(ms + EPS) * fg_ref[...]


def _final_r(ct_v, g1t_v, stats_v, mix_h, x, wot, ln_g, ln_b, gate8, final_g, n_chan):
    r, hc, row_tokens = ct_v.shape
    _, length, d = x.shape
    rs = math.gcd(r, FINAL_ROWS)
    tm = rs * row_tokens
    tl = tm // BATCH
    tok = lambda rows: pl.BlockSpec((rows, tm), lambda i: (0, i))
    nat = pl.BlockSpec((BATCH, tl, d), lambda i: (0, i, 0))
    full = lambda arr: pl.BlockSpec(arr.shape, lambda i: (0,) * arr.ndim)
    return pl.pallas_call(
        functools.partial(_final_r_kernel, n_chan=n_chan),
        grid=(r // rs,),
        in_specs=[pl.BlockSpec((rs, hc, row_tokens), lambda i: (i, 0, 0)), tok(hc), tok(8),
                  pl.BlockSpec((rs, GRID_W // 2, 2 * BATCH, d), lambda i: (i, 0, 0, 0)), nat,
                  pl.BlockSpec((d, hc), lambda i: (0, 1)), full(ln_g), full(ln_b), full(gate8),
                  full(final_g)],
        out_specs=nat,
        out_shape=jax.ShapeDtypeStruct(x.shape, F32),
        scratch_shapes=[pltpu.VMEM((d // LANES, tm, LANES), F32)],
        compiler_params=_cparams(("arbitrary",)),
        name="final_r",
    )(ct_v, g1t_v, stats_v, mix_h, x, wot, ln_g, ln_b, gate8, final_g)


S5PREP_GROUPS = 4


def _s5prep_kernel(arow_ref, bt_ref, ct_ref, wall_ref, cot_ref, lam_ref, pw_sc, kk_sc):
    for gi in range(arow_ref.shape[0]):
        _s5prep_group(arow_ref.at[gi], bt_ref.at[gi], ct_ref.at[gi], wall_ref.at[gi], cot_ref.at[gi],
                      lam_ref.at[gi], pw_sc, kk_sc)


def _s5prep_group(arow_ref, bt_ref, ct_ref, wall_ref, cot_ref, lam_ref, pw_sc, kk_sc):
    kdim = CHUNK * SSM_GROUP
    hp = lax.Precision.HIGHEST
    lane = lax.broadcasted_iota(jnp.int32, (1, LANES), 1)
    is_re = lane < SSM_STATE
    col_t = lax.broadcasted_iota(jnp.int32, (1, kdim), 1) // SSM_GROUP
    kcol = lax.broadcasted_iota(jnp.int32, (pw_sc.shape[1], 1), 0).astype(F32)

    def tile_rows(a):
        return jnp.concatenate([a] * CHUNK, axis=0)

    def power_rows(exps):
        pr = jnp.concatenate([jnp.broadcast_to(pw_sc[0, e:e + 1, :], (SSM_GROUP, LANES))
                              for e in exps], axis=0)
        pi = jnp.concatenate([jnp.broadcast_to(pw_sc[1, e:e + 1, :], (SSM_GROUP, LANES))
                              for e in exps], axis=0)
        return pr, pi

    def cmul(xr, xi, yr, yi, im_sign):
        return jnp.where(is_re, xr * yr - xi * yi, im_sign * (xr * yi + xi * yr))

    toeplitz = jnp.zeros((kdim, kdim), F32)
    for d in range(2):
        ar, ai = arow_ref[d, 0:1, :], arow_ref[d, 1:2, :]
        dt = jnp.exp(arow_ref[d, 2:3, :])
        mag = jnp.exp(kcol * (ar * dt))
        ang = kcol * (ai * dt)
        pw_sc[0] = mag * jnp.cos(ang)
        pw_sc[1] = mag * jnp.sin(ang)
        lam_ref[d, 0] = pw_sc[0, CHUNK:CHUNK + 8, :]
        lam_ref[d, 1] = pw_sc[1, CHUNK:CHUNK + 8, :]
        nr, ni = pw_sc[0, 1:2, :] - 1.0, pw_sc[1, 1:2, :]
        den = ar * ar + ai * ai
        qr, qi = (nr * ar + ni * ai) / den, (ni * ar - nr * ai) / den
        btr, bti = bt_ref[d, 0], bt_ref[d, 1]
        bbr, bbi = tile_rows(qr * btr - qi * bti), tile_rows(qr * bti + qi * btr)
        cr, ci = tile_rows(ct_ref[d, 0]), tile_rows(ct_ref[d, 1])
        tt = list(range(CHUNK))
        e_lag = [CHUNK - 1 - t for t in tt] if d else tt
        e_in = tt if d else [CHUNK - 1 - t for t in tt]
        e_out = [CHUNK - t for t in tt] if d else [t + 1 for t in tt]
        cot_ref[:, kdim + d * LANES:kdim + (d + 1) * LANES] = cmul(
            cr, ci, *power_rows(e_out), -1.0).astype(BF16)
        wall_ref[d * LANES:(d + 1) * LANES, :] = cmul(bbr, bbi, *power_rows(e_in), 1.0).T.astype(BF16)
        c_lag = cmul(cr, ci, *power_rows(e_lag), -1.0)
        bb = jnp.where(is_re, bbr, bbi)
        kk = lax.dot_general(c_lag, bb, (((1,), (1,)), ((), ())), precision=hp,
                             preferred_element_type=F32)
        zeros = jnp.zeros((kdim, kdim), F32)
        if d:
            kk_sc[:kdim] = kk
            kk_sc[kdim:] = zeros
            starts = [SSM_GROUP * (CHUNK - 1 - t) for t in tt]
        else:
            kk_sc[:kdim] = zeros
            kk_sc[kdim:] = kk
            starts = [kdim - SSM_GROUP * t for t in tt]
        tiles = []
        for lt in range(kdim // LANES):
            cols = slice(lt * LANES, (lt + 1) * LANES)
            acc = jnp.zeros((kdim, LANES), F32)
            for t in range(lt * LANES // SSM_GROUP, (lt + 1) * LANES // SSM_GROUP):
                acc = jnp.where(col_t[:, cols] == t, kk_sc[starts[t]:starts[t] + kdim, cols], acc)
            tiles.append(acc)
        toeplitz = toeplitz + jnp.concatenate(tiles, axis=1)
    cot_ref[:, :kdim] = toeplitz.astype(BF16)


def _s5prep(arow, bt, ct):
    g = arow.shape[0]
    gs = math.gcd(g, S5PREP_GROUPS)
    kdim = CHUNK * SSM_GROUP
    return pl.pallas_call(
        _s5prep_kernel,
        grid=(g // gs,),
        in_specs=[pl.BlockSpec((gs, 2, 8, LANES), lambda i: (i, 0, 0, 0)),
                  pl.BlockSpec((gs, 2, 2, SSM_GROUP, LANES), lambda i: (i, 0, 0, 0, 0)),
                  pl.BlockSpec((gs, 2, 2, SSM_GROUP, LANES), lambda i: (i, 0, 0, 0, 0))],
        out_specs=[pl.BlockSpec((gs, kdim, kdim), lambda i: (i, 0, 0)),
                   pl.BlockSpec((gs, kdim, 2 * kdim), lambda i: (i, 0, 0)),
                   pl.BlockSpec((gs, 2, 2, 8, LANES), lambda i: (i, 0, 0, 0, 0))],
        out_shape=[jax.ShapeDtypeStruct((g, kdim, kdim), BF16),
                   jax.ShapeDtypeStruct((g, kdim, 2 * kdim), BF16),
                   jax.ShapeDtypeStruct((g, 2, 2, 8, LANES), F32)],
        scratch_shapes=[pltpu.VMEM((2, CHUNK + 8, LANES), F32),
                        pltpu.VMEM((2 * kdim, kdim), F32)],
        compiler_params=_cparams(("arbitrary",)),
        name="s5prep",
    )(arow, bt, ct)


def _s5prep_inputs(a_re, a_im, log_dt, b_re, b_im, c_re, c_im):
    g = a_re.shape[1]
    dup = lambda a: jnp.concatenate([a, a], axis=-1)
    rows = jnp.stack([dup(a_re), dup(a_im),
                      jnp.broadcast_to(log_dt[..., None], (2, g, LANES))], axis=2)
    arow = jnp.pad(rows, ((0, 0), (0, 0), (0, 5), (0, 0))).transpose(1, 0, 2, 3)
    bt = jnp.stack([dup(b_re.transpose(0, 1, 3, 2)), dup(b_im.transpose(0, 1, 3, 2))], axis=2)
    ct = jnp.stack([dup(c_re), dup(c_im)], axis=2)
    return arow, bt.transpose(1, 0, 2, 3, 4), ct.transpose(1, 0, 2, 3, 4)


def kernel(x, c, ctx, c_ctx, norm_g, w_ada, b_ada, w_in, conv_dw, conv_db, conv_ln_g, conv_ln_b,
           ssm_a_re, ssm_a_im, ssm_log_dt, ssm_b_re, ssm_b_im, ssm_c_re, ssm_c_im, ssm_d,
           ssm_glu_w, ssm_glu_b, w_out, final_g):
    bsz, length, d = x.shape
    ctx_len = ctx.shape[1]
    cw = conv_dw.shape[-1]
    sw = ssm_d.shape[-1]
    assert bsz == BATCH and norm_g.shape[0] == 1
    assert length % (GRID_W * CHUNK) == 0 and ctx_len % CHUNK == 0
    n = length * BATCH
    nch = length // CHUNK
    nchc = ctx_len // CHUNK
    rows = length // GRID_W

    cond16 = jnp.zeros((16, d), F32).at[:BATCH].set(c).at[BATCH].set(c_ctx)
    mod = _ada(cond16, w_ada.reshape(w_ada.shape[1:]), b_ada[0])
    shift, scale, gate = mod[:, :d], mod[:, d:2 * d], mod[:, 2 * d:]
    amp = norm_g[0][None] * (1.0 + scale)
    a8, s8, gate8 = amp[:BATCH], shift[:BATCH], gate[:BATCH]
    a8c = jnp.broadcast_to(amp[BATCH], (BATCH, d))
    s8c = jnp.broadcast_to(shift[BATCH], (BATCH, d))

    w = w_in.reshape(w_in.shape[1:])
    half = cw // 2
    wt_c = _transpose_cast(w, 0, 3 * cw, cw)
    wt_s = _transpose_cast(w, 3 * cw, 2 * sw, sw)
    wall, cot, lam_raw = _s5prep(*_s5prep_inputs(
        ssm_a_re[0], ssm_a_im[0], ssm_log_dt[0], ssm_b_re[0], ssm_b_im[0], ssm_c_re[0], ssm_c_im[0]))
    ng = sw // SSM_GROUP
    lam16 = lam_raw[:, :, :, 0, :SSM_STATE].reshape(ng // 2, 2, 2, 2, SSM_STATE).transpose(
        2, 0, 3, 1, 4).reshape(2, ng // 2, 2, LANES)
    dcol = jnp.broadcast_to(ssm_d[0].reshape(ng, 1, SSM_GROUP), (ng, CHUNK, SSM_GROUP)).reshape(
        ng, CHUNK * SSM_GROUP, 1)

    xt = x.transpose(1, 0, 2)
    ctxt = ctx.transpose(1, 0, 2)
    c4 = ctxt.reshape(nchc, CHUNK // 2, 2 * BATCH, d)

    x4 = xt.reshape(rows, GRID_W, BATCH, d)
    ut, g2t, vt_h, g1t_h = _inproj_w(x4, a8, s8, wt_s, wt_c, half)
    vt_v, g1t_v = _inproj_r(x4, a8, s8, wt_c, half)
    uct = _ctxproj(c4, a8c, s8c, wt_s)
    y1t = _s5(ut, uct, wall, cot, lam16, dcol)

    dw, db = conv_dw[0], conv_db[0]
    ct_h, s1_h, s2_h = _convt(vt_h, _conv_toeplitz(dw[:, :half], GRID_W), db[:half])
    ct_v, s1_v, s2_v = _convt(vt_v, _conv_toeplitz(dw[:, half:], rows), db[half:])
    to_v = lambda a: a.reshape(GRID_W, rows, BATCH).transpose(1, 0, 2).reshape(1, n)
    to_h = lambda a: a.reshape(rows, GRID_W, BATCH).transpose(1, 0, 2).reshape(1, n)
    pad4 = jnp.zeros((4, n), F32)
    stats_v = jnp.concatenate([s1_v.reshape(1, n), s2_v.reshape(1, n), to_v(s1_h), to_v(s2_h), pad4])
    stats_h = jnp.concatenate([s1_h.reshape(1, n), s2_h.reshape(1, n), to_h(s1_v), to_h(s2_v), pad4])

    ln_g, ln_b = conv_ln_g[0].reshape(cw, 1), conv_ln_b[0].reshape(cw, 1)
    wot = _transpose_cast(w_out.reshape(w_out.shape[1:]), 0, d, half)
    mix_h = _tail_w(y1t, g2t, ct_h, g1t_h, stats_h, _transpose_cast(ssm_glu_w[0], 0, sw, sw),
                    ssm_glu_b[0].reshape(sw, 1), wot, ln_g[:half], ln_b[:half], cw)
    return _final_r(ct_v, g1t_v, stats_v, mix_h, x, wot, ln_g[half:],
                    ln_b[half:], gate8, final_g.reshape(1, d), cw)
```

```python
import functools
import math

import jax
import jax.numpy as jnp
import numpy as np
from jax import lax
from jax.experimental import pallas as pl
from jax.experimental.pallas import tpu as pltpu

GRID_W = 64
CONV_TAPS = 31
CONV_PAD = CONV_TAPS // 2
SSM_GROUP = 16
SSM_STATE = 64
CHUNK = 16
BATCH = 8
EPS = 1e-6
LANES = 128
MXU_WIDTH = 256
VMEM_LIMIT = 56 * 1024 * 1024

F32 = jnp.float32
BF16 = jnp.bfloat16


def _cparams(sem):
    return pltpu.CompilerParams(dimension_semantics=sem, vmem_limit_bytes=VMEM_LIMIT)


def _ada_kernel(c_ref, w_ref, b_ref, o_ref):
    s = jax.nn.silu(c_ref[...])
    o_ref[...] = jnp.dot(s, w_ref[...], preferred_element_type=F32,
                         precision=lax.Precision.HIGHEST) + b_ref[...]


def _ada(cond16, w_ada, b_ada):
    d, n3 = w_ada.shape
    return pl.pallas_call(
        _ada_kernel,
        grid=(n3 // d,),
        in_specs=[pl.BlockSpec((16, d), lambda j: (0, 0)),
                  pl.BlockSpec((d, d), lambda j: (0, j)),
                  pl.BlockSpec((1, d), lambda j: (0, j))],
        out_specs=pl.BlockSpec((16, d), lambda j: (0, j)),
        out_shape=jax.ShapeDtypeStruct((16, n3), F32),
        compiler_params=_cparams(("arbitrary",)),
        name="ada",
    )(cond16, w_ada, b_ada.reshape(1, n3))


def _modulated_norm(x, g_ref, shift_ref, scale_ref):
    ms = jnp.mean(x * x, axis=-1, keepdims=True)
    amp = g_ref[...] * (1.0 + scale_ref[...])
    return x * lax.rsqrt(ms + EPS) * amp[None] + shift_ref[...][None]


def _mod_specs(d, row_block):
    const = lambda blk: (lambda *_: blk)
    return [pl.BlockSpec((1, d), const((0, 0))),
            pl.BlockSpec((BATCH, d), const((row_block, 0))),
            pl.BlockSpec((BATCH, d), const((row_block, 1)))]


def _lane_pieces(n_lanes):
    width = MXU_WIDTH if n_lanes % MXU_WIDTH == 0 else n_lanes
    return [(lo, width) for lo in range(0, n_lanes, width)]


def _conv_gates(q, hc, vt_ref, g1t_ref, lane0, gate_lane0=0):
    width = q.shape[1]
    val = (q[:hc] * jax.nn.sigmoid(q[hc:2 * hc])).astype(BF16)
    for j in range(width // LANES):
        vt_ref[lane0 // LANES + j] = val[:, j * LANES:(j + 1) * LANES]
    lo = gate_lane0 + lane0
    g1t_ref[:, lo:lo + width] = jax.nn.silu(q[2 * hc:]).astype(BF16)


def _transpose_cast_kernel(w_ref, o_ref):
    o_ref[...] = w_ref[...].T.astype(o_ref.dtype)


def _transpose_cast(w, col0, ncols, tile):
    k = w.shape[0]
    assert col0 % tile == 0 and ncols % tile == 0
    return pl.pallas_call(
        _transpose_cast_kernel,
        grid=(ncols // tile,),
        in_specs=[pl.BlockSpec((k, tile), lambda j: (0, j + col0 // tile))],
        out_specs=pl.BlockSpec((tile, k), lambda j: (j, 0)),
        out_shape=jax.ShapeDtypeStruct((ncols, k), BF16),
        compiler_params=_cparams(("arbitrary",)),
        name="wtrans",
    )(w)


def _project_t(w_refs, h):
    nt = (((1,), (1,)), ((), ()))
    return jnp.concatenate([lax.dot_general(w[...], h, nt, preferred_element_type=F32)
                            for w in w_refs], axis=0)


def _inproj_w_kernel(x_ref, g_ref, sh_ref, sc_ref, ws_ref, wv_ref, wg_ref, wa_ref, ut_ref, g2t_ref,
                     vt_ref, g1t_ref):
    r, ncol, _, d = x_ref.shape
    sw = ut_ref.shape[1]
    hc = g1t_ref.shape[0]
    tl = r * BATCH
    for k in range(ncol):
        for lo, width in _lane_pieces(tl):
            x = x_ref[lo // BATCH:(lo + width) // BATCH, k]
            h = _modulated_norm(x, g_ref, sh_ref, sc_ref).reshape(width, d).astype(BF16)
            q = _project_t([ws_ref], h)
            ut_ref[k, :, lo:lo + width] = q[:sw].astype(BF16)
            g2t_ref[k, :, lo:lo + width] = jax.nn.silu(q[sw:]).astype(BF16)
            _conv_gates(_project_t([wv_ref, wg_ref, wa_ref], h), hc, vt_ref.at[k], g1t_ref,
                        lo, k * tl)


def _conv_weight_specs(hc, d, part, index_map):
    return [pl.BlockSpec((hc, d), functools.partial(index_map, 2 * k + part)) for k in range(3)]


INPROJ_COLS = 2


def _inproj_w(x4, norm_g, mod, wt_s, wt_c, hc):
    r, _, _, d = x4.shape
    sw = wt_s.shape[0] // 2
    tl = r * BATCH
    nq = tl // LANES
    cpr = GRID_W // CHUNK
    nc = INPROJ_COLS
    per_chunk = CHUNK // nc
    s5_spec = pl.BlockSpec((nc, sw, tl), lambda j: (j % per_chunk, 0, j // per_chunk))
    return pl.pallas_call(
        _inproj_w_kernel,
        grid=(GRID_W // nc,),
        in_specs=[pl.BlockSpec((r, nc, BATCH, d), lambda j: (0, j, 0, 0)),
                  *_mod_specs(d, 0),
                  pl.BlockSpec(wt_s.shape, lambda j: (0, 0)),
                  *_conv_weight_specs(hc, d, 0, lambda blk, j: (blk, 0))],
        out_specs=[s5_spec, s5_spec,
                   pl.BlockSpec((nc, nq, hc, LANES), lambda j: (j, 0, 0, 0)),
                   pl.BlockSpec((hc, nc * tl), lambda j: (0, j))],
        out_shape=[jax.ShapeDtypeStruct((CHUNK, sw, cpr * tl), BF16),
                   jax.ShapeDtypeStruct((CHUNK, sw, cpr * tl), BF16),
                   jax.ShapeDtypeStruct((GRID_W, nq, hc, LANES), BF16),
                   jax.ShapeDtypeStruct((hc, GRID_W * tl), BF16)],
        compiler_params=_cparams(("arbitrary",)),
        name="inproj_w",
    )(x4, norm_g, mod, mod, wt_s, wt_c, wt_c, wt_c)


def _inproj_r_kernel(x_ref, g_ref, sh_ref, sc_ref, wv_ref, wg_ref, wa_ref, vt_ref, g1t_ref):
    nrow, gw, _, d = x_ref.shape
    hc = g1t_ref.shape[0]
    tl = gw * BATCH
    for k in range(nrow):
        for lo, width in _lane_pieces(tl):
            x = x_ref[k, lo // BATCH:(lo + width) // BATCH]
            h = _modulated_norm(x, g_ref, sh_ref, sc_ref).reshape(width, d).astype(BF16)
            _conv_gates(_project_t([wv_ref, wg_ref, wa_ref], h), hc, vt_ref.at[k], g1t_ref,
                        lo, k * tl)


def _inproj_r(x4, norm_g, mod, wt_c, hc):
    r, gw, _, d = x4.shape
    tl = gw * BATCH
    nq = tl // LANES
    nr = math.gcd(r, INPROJ_COLS)
    return pl.pallas_call(
        _inproj_r_kernel,
        grid=(r // nr,),
        in_specs=[pl.BlockSpec((nr, gw, BATCH, d), lambda i: (i, 0, 0, 0)),
                  *_mod_specs(d, 0),
                  *_conv_weight_specs(hc, d, 1, lambda blk, i: (blk, 0))],
        out_specs=[pl.BlockSpec((nr, nq, hc, LANES), lambda i: (i, 0, 0, 0)),
                   pl.BlockSpec((hc, nr * tl), lambda i: (0, i))],
        out_shape=[jax.ShapeDtypeStruct((r, nq, hc, LANES), BF16),
                   jax.ShapeDtypeStruct((hc, r * tl), BF16)],
        compiler_params=_cparams(("arbitrary",)),
        name="inproj_r",
    )(x4, norm_g, mod, mod, wt_c, wt_c, wt_c)


def _ctxproj_kernel(x_ref, g_ref, sh_ref, sc_ref, wut_ref, ut_ref):
    cb, _, d = x_ref.shape
    x = x_ref[...].reshape(cb * 2, BATCH, d)
    h = _modulated_norm(x, g_ref, sh_ref, sc_ref).reshape(cb, 2 * BATCH, d)
    for tl in range(2):
        ht = h[:, tl * BATCH:(tl + 1) * BATCH, :].reshape(cb * BATCH, d).astype(BF16)
        q = lax.dot_general(wut_ref[...], ht, (((1,), (1,)), ((), ())),
                            preferred_element_type=F32)
        ut_ref[tl] = q.astype(BF16)


def _ctxproj(x4, norm_g, mod, wt_s):
    nch, _, _, d = x4.shape
    sw = wt_s.shape[0] // 2
    return pl.pallas_call(
        _ctxproj_kernel,
        grid=(CHUNK // 2,),
        in_specs=[pl.BlockSpec((nch, None, 2 * BATCH, d), lambda t: (0, t, 0, 0)),
                  *_mod_specs(d, 1),
                  pl.BlockSpec((sw, d), lambda t: (0, 0))],
        out_specs=pl.BlockSpec((2, sw, nch * BATCH), lambda t: (t, 0, 0)),
        out_shape=jax.ShapeDtypeStruct((CHUNK, sw, nch * BATCH), BF16),
        compiler_params=_cparams(("arbitrary",)),
        name="ctxproj",
    )(x4, norm_g, mod, mod, wt_s)


GELU_C1 = math.sqrt(2.0 / math.pi)
GELU_C2 = GELU_C1 * 0.044715


def _gelu_tanh(x):
    hx = 0.5 * x
    return hx + hx * jnp.tanh(x * (GELU_C1 + GELU_C2 * (x * x)))


def _s5_kernel(xt_ref, xct_ref, wall_ref, cot_ref, lam_ref, dcol_ref, y_ref,
               s_sc, h_sc, *, lane_chunk):
    nl = xt_ref.shape[-1]
    nlc = xct_ref.shape[-1]
    n_ch = nl // BATCH
    n_chc = nlc // BATCH
    kdim = CHUNK * SSM_GROUP
    p = SSM_STATE

    def group_x(ref, j):
        return ref[:, j * SSM_GROUP:(j + 1) * SSM_GROUP, :].reshape(kdim, ref.shape[-1])

    st_c = [jnp.dot(wall_ref[j], group_x(xct_ref, j), preferred_element_type=F32)
            for j in range(2)]
    st = [jnp.dot(wall_ref[j], group_x(xt_ref, j), preferred_element_type=F32) for j in range(2)]
    for d in range(2):
        for src, off, width in ((st_c, 0, nlc), (st, nlc, nl)):
            for ri in range(2):
                lo = (2 * d + ri) * p
                pair = jnp.concatenate([src[0][lo:lo + p], src[1][lo:lo + p]], axis=0)
                s_sc[d, ri, off:off + width, :] = pair.T

    lam = [[jnp.broadcast_to(lam_ref[d, ri:ri + 1, :], (BATCH, LANES)) for ri in range(2)]
           for d in range(2)]

    def advance(d, state, row):
        hr, hi = state
        ar, ai = lam[d]
        sr = s_sc[d, 0, pl.ds(row, BATCH), :]
        si = s_sc[d, 1, pl.ds(row, BATCH), :]
        return ar * hr - ai * hi + sr, ar * hi + ai * hr + si

    zero = jnp.zeros((BATCH, LANES), F32)

    f = b = (zero, zero)
    for i in range(n_chc):
        f = advance(0, f, i * BATCH)
        b = advance(1, b, (n_chc - 1 - i) * BATCH)

    cpr = GRID_W // CHUNK
    n_rows = n_ch // cpr

    def lane_row(ch):
        return ((ch % cpr) * n_rows + ch // cpr) * BATCH

    for i in range(n_ch):
        rf = lane_row(i)
        rb = lane_row(n_ch - 1 - i)
        h_sc[0, 0, rf:rf + BATCH, :] = f[0]
        h_sc[0, 1, rf:rf + BATCH, :] = f[1]
        h_sc[1, 0, rb:rb + BATCH, :] = b[0]
        h_sc[1, 1, rb:rb + BATCH, :] = b[1]
        f = advance(0, f, nlc + rf)
        b = advance(1, b, nlc + rb)

    for c0 in range(0, nl, lane_chunk):
        cs = slice(c0, c0 + lane_chunk)
        ht = [[h_sc[d, ri, cs, :].T for ri in range(2)] for d in range(2)]
        for j in range(2):
            x = xt_ref[:, j * SSM_GROUP:(j + 1) * SSM_GROUP, cs].reshape(kdim, lane_chunk)
            hj = jnp.concatenate([ht[d][ri][j * p:(j + 1) * p] for d in range(2) for ri in range(2)],
                                 axis=0).astype(BF16)
            acc = (dcol_ref[j] * x.astype(F32)
                   + jnp.dot(cot_ref[j], jnp.concatenate([x, hj], axis=0),
                             preferred_element_type=F32))
            y = _gelu_tanh(acc).astype(BF16)
            y_ref[:, j * SSM_GROUP:(j + 1) * SSM_GROUP, cs] = y.reshape(CHUNK, SSM_GROUP, lane_chunk)


def _s5(ut, uct, wall, cot, lam16, dcol):
    _, sw, nl = ut.shape
    nlc = uct.shape[-1]
    gp = sw // (2 * SSM_GROUP)
    kdim = CHUNK * SSM_GROUP
    lane_chunk = min(nl, 512)
    kern = functools.partial(_s5_kernel, lane_chunk=lane_chunk)
    return pl.pallas_call(
        kern,
        grid=(gp,),
        in_specs=[pl.BlockSpec((CHUNK, 2 * SSM_GROUP, nl), lambda g: (0, g, 0)),
                  pl.BlockSpec((CHUNK, 2 * SSM_GROUP, nlc), lambda g: (0, g, 0)),
                  pl.BlockSpec((2, kdim, kdim), lambda g: (g, 0, 0)),
                  pl.BlockSpec((2, kdim, 2 * kdim), lambda g: (g, 0, 0)),
                  pl.BlockSpec((2, None, 2, LANES), lambda g: (0, g, 0, 0)),
                  pl.BlockSpec((2, kdim, 1), lambda g: (g, 0, 0))],
        out_specs=pl.BlockSpec((CHUNK, 2 * SSM_GROUP, nl), lambda g: (0, g, 0)),
        out_shape=jax.ShapeDtypeStruct((CHUNK, sw, nl), BF16),
        scratch_shapes=[pltpu.VMEM((2, 2, nlc + nl, LANES), F32),
                        pltpu.VMEM((2, 2, nl, LANES), F32)],
        compiler_params=_cparams(("arbitrary",)),
        name="s5",
    )(ut, uct, wall, cot, lam16, dcol)


CONV_CB = 32
CONV_SKEW = 8


def _convt_kernel(v_ref, t_ref, b_ref, o_ref, s1_ref, s2_ref, z_sc, o_sc):
    a, nq, cb, _ = v_ref.shape
    kl = t_ref.shape[-1]
    pitch = cb + CONV_SKEW
    k_pad = jnp.zeros((kl - a, LANES), BF16)
    for q in range(nq):
        lanes = slice(q * LANES, (q + 1) * LANES)
        z = v_ref[:, q].astype(F32)
        for i in range(a):
            z_sc[i * pitch:i * pitch + cb, :] = z[i]
        s1 = jnp.zeros((a, LANES), F32)
        s2 = jnp.zeros((a, LANES), F32)
        for c in range(cb):
            col = z_sc[pl.ds(c, a, stride=pitch), :].astype(BF16)
            rhs = jnp.concatenate([col, k_pad], axis=0) if kl > a else col
            out = jnp.dot(t_ref[c], rhs, preferred_element_type=F32) + b_ref[c:c + 1]
            o_sc[pl.ds(c, a, stride=pitch), :] = out
            s1 = s1 + out
            s2 = s2 + out * out
        for i in range(a):
            o_ref[i, :, lanes] = o_sc[i * pitch:i * pitch + cb, :].astype(o_ref.dtype)

        @pl.when(pl.program_id(0) == 0)
        def _():
            s1_ref[:, lanes] = jnp.zeros((a, LANES), F32)
            s2_ref[:, lanes] = jnp.zeros((a, LANES), F32)

        s1_ref[:, lanes] += s1
        s2_ref[:, lanes] += s2


def _convt(vt, t, bias):
    a, nq, c, _ = vt.shape
    cb = min(c, CONV_CB)
    kl = t.shape[-1]
    stat_spec = pl.BlockSpec((a, nq * LANES), lambda i: (0, 0))
    return pl.pallas_call(
        _convt_kernel,
        grid=(c // cb,),
        in_specs=[pl.BlockSpec((a, nq, cb, LANES), lambda i: (0, 0, i, 0)),
                  pl.BlockSpec((cb, a, kl), lambda i: (i, 0, 0)),
                  pl.BlockSpec((cb, 1), lambda i: (i, 0))],
        out_specs=[pl.BlockSpec((a, cb, nq * LANES), lambda i: (0, i, 0)), stat_spec, stat_spec],
        out_shape=[jax.ShapeDtypeStruct((a, c, nq * LANES), BF16),
                   jax.ShapeDtypeStruct((a, nq * LANES), F32),
                   jax.ShapeDtypeStruct((a, nq * LANES), F32)],
        scratch_shapes=[pltpu.VMEM((a * (cb + CONV_SKEW), LANES), F32),
                        pltpu.VMEM((a * (cb + CONV_SKEW), LANES), F32)],
        compiler_params=_cparams(("arbitrary",)),
        name="convt",
    )(vt, t, bias.reshape(c, 1))


@functools.lru_cache(maxsize=None)
def _tap_selector(a):
    kl = -(-a // LANES) * LANES
    tap = np.arange(kl)[None, :] - np.arange(a)[:, None] + CONV_PAD
    hit = (tap[None] == np.arange(CONV_TAPS)[:, None, None]) & (np.arange(kl) < a)
    return hit.astype(BF16)


def _conv_toeplitz(w, a):
    return jnp.einsum("kc,kxy->cxy", w.astype(BF16), jnp.asarray(_tap_selector(a)),
                      preferred_element_type=BF16)


def _conv_branch_t(cv, g1, st, lng_ref, lnb_ref, n_chan):
    mean = (st[0:1] + st[2:3]) * (1.0 / n_chan)
    var = (st[1:2] + st[3:4]) * (1.0 / n_chan) - mean * mean
    y = (cv.astype(F32) - mean) * lax.rsqrt(var + EPS) * lng_ref[...] + lnb_ref[...]
    return (jax.nn.silu(y) * g1.astype(F32)).astype(BF16)


def _tail_w_kernel(y_ref, g2_ref, cv_ref, g1_ref, st_ref, wt_ref, bt_ref, wos_ref, woh_ref,
                   lng_ref, lnb_ref, o_ref, mt_sc, *, n_chan):
    tl = y_ref.shape[-1]
    st = st_ref[...]
    mix = []
    for k in range(2):
        lanes = slice(k * tl, (k + 1) * tl)
        y = y_ref[k]
        z = jnp.dot(wt_ref[...], y, preferred_element_type=F32) + bt_ref[...]
        s = (y.astype(F32) * jax.nn.sigmoid(z) * g2_ref[k].astype(F32)).astype(BF16)
        co = _conv_branch_t(cv_ref[k], g1_ref[:, lanes], st[:, lanes], lng_ref, lnb_ref, n_chan)
        mt_sc[...] = (jnp.dot(wos_ref[...], s, preferred_element_type=F32)
                      + jnp.dot(woh_ref[...], co, preferred_element_type=F32))
        mix.append(mt_sc[...].T.reshape(tl // BATCH, BATCH, mt_sc.shape[0]))
    o_ref[...] = jnp.concatenate(mix, axis=1).astype(o_ref.dtype)


def _tail_w(y1t, g2t, ct_h, g1t_h, stats_h, glu_wt, glu_bcol, wot, ln_g, ln_b, n_chan):
    _, sw, _ = y1t.shape
    _, hc, tl = ct_h.shape
    d = wot.shape[0]
    assert (2 * hc) % sw == 0
    r = tl // BATCH
    half_chunk = CHUNK // 2
    s5_spec = pl.BlockSpec((2, sw, tl), lambda j: (j % half_chunk, 0, j // half_chunk))
    tok = lambda rows: pl.BlockSpec((rows, 2 * tl), lambda j: (0, j))
    full = lambda arr: pl.BlockSpec(arr.shape, lambda j: (0,) * arr.ndim)
    return pl.pallas_call(
        functools.partial(_tail_w_kernel, n_chan=n_chan),
        grid=(GRID_W // 2,),
        in_specs=[s5_spec, s5_spec, pl.BlockSpec((2, hc, tl), lambda j: (j, 0, 0)), tok(hc),
                  tok(8), full(glu_wt), full(glu_bcol),
                  pl.BlockSpec((d, sw), lambda j: (0, 2 * hc // sw)),
                  pl.BlockSpec((d, hc), lambda j: (0, 0)), full(ln_g), full(ln_b)],
        out_specs=pl.BlockSpec((r, None, 2 * BATCH, d), lambda j: (0, j, 0, 0)),
        out_shape=jax.ShapeDtypeStruct((r, GRID_W // 2, 2 * BATCH, d), BF16),
        scratch_shapes=[pltpu.VMEM((d, tl), F32)],
        compiler_params=_cparams(("arbitrary",)),
        name="tail_w",
    )(y1t, g2t, ct_h, g1t_h, stats_h, glu_wt, glu_bcol, wot, wot, ln_g, ln_b)


FINAL_ROWS = 2


def _final_r_kernel(cv_ref, g1_ref, st_ref, mh_ref, x_ref, wo_ref, lng_ref, lnb_ref, gate_ref,
                    fg_ref, o_ref, mix_sc, *, n_chan):
    tl, d = x_ref.shape[1:]
    cv = jnp.concatenate([cv_ref[i] for i in range(cv_ref.shape[0])], axis=-1)
    co = _conv_branch_t(cv, g1_ref[...], st_ref[...], lng_ref, lnb_ref, n_chan)
    mix = lax.dot_general(co.T, wo_ref[...], (((1,), (1,)), ((), ())),
                          preferred_element_type=F32) + mh_ref[...].reshape(tl * BATCH, d)
    nlt = mix_sc.shape[0]
    for j in range(nlt):
        mix_sc[j] = mix[:, j * LANES:(j + 1) * LANES]
    for b in range(BATCH):
        mb = jnp.concatenate([mix_sc[j, pl.ds(b, tl, stride=BATCH), :] for j in range(nlt)],
                             axis=-1)
        xo = x_ref[b] + gate_ref[b:b + 1, :] * mb
        ms = jnp.mean(xo * xo, axis=-1, keepdims=True)
        o_ref[b] = xo * lax.rsqrt(ms + EPS) * fg_ref[...]


def _final_r(ct_v, g1t_v, stats_v, mix_h, x, wot, ln_g, ln_b, mod, final_g, n_chan):
    r, hc, row_tokens = ct_v.shape
    _, length, d = x.shape
    rs = math.gcd(r, FINAL_ROWS)
    tm = rs * row_tokens
    tl = tm // BATCH
    tok = lambda rows: pl.BlockSpec((rows, tm), lambda i: (0, i))
    nat = pl.BlockSpec((BATCH, tl, d), lambda i: (0, i, 0))
    full = lambda arr: pl.BlockSpec(arr.shape, lambda i: (0,) * arr.ndim)
    return pl.pallas_call(
        functools.partial(_final_r_kernel, n_chan=n_chan),
        grid=(r // rs,),
        in_specs=[pl.BlockSpec((rs, hc, row_tokens), lambda i: (i, 0, 0)), tok(hc), tok(8),
                  pl.BlockSpec((rs, GRID_W // 2, 2 * BATCH, d), lambda i: (i, 0, 0, 0)), nat,
                  pl.BlockSpec((d, hc), lambda i: (0, 1)), full(ln_g), full(ln_b),
                  pl.BlockSpec((BATCH, d), lambda i: (0, 2)), full(final_g)],
        out_specs=nat,
        out_shape=jax.ShapeDtypeStruct(x.shape, F32),
        scratch_shapes=[pltpu.VMEM((d // LANES, tm, LANES), F32)],
        compiler_params=_cparams(("arbitrary",)),
        name="final_r",
    )(ct_v, g1t_v, stats_v, mix_h, x, wot, ln_g, ln_b, mod, final_g)


S5PREP_GROUPS = 4


def _s5prep_kernel(arow_ref, bt_ref, ct_ref, wall_ref, cot_ref, lam_ref, pw_sc, kk_sc):
    for gi in range(arow_ref.shape[0]):
        _s5prep_group(arow_ref.at[gi], bt_ref.at[gi], ct_ref.at[gi], wall_ref.at[gi], cot_ref.at[gi],
                      lam_ref.at[gi], pw_sc, kk_sc)


def _s5prep_group(arow_ref, bt_ref, ct_ref, wall_ref, cot_ref, lam_ref, pw_sc, kk_sc):
    kdim = CHUNK * SSM_GROUP
    hp = lax.Precision.HIGHEST
    lane = lax.broadcasted_iota(jnp.int32, (1, LANES), 1)
    is_re = lane < SSM_STATE
    col_t = lax.broadcasted_iota(jnp.int32, (1, kdim), 1) // SSM_GROUP
    kcol = lax.broadcasted_iota(jnp.int32, (pw_sc.shape[1], 1), 0).astype(F32)

    def tile_rows(a):
        return jnp.concatenate([a] * CHUNK, axis=0)

    def power_rows(exps):
        pr = jnp.concatenate([jnp.broadcast_to(pw_sc[0, e:e + 1, :], (SSM_GROUP, LANES))
                              for e in exps], axis=0)
        pi = jnp.concatenate([jnp.broadcast_to(pw_sc[1, e:e + 1, :], (SSM_GROUP, LANES))
                              for e in exps], axis=0)
        return pr, pi

    def cmul(xr, xi, yr, yi, im_sign):
        return jnp.where(is_re, xr * yr - xi * yi, im_sign * (xr * yi + xi * yr))

    toeplitz = jnp.zeros((kdim, kdim), F32)
    for d in range(2):
        ar, ai = arow_ref[d, 0:1, :], arow_ref[d, 1:2, :]
        dt = jnp.exp(arow_ref[d, 2:3, :])
        mag = jnp.exp(kcol * (ar * dt))
        ang = kcol * (ai * dt)
        pw_sc[0] = mag * jnp.cos(ang)
        pw_sc[1] = mag * jnp.sin(ang)
        lam_ref[d, 0] = pw_sc[0, CHUNK:CHUNK + 8, :]
        lam_ref[d, 1] = pw_sc[1, CHUNK:CHUNK + 8, :]
        nr, ni = pw_sc[0, 1:2, :] - 1.0, pw_sc[1, 1:2, :]
        den = ar * ar + ai * ai
        qr, qi = (nr * ar + ni * ai) / den, (ni * ar - nr * ai) / den
        btr, bti = bt_ref[d, 0], bt_ref[d, 1]
        bbr, bbi = tile_rows(qr * btr - qi * bti), tile_rows(qr * bti + qi * btr)
        cr, ci = tile_rows(ct_ref[d, 0]), tile_rows(ct_ref[d, 1])
        tt = list(range(CHUNK))
        e_lag = [CHUNK - 1 - t for t in tt] if d else tt
        e_in = tt if d else [CHUNK - 1 - t for t in tt]
        e_out = [CHUNK - t for t in tt] if d else [t + 1 for t in tt]
        cot_ref[:, kdim + d * LANES:kdim + (d + 1) * LANES] = cmul(
            cr, ci, *power_rows(e_out), -1.0).astype(BF16)
        wall_ref[d * LANES:(d + 1) * LANES, :] = cmul(bbr, bbi, *power_rows(e_in), 1.0).T.astype(BF16)
        c_lag = cmul(cr, ci, *power_rows(e_lag), -1.0)
        bb = jnp.where(is_re, bbr, bbi)
        kk = lax.dot_general(c_lag, bb, (((1,), (1,)), ((), ())), precision=hp,
                             preferred_element_type=F32)
        zeros = jnp.zeros((kdim, kdim), F32)
        if d:
            kk_sc[:kdim] = kk
            kk_sc[kdim:] = zeros
            starts = [SSM_GROUP * (CHUNK - 1 - t) for t in tt]
        else:
            kk_sc[:kdim] = zeros
            kk_sc[kdim:] = kk
            starts = [kdim - SSM_GROUP * t for t in tt]
        tiles = []
        for lt in range(kdim // LANES):
            cols = slice(lt * LANES, (lt + 1) * LANES)
            acc = jnp.zeros((kdim, LANES), F32)
            for t in range(lt * LANES // SSM_GROUP, (lt + 1) * LANES // SSM_GROUP):
                acc = jnp.where(col_t[:, cols] == t, kk_sc[starts[t]:starts[t] + kdim, cols], acc)
            tiles.append(acc)
        toeplitz = toeplitz + jnp.concatenate(tiles, axis=1)
    cot_ref[:, :kdim] = toeplitz.astype(BF16)


def _s5prep(arow, bt, ct):
    g = arow.shape[0]
    gs = math.gcd(g, S5PREP_GROUPS)
    kdim = CHUNK * SSM_GROUP
    return pl.pallas_call(
        _s5prep_kernel,
        grid=(g // gs,),
        in_specs=[pl.BlockSpec((gs, 2, 8, LANES), lambda i: (i, 0, 0, 0)),
                  pl.BlockSpec((gs, 2, 2, SSM_GROUP, LANES), lambda i: (i, 0, 0, 0, 0)),
                  pl.BlockSpec((gs, 2, 2, SSM_GROUP, LANES), lambda i: (i, 0, 0, 0, 0))],
        out_specs=[pl.BlockSpec((gs, kdim, kdim), lambda i: (i, 0, 0)),
                   pl.BlockSpec((gs, kdim, 2 * kdim), lambda i: (i, 0, 0)),
                   pl.BlockSpec((gs, 2, 2, 8, LANES), lambda i: (i, 0, 0, 0, 0))],
        out_shape=[jax.ShapeDtypeStruct((g, kdim, kdim), BF16),
                   jax.ShapeDtypeStruct((g, kdim, 2 * kdim), BF16),
                   jax.ShapeDtypeStruct((g, 2, 2, 8, LANES), F32)],
        scratch_shapes=[pltpu.VMEM((2, CHUNK + 8, LANES), F32),
                        pltpu.VMEM((2 * kdim, kdim), F32)],
        compiler_params=_cparams(("arbitrary",)),
        name="s5prep",
    )(arow, bt, ct)


def _s5prep_inputs(a_re, a_im, log_dt, b_re, b_im, c_re, c_im):
    g = a_re.shape[1]
    dup = lambda a: jnp.concatenate([a, a], axis=-1)
    rows = jnp.stack([dup(a_re), dup(a_im),
                      jnp.broadcast_to(log_dt[..., None], (2, g, LANES))], axis=2)
    arow = jnp.pad(rows, ((0, 0), (0, 0), (0, 5), (0, 0))).transpose(1, 0, 2, 3)
    bt = jnp.stack([dup(b_re.transpose(0, 1, 3, 2)), dup(b_im.transpose(0, 1, 3, 2))], axis=2)
    ct = jnp.stack([dup(c_re), dup(c_im)], axis=2)
    return arow, bt.transpose(1, 0, 2, 3, 4), ct.transpose(1, 0, 2, 3, 4)


def kernel(x, c, ctx, c_ctx, norm_g, w_ada, b_ada, w_in, conv_dw, conv_db, conv_ln_g, conv_ln_b,
           ssm_a_re, ssm_a_im, ssm_log_dt, ssm_b_re, ssm_b_im, ssm_c_re, ssm_c_im, ssm_d,
           ssm_glu_w, ssm_glu_b, w_out, final_g):
    bsz, length, d = x.shape
    ctx_len = ctx.shape[1]
    cw = conv_dw.shape[-1]
    sw = ssm_d.shape[-1]
    assert bsz == BATCH and norm_g.shape[0] == 1
    assert length % (GRID_W * CHUNK) == 0 and ctx_len % CHUNK == 0
    n = length * BATCH
    nch = length // CHUNK
    nchc = ctx_len // CHUNK
    rows = length // GRID_W

    cond16 = jnp.concatenate([c, jnp.broadcast_to(c_ctx, (BATCH, d))])
    mod = _ada(cond16, w_ada.reshape(w_ada.shape[1:]), b_ada[0])

    w = w_in.reshape(w_in.shape[1:])
    half = cw // 2
    wt_c = _transpose_cast(w, 0, 3 * cw, cw)
    wt_s = _transpose_cast(w, 3 * cw, 2 * sw, sw)
    wall, cot, lam_raw = _s5prep(*_s5prep_inputs(
        ssm_a_re[0], ssm_a_im[0], ssm_log_dt[0], ssm_b_re[0], ssm_b_im[0], ssm_c_re[0], ssm_c_im[0]))
    ng = sw // SSM_GROUP
    lam16 = lam_raw[:, :, :, 0, :SSM_STATE].reshape(ng // 2, 2, 2, 2, SSM_STATE).transpose(
        2, 0, 3, 1, 4).reshape(2, ng // 2, 2, LANES)
    dcol = jnp.broadcast_to(ssm_d[0].reshape(ng, 1, SSM_GROUP), (ng, CHUNK, SSM_GROUP)).reshape(
        ng, CHUNK * SSM_GROUP, 1)

    xt = x.transpose(1, 0, 2)
    ctxt = ctx.transpose(1, 0, 2)
    c4 = ctxt.reshape(nchc, CHUNK // 2, 2 * BATCH, d)

    x4 = xt.reshape(rows, GRID_W, BATCH, d)
    ut, g2t, vt_h, g1t_h = _inproj_w(x4, norm_g, mod, wt_s, wt_c, half)
    vt_v, g1t_v = _inproj_r(x4, norm_g, mod, wt_c, half)
    uct = _ctxproj(c4, norm_g, mod, wt_s)
    y1t = _s5(ut, uct, wall, cot, lam16, dcol)

    dw, db = conv_dw[0], conv_db[0]
    ct_h, s1_h, s2_h = _convt(vt_h, _conv_toeplitz(dw[:, :half], GRID_W), db[:half])
    ct_v, s1_v, s2_v = _convt(vt_v, _conv_toeplitz(dw[:, half:], rows), db[half:])
    to_v = lambda a: a.reshape(GRID_W, rows, BATCH).transpose(1, 0, 2).reshape(1, n)
    to_h = lambda a: a.reshape(rows, GRID_W, BATCH).transpose(1, 0, 2).reshape(1, n)
    pad4 = jnp.zeros((4, n), F32)
    stats_v = jnp.concatenate([s1_v.reshape(1, n), s2_v.reshape(1, n), to_v(s1_h), to_v(s2_h), pad4])
    stats_h = jnp.concatenate([s1_h.reshape(1, n), s2_h.reshape(1, n), to_h(s1_v), to_h(s2_v), pad4])

    ln_g, ln_b = conv_ln_g[0].reshape(cw, 1), conv_ln_b[0].reshape(cw, 1)
    wot = _transpose_cast(w_out.reshape(w_out.shape[1:]), 0, d, half)
    mix_h = _tail_w(y1t, g2t, ct_h, g1t_h, stats_h, _transpose_cast(ssm_glu_w[0], 0, sw, sw),
                    ssm_glu_b[0].reshape(sw, 1), wot, ln_g[:half], ln_b[:half], cw)
    return _final_r(ct_v, g1t_v, stats_v, mix_h, x, wot, ln_g[half:],
                    ln_b[half:], mod, final_g.reshape(1, d), cw)
```

```python
import functools
import math

import jax
import jax.numpy as jnp
import numpy as np
from jax import lax
from jax.experimental import pallas as pl
from jax.experimental.pallas import tpu as pltpu

GRID_W = 64
CONV_TAPS = 31
CONV_PAD = CONV_TAPS // 2
SSM_GROUP = 16
SSM_STATE = 64
CHUNK = 16
BATCH = 8
EPS = 1e-6
LANES = 128
MXU_WIDTH = 256
VMEM_LIMIT = 56 * 1024 * 1024

F32 = jnp.float32
BF16 = jnp.bfloat16


def _cparams(sem):
    return pltpu.CompilerParams(dimension_semantics=sem, vmem_limit_bytes=VMEM_LIMIT)


def _ada_kernel(c_ref, w_ref, b_ref, o_ref):
    s = jax.nn.silu(c_ref[...])
    o_ref[...] = jnp.dot(s, w_ref[...], preferred_element_type=F32,
                         precision=lax.Precision.HIGHEST) + b_ref[...]


def _ada(cond16, w_ada, b_ada):
    d, n3 = w_ada.shape
    return pl.pallas_call(
        _ada_kernel,
        grid=(n3 // d,),
        in_specs=[pl.BlockSpec((16, d), lambda j: (0, 0)),
                  pl.BlockSpec((d, d), lambda j: (0, j)),
                  pl.BlockSpec((1, d), lambda j: (0, j))],
        out_specs=pl.BlockSpec((16, d), lambda j: (0, j)),
        out_shape=jax.ShapeDtypeStruct((16, n3), F32),
        compiler_params=_cparams(("arbitrary",)),
        name="ada",
    )(cond16, w_ada, b_ada.reshape(1, n3))


def _modulated_norm(x, g_ref, shift_ref, scale_ref):
    ms = jnp.mean(x * x, axis=-1, keepdims=True)
    amp = g_ref[...] * (1.0 + scale_ref[...])
    return x * lax.rsqrt(ms + EPS) * amp[None] + shift_ref[...][None]


def _mod_specs(d, row_block):
    const = lambda blk: (lambda *_: blk)
    return [pl.BlockSpec((1, d), const((0, 0))),
            pl.BlockSpec((BATCH, d), const((row_block, 0))),
            pl.BlockSpec((BATCH, d), const((row_block, 1)))]


def _lane_pieces(n_lanes):
    width = MXU_WIDTH if n_lanes % MXU_WIDTH == 0 else n_lanes
    return [(lo, width) for lo in range(0, n_lanes, width)]


def _conv_gates(q, hc, vt_ref, g1t_ref, lane0, gate_lane0=0):
    width = q.shape[1]
    val = (q[:hc] * jax.nn.sigmoid(q[hc:2 * hc])).astype(BF16)
    for j in range(width // LANES):
        vt_ref[lane0 // LANES + j] = val[:, j * LANES:(j + 1) * LANES]
    lo = gate_lane0 + lane0
    g1t_ref[:, lo:lo + width] = jax.nn.silu(q[2 * hc:]).astype(BF16)


def _transpose_cast_kernel(w_ref, o_ref):
    o_ref[...] = w_ref[...].T.astype(o_ref.dtype)


def _transpose_cast(w, col0, ncols, tile):
    k = w.shape[0]
    assert col0 % tile == 0 and ncols % tile == 0
    return pl.pallas_call(
        _transpose_cast_kernel,
        grid=(ncols // tile,),
        in_specs=[pl.BlockSpec((k, tile), lambda j: (0, j + col0 // tile))],
        out_specs=pl.BlockSpec((tile, k), lambda j: (j, 0)),
        out_shape=jax.ShapeDtypeStruct((ncols, k), BF16),
        compiler_params=_cparams(("arbitrary",)),
        name="wtrans",
    )(w)


def _project_t(w_refs, h):
    nt = (((1,), (1,)), ((), ()))
    return jnp.concatenate([lax.dot_general(w[...], h, nt, preferred_element_type=F32)
                            for w in w_refs], axis=0)


def _inproj_w_kernel(x_ref, g_ref, sh_ref, sc_ref, ws_ref, wv_ref, wg_ref, wa_ref, ut_ref, g2t_ref,
                     vt_ref, g1t_ref):
    r, ncol, _, d = x_ref.shape
    sw = ut_ref.shape[1]
    hc = g1t_ref.shape[0]
    tl = r * BATCH
    for k in range(ncol):
        for lo, width in _lane_pieces(tl):
            x = x_ref[lo // BATCH:(lo + width) // BATCH, k]
            h = _modulated_norm(x, g_ref, sh_ref, sc_ref).reshape(width, d).astype(BF16)
            q = _project_t([ws_ref], h)
            ut_ref[k, :, lo:lo + width] = q[:sw].astype(BF16)
            g2t_ref[k, :, lo:lo + width] = jax.nn.silu(q[sw:]).astype(BF16)
            _conv_gates(_project_t([wv_ref, wg_ref, wa_ref], h), hc, vt_ref.at[k], g1t_ref,
                        lo, k * tl)


def _conv_weight_specs(hc, d, part, index_map):
    return [pl.BlockSpec((hc, d), functools.partial(index_map, 2 * k + part)) for k in range(3)]


INPROJ_COLS = 2
INPROJ_ROWS = 4


def _inproj_w(x4, norm_g, mod, wt_s, wt_c, hc):
    r, _, _, d = x4.shape
    sw = wt_s.shape[0] // 2
    tl = r * BATCH
    nq = tl // LANES
    cpr = GRID_W // CHUNK
    nc = INPROJ_COLS
    per_chunk = CHUNK // nc
    s5_spec = pl.BlockSpec((nc, sw, tl), lambda j: (j % per_chunk, 0, j // per_chunk))
    return pl.pallas_call(
        _inproj_w_kernel,
        grid=(GRID_W // nc,),
        in_specs=[pl.BlockSpec((r, nc, BATCH, d), lambda j: (0, j, 0, 0)),
                  *_mod_specs(d, 0),
                  pl.BlockSpec(wt_s.shape, lambda j: (0, 0)),
                  *_conv_weight_specs(hc, d, 0, lambda blk, j: (blk, 0))],
        out_specs=[s5_spec, s5_spec,
                   pl.BlockSpec((nc, nq, hc, LANES), lambda j: (j, 0, 0, 0)),
                   pl.BlockSpec((hc, nc * tl), lambda j: (0, j))],
        out_shape=[jax.ShapeDtypeStruct((CHUNK, sw, cpr * tl), BF16),
                   jax.ShapeDtypeStruct((CHUNK, sw, cpr * tl), BF16),
                   jax.ShapeDtypeStruct((GRID_W, nq, hc, LANES), BF16),
                   jax.ShapeDtypeStruct((hc, GRID_W * tl), BF16)],
        compiler_params=_cparams(("arbitrary",)),
        name="inproj_w",
    )(x4, norm_g, mod, mod, wt_s, wt_c, wt_c, wt_c)


def _inproj_r_kernel(x_ref, g_ref, sh_ref, sc_ref, wv_ref, wg_ref, wa_ref, vt_ref, g1t_ref):
    nrow, gw, _, d = x_ref.shape
    hc = g1t_ref.shape[0]
    tl = gw * BATCH
    for k in range(nrow):
        for lo, width in _lane_pieces(tl):
            x = x_ref[k, lo // BATCH:(lo + width) // BATCH]
            h = _modulated_norm(x, g_ref, sh_ref, sc_ref).reshape(width, d).astype(BF16)
            _conv_gates(_project_t([wv_ref, wg_ref, wa_ref], h), hc, vt_ref.at[k], g1t_ref,
                        lo, k * tl)


def _inproj_r(x4, norm_g, mod, wt_c, hc):
    r, gw, _, d = x4.shape
    tl = gw * BATCH
    nq = tl // LANES
    nr = math.gcd(r, INPROJ_ROWS)
    return pl.pallas_call(
        _inproj_r_kernel,
        grid=(r // nr,),
        in_specs=[pl.BlockSpec((nr, gw, BATCH, d), lambda i: (i, 0, 0, 0)),
                  *_mod_specs(d, 0),
                  *_conv_weight_specs(hc, d, 1, lambda blk, i: (blk, 0))],
        out_specs=[pl.BlockSpec((nr, nq, hc, LANES), lambda i: (i, 0, 0, 0)),
                   pl.BlockSpec((hc, nr * tl), lambda i: (0, i))],
        out_shape=[jax.ShapeDtypeStruct((r, nq, hc, LANES), BF16),
                   jax.ShapeDtypeStruct((hc, r * tl), BF16)],
        compiler_params=_cparams(("arbitrary",)),
        name="inproj_r",
    )(x4, norm_g, mod, mod, wt_c, wt_c, wt_c)


def _ctxproj_kernel(x_ref, g_ref, sh_ref, sc_ref, wut_ref, ut_ref):
    cb, _, d = x_ref.shape
    x = x_ref[...].reshape(cb * 2, BATCH, d)
    h = _modulated_norm(x, g_ref, sh_ref, sc_ref).reshape(cb, 2 * BATCH, d)
    for tl in range(2):
        ht = h[:, tl * BATCH:(tl + 1) * BATCH, :].reshape(cb * BATCH, d).astype(BF16)
        q = lax.dot_general(wut_ref[...], ht, (((1,), (1,)), ((), ())),
                            preferred_element_type=F32)
        ut_ref[tl] = q.astype(BF16)


def _ctxproj(x4, norm_g, mod, wt_s):
    nch, _, _, d = x4.shape
    sw = wt_s.shape[0] // 2
    return pl.pallas_call(
        _ctxproj_kernel,
        grid=(CHUNK // 2,),
        in_specs=[pl.BlockSpec((nch, None, 2 * BATCH, d), lambda t: (0, t, 0, 0)),
                  *_mod_specs(d, 1),
                  pl.BlockSpec((sw, d), lambda t: (0, 0))],
        out_specs=pl.BlockSpec((2, sw, nch * BATCH), lambda t: (t, 0, 0)),
        out_shape=jax.ShapeDtypeStruct((CHUNK, sw, nch * BATCH), BF16),
        compiler_params=_cparams(("arbitrary",)),
        name="ctxproj",
    )(x4, norm_g, mod, mod, wt_s)


GELU_C1 = math.sqrt(2.0 / math.pi)
GELU_C2 = GELU_C1 * 0.044715


def _gelu_tanh(x):
    hx = 0.5 * x
    return hx + hx * jnp.tanh(x * (GELU_C1 + GELU_C2 * (x * x)))


def _s5_kernel(xt_ref, xct_ref, wall_ref, cot_ref, lam_ref, dcol_ref, y_ref,
               s_sc, h_sc, *, lane_chunk):
    nl = xt_ref.shape[-1]
    nlc = xct_ref.shape[-1]
    n_ch = nl // BATCH
    n_chc = nlc // BATCH
    kdim = CHUNK * SSM_GROUP
    p = SSM_STATE

    def group_x(ref, j):
        return ref[:, j * SSM_GROUP:(j + 1) * SSM_GROUP, :].reshape(kdim, ref.shape[-1])

    st_c = [jnp.dot(wall_ref[j], group_x(xct_ref, j), preferred_element_type=F32)
            for j in range(2)]
    st = [jnp.dot(wall_ref[j], group_x(xt_ref, j), preferred_element_type=F32) for j in range(2)]
    for d in range(2):
        for src, off, width in ((st_c, 0, nlc), (st, nlc, nl)):
            for ri in range(2):
                lo = (2 * d + ri) * p
                pair = jnp.concatenate([src[0][lo:lo + p], src[1][lo:lo + p]], axis=0)
                s_sc[d, ri, off:off + width, :] = pair.T

    lam = [[jnp.broadcast_to(lam_ref[d, ri:ri + 1, :], (BATCH, LANES)) for ri in range(2)]
           for d in range(2)]

    def advance(d, state, row):
        hr, hi = state
        ar, ai = lam[d]
        sr = s_sc[d, 0, pl.ds(row, BATCH), :]
        si = s_sc[d, 1, pl.ds(row, BATCH), :]
        return ar * hr - ai * hi + sr, ar * hi + ai * hr + si

    zero = jnp.zeros((BATCH, LANES), F32)

    f = b = (zero, zero)
    for i in range(n_chc):
        f = advance(0, f, i * BATCH)
        b = advance(1, b, (n_chc - 1 - i) * BATCH)

    cpr = GRID_W // CHUNK
    n_rows = n_ch // cpr

    def lane_row(ch):
        return ((ch % cpr) * n_rows + ch // cpr) * BATCH

    for i in range(n_ch):
        rf = lane_row(i)
        rb = lane_row(n_ch - 1 - i)
        h_sc[0, 0, rf:rf + BATCH, :] = f[0]
        h_sc[0, 1, rf:rf + BATCH, :] = f[1]
        h_sc[1, 0, rb:rb + BATCH, :] = b[0]
        h_sc[1, 1, rb:rb + BATCH, :] = b[1]
        f = advance(0, f, nlc + rf)
        b = advance(1, b, nlc + rb)

    for c0 in range(0, nl, lane_chunk):
        cs = slice(c0, c0 + lane_chunk)
        ht = [[h_sc[d, ri, cs, :].T for ri in range(2)] for d in range(2)]
        for j in range(2):
            x = xt_ref[:, j * SSM_GROUP:(j + 1) * SSM_GROUP, cs].reshape(kdim, lane_chunk)
            hj = jnp.concatenate([ht[d][ri][j * p:(j + 1) * p] for d in range(2) for ri in range(2)],
                                 axis=0).astype(BF16)
            acc = (dcol_ref[j] * x.astype(F32)
                   + jnp.dot(cot_ref[j], jnp.concatenate([x, hj], axis=0),
                             preferred_element_type=F32))
            y = _gelu_tanh(acc).astype(BF16)
            y_ref[:, j * SSM_GROUP:(j + 1) * SSM_GROUP, cs] = y.reshape(CHUNK, SSM_GROUP, lane_chunk)


def _s5(ut, uct, wall, cot, lam16, dcol):
    _, sw, nl = ut.shape
    nlc = uct.shape[-1]
    gp = sw // (2 * SSM_GROUP)
    kdim = CHUNK * SSM_GROUP
    lane_chunk = min(nl, 512)
    kern = functools.partial(_s5_kernel, lane_chunk=lane_chunk)
    return pl.pallas_call(
        kern,
        grid=(gp,),
        in_specs=[pl.BlockSpec((CHUNK, 2 * SSM_GROUP, nl), lambda g: (0, g, 0)),
                  pl.BlockSpec((CHUNK, 2 * SSM_GROUP, nlc), lambda g: (0, g, 0)),
                  pl.BlockSpec((2, kdim, kdim), lambda g: (g, 0, 0)),
                  pl.BlockSpec((2, kdim, 2 * kdim), lambda g: (g, 0, 0)),
                  pl.BlockSpec((2, None, 2, LANES), lambda g: (0, g, 0, 0)),
                  pl.BlockSpec((2, kdim, 1), lambda g: (g, 0, 0))],
        out_specs=pl.BlockSpec((CHUNK, 2 * SSM_GROUP, nl), lambda g: (0, g, 0)),
        out_shape=jax.ShapeDtypeStruct((CHUNK, sw, nl), BF16),
        scratch_shapes=[pltpu.VMEM((2, 2, nlc + nl, LANES), F32),
                        pltpu.VMEM((2, 2, nl, LANES), F32)],
        compiler_params=_cparams(("arbitrary",)),
        name="s5",
    )(ut, uct, wall, cot, lam16, dcol)


CONV_CB = 32
CONV_SKEW = 8


def _convt_kernel(v_ref, t_ref, b_ref, o_ref, s1_ref, s2_ref, z_sc, o_sc):
    a, nq, cb, _ = v_ref.shape
    kl = t_ref.shape[-1]
    pitch = cb + CONV_SKEW
    k_pad = jnp.zeros((kl - a, LANES), BF16)
    for q in range(nq):
        lanes = slice(q * LANES, (q + 1) * LANES)
        z = v_ref[:, q].astype(F32)
        for i in range(a):
            z_sc[i * pitch:i * pitch + cb, :] = z[i]
        s1 = jnp.zeros((a, LANES), F32)
        s2 = jnp.zeros((a, LANES), F32)
        for c in range(cb):
            col = z_sc[pl.ds(c, a, stride=pitch), :].astype(BF16)
            rhs = jnp.concatenate([col, k_pad], axis=0) if kl > a else col
            out = jnp.dot(t_ref[c], rhs, preferred_element_type=F32) + b_ref[c:c + 1]
            o_sc[pl.ds(c, a, stride=pitch), :] = out
            s1 = s1 + out
            s2 = s2 + out * out
        for i in range(a):
            o_ref[i, :, lanes] = o_sc[i * pitch:i * pitch + cb, :].astype(o_ref.dtype)

        @pl.when(pl.program_id(0) == 0)
        def _():
            s1_ref[:, lanes] = jnp.zeros((a, LANES), F32)
            s2_ref[:, lanes] = jnp.zeros((a, LANES), F32)

        s1_ref[:, lanes] += s1
        s2_ref[:, lanes] += s2


def _convt(vt, t, bias):
    a, nq, c, _ = vt.shape
    cb = min(c, CONV_CB)
    kl = t.shape[-1]
    stat_spec = pl.BlockSpec((a, nq * LANES), lambda i: (0, 0))
    return pl.pallas_call(
        _convt_kernel,
        grid=(c // cb,),
        in_specs=[pl.BlockSpec((a, nq, cb, LANES), lambda i: (0, 0, i, 0)),
                  pl.BlockSpec((cb, a, kl), lambda i: (i, 0, 0)),
                  pl.BlockSpec((cb, 1), lambda i: (i, 0))],
        out_specs=[pl.BlockSpec((a, cb, nq * LANES), lambda i: (0, i, 0)), stat_spec, stat_spec],
        out_shape=[jax.ShapeDtypeStruct((a, c, nq * LANES), BF16),
                   jax.ShapeDtypeStruct((a, nq * LANES), F32),
                   jax.ShapeDtypeStruct((a, nq * LANES), F32)],
        scratch_shapes=[pltpu.VMEM((a * (cb + CONV_SKEW), LANES), F32),
                        pltpu.VMEM((a * (cb + CONV_SKEW), LANES), F32)],
        compiler_params=_cparams(("arbitrary",)),
        name="convt",
    )(vt, t, bias.reshape(c, 1))


@functools.lru_cache(maxsize=None)
def _tap_selector(a):
    kl = -(-a // LANES) * LANES
    tap = np.arange(kl)[None, :] - np.arange(a)[:, None] + CONV_PAD
    hit = (tap[None] == np.arange(CONV_TAPS)[:, None, None]) & (np.arange(kl) < a)
    return hit.astype(BF16)


def _conv_toeplitz(w, a):
    return jnp.einsum("kc,kxy->cxy", w.astype(BF16), jnp.asarray(_tap_selector(a)),
                      preferred_element_type=BF16)


def _conv_branch_t(cv, g1, st, lng_ref, lnb_ref, n_chan):
    mean = (st[0:1] + st[2:3]) * (1.0 / n_chan)
    var = (st[1:2] + st[3:4]) * (1.0 / n_chan) - mean * mean
    y = (cv.astype(F32) - mean) * lax.rsqrt(var + EPS) * lng_ref[...] + lnb_ref[...]
    return (jax.nn.silu(y) * g1.astype(F32)).astype(BF16)


def _tail_w_kernel(y_ref, g2_ref, cv_ref, g1_ref, st_ref, wt_ref, bt_ref, wos_ref, woh_ref,
                   lng_ref, lnb_ref, o_ref, mt_sc, *, n_chan):
    tl = y_ref.shape[-1]
    st = st_ref[...]
    mix = []
    for k in range(2):
        lanes = slice(k * tl, (k + 1) * tl)
        y = y_ref[k]
        z = jnp.dot(wt_ref[...], y, preferred_element_type=F32) + bt_ref[...]
        s = (y.astype(F32) * jax.nn.sigmoid(z) * g2_ref[k].astype(F32)).astype(BF16)
        co = _conv_branch_t(cv_ref[k], g1_ref[:, lanes], st[:, lanes], lng_ref, lnb_ref, n_chan)
        mt_sc[...] = (jnp.dot(wos_ref[...], s, preferred_element_type=F32)
                      + jnp.dot(woh_ref[...], co, preferred_element_type=F32))
        mix.append(mt_sc[...].T.reshape(tl // BATCH, BATCH, mt_sc.shape[0]))
    o_ref[...] = jnp.concatenate(mix, axis=1).astype(o_ref.dtype)


def _tail_w(y1t, g2t, ct_h, g1t_h, stats_h, glu_wt, glu_bcol, wot, ln_g, ln_b, n_chan):
    _, sw, _ = y1t.shape
    _, hc, tl = ct_h.shape
    d = wot.shape[0]
    assert (2 * hc) % sw == 0
    r = tl // BATCH
    half_chunk = CHUNK // 2
    s5_spec = pl.BlockSpec((2, sw, tl), lambda j: (j % half_chunk, 0, j // half_chunk))
    tok = lambda rows: pl.BlockSpec((rows, 2 * tl), lambda j: (0, j))
    full = lambda arr: pl.BlockSpec(arr.shape, lambda j: (0,) * arr.ndim)
    return pl.pallas_call(
        functools.partial(_tail_w_kernel, n_chan=n_chan),
        grid=(GRID_W // 2,),
        in_specs=[s5_spec, s5_spec, pl.BlockSpec((2, hc, tl), lambda j: (j, 0, 0)), tok(hc),
                  tok(8), full(glu_wt), full(glu_bcol),
                  pl.BlockSpec((d, sw), lambda j: (0, 2 * hc // sw)),
                  pl.BlockSpec((d, hc), lambda j: (0, 0)), full(ln_g), full(ln_b)],
        out_specs=pl.BlockSpec((r, None, 2 * BATCH, d), lambda j: (0, j, 0, 0)),
        out_shape=jax.ShapeDtypeStruct((r, GRID_W // 2, 2 * BATCH, d), BF16),
        scratch_shapes=[pltpu.VMEM((d, tl), F32)],
        compiler_params=_cparams(("arbitrary",)),
        name="tail_w",
    )(y1t, g2t, ct_h, g1t_h, stats_h, glu_wt, glu_bcol, wot, wot, ln_g, ln_b)


FINAL_ROWS = 2


def _final_r_kernel(cv_ref, g1_ref, st_ref, mh_ref, x_ref, wo_ref, lng_ref, lnb_ref, gate_ref,
                    fg_ref, o_ref, mix_sc, *, n_chan):
    tl, d = x_ref.shape[1:]
    cv = jnp.concatenate([cv_ref[i] for i in range(cv_ref.shape[0])], axis=-1)
    co = _conv_branch_t(cv, g1_ref[...], st_ref[...], lng_ref, lnb_ref, n_chan)
    mix = lax.dot_general(co.T, wo_ref[...], (((1,), (1,)), ((), ())),
                          preferred_element_type=F32) + mh_ref[...].reshape(tl * BATCH, d)
    nlt = mix_sc.shape[0]
    for j in range(nlt):
        mix_sc[j] = mix[:, j * LANES:(j + 1) * LANES]
    for b in range(BATCH):
        mb = jnp.concatenate([mix_sc[j, pl.ds(b, tl, stride=BATCH), :] for j in range(nlt)],
                             axis=-1)
        xo = x_ref[b] + gate_ref[b:b + 1, :] * mb
        ms = jnp.mean(xo * xo, axis=-1, keepdims=True)
        o_ref[b] = xo * lax.rsqrt(ms + EPS) * fg_ref[...]


def _final_r(ct_v, g1t_v, stats_v, mix_h, x, wot, ln_g, ln_b, mod, final_g, n_chan):
    r, hc, row_tokens = ct_v.shape
    _, length, d = x.shape
    rs = math.gcd(r, FINAL_ROWS)
    tm = rs * row_tokens
    tl = tm // BATCH
    tok = lambda rows: pl.BlockSpec((rows, tm), lambda i: (0, i))
    nat = pl.BlockSpec((BATCH, tl, d), lambda i: (0, i, 0))
    full = lambda arr: pl.BlockSpec(arr.shape, lambda i: (0,) * arr.ndim)
    return pl.pallas_call(
        functools.partial(_final_r_kernel, n_chan=n_chan),
        grid=(r // rs,),
        in_specs=[pl.BlockSpec((rs, hc, row_tokens), lambda i: (i, 0, 0)), tok(hc), tok(8),
                  pl.BlockSpec((rs, GRID_W // 2, 2 * BATCH, d), lambda i: (i, 0, 0, 0)), nat,
                  pl.BlockSpec((d, hc), lambda i: (0, 1)), full(ln_g), full(ln_b),
                  pl.BlockSpec((BATCH, d), lambda i: (0, 2)), full(final_g)],
        out_specs=nat,
        out_shape=jax.ShapeDtypeStruct(x.shape, F32),
        scratch_shapes=[pltpu.VMEM((d // LANES, tm, LANES), F32)],
        compiler_params=_cparams(("arbitrary",)),
        name="final_r",
    )(ct_v, g1t_v, stats_v, mix_h, x, wot, ln_g, ln_b, mod, final_g)


S5PREP_GROUPS = 4


def _s5prep_kernel(arow_ref, bt_ref, ct_ref, wall_ref, cot_ref, lam_ref, pw_sc, kk_sc):
    for gi in range(arow_ref.shape[0]):
        _s5prep_group(arow_ref.at[gi], bt_ref.at[gi], ct_ref.at[gi], wall_ref.at[gi], cot_ref.at[gi],
                      lam_ref.at[gi], pw_sc, kk_sc)


def _s5prep_group(arow_ref, bt_ref, ct_ref, wall_ref, cot_ref, lam_ref, pw_sc, kk_sc):
    kdim = CHUNK * SSM_GROUP
    hp = lax.Precision.HIGHEST
    lane = lax.broadcasted_iota(jnp.int32, (1, LANES), 1)
    is_re = lane < SSM_STATE
    col_t = lax.broadcasted_iota(jnp.int32, (1, kdim), 1) // SSM_GROUP
    kcol = lax.broadcasted_iota(jnp.int32, (pw_sc.shape[1], 1), 0).astype(F32)

    def tile_rows(a):
        return jnp.concatenate([a] * CHUNK, axis=0)

    def power_rows(exps):
        pr = jnp.concatenate([jnp.broadcast_to(pw_sc[0, e:e + 1, :], (SSM_GROUP, LANES))
                              for e in exps], axis=0)
        pi = jnp.concatenate([jnp.broadcast_to(pw_sc[1, e:e + 1, :], (SSM_GROUP, LANES))
                              for e in exps], axis=0)
        return pr, pi

    def cmul(xr, xi, yr, yi, im_sign):
        return jnp.where(is_re, xr * yr - xi * yi, im_sign * (xr * yi + xi * yr))

    toeplitz = jnp.zeros((kdim, kdim), F32)
    for d in range(2):
        ar, ai = arow_ref[d, 0:1, :], arow_ref[d, 1:2, :]
        dt = jnp.exp(arow_ref[d, 2:3, :])
        mag = jnp.exp(kcol * (ar * dt))
        ang = kcol * (ai * dt)
        pw_sc[0] = mag * jnp.cos(ang)
        pw_sc[1] = mag * jnp.sin(ang)
        lam_ref[d, 0] = pw_sc[0, CHUNK:CHUNK + 8, :]
        lam_ref[d, 1] = pw_sc[1, CHUNK:CHUNK + 8, :]
        nr, ni = pw_sc[0, 1:2, :] - 1.0, pw_sc[1, 1:2, :]
        den = ar * ar + ai * ai
        qr, qi = (nr * ar + ni * ai) / den, (ni * ar - nr * ai) / den
        btr, bti = bt_ref[d, 0], bt_ref[d, 1]
        bbr, bbi = tile_rows(qr * btr - qi * bti), tile_rows(qr * bti + qi * btr)
        cr, ci = tile_rows(ct_ref[d, 0]), tile_rows(ct_ref[d, 1])
        tt = list(range(CHUNK))
        e_lag = [CHUNK - 1 - t for t in tt] if d else tt
        e_in = tt if d else [CHUNK - 1 - t for t in tt]
        e_out = [CHUNK - t for t in tt] if d else [t + 1 for t in tt]
        cot_ref[:, kdim + d * LANES:kdim + (d + 1) * LANES] = cmul(
            cr, ci, *power_rows(e_out), -1.0).astype(BF16)
        wall_ref[d * LANES:(d + 1) * LANES, :] = cmul(bbr, bbi, *power_rows(e_in), 1.0).T.astype(BF16)
        c_lag = cmul(cr, ci, *power_rows(e_lag), -1.0)
        bb = jnp.where(is_re, bbr, bbi)
        kk = lax.dot_general(c_lag, bb, (((1,), (1,)), ((), ())), precision=hp,
                             preferred_element_type=F32)
        zeros = jnp.zeros((kdim, kdim), F32)
        if d:
            kk_sc[:kdim] = kk
            kk_sc[kdim:] = zeros
            starts = [SSM_GROUP * (CHUNK - 1 - t) for t in tt]
        else:
            kk_sc[:kdim] = zeros
            kk_sc[kdim:] = kk
            starts = [kdim - SSM_GROUP * t for t in tt]
        tiles = []
        for lt in range(kdim // LANES):
            cols = slice(lt * LANES, (lt + 1) * LANES)
            acc = jnp.zeros((kdim, LANES), F32)
            for t in range(lt * LANES // SSM_GROUP, (lt + 1) * LANES // SSM_GROUP):
                acc = jnp.where(col_t[:, cols] == t, kk_sc[starts[t]:starts[t] + kdim, cols], acc)
            tiles.append(acc)
        toeplitz = toeplitz + jnp.concatenate(tiles, axis=1)
    cot_ref[:, :kdim] = toeplitz.astype(BF16)


def _s5prep(arow, bt, ct):
    g = arow.shape[0]
    gs = math.gcd(g, S5PREP_GROUPS)
    kdim = CHUNK * SSM_GROUP
    return pl.pallas_call(
        _s5prep_kernel,
        grid=(g // gs,),
        in_specs=[pl.BlockSpec((gs, 2, 8, LANES), lambda i: (i, 0, 0, 0)),
                  pl.BlockSpec((gs, 2, 2, SSM_GROUP, LANES), lambda i: (i, 0, 0, 0, 0)),
                  pl.BlockSpec((gs, 2, 2, SSM_GROUP, LANES), lambda i: (i, 0, 0, 0, 0))],
        out_specs=[pl.BlockSpec((gs, kdim, kdim), lambda i: (i, 0, 0)),
                   pl.BlockSpec((gs, kdim, 2 * kdim), lambda i: (i, 0, 0)),
                   pl.BlockSpec((gs, 2, 2, 8, LANES), lambda i: (i, 0, 0, 0, 0))],
        out_shape=[jax.ShapeDtypeStruct((g, kdim, kdim), BF16),
                   jax.ShapeDtypeStruct((g, kdim, 2 * kdim), BF16),
                   jax.ShapeDtypeStruct((g, 2, 2, 8, LANES), F32)],
        scratch_shapes=[pltpu.VMEM((2, CHUNK + 8, LANES), F32),
                        pltpu.VMEM((2 * kdim, kdim), F32)],
        compiler_params=_cparams(("arbitrary",)),
        name="s5prep",
    )(arow, bt, ct)


def _s5prep_inputs(a_re, a_im, log_dt, b_re, b_im, c_re, c_im):
    g = a_re.shape[1]
    dup = lambda a: jnp.concatenate([a, a], axis=-1)
    rows = jnp.stack([dup(a_re), dup(a_im),
                      jnp.broadcast_to(log_dt[..., None], (2, g, LANES))], axis=2)
    arow = jnp.pad(rows, ((0, 0), (0, 0), (0, 5), (0, 0))).transpose(1, 0, 2, 3)
    bt = jnp.stack([dup(b_re.transpose(0, 1, 3, 2)), dup(b_im.transpose(0, 1, 3, 2))], axis=2)
    ct = jnp.stack([dup(c_re), dup(c_im)], axis=2)
    return arow, bt.transpose(1, 0, 2, 3, 4), ct.transpose(1, 0, 2, 3, 4)


def kernel(x, c, ctx, c_ctx, norm_g, w_ada, b_ada, w_in, conv_dw, conv_db, conv_ln_g, conv_ln_b,
           ssm_a_re, ssm_a_im, ssm_log_dt, ssm_b_re, ssm_b_im, ssm_c_re, ssm_c_im, ssm_d,
           ssm_glu_w, ssm_glu_b, w_out, final_g):
    bsz, length, d = x.shape
    ctx_len = ctx.shape[1]
    cw = conv_dw.shape[-1]
    sw = ssm_d.shape[-1]
    assert bsz == BATCH and norm_g.shape[0] == 1
    assert length % (GRID_W * CHUNK) == 0 and ctx_len % CHUNK == 0
    n = length * BATCH
    nch = length // CHUNK
    nchc = ctx_len // CHUNK
    rows = length // GRID_W

    cond16 = jnp.concatenate([c, jnp.broadcast_to(c_ctx, (BATCH, d))])
    mod = _ada(cond16, w_ada.reshape(w_ada.shape[1:]), b_ada[0])

    w = w_in.reshape(w_in.shape[1:])
    half = cw // 2
    wt_c = _transpose_cast(w, 0, 3 * cw, cw)
    wt_s = _transpose_cast(w, 3 * cw, 2 * sw, sw)
    wall, cot, lam_raw = _s5prep(*_s5prep_inputs(
        ssm_a_re[0], ssm_a_im[0], ssm_log_dt[0], ssm_b_re[0], ssm_b_im[0], ssm_c_re[0], ssm_c_im[0]))
    ng = sw // SSM_GROUP
    lam16 = lam_raw[:, :, :, 0, :SSM_STATE].reshape(ng // 2, 2, 2, 2, SSM_STATE).transpose(
        2, 0, 3, 1, 4).reshape(2, ng // 2, 2, LANES)
    dcol = jnp.broadcast_to(ssm_d[0].reshape(ng, 1, SSM_GROUP), (ng, CHUNK, SSM_GROUP)).reshape(
        ng, CHUNK * SSM_GROUP, 1)

    xt = x.transpose(1, 0, 2)
    ctxt = ctx.transpose(1, 0, 2)
    c4 = ctxt.reshape(nchc, CHUNK // 2, 2 * BATCH, d)

    x4 = xt.reshape(rows, GRID_W, BATCH, d)
    ut, g2t, vt_h, g1t_h = _inproj_w(x4, norm_g, mod, wt_s, wt_c, half)
    vt_v, g1t_v = _inproj_r(x4, norm_g, mod, wt_c, half)
    uct = _ctxproj(c4, norm_g, mod, wt_s)
    y1t = _s5(ut, uct, wall, cot, lam16, dcol)

    dw, db = conv_dw[0], conv_db[0]
    ct_h, s1_h, s2_h = _convt(vt_h, _conv_toeplitz(dw[:, :half], GRID_W), db[:half])
    ct_v, s1_v, s2_v = _convt(vt_v, _conv_toeplitz(dw[:, half:], rows), db[half:])
    to_v = lambda a: a.reshape(GRID_W, rows, BATCH).transpose(1, 0, 2).reshape(1, n)
    to_h = lambda a: a.reshape(rows, GRID_W, BATCH).transpose(1, 0, 2).reshape(1, n)
    pad4 = jnp.zeros((4, n), F32)
    stats_v = jnp.concatenate([s1_v.reshape(1, n), s2_v.reshape(1, n), to_v(s1_h), to_v(s2_h), pad4])
    stats_h = jnp.concatenate([s1_h.reshape(1, n), s2_h.reshape(1, n), to_h(s1_v), to_h(s2_v), pad4])

    ln_g, ln_b = conv_ln_g[0].reshape(cw, 1), conv_ln_b[0].reshape(cw, 1)
    wot = _transpose_cast(w_out.reshape(w_out.shape[1:]), 0, d, half)
    mix_h = _tail_w(y1t, g2t, ct_h, g1t_h, stats_h, _transpose_cast(ssm_glu_w[0], 0, sw, sw),
                    ssm_glu_b[0].reshape(sw, 1), wot, ln_g[:half], ln_b[:half], cw)
    return _final_r(ct_v, g1t_v, stats_v, mix_h, x, wot, ln_g[half:],
                    ln_b[half:], mod, final_g.reshape(1, d), cw)
```

```python
import functools
import math

import jax
import jax.numpy as jnp
import numpy as np
from jax import lax
from jax.experimental import pallas as pl
from jax.experimental.pallas import tpu as pltpu

GRID_W = 64
CONV_TAPS = 31
CONV_PAD = CONV_TAPS // 2
SSM_GROUP = 16
SSM_STATE = 64
CHUNK = 16
BATCH = 8
EPS = 1e-6
LANES = 128
MXU_WIDTH = 256
VMEM_LIMIT = 56 * 1024 * 1024

F32 = jnp.float32
BF16 = jnp.bfloat16


def _cparams(sem):
    return pltpu.CompilerParams(dimension_semantics=sem, vmem_limit_bytes=VMEM_LIMIT)


def _ada_kernel(c_ref, w_ref, b_ref, o_ref):
    s = jax.nn.silu(c_ref[...])
    o_ref[...] = jnp.dot(s, w_ref[...], preferred_element_type=F32,
                         precision=lax.Precision.HIGHEST) + b_ref[...]


def _ada(cond16, w_ada, b_ada):
    d, n3 = w_ada.shape
    return pl.pallas_call(
        _ada_kernel,
        grid=(n3 // d,),
        in_specs=[pl.BlockSpec((16, d), lambda j: (0, 0)),
                  pl.BlockSpec((d, d), lambda j: (0, j)),
                  pl.BlockSpec((1, d), lambda j: (0, j))],
        out_specs=pl.BlockSpec((16, d), lambda j: (0, j)),
        out_shape=jax.ShapeDtypeStruct((16, n3), F32),
        compiler_params=_cparams(("arbitrary",)),
        name="ada",
    )(cond16, w_ada, b_ada.reshape(1, n3))


def _modulated_norm(x, g_ref, shift_ref, scale_ref):
    ms = jnp.mean(x * x, axis=-1, keepdims=True)
    amp = g_ref[...] * (1.0 + scale_ref[...])
    return x * lax.rsqrt(ms + EPS) * amp[None] + shift_ref[...][None]


def _mod_specs(d, row_block):
    const = lambda blk: (lambda *_: blk)
    return [pl.BlockSpec((1, d), const((0, 0))),
            pl.BlockSpec((BATCH, d), const((row_block, 0))),
            pl.BlockSpec((BATCH, d), const((row_block, 1)))]


def _lane_pieces(n_lanes):
    width = MXU_WIDTH if n_lanes % MXU_WIDTH == 0 else n_lanes
    return [(lo, width) for lo in range(0, n_lanes, width)]


def _conv_gates(q, hc, vt_ref, g1t_ref, lane0, gate_lane0=0):
    width = q.shape[1]
    val = (q[:hc] * jax.nn.sigmoid(q[hc:2 * hc])).astype(BF16)
    for j in range(width // LANES):
        vt_ref[lane0 // LANES + j] = val[:, j * LANES:(j + 1) * LANES]
    lo = gate_lane0 + lane0
    g1t_ref[:, lo:lo + width] = jax.nn.silu(q[2 * hc:]).astype(BF16)


def _transpose_cast_kernel(w_ref, o_ref):
    o_ref[...] = w_ref[...].T.astype(o_ref.dtype)


def _transpose_cast(w, col0, ncols, tile):
    k = w.shape[0]
    assert col0 % tile == 0 and ncols % tile == 0
    return pl.pallas_call(
        _transpose_cast_kernel,
        grid=(ncols // tile,),
        in_specs=[pl.BlockSpec((k, tile), lambda j: (0, j + col0 // tile))],
        out_specs=pl.BlockSpec((tile, k), lambda j: (j, 0)),
        out_shape=jax.ShapeDtypeStruct((ncols, k), BF16),
        compiler_params=_cparams(("arbitrary",)),
        name="wtrans",
    )(w)


def _project_t(w_refs, h):
    nt = (((1,), (1,)), ((), ()))
    return jnp.concatenate([lax.dot_general(w[...], h, nt, preferred_element_type=F32)
                            for w in w_refs], axis=0)


def _inproj_w_kernel(x_ref, g_ref, sh_ref, sc_ref, ws_ref, wv_ref, wg_ref, wa_ref, ut_ref, g2t_ref,
                     vt_ref, g1t_ref):
    r, ncol, _, d = x_ref.shape
    sw = ut_ref.shape[1]
    hc = g1t_ref.shape[0]
    tl = r * BATCH
    for k in range(ncol):
        for lo, width in _lane_pieces(tl):
            x = x_ref[lo // BATCH:(lo + width) // BATCH, k]
            h = _modulated_norm(x, g_ref, sh_ref, sc_ref).reshape(width, d).astype(BF16)
            q = _project_t([ws_ref], h)
            ut_ref[k, :, lo:lo + width] = q[:sw].astype(BF16)
            g2t_ref[k, :, lo:lo + width] = jax.nn.silu(q[sw:]).astype(BF16)
            _conv_gates(_project_t([wv_ref, wg_ref, wa_ref], h), hc, vt_ref.at[k], g1t_ref,
                        lo, k * tl)


def _conv_weight_specs(hc, d, part, index_map):
    return [pl.BlockSpec((hc, d), functools.partial(index_map, 2 * k + part)) for k in range(3)]


INPROJ_COLS = 2
INPROJ_ROWS = 4


def _inproj_w(x4, norm_g, mod, wt_s, wt_c, hc):
    r, _, _, d = x4.shape
    sw = wt_s.shape[0] // 2
    tl = r * BATCH
    nq = tl // LANES
    cpr = GRID_W // CHUNK
    nc = INPROJ_COLS
    per_chunk = CHUNK // nc
    s5_spec = pl.BlockSpec((nc, sw, tl), lambda j: (j % per_chunk, 0, j // per_chunk))
    return pl.pallas_call(
        _inproj_w_kernel,
        grid=(GRID_W // nc,),
        in_specs=[pl.BlockSpec((r, nc, BATCH, d), lambda j: (0, j, 0, 0)),
                  *_mod_specs(d, 0),
                  pl.BlockSpec(wt_s.shape, lambda j: (0, 0)),
                  *_conv_weight_specs(hc, d, 0, lambda blk, j: (blk, 0))],
        out_specs=[s5_spec, s5_spec,
                   pl.BlockSpec((nc, nq, hc, LANES), lambda j: (j, 0, 0, 0)),
                   pl.BlockSpec((hc, nc * tl), lambda j: (0, j))],
        out_shape=[jax.ShapeDtypeStruct((CHUNK, sw, cpr * tl), BF16),
                   jax.ShapeDtypeStruct((CHUNK, sw, cpr * tl), BF16),
                   jax.ShapeDtypeStruct((GRID_W, nq, hc, LANES), BF16),
                   jax.ShapeDtypeStruct((hc, GRID_W * tl), BF16)],
        compiler_params=_cparams(("arbitrary",)),
        name="inproj_w",
    )(x4, norm_g, mod, mod, wt_s, wt_c, wt_c, wt_c)


def _inproj_r_kernel(x_ref, g_ref, sh_ref, sc_ref, wv_ref, wg_ref, wa_ref, vt_ref, g1t_ref):
    nrow, gw, _, d = x_ref.shape
    hc = g1t_ref.shape[0]
    tl = gw * BATCH
    for k in range(nrow):
        for lo, width in _lane_pieces(tl):
            x = x_ref[k, lo // BATCH:(lo + width) // BATCH]
            h = _modulated_norm(x, g_ref, sh_ref, sc_ref).reshape(width, d).astype(BF16)
            _conv_gates(_project_t([wv_ref, wg_ref, wa_ref], h), hc, vt_ref.at[k], g1t_ref,
                        lo, k * tl)


def _inproj_r(x4, norm_g, mod, wt_c, hc):
    r, gw, _, d = x4.shape
    tl = gw * BATCH
    nq = tl // LANES
    nr = math.gcd(r, INPROJ_ROWS)
    return pl.pallas_call(
        _inproj_r_kernel,
        grid=(r // nr,),
        in_specs=[pl.BlockSpec((nr, gw, BATCH, d), lambda i: (i, 0, 0, 0)),
                  *_mod_specs(d, 0),
                  *_conv_weight_specs(hc, d, 1, lambda blk, i: (blk, 0))],
        out_specs=[pl.BlockSpec((nr, nq, hc, LANES), lambda i: (i, 0, 0, 0)),
                   pl.BlockSpec((hc, nr * tl), lambda i: (0, i))],
        out_shape=[jax.ShapeDtypeStruct((r, nq, hc, LANES), BF16),
                   jax.ShapeDtypeStruct((hc, r * tl), BF16)],
        compiler_params=_cparams(("arbitrary",)),
        name="inproj_r",
    )(x4, norm_g, mod, mod, wt_c, wt_c, wt_c)


def _ctxproj_kernel(x_ref, g_ref, sh_ref, sc_ref, wut_ref, ut_ref):
    cb, _, d = x_ref.shape
    x = x_ref[...].reshape(cb * 2, BATCH, d)
    h = _modulated_norm(x, g_ref, sh_ref, sc_ref).reshape(cb, 2 * BATCH, d)
    for tl in range(2):
        ht = h[:, tl * BATCH:(tl + 1) * BATCH, :].reshape(cb * BATCH, d).astype(BF16)
        q = lax.dot_general(wut_ref[...], ht, (((1,), (1,)), ((), ())),
                            preferred_element_type=F32)
        ut_ref[tl] = q.astype(BF16)


def _ctxproj(x4, norm_g, mod, wt_s):
    nch, _, _, d = x4.shape
    sw = wt_s.shape[0] // 2
    return pl.pallas_call(
        _ctxproj_kernel,
        grid=(CHUNK // 2,),
        in_specs=[pl.BlockSpec((nch, None, 2 * BATCH, d), lambda t: (0, t, 0, 0)),
                  *_mod_specs(d, 1),
                  pl.BlockSpec((sw, d), lambda t: (0, 0))],
        out_specs=pl.BlockSpec((2, sw, nch * BATCH), lambda t: (t, 0, 0)),
        out_shape=jax.ShapeDtypeStruct((CHUNK, sw, nch * BATCH), BF16),
        compiler_params=_cparams(("arbitrary",)),
        name="ctxproj",
    )(x4, norm_g, mod, mod, wt_s)


GELU_C1 = math.sqrt(2.0 / math.pi)
GELU_C2 = GELU_C1 * 0.044715


def _gelu_tanh(x):
    hx = 0.5 * x
    return hx + hx * jnp.tanh(x * (GELU_C1 + GELU_C2 * (x * x)))


def _s5_kernel(xt_ref, xct_ref, wall_ref, cot_ref, lam_ref, dcol_ref, y_ref,
               s_sc, h_sc, *, lane_chunk):
    nl = xt_ref.shape[-1]
    nlc = xct_ref.shape[-1]
    n_ch = nl // BATCH
    n_chc = nlc // BATCH
    kdim = CHUNK * SSM_GROUP
    p = SSM_STATE

    def group_x(ref, j):
        return ref[:, j * SSM_GROUP:(j + 1) * SSM_GROUP, :].reshape(kdim, ref.shape[-1])

    st_c = [jnp.dot(wall_ref[j], group_x(xct_ref, j), preferred_element_type=F32)
            for j in range(2)]
    st = [jnp.dot(wall_ref[j], group_x(xt_ref, j), preferred_element_type=F32) for j in range(2)]
    for d in range(2):
        for src, off, width in ((st_c, 0, nlc), (st, nlc, nl)):
            for ri in range(2):
                lo = (2 * d + ri) * p
                pair = jnp.concatenate([src[0][lo:lo + p], src[1][lo:lo + p]], axis=0)
                s_sc[d, ri, off:off + width, :] = pair.T

    lam = [[jnp.broadcast_to(lam_ref[d, ri:ri + 1, :], (BATCH, LANES)) for ri in range(2)]
           for d in range(2)]

    def advance(d, state, row):
        hr, hi = state
        ar, ai = lam[d]
        sr = s_sc[d, 0, pl.ds(row, BATCH), :]
        si = s_sc[d, 1, pl.ds(row, BATCH), :]
        return ar * hr - ai * hi + sr, ar * hi + ai * hr + si

    zero = jnp.zeros((BATCH, LANES), F32)

    f = b = (zero, zero)
    for i in range(n_chc):
        f = advance(0, f, i * BATCH)
        b = advance(1, b, (n_chc - 1 - i) * BATCH)

    cpr = GRID_W // CHUNK
    n_rows = n_ch // cpr

    def lane_row(ch):
        return ((ch % cpr) * n_rows + ch // cpr) * BATCH

    for i in range(n_ch):
        rf = lane_row(i)
        rb = lane_row(n_ch - 1 - i)
        h_sc[0, 0, rf:rf + BATCH, :] = f[0]
        h_sc[0, 1, rf:rf + BATCH, :] = f[1]
        h_sc[1, 0, rb:rb + BATCH, :] = b[0]
        h_sc[1, 1, rb:rb + BATCH, :] = b[1]
        f = advance(0, f, nlc + rf)
        b = advance(1, b, nlc + rb)

    for c0 in range(0, nl, lane_chunk):
        cs = slice(c0, c0 + lane_chunk)
        ht = [[h_sc[d, ri, cs, :].T for ri in range(2)] for d in range(2)]
        for j in range(2):
            x = xt_ref[:, j * SSM_GROUP:(j + 1) * SSM_GROUP, cs].reshape(kdim, lane_chunk)
            hj = jnp.concatenate([ht[d][ri][j * p:(j + 1) * p] for d in range(2) for ri in range(2)],
                                 axis=0).astype(BF16)
            acc = (dcol_ref[j] * x.astype(F32)
                   + jnp.dot(cot_ref[j], jnp.concatenate([x, hj], axis=0),
                             preferred_element_type=F32))
            y = _gelu_tanh(acc).astype(BF16)
            y_ref[:, j * SSM_GROUP:(j + 1) * SSM_GROUP, cs] = y.reshape(CHUNK, SSM_GROUP, lane_chunk)


def _s5(ut, uct, wall, cot, lam16, dcol):
    _, sw, nl = ut.shape
    nlc = uct.shape[-1]
    gp = sw // (2 * SSM_GROUP)
    kdim = CHUNK * SSM_GROUP
    lane_chunk = min(nl, 512)
    kern = functools.partial(_s5_kernel, lane_chunk=lane_chunk)
    return pl.pallas_call(
        kern,
        grid=(gp,),
        in_specs=[pl.BlockSpec((CHUNK, 2 * SSM_GROUP, nl), lambda g: (0, g, 0)),
                  pl.BlockSpec((CHUNK, 2 * SSM_GROUP, nlc), lambda g: (0, g, 0)),
                  pl.BlockSpec((2, kdim, kdim), lambda g: (g, 0, 0)),
                  pl.BlockSpec((2, kdim, 2 * kdim), lambda g: (g, 0, 0)),
                  pl.BlockSpec((2, None, 2, LANES), lambda g: (0, g, 0, 0)),
                  pl.BlockSpec((2, kdim, 1), lambda g: (g, 0, 0))],
        out_specs=pl.BlockSpec((CHUNK, 2 * SSM_GROUP, nl), lambda g: (0, g, 0)),
        out_shape=jax.ShapeDtypeStruct((CHUNK, sw, nl), BF16),
        scratch_shapes=[pltpu.VMEM((2, 2, nlc + nl, LANES), F32),
                        pltpu.VMEM((2, 2, nl, LANES), F32)],
        compiler_params=_cparams(("arbitrary",)),
        name="s5",
    )(ut, uct, wall, cot, lam16, dcol)


CONV_CB = 64
CONV_SKEW = 8


def _convt_kernel(v_ref, t_ref, b_ref, o_ref, s1_ref, s2_ref, z_sc, o_sc):
    a, nq, cb, _ = v_ref.shape
    kl = t_ref.shape[-1]
    pitch = cb + CONV_SKEW
    k_pad = jnp.zeros((kl - a, LANES), BF16)
    for q in range(nq):
        lanes = slice(q * LANES, (q + 1) * LANES)
        z = v_ref[:, q].astype(F32)
        for i in range(a):
            z_sc[i * pitch:i * pitch + cb, :] = z[i]
        s1 = jnp.zeros((a, LANES), F32)
        s2 = jnp.zeros((a, LANES), F32)
        for c in range(cb):
            col = z_sc[pl.ds(c, a, stride=pitch), :].astype(BF16)
            rhs = jnp.concatenate([col, k_pad], axis=0) if kl > a else col
            out = jnp.dot(t_ref[c], rhs, preferred_element_type=F32) + b_ref[c:c + 1]
            o_sc[pl.ds(c, a, stride=pitch), :] = out
            s1 = s1 + out
            s2 = s2 + out * out
        for i in range(a):
            o_ref[i, :, lanes] = o_sc[i * pitch:i * pitch + cb, :].astype(o_ref.dtype)

        @pl.when(pl.program_id(0) == 0)
        def _():
            s1_ref[:, lanes] = jnp.zeros((a, LANES), F32)
            s2_ref[:, lanes] = jnp.zeros((a, LANES), F32)

        s1_ref[:, lanes] += s1
        s2_ref[:, lanes] += s2


def _convt(vt, t, bias):
    a, nq, c, _ = vt.shape
    cb = min(c, CONV_CB)
    kl = t.shape[-1]
    stat_spec = pl.BlockSpec((a, nq * LANES), lambda i: (0, 0))
    return pl.pallas_call(
        _convt_kernel,
        grid=(c // cb,),
        in_specs=[pl.BlockSpec((a, nq, cb, LANES), lambda i: (0, 0, i, 0)),
                  pl.BlockSpec((cb, a, kl), lambda i: (i, 0, 0)),
                  pl.BlockSpec((cb, 1), lambda i: (i, 0))],
        out_specs=[pl.BlockSpec((a, cb, nq * LANES), lambda i: (0, i, 0)), stat_spec, stat_spec],
        out_shape=[jax.ShapeDtypeStruct((a, c, nq * LANES), BF16),
                   jax.ShapeDtypeStruct((a, nq * LANES), F32),
                   jax.ShapeDtypeStruct((a, nq * LANES), F32)],
        scratch_shapes=[pltpu.VMEM((a * (cb + CONV_SKEW), LANES), F32),
                        pltpu.VMEM((a * (cb + CONV_SKEW), LANES), F32)],
        compiler_params=_cparams(("arbitrary",)),
        name="convt",
    )(vt, t, bias.reshape(c, 1))


@functools.lru_cache(maxsize=None)
def _tap_selector(a):
    kl = -(-a // LANES) * LANES
    tap = np.arange(kl)[None, :] - np.arange(a)[:, None] + CONV_PAD
    hit = (tap[None] == np.arange(CONV_TAPS)[:, None, None]) & (np.arange(kl) < a)
    return hit.astype(BF16)


def _conv_toeplitz(w, a):
    return jnp.einsum("kc,kxy->cxy", w.astype(BF16), jnp.asarray(_tap_selector(a)),
                      preferred_element_type=BF16)


def _conv_branch_t(cv, g1, st, lng_ref, lnb_ref, n_chan):
    mean = (st[0:1] + st[2:3]) * (1.0 / n_chan)
    var = (st[1:2] + st[3:4]) * (1.0 / n_chan) - mean * mean
    y = (cv.astype(F32) - mean) * lax.rsqrt(var + EPS) * lng_ref[...] + lnb_ref[...]
    return (jax.nn.silu(y) * g1.astype(F32)).astype(BF16)


def _tail_w_kernel(y_ref, g2_ref, cv_ref, g1_ref, st_ref, wt_ref, bt_ref, wos_ref, woh_ref,
                   lng_ref, lnb_ref, o_ref, mt_sc, *, n_chan):
    tl = y_ref.shape[-1]
    st = st_ref[...]
    mix = []
    for k in range(2):
        lanes = slice(k * tl, (k + 1) * tl)
        y = y_ref[k]
        z = jnp.dot(wt_ref[...], y, preferred_element_type=F32) + bt_ref[...]
        s = (y.astype(F32) * jax.nn.sigmoid(z) * g2_ref[k].astype(F32)).astype(BF16)
        co = _conv_branch_t(cv_ref[k], g1_ref[:, lanes], st[:, lanes], lng_ref, lnb_ref, n_chan)
        mt_sc[...] = (jnp.dot(wos_ref[...], s, preferred_element_type=F32)
                      + jnp.dot(woh_ref[...], co, preferred_element_type=F32))
        mix.append(mt_sc[...].T.reshape(tl // BATCH, BATCH, mt_sc.shape[0]))
    o_ref[...] = jnp.concatenate(mix, axis=1).astype(o_ref.dtype)


def _tail_w(y1t, g2t, ct_h, g1t_h, stats_h, glu_wt, glu_bcol, wot, ln_g, ln_b, n_chan):
    _, sw, _ = y1t.shape
    _, hc, tl = ct_h.shape
    d = wot.shape[0]
    assert (2 * hc) % sw == 0
    r = tl // BATCH
    half_chunk = CHUNK // 2
    s5_spec = pl.BlockSpec((2, sw, tl), lambda j: (j % half_chunk, 0, j // half_chunk))
    tok = lambda rows: pl.BlockSpec((rows, 2 * tl), lambda j: (0, j))
    full = lambda arr: pl.BlockSpec(arr.shape, lambda j: (0,) * arr.ndim)
    return pl.pallas_call(
        functools.partial(_tail_w_kernel, n_chan=n_chan),
        grid=(GRID_W // 2,),
        in_specs=[s5_spec, s5_spec, pl.BlockSpec((2, hc, tl), lambda j: (j, 0, 0)), tok(hc),
                  tok(8), full(glu_wt), full(glu_bcol),
                  pl.BlockSpec((d, sw), lambda j: (0, 2 * hc // sw)),
                  pl.BlockSpec((d, hc), lambda j: (0, 0)), full(ln_g), full(ln_b)],
        out_specs=pl.BlockSpec((r, None, 2 * BATCH, d), lambda j: (0, j, 0, 0)),
        out_shape=jax.ShapeDtypeStruct((r, GRID_W // 2, 2 * BATCH, d), BF16),
        scratch_shapes=[pltpu.VMEM((d, tl), F32)],
        compiler_params=_cparams(("arbitrary",)),
        name="tail_w",
    )(y1t, g2t, ct_h, g1t_h, stats_h, glu_wt, glu_bcol, wot, wot, ln_g, ln_b)


FINAL_ROWS = 2


def _final_r_kernel(cv_ref, g1_ref, st_ref, mh_ref, x_ref, wo_ref, lng_ref, lnb_ref, gate_ref,
                    fg_ref, o_ref, mix_sc, *, n_chan):
    tl, d = x_ref.shape[1:]
    cv = jnp.concatenate([cv_ref[i] for i in range(cv_ref.shape[0])], axis=-1)
    co = _conv_branch_t(cv, g1_ref[...], st_ref[...], lng_ref, lnb_ref, n_chan)
    mix = lax.dot_general(co.T, wo_ref[...], (((1,), (1,)), ((), ())),
                          preferred_element_type=F32) + mh_ref[...].reshape(tl * BATCH, d)
    nlt = mix_sc.shape[0]
    for j in range(nlt):
        mix_sc[j] = mix[:, j * LANES:(j + 1) * LANES]
    for b in range(BATCH):
        mb = jnp.concatenate([mix_sc[j, pl.ds(b, tl, stride=BATCH), :] for j in range(nlt)],
                             axis=-1)
        xo = x_ref[b] + gate_ref[b:b + 1, :] * mb
        ms = jnp.mean(xo * xo, axis=-1, keepdims=True)
        o_ref[b] = xo * lax.rsqrt(ms + EPS) * fg_ref[...]


def _final_r(ct_v, g1t_v, stats_v, mix_h, x, wot, ln_g, ln_b, mod, final_g, n_chan):
    r, hc, row_tokens = ct_v.shape
    _, length, d = x.shape
    rs = math.gcd(r, FINAL_ROWS)
    tm = rs * row_tokens
    tl = tm // BATCH
    tok = lambda rows: pl.BlockSpec((rows, tm), lambda i: (0, i))
    nat = pl.BlockSpec((BATCH, tl, d), lambda i: (0, i, 0))
    full = lambda arr: pl.BlockSpec(arr.shape, lambda i: (0,) * arr.ndim)
    return pl.pallas_call(
        functools.partial(_final_r_kernel, n_chan=n_chan),
        grid=(r // rs,),
        in_specs=[pl.BlockSpec((rs, hc, row_tokens), lambda i: (i, 0, 0)), tok(hc), tok(8),
                  pl.BlockSpec((rs, GRID_W // 2, 2 * BATCH, d), lambda i: (i, 0, 0, 0)), nat,
                  pl.BlockSpec((d, hc), lambda i: (0, 1)), full(ln_g), full(ln_b),
                  pl.BlockSpec((BATCH, d), lambda i: (0, 2)), full(final_g)],
        out_specs=nat,
        out_shape=jax.ShapeDtypeStruct(x.shape, F32),
        scratch_shapes=[pltpu.VMEM((d // LANES, tm, LANES), F32)],
        compiler_params=_cparams(("arbitrary",)),
        name="final_r",
    )(ct_v, g1t_v, stats_v, mix_h, x, wot, ln_g, ln_b, mod, final_g)


S5PREP_GROUPS = 8


def _s5prep_kernel(arow_ref, bt_ref, ct_ref, wall_ref, cot_ref, lam_ref, pw_sc, kk_sc):
    for gi in range(arow_ref.shape[0]):
        _s5prep_group(arow_ref.at[gi], bt_ref.at[gi], ct_ref.at[gi], wall_ref.at[gi], cot_ref.at[gi],
                      lam_ref.at[gi], pw_sc, kk_sc)


def _s5prep_group(arow_ref, bt_ref, ct_ref, wall_ref, cot_ref, lam_ref, pw_sc, kk_sc):
    kdim = CHUNK * SSM_GROUP
    hp = lax.Precision.HIGHEST
    lane = lax.broadcasted_iota(jnp.int32, (1, LANES), 1)
    is_re = lane < SSM_STATE
    col_t = lax.broadcasted_iota(jnp.int32, (1, kdim), 1) // SSM_GROUP
    kcol = lax.broadcasted_iota(jnp.int32, (pw_sc.shape[1], 1), 0).astype(F32)

    def tile_rows(a):
        return jnp.concatenate([a] * CHUNK, axis=0)

    def power_rows(exps):
        pr = jnp.concatenate([jnp.broadcast_to(pw_sc[0, e:e + 1, :], (SSM_GROUP, LANES))
                              for e in exps], axis=0)
        pi = jnp.concatenate([jnp.broadcast_to(pw_sc[1, e:e + 1, :], (SSM_GROUP, LANES))
                              for e in exps], axis=0)
        return pr, pi

    def cmul(xr, xi, yr, yi, im_sign):
        return jnp.where(is_re, xr * yr - xi * yi, im_sign * (xr * yi + xi * yr))

    toeplitz = jnp.zeros((kdim, kdim), F32)
    for d in range(2):
        ar, ai = arow_ref[d, 0:1, :], arow_ref[d, 1:2, :]
        dt = jnp.exp(arow_ref[d, 2:3, :])
        mag = jnp.exp(kcol * (ar * dt))
        ang = kcol * (ai * dt)
        pw_sc[0] = mag * jnp.cos(ang)
        pw_sc[1] = mag * jnp.sin(ang)
        lam_ref[d, 0] = pw_sc[0, CHUNK:CHUNK + 8, :]
        lam_ref[d, 1] = pw_sc[1, CHUNK:CHUNK + 8, :]
        nr, ni = pw_sc[0, 1:2, :] - 1.0, pw_sc[1, 1:2, :]
        den = ar * ar + ai * ai
        qr, qi = (nr * ar + ni * ai) / den, (ni * ar - nr * ai) / den
        btr, bti = bt_ref[d, 0], bt_ref[d, 1]
        bbr, bbi = tile_rows(qr * btr - qi * bti), tile_rows(qr * bti + qi * btr)
        cr, ci = tile_rows(ct_ref[d, 0]), tile_rows(ct_ref[d, 1])
        tt = list(range(CHUNK))
        e_lag = [CHUNK - 1 - t for t in tt] if d else tt
        e_in = tt if d else [CHUNK - 1 - t for t in tt]
        e_out = [CHUNK - t for t in tt] if d else [t + 1 for t in tt]
        cot_ref[:, kdim + d * LANES:kdim + (d + 1) * LANES] = cmul(
            cr, ci, *power_rows(e_out), -1.0).astype(BF16)
        wall_ref[d * LANES:(d + 1) * LANES, :] = cmul(bbr, bbi, *power_rows(e_in), 1.0).T.astype(BF16)
        c_lag = cmul(cr, ci, *power_rows(e_lag), -1.0)
        bb = jnp.where(is_re, bbr, bbi)
        kk = lax.dot_general(c_lag, bb, (((1,), (1,)), ((), ())), precision=hp,
                             preferred_element_type=F32)
        zeros = jnp.zeros((kdim, kdim), F32)
        if d:
            kk_sc[:kdim] = kk
            kk_sc[kdim:] = zeros
            starts = [SSM_GROUP * (CHUNK - 1 - t) for t in tt]
        else:
            kk_sc[:kdim] = zeros
            kk_sc[kdim:] = kk
            starts = [kdim - SSM_GROUP * t for t in tt]
        tiles = []
        for lt in range(kdim // LANES):
            cols = slice(lt * LANES, (lt + 1) * LANES)
            acc = jnp.zeros((kdim, LANES), F32)
            for t in range(lt * LANES // SSM_GROUP, (lt + 1) * LANES // SSM_GROUP):
                acc = jnp.where(col_t[:, cols] == t, kk_sc[starts[t]:starts[t] + kdim, cols], acc)
            tiles.append(acc)
        toeplitz = toeplitz + jnp.concatenate(tiles, axis=1)
    cot_ref[:, :kdim] = toeplitz.astype(BF16)


def _s5prep(arow, bt, ct):
    g = arow.shape[0]
    gs = math.gcd(g, S5PREP_GROUPS)
    kdim = CHUNK * SSM_GROUP
    return pl.pallas_call(
        _s5prep_kernel,
        grid=(g // gs,),
        in_specs=[pl.BlockSpec((gs, 2, 8, LANES), lambda i: (i, 0, 0, 0)),
                  pl.BlockSpec((gs, 2, 2, SSM_GROUP, LANES), lambda i: (i, 0, 0, 0, 0)),
                  pl.BlockSpec((gs, 2, 2, SSM_GROUP, LANES), lambda i: (i, 0, 0, 0, 0))],
        out_specs=[pl.BlockSpec((gs, kdim, kdim), lambda i: (i, 0, 0)),
                   pl.BlockSpec((gs, kdim, 2 * kdim), lambda i: (i, 0, 0)),
                   pl.BlockSpec((gs, 2, 2, 8, LANES), lambda i: (i, 0, 0, 0, 0))],
        out_shape=[jax.ShapeDtypeStruct((g, kdim, kdim), BF16),
                   jax.ShapeDtypeStruct((g, kdim, 2 * kdim), BF16),
                   jax.ShapeDtypeStruct((g, 2, 2, 8, LANES), F32)],
        scratch_shapes=[pltpu.VMEM((2, CHUNK + 8, LANES), F32),
                        pltpu.VMEM((2 * kdim, kdim), F32)],
        compiler_params=_cparams(("arbitrary",)),
        name="s5prep",
    )(arow, bt, ct)


def _s5prep_inputs(a_re, a_im, log_dt, b_re, b_im, c_re, c_im):
    g = a_re.shape[1]
    dup = lambda a: jnp.concatenate([a, a], axis=-1)
    rows = jnp.stack([dup(a_re), dup(a_im),
                      jnp.broadcast_to(log_dt[..., None], (2, g, LANES))], axis=2)
    arow = jnp.pad(rows, ((0, 0), (0, 0), (0, 5), (0, 0))).transpose(1, 0, 2, 3)
    bt = jnp.stack([dup(b_re.transpose(0, 1, 3, 2)), dup(b_im.transpose(0, 1, 3, 2))], axis=2)
    ct = jnp.stack([dup(c_re), dup(c_im)], axis=2)
    return arow, bt.transpose(1, 0, 2, 3, 4), ct.transpose(1, 0, 2, 3, 4)


def kernel(x, c, ctx, c_ctx, norm_g, w_ada, b_ada, w_in, conv_dw, conv_db, conv_ln_g, conv_ln_b,
           ssm_a_re, ssm_a_im, ssm_log_dt, ssm_b_re, ssm_b_im, ssm_c_re, ssm_c_im, ssm_d,
           ssm_glu_w, ssm_glu_b, w_out, final_g):
    bsz, length, d = x.shape
    ctx_len = ctx.shape[1]
    cw = conv_dw.shape[-1]
    sw = ssm_d.shape[-1]
    assert bsz == BATCH and norm_g.shape[0] == 1
    assert length % (GRID_W * CHUNK) == 0 and ctx_len % CHUNK == 0
    n = length * BATCH
    nch = length // CHUNK
    nchc = ctx_len // CHUNK
    rows = length // GRID_W

    cond16 = jnp.concatenate([c, jnp.broadcast_to(c_ctx, (BATCH, d))])
    mod = _ada(cond16, w_ada.reshape(w_ada.shape[1:]), b_ada[0])

    w = w_in.reshape(w_in.shape[1:])
    half = cw // 2
    wt_c = _transpose_cast(w, 0, 3 * cw, cw)
    wt_s = _transpose_cast(w, 3 * cw, 2 * sw, sw)
    wall, cot, lam_raw = _s5prep(*_s5prep_inputs(
        ssm_a_re[0], ssm_a_im[0], ssm_log_dt[0], ssm_b_re[0], ssm_b_im[0], ssm_c_re[0], ssm_c_im[0]))
    ng = sw // SSM_GROUP
    lam16 = lam_raw[:, :, :, 0, :SSM_STATE].reshape(ng // 2, 2, 2, 2, SSM_STATE).transpose(
        2, 0, 3, 1, 4).reshape(2, ng // 2, 2, LANES)
    dcol = jnp.broadcast_to(ssm_d[0].reshape(ng, 1, SSM_GROUP), (ng, CHUNK, SSM_GROUP)).reshape(
        ng, CHUNK * SSM_GROUP, 1)

    xt = x.transpose(1, 0, 2)
    ctxt = ctx.transpose(1, 0, 2)
    c4 = ctxt.reshape(nchc, CHUNK // 2, 2 * BATCH, d)

    x4 = xt.reshape(rows, GRID_W, BATCH, d)
    ut, g2t, vt_h, g1t_h = _inproj_w(x4, norm_g, mod, wt_s, wt_c, half)
    vt_v, g1t_v = _inproj_r(x4, norm_g, mod, wt_c, half)
    uct = _ctxproj(c4, norm_g, mod, wt_s)
    y1t = _s5(ut, uct, wall, cot, lam16, dcol)

    dw, db = conv_dw[0], conv_db[0]
    ct_h, s1_h, s2_h = _convt(vt_h, _conv_toeplitz(dw[:, :half], GRID_W), db[:half])
    ct_v, s1_v, s2_v = _convt(vt_v, _conv_toeplitz(dw[:, half:], rows), db[half:])
    to_v = lambda a: a.reshape(GRID_W, rows, BATCH).transpose(1, 0, 2).reshape(1, n)
    to_h = lambda a: a.reshape(rows, GRID_W, BATCH).transpose(1, 0, 2).reshape(1, n)
    pad4 = jnp.zeros((4, n), F32)
    stats_v = jnp.concatenate([s1_v.reshape(1, n), s2_v.reshape(1, n), to_v(s1_h), to_v(s2_h), pad4])
    stats_h = jnp.concatenate([s1_h.reshape(1, n), s2_h.reshape(1, n), to_h(s1_v), to_h(s2_v), pad4])

    ln_g, ln_b = conv_ln_g[0].reshape(cw, 1), conv_ln_b[0].reshape(cw, 1)
    wot = _transpose_cast(w_out.reshape(w_out.shape[1:]), 0, d, half)
    mix_h = _tail_w(y1t, g2t, ct_h, g1t_h, stats_h, _transpose_cast(ssm_glu_w[0], 0, sw, sw),
                    ssm_glu_b[0].reshape(sw, 1), wot, ln_g[:half], ln_b[:half], cw)
    return _final_r(ct_v, g1t_v, stats_v, mix_h, x, wot, ln_g[half:],
                    ln_b[half:], mod, final_g.reshape(1, d), cw)
```

```python
import functools
import math

import jax
import jax.numpy as jnp
import numpy as np
from jax import lax
from jax.experimental import pallas as pl
from jax.experimental.pallas import tpu as pltpu

GRID_W = 64
CONV_TAPS = 31
CONV_PAD = CONV_TAPS // 2
SSM_GROUP = 16
SSM_STATE = 64
CHUNK = 16
BATCH = 8
EPS = 1e-6
LANES = 128
MXU_WIDTH = 256
VMEM_LIMIT = 56 * 1024 * 1024

F32 = jnp.float32
BF16 = jnp.bfloat16


def _cparams(sem):
    return pltpu.CompilerParams(dimension_semantics=sem, vmem_limit_bytes=VMEM_LIMIT)


def _ada_kernel(c_ref, w_ref, b_ref, o_ref):
    s = jax.nn.silu(c_ref[...])
    o_ref[...] = jnp.dot(s, w_ref[...], preferred_element_type=F32,
                         precision=lax.Precision.HIGHEST) + b_ref[...]


def _ada(cond16, w_ada, b_ada):
    d, n3 = w_ada.shape
    return pl.pallas_call(
        _ada_kernel,
        grid=(n3 // d,),
        in_specs=[pl.BlockSpec((16, d), lambda j: (0, 0)),
                  pl.BlockSpec((d, d), lambda j: (0, j)),
                  pl.BlockSpec((1, d), lambda j: (0, j))],
        out_specs=pl.BlockSpec((16, d), lambda j: (0, j)),
        out_shape=jax.ShapeDtypeStruct((16, n3), F32),
        compiler_params=_cparams(("arbitrary",)),
        name="ada",
    )(cond16, w_ada, b_ada.reshape(1, n3))


def _modulated_norm(x, g_ref, shift_ref, scale_ref):
    ms = jnp.mean(x * x, axis=-1, keepdims=True)
    amp = g_ref[...] * (1.0 + scale_ref[...])
    return x * lax.rsqrt(ms + EPS) * amp[None] + shift_ref[...][None]


def _mod_specs(d, row_block):
    const = lambda blk: (lambda *_: blk)
    return [pl.BlockSpec((1, d), const((0, 0))),
            pl.BlockSpec((BATCH, d), const((row_block, 0))),
            pl.BlockSpec((BATCH, d), const((row_block, 1)))]


def _lane_pieces(n_lanes):
    width = MXU_WIDTH if n_lanes % MXU_WIDTH == 0 else n_lanes
    return [(lo, width) for lo in range(0, n_lanes, width)]


def _conv_gates(q, hc, vt_ref, g1t_ref, lane0, gate_lane0=0):
    width = q.shape[1]
    val = (q[:hc] * jax.nn.sigmoid(q[hc:2 * hc])).astype(BF16)
    for j in range(width // LANES):
        vt_ref[lane0 // LANES + j] = val[:, j * LANES:(j + 1) * LANES]
    lo = gate_lane0 + lane0
    g1t_ref[:, lo:lo + width] = jax.nn.silu(q[2 * hc:]).astype(BF16)


def _transpose_cast_kernel(w_ref, o_ref):
    o_ref[...] = w_ref[...].T.astype(o_ref.dtype)


def _transpose_cast(w, col0, ncols, tile):
    k = w.shape[0]
    assert col0 % tile == 0 and ncols % tile == 0
    return pl.pallas_call(
        _transpose_cast_kernel,
        grid=(ncols // tile,),
        in_specs=[pl.BlockSpec((k, tile), lambda j: (0, j + col0 // tile))],
        out_specs=pl.BlockSpec((tile, k), lambda j: (j, 0)),
        out_shape=jax.ShapeDtypeStruct((ncols, k), BF16),
        compiler_params=_cparams(("arbitrary",)),
        name="wtrans",
    )(w)


def _project_t(w_refs, h):
    nt = (((1,), (1,)), ((), ()))
    return jnp.concatenate([lax.dot_general(w[...], h, nt, preferred_element_type=F32)
                            for w in w_refs], axis=0)


def _inproj_w_kernel(x_ref, g_ref, sh_ref, sc_ref, ws_ref, wv_ref, wg_ref, wa_ref, ut_ref, g2t_ref,
                     vt_ref, g1t_ref):
    r, ncol, _, d = x_ref.shape
    sw = ut_ref.shape[1]
    hc = g1t_ref.shape[0]
    tl = r * BATCH
    for k in range(ncol):
        for lo, width in _lane_pieces(tl):
            x = x_ref[lo // BATCH:(lo + width) // BATCH, k]
            h = _modulated_norm(x, g_ref, sh_ref, sc_ref).reshape(width, d).astype(BF16)
            q = _project_t([ws_ref], h)
            ut_ref[k, :, lo:lo + width] = q[:sw].astype(BF16)
            g2t_ref[k, :, lo:lo + width] = jax.nn.silu(q[sw:]).astype(BF16)
            _conv_gates(_project_t([wv_ref, wg_ref, wa_ref], h), hc, vt_ref.at[k], g1t_ref,
                        lo, k * tl)


def _conv_weight_specs(hc, d, part, index_map):
    return [pl.BlockSpec((hc, d), functools.partial(index_map, 2 * k + part),
                         pipeline_mode=pl.Buffered(1)) for k in range(3)]


INPROJ_COLS = 4
INPROJ_ROWS = 4


def _inproj_w(x4, norm_g, mod, wt_s, wt_c, hc):
    r, _, _, d = x4.shape
    sw = wt_s.shape[0] // 2
    tl = r * BATCH
    nq = tl // LANES
    cpr = GRID_W // CHUNK
    nc = INPROJ_COLS
    per_chunk = CHUNK // nc
    s5_spec = pl.BlockSpec((nc, sw, tl), lambda j: (j % per_chunk, 0, j // per_chunk))
    return pl.pallas_call(
        _inproj_w_kernel,
        grid=(GRID_W // nc,),
        in_specs=[pl.BlockSpec((r, nc, BATCH, d), lambda j: (0, j, 0, 0)),
                  *_mod_specs(d, 0),
                  pl.BlockSpec(wt_s.shape, lambda j: (0, 0), pipeline_mode=pl.Buffered(1)),
                  *_conv_weight_specs(hc, d, 0, lambda blk, j: (blk, 0))],
        out_specs=[s5_spec, s5_spec,
                   pl.BlockSpec((nc, nq, hc, LANES), lambda j: (j, 0, 0, 0)),
                   pl.BlockSpec((hc, nc * tl), lambda j: (0, j))],
        out_shape=[jax.ShapeDtypeStruct((CHUNK, sw, cpr * tl), BF16),
                   jax.ShapeDtypeStruct((CHUNK, sw, cpr * tl), BF16),
                   jax.ShapeDtypeStruct((GRID_W, nq, hc, LANES), BF16),
                   jax.ShapeDtypeStruct((hc, GRID_W * tl), BF16)],
        compiler_params=_cparams(("arbitrary",)),
        name="inproj_w",
    )(x4, norm_g, mod, mod, wt_s, wt_c, wt_c, wt_c)


def _inproj_r_kernel(x_ref, g_ref, sh_ref, sc_ref, wv_ref, wg_ref, wa_ref, vt_ref, g1t_ref):
    nrow, gw, _, d = x_ref.shape
    hc = g1t_ref.shape[0]
    tl = gw * BATCH
    for k in range(nrow):
        for lo, width in _lane_pieces(tl):
            x = x_ref[k, lo // BATCH:(lo + width) // BATCH]
            h = _modulated_norm(x, g_ref, sh_ref, sc_ref).reshape(width, d).astype(BF16)
            _conv_gates(_project_t([wv_ref, wg_ref, wa_ref], h), hc, vt_ref.at[k], g1t_ref,
                        lo, k * tl)


def _inproj_r(x4, norm_g, mod, wt_c, hc):
    r, gw, _, d = x4.shape
    tl = gw * BATCH
    nq = tl // LANES
    nr = math.gcd(r, INPROJ_ROWS)
    return pl.pallas_call(
        _inproj_r_kernel,
        grid=(r // nr,),
        in_specs=[pl.BlockSpec((nr, gw, BATCH, d), lambda i: (i, 0, 0, 0)),
                  *_mod_specs(d, 0),
                  *_conv_weight_specs(hc, d, 1, lambda blk, i: (blk, 0))],
        out_specs=[pl.BlockSpec((nr, nq, hc, LANES), lambda i: (i, 0, 0, 0)),
                   pl.BlockSpec((hc, nr * tl), lambda i: (0, i))],
        out_shape=[jax.ShapeDtypeStruct((r, nq, hc, LANES), BF16),
                   jax.ShapeDtypeStruct((hc, r * tl), BF16)],
        compiler_params=_cparams(("arbitrary",)),
        name="inproj_r",
    )(x4, norm_g, mod, mod, wt_c, wt_c, wt_c)


def _ctxproj_kernel(x_ref, g_ref, sh_ref, sc_ref, wut_ref, ut_ref):
    cb, _, d = x_ref.shape
    x = x_ref[...].reshape(cb * 2, BATCH, d)
    h = _modulated_norm(x, g_ref, sh_ref, sc_ref).reshape(cb, 2 * BATCH, d)
    for tl in range(2):
        ht = h[:, tl * BATCH:(tl + 1) * BATCH, :].reshape(cb * BATCH, d).astype(BF16)
        q = lax.dot_general(wut_ref[...], ht, (((1,), (1,)), ((), ())),
                            preferred_element_type=F32)
        ut_ref[tl] = q.astype(BF16)


def _ctxproj(x4, norm_g, mod, wt_s):
    nch, _, _, d = x4.shape
    sw = wt_s.shape[0] // 2
    return pl.pallas_call(
        _ctxproj_kernel,
        grid=(CHUNK // 2,),
        in_specs=[pl.BlockSpec((nch, None, 2 * BATCH, d), lambda t: (0, t, 0, 0)),
                  *_mod_specs(d, 1),
                  pl.BlockSpec((sw, d), lambda t: (0, 0))],
        out_specs=pl.BlockSpec((2, sw, nch * BATCH), lambda t: (t, 0, 0)),
        out_shape=jax.ShapeDtypeStruct((CHUNK, sw, nch * BATCH), BF16),
        compiler_params=_cparams(("arbitrary",)),
        name="ctxproj",
    )(x4, norm_g, mod, mod, wt_s)


GELU_C1 = math.sqrt(2.0 / math.pi)
GELU_C2 = GELU_C1 * 0.044715


def _gelu_tanh(x):
    hx = 0.5 * x
    return hx + hx * jnp.tanh(x * (GELU_C1 + GELU_C2 * (x * x)))


def _s5_kernel(xt_ref, xct_ref, wall_ref, cot_ref, lam_ref, dcol_ref, y_ref,
               s_sc, h_sc, *, lane_chunk):
    nl = xt_ref.shape[-1]
    nlc = xct_ref.shape[-1]
    n_ch = nl // BATCH
    n_chc = nlc // BATCH
    kdim = CHUNK * SSM_GROUP
    p = SSM_STATE

    def group_x(ref, j):
        return ref[:, j * SSM_GROUP:(j + 1) * SSM_GROUP, :].reshape(kdim, ref.shape[-1])

    st_c = [jnp.dot(wall_ref[j], group_x(xct_ref, j), preferred_element_type=F32)
            for j in range(2)]
    st = [jnp.dot(wall_ref[j], group_x(xt_ref, j), preferred_element_type=F32) for j in range(2)]
    for d in range(2):
        for src, off, width in ((st_c, 0, nlc), (st, nlc, nl)):
            for ri in range(2):
                lo = (2 * d + ri) * p
                pair = jnp.concatenate([src[0][lo:lo + p], src[1][lo:lo + p]], axis=0)
                s_sc[d, ri, off:off + width, :] = pair.T

    lam = [[jnp.broadcast_to(lam_ref[d, ri:ri + 1, :], (BATCH, LANES)) for ri in range(2)]
           for d in range(2)]

    def advance(d, state, row):
        hr, hi = state
        ar, ai = lam[d]
        sr = s_sc[d, 0, pl.ds(row, BATCH), :]
        si = s_sc[d, 1, pl.ds(row, BATCH), :]
        return ar * hr - ai * hi + sr, ar * hi + ai * hr + si

    zero = jnp.zeros((BATCH, LANES), F32)

    f = b = (zero, zero)
    for i in range(n_chc):
        f = advance(0, f, i * BATCH)
        b = advance(1, b, (n_chc - 1 - i) * BATCH)

    cpr = GRID_W // CHUNK
    n_rows = n_ch // cpr

    def lane_row(ch):
        return ((ch % cpr) * n_rows + ch // cpr) * BATCH

    for i in range(n_ch):
        rf = lane_row(i)
        rb = lane_row(n_ch - 1 - i)
        h_sc[0, 0, rf:rf + BATCH, :] = f[0]
        h_sc[0, 1, rf:rf + BATCH, :] = f[1]
        h_sc[1, 0, rb:rb + BATCH, :] = b[0]
        h_sc[1, 1, rb:rb + BATCH, :] = b[1]
        f = advance(0, f, nlc + rf)
        b = advance(1, b, nlc + rb)

    for c0 in range(0, nl, lane_chunk):
        cs = slice(c0, c0 + lane_chunk)
        ht = [[h_sc[d, ri, cs, :].T for ri in range(2)] for d in range(2)]
        for j in range(2):
            x = xt_ref[:, j * SSM_GROUP:(j + 1) * SSM_GROUP, cs].reshape(kdim, lane_chunk)
            hj = jnp.concatenate([ht[d][ri][j * p:(j + 1) * p] for d in range(2) for ri in range(2)],
                                 axis=0).astype(BF16)
            acc = (dcol_ref[j] * x.astype(F32)
                   + jnp.dot(cot_ref[j], jnp.concatenate([x, hj], axis=0),
                             preferred_element_type=F32))
            y = _gelu_tanh(acc).astype(BF16)
            y_ref[:, j * SSM_GROUP:(j + 1) * SSM_GROUP, cs] = y.reshape(CHUNK, SSM_GROUP, lane_chunk)


def _s5(ut, uct, wall, cot, lam16, dcol):
    _, sw, nl = ut.shape
    nlc = uct.shape[-1]
    gp = sw // (2 * SSM_GROUP)
    kdim = CHUNK * SSM_GROUP
    lane_chunk = min(nl, 512)
    kern = functools.partial(_s5_kernel, lane_chunk=lane_chunk)
    return pl.pallas_call(
        kern,
        grid=(gp,),
        in_specs=[pl.BlockSpec((CHUNK, 2 * SSM_GROUP, nl), lambda g: (0, g, 0)),
                  pl.BlockSpec((CHUNK, 2 * SSM_GROUP, nlc), lambda g: (0, g, 0)),
                  pl.BlockSpec((2, kdim, kdim), lambda g: (g, 0, 0)),
                  pl.BlockSpec((2, kdim, 2 * kdim), lambda g: (g, 0, 0)),
                  pl.BlockSpec((2, None, 2, LANES), lambda g: (0, g, 0, 0)),
                  pl.BlockSpec((2, kdim, 1), lambda g: (g, 0, 0))],
        out_specs=pl.BlockSpec((CHUNK, 2 * SSM_GROUP, nl), lambda g: (0, g, 0)),
        out_shape=jax.ShapeDtypeStruct((CHUNK, sw, nl), BF16),
        scratch_shapes=[pltpu.VMEM((2, 2, nlc + nl, LANES), F32),
                        pltpu.VMEM((2, 2, nl, LANES), F32)],
        compiler_params=_cparams(("arbitrary",)),
        name="s5",
    )(ut, uct, wall, cot, lam16, dcol)


CONV_CB = 32
CONV_SKEW = 8


def _convt_kernel(v_ref, t_ref, b_ref, o_ref, s1_ref, s2_ref, z_sc, o_sc):
    a, nq, cb, _ = v_ref.shape
    kl = t_ref.shape[-1]
    pitch = cb + CONV_SKEW
    k_pad = jnp.zeros((kl - a, LANES), BF16)
    for q in range(nq):
        lanes = slice(q * LANES, (q + 1) * LANES)
        z = v_ref[:, q].astype(F32)
        for i in range(a):
            z_sc[i * pitch:i * pitch + cb, :] = z[i]
        s1 = jnp.zeros((a, LANES), F32)
        s2 = jnp.zeros((a, LANES), F32)
        for c in range(cb):
            col = z_sc[pl.ds(c, a, stride=pitch), :].astype(BF16)
            rhs = jnp.concatenate([col, k_pad], axis=0) if kl > a else col
            out = jnp.dot(t_ref[c], rhs, preferred_element_type=F32) + b_ref[c:c + 1]
            o_sc[pl.ds(c, a, stride=pitch), :] = out
            s1 = s1 + out
            s2 = s2 + out * out
        for i in range(a):
            o_ref[i, :, lanes] = o_sc[i * pitch:i * pitch + cb, :].astype(o_ref.dtype)

        @pl.when(pl.program_id(0) == 0)
        def _():
            s1_ref[:, lanes] = jnp.zeros((a, LANES), F32)
            s2_ref[:, lanes] = jnp.zeros((a, LANES), F32)

        s1_ref[:, lanes] += s1
        s2_ref[:, lanes] += s2


def _convt(vt, t, bias):
    a, nq, c, _ = vt.shape
    cb = min(c, CONV_CB)
    kl = t.shape[-1]
    stat_spec = pl.BlockSpec((a, nq * LANES), lambda i: (0, 0))
    return pl.pallas_call(
        _convt_kernel,
        grid=(c // cb,),
        in_specs=[pl.BlockSpec((a, nq, cb, LANES), lambda i: (0, 0, i, 0)),
                  pl.BlockSpec((cb, a, kl), lambda i: (i, 0, 0)),
                  pl.BlockSpec((cb, 1), lambda i: (i, 0))],
        out_specs=[pl.BlockSpec((a, cb, nq * LANES), lambda i: (0, i, 0)), stat_spec, stat_spec],
        out_shape=[jax.ShapeDtypeStruct((a, c, nq * LANES), BF16),
                   jax.ShapeDtypeStruct((a, nq * LANES), F32),
                   jax.ShapeDtypeStruct((a, nq * LANES), F32)],
        scratch_shapes=[pltpu.VMEM((a * (cb + CONV_SKEW), LANES), F32),
                        pltpu.VMEM((a * (cb + CONV_SKEW), LANES), F32)],
        compiler_params=_cparams(("arbitrary",)),
        name="convt",
    )(vt, t, bias.reshape(c, 1))


@functools.lru_cache(maxsize=None)
def _tap_selector(a):
    kl = -(-a // LANES) * LANES
    tap = np.arange(kl)[None, :] - np.arange(a)[:, None] + CONV_PAD
    hit = (tap[None] == np.arange(CONV_TAPS)[:, None, None]) & (np.arange(kl) < a)
    return hit.astype(BF16)


def _conv_toeplitz(w, a):
    return jnp.einsum("kc,kxy->cxy", w.astype(BF16), jnp.asarray(_tap_selector(a)),
                      preferred_element_type=BF16)


def _conv_branch_t(cv, g1, st, lng_ref, lnb_ref, n_chan):
    mean = (st[0:1] + st[2:3]) * (1.0 / n_chan)
    var = (st[1:2] + st[3:4]) * (1.0 / n_chan) - mean * mean
    y = (cv.astype(F32) - mean) * lax.rsqrt(var + EPS) * lng_ref[...] + lnb_ref[...]
    return (jax.nn.silu(y) * g1.astype(F32)).astype(BF16)


def _tail_w_kernel(y_ref, g2_ref, cv_ref, g1_ref, st_ref, wt_ref, bt_ref, wos_ref, woh_ref,
                   lng_ref, lnb_ref, o_ref, mt_sc, *, n_chan):
    tl = y_ref.shape[-1]
    st = st_ref[...]
    mix = []
    for k in range(2):
        lanes = slice(k * tl, (k + 1) * tl)
        y = y_ref[k]
        z = jnp.dot(wt_ref[...], y, preferred_element_type=F32) + bt_ref[...]
        s = (y.astype(F32) * jax.nn.sigmoid(z) * g2_ref[k].astype(F32)).astype(BF16)
        co = _conv_branch_t(cv_ref[k], g1_ref[:, lanes], st[:, lanes], lng_ref, lnb_ref, n_chan)
        mt_sc[...] = (jnp.dot(wos_ref[...], s, preferred_element_type=F32)
                      + jnp.dot(woh_ref[...], co, preferred_element_type=F32))
        mix.append(mt_sc[...].T.reshape(tl // BATCH, BATCH, mt_sc.shape[0]))
    o_ref[...] = jnp.concatenate(mix, axis=1).astype(o_ref.dtype)


def _tail_w(y1t, g2t, ct_h, g1t_h, stats_h, glu_wt, glu_bcol, wot, ln_g, ln_b, n_chan):
    _, sw, _ = y1t.shape
    _, hc, tl = ct_h.shape
    d = wot.shape[0]
    assert (2 * hc) % sw == 0
    r = tl // BATCH
    half_chunk = CHUNK // 2
    s5_spec = pl.BlockSpec((2, sw, tl), lambda j: (j % half_chunk, 0, j // half_chunk))
    tok = lambda rows: pl.BlockSpec((rows, 2 * tl), lambda j: (0, j))
    full = lambda arr: pl.BlockSpec(arr.shape, lambda j: (0,) * arr.ndim)
    return pl.pallas_call(
        functools.partial(_tail_w_kernel, n_chan=n_chan),
        grid=(GRID_W // 2,),
        in_specs=[s5_spec, s5_spec, pl.BlockSpec((2, hc, tl), lambda j: (j, 0, 0)), tok(hc),
                  tok(8), full(glu_wt), full(glu_bcol),
                  pl.BlockSpec((d, sw), lambda j: (0, 2 * hc // sw)),
                  pl.BlockSpec((d, hc), lambda j: (0, 0)), full(ln_g), full(ln_b)],
        out_specs=pl.BlockSpec((r, None, 2 * BATCH, d), lambda j: (0, j, 0, 0)),
        out_shape=jax.ShapeDtypeStruct((r, GRID_W // 2, 2 * BATCH, d), BF16),
        scratch_shapes=[pltpu.VMEM((d, tl), F32)],
        compiler_params=_cparams(("arbitrary",)),
        name="tail_w",
    )(y1t, g2t, ct_h, g1t_h, stats_h, glu_wt, glu_bcol, wot, wot, ln_g, ln_b)


FINAL_ROWS = 2


def _final_r_kernel(cv_ref, g1_ref, st_ref, mh_ref, x_ref, wo_ref, lng_ref, lnb_ref, gate_ref,
                    fg_ref, o_ref, mix_sc, *, n_chan):
    tl, d = x_ref.shape[1:]
    cv = jnp.concatenate([cv_ref[i] for i in range(cv_ref.shape[0])], axis=-1)
    co = _conv_branch_t(cv, g1_ref[...], st_ref[...], lng_ref, lnb_ref, n_chan)
    mix = lax.dot_general(co.T, wo_ref[...], (((1,), (1,)), ((), ())),
                          preferred_element_type=F32) + mh_ref[...].reshape(tl * BATCH, d)
    nlt = mix_sc.shape[0]
    for j in range(nlt):
        mix_sc[j] = mix[:, j * LANES:(j + 1) * LANES]
    for b in range(BATCH):
        mb = jnp.concatenate([mix_sc[j, pl.ds(b, tl, stride=BATCH), :] for j in range(nlt)],
                             axis=-1)
        xo = x_ref[b] + gate_ref[b:b + 1, :] * mb
        ms = jnp.mean(xo * xo, axis=-1, keepdims=True)
        o_ref[b] = xo * lax.rsqrt(ms + EPS) * fg_ref[...]


def _final_r(ct_v, g1t_v, stats_v, mix_h, x, wot, ln_g, ln_b, mod, final_g, n_chan):
    r, hc, row_tokens = ct_v.shape
    _, length, d = x.shape
    rs = math.gcd(r, FINAL_ROWS)
    tm = rs * row_tokens
    tl = tm // BATCH
    once = pl.Buffered(1)
    tok = lambda rows: pl.BlockSpec((rows, tm), lambda i: (0, i))
    nat = pl.BlockSpec((BATCH, tl, d), lambda i: (0, i, 0))
    full = lambda arr: pl.BlockSpec(arr.shape, lambda i: (0,) * arr.ndim, pipeline_mode=once)
    return pl.pallas_call(
        functools.partial(_final_r_kernel, n_chan=n_chan),
        grid=(r // rs,),
        in_specs=[pl.BlockSpec((rs, hc, row_tokens), lambda i: (i, 0, 0)),
                  tok(hc), tok(8),
                  pl.BlockSpec((rs, GRID_W // 2, 2 * BATCH, d), lambda i: (i, 0, 0, 0)), nat,
                  pl.BlockSpec((d, hc), lambda i: (0, 1), pipeline_mode=once), full(ln_g),
                  full(ln_b), pl.BlockSpec((BATCH, d), lambda i: (0, 2), pipeline_mode=once),
                  full(final_g)],
        out_specs=nat,
        out_shape=jax.ShapeDtypeStruct(x.shape, F32),
        scratch_shapes=[pltpu.VMEM((d // LANES, tm, LANES), F32)],
        compiler_params=_cparams(("arbitrary",)),
        name="final_r",
    )(ct_v, g1t_v, stats_v, mix_h, x, wot, ln_g, ln_b, mod, final_g)


S5PREP_GROUPS = 4


def _s5prep_kernel(arow_ref, bt_ref, ct_ref, wall_ref, cot_ref, lam_ref, pw_sc, kk_sc):
    for gi in range(arow_ref.shape[0]):
        _s5prep_group(arow_ref.at[gi], bt_ref.at[gi], ct_ref.at[gi], wall_ref.at[gi], cot_ref.at[gi],
                      lam_ref.at[gi], pw_sc, kk_sc)


def _s5prep_group(arow_ref, bt_ref, ct_ref, wall_ref, cot_ref, lam_ref, pw_sc, kk_sc):
    kdim = CHUNK * SSM_GROUP
    hp = lax.Precision.HIGHEST
    lane = lax.broadcasted_iota(jnp.int32, (1, LANES), 1)
    is_re = lane < SSM_STATE
    col_t = lax.broadcasted_iota(jnp.int32, (1, kdim), 1) // SSM_GROUP
    kcol = lax.broadcasted_iota(jnp.int32, (pw_sc.shape[1], 1), 0).astype(F32)

    def tile_rows(a):
        return jnp.concatenate([a] * CHUNK, axis=0)

    def power_rows(exps):
        pr = jnp.concatenate([jnp.broadcast_to(pw_sc[0, e:e + 1, :], (SSM_GROUP, LANES))
                              for e in exps], axis=0)
        pi = jnp.concatenate([jnp.broadcast_to(pw_sc[1, e:e + 1, :], (SSM_GROUP, LANES))
                              for e in exps], axis=0)
        return pr, pi

    def cmul(xr, xi, yr, yi, im_sign):
        return jnp.where(is_re, xr * yr - xi * yi, im_sign * (xr * yi + xi * yr))

    toeplitz = jnp.zeros((kdim, kdim), F32)
    for d in range(2):
        ar, ai = arow_ref[d, 0:1, :], arow_ref[d, 1:2, :]
        dt = jnp.exp(arow_ref[d, 2:3, :])
        mag = jnp.exp(kcol * (ar * dt))
        ang = kcol * (ai * dt)
        pw_sc[0] = mag * jnp.cos(ang)
        pw_sc[1] = mag * jnp.sin(ang)
        lam_ref[d, 0] = pw_sc[0, CHUNK:CHUNK + 8, :]
        lam_ref[d, 1] = pw_sc[1, CHUNK:CHUNK + 8, :]
        nr, ni = pw_sc[0, 1:2, :] - 1.0, pw_sc[1, 1:2, :]
        den = ar * ar + ai * ai
        qr, qi = (nr * ar + ni * ai) / den, (ni * ar - nr * ai) / den
        btr, bti = bt_ref[d, 0], bt_ref[d, 1]
        bbr, bbi = tile_rows(qr * btr - qi * bti), tile_rows(qr * bti + qi * btr)
        cr, ci = tile_rows(ct_ref[d, 0]), tile_rows(ct_ref[d, 1])
        tt = list(range(CHUNK))
        e_lag = [CHUNK - 1 - t for t in tt] if d else tt
        e_in = tt if d else [CHUNK - 1 - t for t in tt]
        e_out = [CHUNK - t for t in tt] if d else [t + 1 for t in tt]
        cot_ref[:, kdim + d * LANES:kdim + (d + 1) * LANES] = cmul(
            cr, ci, *power_rows(e_out), -1.0).astype(BF16)
        wall_ref[d * LANES:(d + 1) * LANES, :] = cmul(bbr, bbi, *power_rows(e_in), 1.0).T.astype(BF16)
        c_lag = cmul(cr, ci, *power_rows(e_lag), -1.0)
        bb = jnp.where(is_re, bbr, bbi)
        kk = lax.dot_general(c_lag, bb, (((1,), (1,)), ((), ())), precision=hp,
                             preferred_element_type=F32)
        zeros = jnp.zeros((kdim, kdim), F32)
        if d:
            kk_sc[:kdim] = kk
            kk_sc[kdim:] = zeros
            starts = [SSM_GROUP * (CHUNK - 1 - t) for t in tt]
        else:
            kk_sc[:kdim] = zeros
            kk_sc[kdim:] = kk
            starts = [kdim - SSM_GROUP * t for t in tt]
        tiles = []
        for lt in range(kdim // LANES):
            cols = slice(lt * LANES, (lt + 1) * LANES)
            acc = jnp.zeros((kdim, LANES), F32)
            for t in range(lt * LANES // SSM_GROUP, (lt + 1) * LANES // SSM_GROUP):
                acc = jnp.where(col_t[:, cols] == t, kk_sc[starts[t]:starts[t] + kdim, cols], acc)
            tiles.append(acc)
        toeplitz = toeplitz + jnp.concatenate(tiles, axis=1)
    cot_ref[:, :kdim] = toeplitz.astype(BF16)


def _s5prep(arow, bt, ct):
    g = arow.shape[0]
    gs = math.gcd(g, S5PREP_GROUPS)
    kdim = CHUNK * SSM_GROUP
    return pl.pallas_call(
        _s5prep_kernel,
        grid=(g // gs,),
        in_specs=[pl.BlockSpec((gs, 2, 8, LANES), lambda i: (i, 0, 0, 0)),
                  pl.BlockSpec((gs, 2, 2, SSM_GROUP, LANES), lambda i: (i, 0, 0, 0, 0)),
                  pl.BlockSpec((gs, 2, 2, SSM_GROUP, LANES), lambda i: (i, 0, 0, 0, 0))],
        out_specs=[pl.BlockSpec((gs, kdim, kdim), lambda i: (i, 0, 0)),
                   pl.BlockSpec((gs, kdim, 2 * kdim), lambda i: (i, 0, 0)),
                   pl.BlockSpec((gs, 2, 2, 8, LANES), lambda i: (i, 0, 0, 0, 0))],
        out_shape=[jax.ShapeDtypeStruct((g, kdim, kdim), BF16),
                   jax.ShapeDtypeStruct((g, kdim, 2 * kdim), BF16),
                   jax.ShapeDtypeStruct((g, 2, 2, 8, LANES), F32)],
        scratch_shapes=[pltpu.VMEM((2, CHUNK + 8, LANES), F32),
                        pltpu.VMEM((2 * kdim, kdim), F32)],
        compiler_params=_cparams(("arbitrary",)),
        name="s5prep",
    )(arow, bt, ct)


def _s5prep_inputs(a_re, a_im, log_dt, b_re, b_im, c_re, c_im):
    g = a_re.shape[1]
    dup = lambda a: jnp.concatenate([a, a], axis=-1)
    rows = jnp.stack([dup(a_re), dup(a_im),
                      jnp.broadcast_to(log_dt[..., None], (2, g, LANES))], axis=2)
    arow = jnp.pad(rows, ((0, 0), (0, 0), (0, 5), (0, 0))).transpose(1, 0, 2, 3)
    bt = jnp.stack([dup(b_re.transpose(0, 1, 3, 2)), dup(b_im.transpose(0, 1, 3, 2))], axis=2)
    ct = jnp.stack([dup(c_re), dup(c_im)], axis=2)
    return arow, bt.transpose(1, 0, 2, 3, 4), ct.transpose(1, 0, 2, 3, 4)


def kernel(x, c, ctx, c_ctx, norm_g, w_ada, b_ada, w_in, conv_dw, conv_db, conv_ln_g, conv_ln_b,
           ssm_a_re, ssm_a_im, ssm_log_dt, ssm_b_re, ssm_b_im, ssm_c_re, ssm_c_im, ssm_d,
           ssm_glu_w, ssm_glu_b, w_out, final_g):
    bsz, length, d = x.shape
    ctx_len = ctx.shape[1]
    cw = conv_dw.shape[-1]
    sw = ssm_d.shape[-1]
    assert bsz == BATCH and norm_g.shape[0] == 1
    assert length % (GRID_W * CHUNK) == 0 and ctx_len % CHUNK == 0
    n = length * BATCH
    nch = length // CHUNK
    nchc = ctx_len // CHUNK
    rows = length // GRID_W

    cond16 = jnp.concatenate([c, jnp.broadcast_to(c_ctx, (BATCH, d))])
    mod = _ada(cond16, w_ada.reshape(w_ada.shape[1:]), b_ada[0])

    w = w_in.reshape(w_in.shape[1:])
    half = cw // 2
    wt_c = _transpose_cast(w, 0, 3 * cw, cw)
    wt_s = _transpose_cast(w, 3 * cw, 2 * sw, sw)
    wall, cot, lam_raw = _s5prep(*_s5prep_inputs(
        ssm_a_re[0], ssm_a_im[0], ssm_log_dt[0], ssm_b_re[0], ssm_b_im[0], ssm_c_re[0], ssm_c_im[0]))
    ng = sw // SSM_GROUP
    lam16 = lam_raw[:, :, :, 0, :SSM_STATE].reshape(ng // 2, 2, 2, 2, SSM_STATE).transpose(
        2, 0, 3, 1, 4).reshape(2, ng // 2, 2, LANES)
    dcol = jnp.broadcast_to(ssm_d[0].reshape(ng, 1, SSM_GROUP), (ng, CHUNK, SSM_GROUP)).reshape(
        ng, CHUNK * SSM_GROUP, 1)

    xt = x.transpose(1, 0, 2)
    ctxt = ctx.transpose(1, 0, 2)
    c4 = ctxt.reshape(nchc, CHUNK // 2, 2 * BATCH, d)

    x4 = xt.reshape(rows, GRID_W, BATCH, d)
    ut, g2t, vt_h, g1t_h = _inproj_w(x4, norm_g, mod, wt_s, wt_c, half)
    vt_v, g1t_v = _inproj_r(x4, norm_g, mod, wt_c, half)
    uct = _ctxproj(c4, norm_g, mod, wt_s)
    y1t = _s5(ut, uct, wall, cot, lam16, dcol)

    dw, db = conv_dw[0], conv_db[0]
    ct_h, s1_h, s2_h = _convt(vt_h, _conv_toeplitz(dw[:, :half], GRID_W), db[:half])
    ct_v, s1_v, s2_v = _convt(vt_v, _conv_toeplitz(dw[:, half:], rows), db[half:])
    to_v = lambda a: a.reshape(GRID_W, rows, BATCH).transpose(1, 0, 2).reshape(1, n)
    to_h = lambda a: a.reshape(rows, GRID_W, BATCH).transpose(1, 0, 2).reshape(1, n)
    pad4 = jnp.zeros((4, n), F32)
    stats_v = jnp.concatenate([s1_v.reshape(1, n), s2_v.reshape(1, n), to_v(s1_h), to_v(s2_h), pad4])
    stats_h = jnp.concatenate([s1_h.reshape(1, n), s2_h.reshape(1, n), to_h(s1_v), to_h(s2_v), pad4])

    ln_g, ln_b = conv_ln_g[0].reshape(cw, 1), conv_ln_b[0].reshape(cw, 1)
    wot = _transpose_cast(w_out.reshape(w_out.shape[1:]), 0, d, half)
    mix_h = _tail_w(y1t, g2t, ct_h, g1t_h, stats_h, _transpose_cast(ssm_glu_w[0], 0, sw, sw),
                    ssm_glu_b[0].reshape(sw, 1), wot, ln_g[:half], ln_b[:half], cw)
    return _final_r(ct_v, g1t_v, stats_v, mix_h, x, wot, ln_g[half:],
                    ln_b[half:], mod, final_g.reshape(1, d), cw)
```
